```python
import math
import jax, jax.numpy as jnp
from jax import lax
import numpy as np

D_MODEL = 2048
BATCH = 8
SEQ = 2048
DEPTH = 1

ROPE_THETA = 500000.0
Q_BLOCK = 128
NORM_EPS = 1e-6

DIFF_HEADS = 8
DIFF_HEAD_DIM = 64
DIFF_V_DIM = 2 * DIFF_HEAD_DIM
DIFF_ROT = DIFF_HEAD_DIM // 4
DIFF_SUBLN_EPS = 1e-5

MLA_HEADS = 8
MLA_Q_RANK = 512
MLA_KV_RANK = 256
MLA_NOPE_DIM = 128
MLA_ROPE_DIM = 64
MLA_V_DIM = 128

IN_SPLITS = (DIFF_HEADS * 2 * DIFF_HEAD_DIM,
             DIFF_HEADS * 2 * DIFF_HEAD_DIM,
             DIFF_HEADS * DIFF_V_DIM,
             MLA_Q_RANK,
             MLA_KV_RANK,
             MLA_ROPE_DIM,
             D_MODEL,
             D_MODEL)
D_IN = sum(IN_SPLITS)

MEM_LEN = 256
CROSS_HEADS = 4
CROSS_HEAD_DIM = 128

N_GROUPS = 4
EXPERTS_PER_GROUP = 8
N_EXPERTS = N_GROUPS * EXPERTS_PER_GROUP
TOP_K_IN_GROUP = 2
D_EXPERT = 512
MOE_BLOCK = 128

kernel_name = 'hybrid_diffattn_mla_hiermoe_encoder'


def rmsnorm(x, g, eps=NORM_EPS):
    xf = x.astype(jnp.float32)
    y = xf * lax.rsqrt(jnp.mean(xf * xf, axis=-1, keepdims=True) + eps)
    return (y * g.astype(jnp.float32)).astype(x.dtype)


def rope(x, positions, rot_dim):
    half = rot_dim // 2
    inv_freq = jnp.float32(ROPE_THETA) ** (-jnp.arange(half, dtype=jnp.float32) * 2.0 / rot_dim)
    ang = positions.astype(jnp.float32)[:, :, None] * inv_freq
    cos = jnp.cos(ang)[:, :, None, :]
    sin = jnp.sin(ang)[:, :, None, :]
    xr = x[..., :rot_dim].astype(jnp.float32)
    x1, x2 = xr[..., :half], xr[..., half:]
    rot = jnp.concatenate([x1 * cos - x2 * sin, x2 * cos + x1 * sin], axis=-1)
    return jnp.concatenate([rot.astype(x.dtype), x[..., rot_dim:]], axis=-1)


def sweep_query_blocks(fn, *q_arrays):
    b, s = q_arrays[0].shape[:2]
    nb = s // Q_BLOCK
    blocks = tuple(jnp.moveaxis(q.reshape(b, nb, Q_BLOCK, *q.shape[2:]), 1, 0) for q in q_arrays)
    out = lax.map(lambda blk: fn(*blk), blocks)
    out = jnp.moveaxis(out, 0, 1)
    return out.reshape(b, s, *out.shape[3:])


def differential_attention(q, k, v, lam, subln_g, lam_init):
    scale = DIFF_HEAD_DIM ** -0.5
    k1, k2 = k[..., 0, :], k[..., 1, :]

    def block(qb):
        s1 = jnp.einsum('bqhd,bkhd->bhqk', qb[..., 0, :], k1).astype(jnp.float32) * scale
        s2 = jnp.einsum('bqhd,bkhd->bhqk', qb[..., 1, :], k2).astype(jnp.float32) * scale
        a = jax.nn.softmax(s1, axis=-1) - lam * jax.nn.softmax(s2, axis=-1)
        return jnp.einsum('bhqk,bkhe->bqhe', a.astype(v.dtype), v)

    o = sweep_query_blocks(block, q)
    return rmsnorm(o, subln_g, DIFF_SUBLN_EPS) * (1.0 - lam_init)


def latent_attention(c_q, c_kv, k_pe_raw, positions, q_norm_g, w_uq, kv_norm_g, w_ukv):
    b, s = c_q.shape[:2]
    q = (rmsnorm(c_q, q_norm_g) @ w_uq).reshape(b, s, MLA_HEADS, MLA_NOPE_DIM + MLA_ROPE_DIM)
    q_nope = q[..., :MLA_NOPE_DIM]
    q_pe = rope(q[..., MLA_NOPE_DIM:], positions, MLA_ROPE_DIM)
    kv = (rmsnorm(c_kv, kv_norm_g) @ w_ukv).reshape(b, s, MLA_HEADS, MLA_NOPE_DIM + MLA_V_DIM)
    k_nope, v = kv[..., :MLA_NOPE_DIM], kv[..., MLA_NOPE_DIM:]
    k_pe = rope(k_pe_raw[:, :, None, :], positions, MLA_ROPE_DIM)[:, :, 0, :]
    scale = (MLA_NOPE_DIM + MLA_ROPE_DIM) ** -0.5

    def block(qn, qp):
        sc = (jnp.einsum('bqhd,bkhd->bhqk', qn, k_nope)
              + jnp.einsum('bqhr,bkr->bhqk', qp, k_pe)).astype(jnp.float32) * scale
        p = jax.nn.softmax(sc, axis=-1)
        return jnp.einsum('bhqk,bkhe->bqhe', p.astype(v.dtype), v)

    return sweep_query_blocks(block, q_nope, q_pe)


def memory_cross_attention(hn, mem, mem_norm_g, w_cq, w_ckv, w_co):
    b, s, _ = hn.shape
    m = mem.shape[1]
    q = (hn @ w_cq).reshape(b, s, CROSS_HEADS, CROSS_HEAD_DIM)
    kv = (rmsnorm(mem, mem_norm_g) @ w_ckv).reshape(b, m, 2, CROSS_HEADS, CROSS_HEAD_DIM)
    k, v = kv[:, :, 0], kv[:, :, 1]
    sc = jnp.einsum('bqhd,bmhd->bhqm', q, k).astype(jnp.float32) * CROSS_HEAD_DIM ** -0.5
    p = jax.nn.softmax(sc, axis=-1)
    o = jnp.einsum('bhqm,bmhd->bqhd', p.astype(v.dtype), v).reshape(b, s, CROSS_HEADS * CROSS_HEAD_DIM)
    return o @ w_co


def hierarchical_moe(t, w_rg, b_rg, w_re, b_re, w_gate, w_up, w_down):
    n, d = t.shape
    tf = t.astype(jnp.float32)
    g_prob = jax.nn.softmax(tf @ w_rg.astype(jnp.float32) + b_rg.astype(jnp.float32), axis=-1)
    g_p, g_idx = lax.top_k(g_prob, 1)
    e_logits = (tf @ w_re.astype(jnp.float32) + b_re.astype(jnp.float32)).reshape(n, N_GROUPS, EXPERTS_PER_GROUP)
    e_logits = jnp.take_along_axis(e_logits, g_idx[:, :, None], axis=1)[:, 0]
    e_p, e_local = lax.top_k(jax.nn.softmax(e_logits, axis=-1), TOP_K_IN_GROUP)
    e_p = e_p / jnp.sum(e_p, axis=-1, keepdims=True)
    weights = (g_p * e_p).reshape(-1)
    expert_ids = (g_idx * EXPERTS_PER_GROUP + e_local).reshape(-1).astype(jnp.int32)
    tok_ids = jnp.repeat(jnp.arange(n, dtype=jnp.int32), TOP_K_IN_GROUP)
    a = n * TOP_K_IN_GROUP

    order = jnp.argsort(expert_ids)
    e_sorted, tok_sorted, w_sorted = expert_ids[order], tok_ids[order], weights[order]
    counts = jnp.zeros((N_EXPERTS,), jnp.int32).at[expert_ids].add(1)
    offsets = jnp.cumsum(counts) - counts
    padded = ((counts + MOE_BLOCK - 1) // MOE_BLOCK) * MOE_BLOCK
    padded_end = jnp.cumsum(padded)
    padded_off = padded_end - padded
    dest = padded_off[e_sorted] + (jnp.arange(a, dtype=jnp.int32) - offsets[e_sorted])
    p_rows = ((a + MOE_BLOCK - 1) // MOE_BLOCK) * MOE_BLOCK + N_EXPERTS * MOE_BLOCK
    n_blocks = p_rows // MOE_BLOCK
    row_tok = jnp.full((p_rows,), n, jnp.int32).at[dest].set(tok_sorted)
    row_w = jnp.zeros((p_rows,), jnp.float32).at[dest].set(w_sorted)
    block_start = jnp.arange(n_blocks, dtype=jnp.int32) * MOE_BLOCK
    block_expert = jnp.clip(jnp.searchsorted(padded_end, block_start, side='right'), 0, N_EXPERTS - 1)

    xs = jnp.concatenate([t, jnp.zeros((1, d), t.dtype)], axis=0)
    xb = xs[row_tok].reshape(n_blocks, MOE_BLOCK, d)

    def expert_block(args):
        xblk, e = args
        hid = jax.nn.silu(xblk @ w_gate[e]) * (xblk @ w_up[e])
        return hid @ w_down[e]

    y = lax.map(expert_block, (xb, block_expert)).reshape(p_rows, d)
    out = jax.ops.segment_sum(y.astype(jnp.float32) * row_w[:, None], row_tok, num_segments=n + 1)[:n]
    return out.astype(t.dtype)


def setup_inputs(seed: int = 0) -> dict:
    key = jax.random.key(seed)
    ks = iter(jax.random.split(key, 40))
    L = DEPTH

    def w(shape, fan_in):
        return jax.random.normal(next(ks), shape, jnp.float32) * fan_in ** -0.5

    def gain(shape):
        return 1.0 + 0.01 * jax.random.normal(next(ks), shape, jnp.float32)

    def small(shape, s):
        return s * jax.random.normal(next(ks), shape, jnp.float32)

    return {
        'x': jax.random.normal(next(ks), (BATCH, SEQ, D_MODEL), jnp.float32),
        'mem': jax.random.normal(next(ks), (BATCH, MEM_LEN, D_MODEL), jnp.float32),
        'positions': jnp.broadcast_to(jnp.arange(SEQ, dtype=jnp.int32)[None, :], (BATCH, SEQ)),
        'attn_norm_g': gain((L, D_MODEL)),
        'w_in': w((L, D_MODEL, D_IN), D_MODEL),
        'diff_lambda_q1': small((L, DIFF_HEAD_DIM), 0.1),
        'diff_lambda_k1': small((L, DIFF_HEAD_DIM), 0.1),
        'diff_lambda_q2': small((L, DIFF_HEAD_DIM), 0.1),
        'diff_lambda_k2': small((L, DIFF_HEAD_DIM), 0.1),
        'diff_subln_g': gain((L, DIFF_V_DIM)),
        'w_o_diff': w((L, DIFF_HEADS * DIFF_V_DIM, D_MODEL), DIFF_HEADS * DIFF_V_DIM),
        'mla_q_norm_g': gain((L, MLA_Q_RANK)),
        'w_uq': w((L, MLA_Q_RANK, MLA_HEADS * (MLA_NOPE_DIM + MLA_ROPE_DIM)), MLA_Q_RANK),
        'mla_kv_norm_g': gain((L, MLA_KV_RANK)),
        'w_ukv': w((L, MLA_KV_RANK, MLA_HEADS * (MLA_NOPE_DIM + MLA_V_DIM)), MLA_KV_RANK),
        'w_o_mla': w((L, MLA_HEADS * MLA_V_DIM, D_MODEL), MLA_HEADS * MLA_V_DIM),
        'w_out': w((L, D_MODEL, D_MODEL), D_MODEL),
        'cross_norm_g': gain((L, D_MODEL)),
        'mem_norm_g': gain((L, D_MODEL)),
        'w_cq': w((L, D_MODEL, CROSS_HEADS * CROSS_HEAD_DIM), D_MODEL),
        'w_ckv': w((L, D_MODEL, 2 * CROSS_HEADS * CROSS_HEAD_DIM), D_MODEL),
        'w_co': w((L, CROSS_HEADS * CROSS_HEAD_DIM, D_MODEL), CROSS_HEADS * CROSS_HEAD_DIM),
        'ffn_norm_g': gain((L, D_MODEL)),
        'w_router_group': w((L, D_MODEL, N_GROUPS), D_MODEL),
        'b_router_group': small((L, N_GROUPS), 0.01),
        'w_router_expert': w((L, D_MODEL, N_EXPERTS), D_MODEL),
        'b_router_expert': small((L, N_EXPERTS), 0.01),
        'w_expert_gate': w((L, N_EXPERTS, D_MODEL, D_EXPERT), D_MODEL),
        'w_expert_up': w((L, N_EXPERTS, D_MODEL, D_EXPERT), D_MODEL),
        'w_expert_down': w((L, N_EXPERTS, D_EXPERT, D_MODEL), D_EXPERT),
        'final_norm_g': gain((D_MODEL,)),
    }


def reference(x, mem, positions, attn_norm_g, w_in,
              diff_lambda_q1, diff_lambda_k1, diff_lambda_q2, diff_lambda_k2, diff_subln_g, w_o_diff,
              mla_q_norm_g, w_uq, mla_kv_norm_g, w_ukv, w_o_mla, w_out,
              cross_norm_g, mem_norm_g, w_cq, w_ckv, w_co,
              ffn_norm_g, w_router_group, b_router_group, w_router_expert, b_router_expert,
              w_expert_gate, w_expert_up, w_expert_down, final_norm_g):
    b, s, d = x.shape
    split_points = [int(i) for i in np.cumsum(IN_SPLITS)[:-1]]
    h = x
    for l in range(DEPTH):
        lam_init = 0.8 - 0.6 * math.exp(-0.3 * l)

        xn = rmsnorm(h, attn_norm_g[l])
        proj = xn @ w_in[l]
        dq, dk, dv, c_q, c_kv, k_pe, g_a, g_b = jnp.split(proj, split_points, axis=-1)

        dq = rope(dq.reshape(b, s, 2 * DIFF_HEADS, DIFF_HEAD_DIM), positions, DIFF_ROT)
        dk = rope(dk.reshape(b, s, 2 * DIFF_HEADS, DIFF_HEAD_DIM), positions, DIFF_ROT)
        dq = dq.reshape(b, s, DIFF_HEADS, 2, DIFF_HEAD_DIM)
        dk = dk.reshape(b, s, DIFF_HEADS, 2, DIFF_HEAD_DIM)
        dv = dv.reshape(b, s, DIFF_HEADS, DIFF_V_DIM)
        lam = (jnp.exp(jnp.sum(diff_lambda_q1[l].astype(jnp.float32) * diff_lambda_k1[l].astype(jnp.float32)))
               - jnp.exp(jnp.sum(diff_lambda_q2[l].astype(jnp.float32) * diff_lambda_k2[l].astype(jnp.float32)))
               + lam_init)
        o_a = differential_attention(dq, dk, dv, lam, diff_subln_g[l], lam_init)
        y_a = o_a.reshape(b, s, DIFF_HEADS * DIFF_V_DIM) @ w_o_diff[l]

        o_b = latent_attention(c_q, c_kv, k_pe, positions, mla_q_norm_g[l], w_uq[l], mla_kv_norm_g[l], w_ukv[l])
        y_b = o_b.reshape(b, s, MLA_HEADS * MLA_V_DIM) @ w_o_mla[l]

        merged = jax.nn.sigmoid(g_a) * y_a + jax.nn.sigmoid(g_b) * y_b
        h = h + merged @ w_out[l]

        h = h + memory_cross_attention(rmsnorm(h, cross_norm_g[l]), mem, mem_norm_g[l], w_cq[l], w_ckv[l], w_co[l])

        t = rmsnorm(h, ffn_norm_g[l]).reshape(b * s, d)
        h = h + hierarchical_moe(t, w_router_group[l], b_router_group[l], w_router_expert[l], b_router_expert[l],
                                 w_expert_gate[l], w_expert_up[l], w_expert_down[l]).reshape(b, s, d)
    return rmsnorm(h, final_norm_g)
```

```python
import functools
import math

import jax
import jax.numpy as jnp
from jax import lax
from jax.experimental import pallas as pl
from jax.experimental.pallas import tpu as pltpu

D_MODEL = 2048
ROPE_THETA = 500000.0
NORM_EPS = 1e-6

DIFF_HEADS = 8
DIFF_HEAD_DIM = 64
DIFF_V_DIM = 2 * DIFF_HEAD_DIM
DIFF_ROT = DIFF_HEAD_DIM // 4
DIFF_SUBLN_EPS = 1e-5
DIFF_LAMBDA_INIT = 0.8 - 0.6 * math.exp(-0.3 * 0)

MLA_HEADS = 8
MLA_Q_RANK = 512
MLA_KV_RANK = 256
MLA_NOPE_DIM = 128
MLA_ROPE_DIM = 64
MLA_V_DIM = 128
MLA_QK_DIM = MLA_NOPE_DIM + MLA_ROPE_DIM

CROSS_HEADS = 4
CROSS_HEAD_DIM = 128

N_GROUPS = 4
EXPERTS_PER_GROUP = 8
N_EXPERTS = N_GROUPS * EXPERTS_PER_GROUP
D_EXPERT = 512

LANES = 128
LOG2E = 1.4426950408889634
VMEM_LIMIT_BYTES = 56 * 1024 * 1024

ROUTER_EXPERT_LANE0 = N_GROUPS

QKV_COLS = 3 * DIFF_HEADS * DIFF_V_DIM
LATENT_COLS = 1024
GATE_COLS = 2 * D_MODEL
KPE_COL0 = MLA_Q_RANK + MLA_KV_RANK

MOE_ROWS_PER_BLOCK = 256


def _params(*semantics):
    return pltpu.CompilerParams(dimension_semantics=semantics, vmem_limit_bytes=VMEM_LIMIT_BYTES)


def _resident(shape):
    zeros = (0,) * len(shape)
    return pl.BlockSpec(shape, lambda *_: zeros, pipeline_mode=pl.Buffered(1))


def _rms_scale(xf, eps):
    return lax.rsqrt(jnp.mean(xf * xf, axis=-1, keepdims=True) + eps)


def _lane_iota(shape):
    return lax.broadcasted_iota(jnp.int32, shape, len(shape) - 1)


def _trig_kernel(pos_ref, invf_ref, cos_ref, sin_ref):
    ang = pos_ref[...].astype(jnp.float32) * invf_ref[...]
    cos_ref[...] = jnp.cos(ang)
    sin_ref[...] = jnp.sin(ang)


def _rope_tables(positions, n_tok, tm):
    half_m = MLA_ROPE_DIM // 2
    half_d = DIFF_ROT // 2
    inv_m = jnp.float32(ROPE_THETA) ** (-jnp.arange(half_m, dtype=jnp.float32) * 2.0 / MLA_ROPE_DIM)
    inv_d = jnp.float32(ROPE_THETA) ** (-jnp.arange(half_d, dtype=jnp.float32) * 2.0 / DIFF_ROT)
    invf = jnp.concatenate([inv_m, inv_m, inv_d, inv_d,
                            jnp.zeros((DIFF_HEAD_DIM - DIFF_ROT,), jnp.float32)]).reshape(1, LANES)
    pos = positions.reshape(n_tok, 1)
    return pl.pallas_call(
        _trig_kernel,
        grid=(n_tok // tm,),
        in_specs=[pl.BlockSpec((tm, 1), lambda i: (i, 0)), _resident((1, LANES))],
        out_specs=[pl.BlockSpec((tm, LANES), lambda i: (i, 0))] * 2,
        out_shape=[jax.ShapeDtypeStruct((n_tok, LANES), jnp.float32)] * 2,
        compiler_params=_params("parallel"),
        name="rope_tables",
    )(pos, invf)


def _diff_rope_coeffs(cos_t, sin_t):
    lane = _lane_iota(cos_t.shape)
    upper = lane >= DIFF_HEAD_DIM
    cos_d = jnp.where(upper, cos_t, pltpu.roll(cos_t, DIFF_HEAD_DIM, 1))
    sin_d = jnp.where(upper, sin_t, pltpu.roll(sin_t, DIFF_HEAD_DIM, 1))
    in_head = lane % DIFF_HEAD_DIM
    half = DIFF_ROT // 2
    s_next = jnp.where(in_head < half, -sin_d, 0.0)
    s_prev = jnp.where((in_head >= half) & (in_head < DIFF_ROT), sin_d, 0.0)
    return cos_d, s_next, s_prev


def _mla_rope(pair, cos_t, sin_t):
    lane = _lane_iota(pair.shape)
    sin_signed = jnp.where(lane < MLA_ROPE_DIM // 2, -sin_t, sin_t)
    return pair * cos_t + pltpu.roll(pair, MLA_ROPE_DIM, 1) * sin_signed


def _inproj_kernel(*refs, mode, tn):
    if mode == "qkv":
        x_ref, g_ref, w_ref, cos_ref, sin_ref, o_ref, xn_ref = refs
    else:
        x_ref, g_ref, w_ref, o_ref, xn_ref = refs
    j = pl.program_id(1)

    @pl.when(j == 0)
    def _():
        xf = x_ref[...]
        xn_ref[...] = (xf * _rms_scale(xf, NORM_EPS) * g_ref[...]).astype(jnp.bfloat16)

    acc = jnp.dot(xn_ref[...], w_ref[...], preferred_element_type=jnp.float32)

    if mode == "latent":
        o_ref[...] = acc
    elif mode == "gates":
        o_ref[...] = jax.nn.sigmoid(acc).astype(o_ref.dtype)
    else:
        q_tiles = DIFF_HEADS * DIFF_V_DIM // tn

        @pl.when(j < 2 * q_tiles)
        def _():
            cos_d, s_next, s_prev = _diff_rope_coeffs(cos_ref[...], sin_ref[...])
            qscale = jnp.where(j < q_tiles, DIFF_HEAD_DIM ** -0.5 * LOG2E, 1.0).astype(jnp.float32)
            for c in range(tn // LANES):
                xc = acc[:, c * LANES:(c + 1) * LANES]
                rot = (xc * cos_d + pltpu.roll(xc, LANES - DIFF_ROT // 2, 1) * s_next
                       + pltpu.roll(xc, DIFF_ROT // 2, 1) * s_prev)
                o_ref[:, c * LANES:(c + 1) * LANES] = (rot * qscale).astype(o_ref.dtype)

        @pl.when(j >= 2 * q_tiles)
        def _():
            o_ref[...] = acc.astype(o_ref.dtype)


def _inproj(x2, g, w, mode, out_dtype, tm, tn, tables=None):
    n_tok, d = x2.shape
    ncols = w.shape[1]
    in_specs = [pl.BlockSpec((tm, d), lambda i, j: (i, 0)),
                _resident((1, d)),
                pl.BlockSpec((d, tn), lambda i, j: (0, j))]
    args = [x2, g, w]
    if mode == "qkv":
        in_specs += [pl.BlockSpec((tm, LANES), lambda i, j: (i, 0))] * 2
        args += list(tables)
    return pl.pallas_call(
        functools.partial(_inproj_kernel, mode=mode, tn=tn),
        grid=(n_tok // tm, ncols // tn),
        in_specs=in_specs,
        out_specs=pl.BlockSpec((tm, tn), lambda i, j: (i, j)),
        out_shape=jax.ShapeDtypeStruct((n_tok, ncols), out_dtype),
        scratch_shapes=[pltpu.VMEM((tm, d), jnp.bfloat16)],
        compiler_params=_params("parallel", "arbitrary"),
        name="inproj_" + mode,
    )(*args)


def _mla_proj_kernel(c_ref, gq_ref, gkv_ref, wuq_ref, wuk_ref, wuv_ref, cos_ref, sin_ref,
                     q_ref, k_ref, v_ref):
    cos_t = cos_ref[...]
    sin_t = sin_ref[...]
    cq = c_ref[:, :MLA_Q_RANK]
    cqn = (cq * _rms_scale(cq, NORM_EPS) * gq_ref[...]).astype(jnp.bfloat16)
    ckv = c_ref[:, MLA_Q_RANK:KPE_COL0]
    ckvn = (ckv * _rms_scale(ckv, NORM_EPS) * gkv_ref[...]).astype(jnp.bfloat16)
    kpe = _mla_rope(c_ref[:, KPE_COL0:KPE_COL0 + LANES], cos_t, sin_t)[:, :MLA_ROPE_DIM].astype(k_ref.dtype)
    qscale = MLA_QK_DIM ** -0.5 * LOG2E
    for h in range(MLA_HEADS):
        r = jnp.dot(cqn, wuq_ref[h], preferred_element_type=jnp.float32)
        q_ref[0, h, :, :MLA_NOPE_DIM] = (r[:, :MLA_NOPE_DIM] * qscale).astype(q_ref.dtype)
        qpe = _mla_rope(r[:, MLA_NOPE_DIM:], cos_t, sin_t)[:, :MLA_ROPE_DIM]
        q_ref[0, h, :, MLA_NOPE_DIM:] = (qpe * qscale).astype(q_ref.dtype)
        kn = jnp.dot(ckvn, wuk_ref[h], preferred_element_type=jnp.float32)
        k_ref[0, h, :, :MLA_NOPE_DIM] = kn.astype(k_ref.dtype)
        k_ref[0, h, :, MLA_NOPE_DIM:] = kpe
    v_ref[...] = jnp.dot(ckvn, wuv_ref[...], preferred_element_type=jnp.float32).astype(v_ref.dtype)


def _mla_proj(latent, gq, gkv, wuq, wuk, wuv, cos_t, sin_t, batch, seq, tm):
    n_tok = latent.shape[0]
    per_b = seq // tm
    head_spec = pl.BlockSpec((1, MLA_HEADS, tm, MLA_QK_DIM), lambda i: (i // per_b, 0, i % per_b, 0))
    head_shape = jax.ShapeDtypeStruct((batch, MLA_HEADS, seq, MLA_QK_DIM), jnp.bfloat16)
    return pl.pallas_call(
        _mla_proj_kernel,
        grid=(n_tok // tm,),
        in_specs=[pl.BlockSpec((tm, LATENT_COLS), lambda i: (i, 0)),
                  _resident(gq.shape), _resident(gkv.shape),
                  _resident(wuq.shape), _resident(wuk.shape), _resident(wuv.shape),
                  pl.BlockSpec((tm, LANES), lambda i: (i, 0)),
                  pl.BlockSpec((tm, LANES), lambda i: (i, 0))],
        out_specs=[head_spec, head_spec,
                   pl.BlockSpec((tm, MLA_HEADS * MLA_V_DIM), lambda i: (i, 0))],
        out_shape=[head_shape, head_shape,
                   jax.ShapeDtypeStruct((n_tok, MLA_HEADS * MLA_V_DIM), jnp.bfloat16)],
        compiler_params=_params("parallel"),
        name="mla_proj",
    )(latent, gq, gkv, wuq, wuk, wuv, cos_t, sin_t)


def _softmax_pv(s, v):
    m = jnp.max(s, axis=-1, keepdims=True)
    p = jnp.exp2(s - m)
    l = jnp.sum(p, axis=-1, keepdims=True)
    pv = jnp.dot(p.astype(v.dtype), v, preferred_element_type=jnp.float32)
    return pv, l


def _diff_attn_kernel(q_ref, k_ref, v_ref, lq1_ref, lk1_ref, lq2_ref, lk2_ref, g_ref, o_ref):
    q = q_ref[0]
    k = k_ref[0]
    v = v_ref[0]
    tq = q.shape[0]
    lane = _lane_iota(q.shape)
    zero = jnp.zeros_like(q)
    q12 = jnp.concatenate([jnp.where(lane < DIFF_HEAD_DIM, q, zero),
                           jnp.where(lane >= DIFF_HEAD_DIM, q, zero)], axis=0)
    s = lax.dot_general(q12, k, (((1,), (1,)), ((), ())), preferred_element_type=jnp.float32)
    pv, l = _softmax_pv(s, v)
    lam = (jnp.exp(jnp.sum(lq1_ref[...] * lk1_ref[...], axis=-1, keepdims=True))
           - jnp.exp(jnp.sum(lq2_ref[...] * lk2_ref[...], axis=-1, keepdims=True))
           + DIFF_LAMBDA_INIT)
    o = pv[:tq] / l[:tq] - lam * (pv[tq:] / l[tq:])
    o = o * _rms_scale(o, DIFF_SUBLN_EPS) * g_ref[...] * (1.0 - DIFF_LAMBDA_INIT)
    o_ref[0] = o.astype(o_ref.dtype)


def _diff_attn(qkv3, lq1, lk1, lq2, lk2, subln_g, tq):
    batch, seq, _ = qkv3.shape
    h = DIFF_HEADS
    return pl.pallas_call(
        _diff_attn_kernel,
        grid=(batch, h, seq // tq),
        in_specs=[pl.BlockSpec((1, tq, LANES), lambda b, hh, i: (b, i, hh)),
                  pl.BlockSpec((1, seq, LANES), lambda b, hh, i: (b, 0, h + hh)),
                  pl.BlockSpec((1, seq, LANES), lambda b, hh, i: (b, 0, 2 * h + hh)),
                  _resident(lq1.shape), _resident(lk1.shape), _resident(lq2.shape), _resident(lk2.shape),
                  _resident(subln_g.shape)],
        out_specs=pl.BlockSpec((1, tq, DIFF_V_DIM), lambda b, hh, i: (b, i, hh)),
        out_shape=jax.ShapeDtypeStruct((batch, seq, h * DIFF_V_DIM), jnp.bfloat16),
        compiler_params=_params("parallel", "parallel", "arbitrary"),
        name="diff_attn",
    )(qkv3, qkv3, qkv3, lq1, lk1, lq2, lk2, subln_g)


def _mla_attn_kernel(q_ref, k_ref, v_ref, o_ref):
    s = lax.dot_general(q_ref[0, 0], k_ref[0, 0], (((1,), (1,)), ((), ())),
                        preferred_element_type=jnp.float32)
    pv, l = _softmax_pv(s, v_ref[0])
    o_ref[0] = (pv / l).astype(o_ref.dtype)


def _mla_attn(q_cat, k_cat, v3, tq):
    batch, heads, seq, dqk = q_cat.shape
    return pl.pallas_call(
        _mla_attn_kernel,
        grid=(batch, heads, seq // tq),
        in_specs=[pl.BlockSpec((1, 1, tq, dqk), lambda b, h, i: (b, h, i, 0)),
                  pl.BlockSpec((1, 1, seq, dqk), lambda b, h, i: (b, h, 0, 0)),
                  pl.BlockSpec((1, seq, MLA_V_DIM), lambda b, h, i: (b, 0, h))],
        out_specs=pl.BlockSpec((1, tq, MLA_V_DIM), lambda b, h, i: (b, i, h)),
        out_shape=jax.ShapeDtypeStruct((batch, seq, heads * MLA_V_DIM), jnp.bfloat16),
        compiler_params=_params("parallel", "parallel", "arbitrary"),
        name="mla_attn",
    )(q_cat, k_cat, v3)


def _merge_out_kernel(oa_ref, ob_ref, sga_ref, sgb_ref, x_ref, woa_ref, wob_ref, wout_ref, h_ref):
    ya = jnp.dot(oa_ref[...], woa_ref[...], preferred_element_type=jnp.float32)
    yb = jnp.dot(ob_ref[...], wob_ref[...], preferred_element_type=jnp.float32)
    merged = sga_ref[...].astype(jnp.float32) * ya + sgb_ref[...].astype(jnp.float32) * yb
    h_ref[...] = x_ref[...] + jnp.dot(merged.astype(jnp.bfloat16), wout_ref[...],
                                       preferred_element_type=jnp.float32)


def _merge_out(o_a, o_b, gates, x2, w_oa, w_ob, w_out, tm):
    n_tok, d = x2.shape
    return pl.pallas_call(
        _merge_out_kernel,
        grid=(n_tok // tm,),
        in_specs=[pl.BlockSpec((tm, o_a.shape[1]), lambda i: (i, 0)),
                  pl.BlockSpec((tm, o_b.shape[1]), lambda i: (i, 0)),
                  pl.BlockSpec((tm, d), lambda i: (i, 0)),
                  pl.BlockSpec((tm, d), lambda i: (i, 1)),
                  pl.BlockSpec((tm, d), lambda i: (i, 0)),
                  _resident(w_oa.shape), _resident(w_ob.shape), _resident(w_out.shape)],
        out_specs=pl.BlockSpec((tm, d), lambda i: (i, 0)),
        out_shape=jax.ShapeDtypeStruct((n_tok, d), jnp.float32),
        compiler_params=_params("parallel"),
        name="merge_out",
    )(o_a, o_b, gates, gates, x2, w_oa, w_ob, w_out)


def _mem_kv_kernel(mem_ref, g_ref, w_ref, kv_ref):
    mf = mem_ref[0]
    mn = (mf * _rms_scale(mf, NORM_EPS) * g_ref[...]).astype(jnp.bfloat16)
    kv_ref[0] = jnp.dot(mn, w_ref[...], preferred_element_type=jnp.float32).astype(kv_ref.dtype)


def _mem_kv(mem, g, w_ckv):
    batch, m, d = mem.shape
    return pl.pallas_call(
        _mem_kv_kernel,
        grid=(batch,),
        in_specs=[pl.BlockSpec((1, m, d), lambda b: (b, 0, 0)), _resident(g.shape), _resident(w_ckv.shape)],
        out_specs=pl.BlockSpec((1, m, w_ckv.shape[1]), lambda b: (b, 0, 0)),
        out_shape=jax.ShapeDtypeStruct((batch, m, w_ckv.shape[1]), jnp.bfloat16),
        compiler_params=_params("parallel"),
        name="mem_kv",
    )(mem, g, w_ckv)


def _cross_router_kernel(h_ref, gc_ref, wcq_ref, kv_ref, wco_ref, gf_ref, wr_ref, br_ref,
                         h2_ref, eid_ref, rank_ref, wts_ref, cnt_ref, carry_ref):
    i = pl.program_id(0)

    @pl.when(i == 0)
    def _():
        carry_ref[...] = jnp.zeros_like(carry_ref)

    h1 = h_ref[...]
    tm = h1.shape[0]
    hn = (h1 * _rms_scale(h1, NORM_EPS) * gc_ref[...]).astype(jnp.bfloat16)
    q = jnp.dot(hn, wcq_ref[...], preferred_element_type=jnp.float32) * (CROSS_HEAD_DIM ** -0.5 * LOG2E)
    q = q.astype(jnp.bfloat16)
    kv_cols = CROSS_HEADS * CROSS_HEAD_DIM
    heads = []
    for hd in range(CROSS_HEADS):
        lo = hd * CROSS_HEAD_DIM
        kh = kv_ref[0, :, lo:lo + CROSS_HEAD_DIM]
        vh = kv_ref[0, :, kv_cols + lo:kv_cols + lo + CROSS_HEAD_DIM]
        s = lax.dot_general(q[:, lo:lo + CROSS_HEAD_DIM], kh, (((1,), (1,)), ((), ())),
                            preferred_element_type=jnp.float32)
        pv, l = _softmax_pv(s, vh)
        heads.append((pv / l).astype(jnp.bfloat16))
    o = jnp.concatenate(heads, axis=-1)
    h2 = h1 + jnp.dot(o, wco_ref[...], preferred_element_type=jnp.float32)
    h2_ref[...] = h2

    t = h2 * _rms_scale(h2, NORM_EPS) * gf_ref[...]
    logits = jnp.dot(t, wr_ref[...], preferred_element_type=jnp.float32,
                     precision=lax.Precision.HIGHEST) + br_ref[...]
    lane = _lane_iota(logits.shape)
    neg = jnp.float32(-jnp.inf)
    big = jnp.int32(2 * LANES)
    is_group = lane < N_GROUPS
    lg = jnp.where(is_group, logits, neg)
    mg = jnp.max(lg, axis=-1, keepdims=True)
    g_idx = jnp.min(jnp.where(is_group & (logits == mg), lane, big), axis=-1, keepdims=True)
    g_p = 1.0 / jnp.sum(jnp.exp(lg - mg), axis=-1, keepdims=True)
    lo_lane = ROUTER_EXPERT_LANE0 + EXPERTS_PER_GROUP * g_idx
    in_grp = (lane >= lo_lane) & (lane < lo_lane + EXPERTS_PER_GROUP)
    l1 = jnp.max(jnp.where(in_grp, logits, neg), axis=-1, keepdims=True)
    i1 = jnp.min(jnp.where(in_grp & (logits == l1), lane, big), axis=-1, keepdims=True)
    rest = in_grp & (lane != i1)
    l2 = jnp.max(jnp.where(rest, logits, neg), axis=-1, keepdims=True)
    i2 = jnp.min(jnp.where(rest & (logits == l2), lane, big), axis=-1, keepdims=True)
    d = jnp.exp(l2 - l1)
    w1 = g_p / (1.0 + d)
    w2 = w1 * d

    oh1 = lane == i1
    oh2 = lane == i2
    cnt = (oh1 | oh2).astype(jnp.bfloat16)
    row = lax.broadcasted_iota(jnp.int32, (tm, tm), 0)
    col = lax.broadcasted_iota(jnp.int32, (tm, tm), 1)
    before = (col < row).astype(jnp.bfloat16)
    slot = jnp.dot(before, cnt, preferred_element_type=jnp.float32) + carry_ref[...]
    r1 = jnp.sum(jnp.where(oh1, slot, 0.0), axis=-1, keepdims=True)
    r2 = jnp.sum(jnp.where(oh2, slot, 0.0), axis=-1, keepdims=True)
    carry_ref[...] += jnp.sum(cnt.astype(jnp.float32), axis=0, keepdims=True)
    cnt_ref[...] = carry_ref[...]

    pair = _lane_iota((tm, 2))
    eid_ref[...] = jnp.where(pair == 0, i1, i2) - ROUTER_EXPERT_LANE0
    rank_ref[...] = jnp.where(pair == 0, r1, r2).astype(jnp.int32)
    wts_ref[...] = jnp.where(pair == 0, w1, w2)


def _cross_router(h1, gc, w_cq, kv_mem, w_co, gf, w_r, b_r, seq, tm):
    n_tok, d = h1.shape
    per_b = seq // tm
    row2 = pl.BlockSpec((tm, 2), lambda i: (i, 0))
    return pl.pallas_call(
        _cross_router_kernel,
        grid=(n_tok // tm,),
        in_specs=[pl.BlockSpec((tm, d), lambda i: (i, 0)),
                  _resident(gc.shape), _resident(w_cq.shape),
                  pl.BlockSpec((1,) + kv_mem.shape[1:], lambda i: (i // per_b, 0, 0)),
                  _resident(w_co.shape), _resident(gf.shape), _resident(w_r.shape), _resident(b_r.shape)],
        out_specs=[pl.BlockSpec((tm, d), lambda i: (i, 0)), row2, row2, row2,
                   pl.BlockSpec((1, LANES), lambda i: (0, 0))],
        out_shape=[jax.ShapeDtypeStruct((n_tok, d), jnp.float32),
                   jax.ShapeDtypeStruct((n_tok, 2), jnp.int32),
                   jax.ShapeDtypeStruct((n_tok, 2), jnp.int32),
                   jax.ShapeDtypeStruct((n_tok, 2), jnp.float32),
                   jax.ShapeDtypeStruct((1, LANES), jnp.float32)],
        scratch_shapes=[pltpu.VMEM((1, LANES), jnp.float32)],
        compiler_params=_params("arbitrary"),
        name="cross_router",
    )(h1, gc, w_cq, kv_mem, w_co, gf, w_r, b_r)


def _dispatch_kernel(dest_ref, h_ref, g_ref, xb_in_ref, xb_ref, t_ref, sem):
    del xb_in_ref
    h2 = h_ref[...]
    tm = h2.shape[0]
    t_ref[...] = h2 * _rms_scale(h2, NORM_EPS) * g_ref[...]

    def row_copy(r, k):
        return pltpu.make_async_copy(t_ref.at[pl.ds(r, 1)], xb_ref.at[pl.ds(dest_ref[0, 2 * r + k], 1)], sem)

    def issue(r, carry):
        row_copy(r, 0).start()
        row_copy(r, 1).start()
        return carry

    lax.fori_loop(0, tm, issue, 0)

    def drain(r, carry):
        row_copy(r, 0).wait()
        row_copy(r, 1).wait()
        return carry

    lax.fori_loop(0, tm, drain, 0)


def _dispatch(dest3, h2, gf, xb_init, tm):
    n_tok, d = h2.shape
    return pl.pallas_call(
        _dispatch_kernel,
        grid=(n_tok // tm,),
        in_specs=[pl.BlockSpec((None, 1, 2 * tm), lambda i: (i, 0, 0), memory_space=pltpu.SMEM),
                  pl.BlockSpec((tm, d), lambda i: (i, 0)),
                  _resident(gf.shape),
                  pl.BlockSpec(memory_space=pl.ANY)],
        out_specs=pl.BlockSpec(memory_space=pl.ANY),
        out_shape=jax.ShapeDtypeStruct(xb_init.shape, xb_init.dtype),
        scratch_shapes=[pltpu.VMEM((tm, d), jnp.float32), pltpu.SemaphoreType.DMA(())],
        input_output_aliases={3: 0},
        compiler_params=_params("arbitrary"),
        name="moe_dispatch",
    )(dest3, h2, gf, xb_init)


def _expert_kernel(be_ref, nact_ref, x_ref, wg_ref, wu_ref, wd_ref, y_ref, wg_b, wu_b, wd_b):
    i = pl.program_id(0)

    @pl.when(i < nact_ref[0])
    def _():
        prev = be_ref[jnp.maximum(i - 1, 0)]

        @pl.when((i == 0) | (be_ref[i] != prev))
        def _():
            wg_b[...] = wg_ref[0].astype(jnp.bfloat16)
            wu_b[...] = wu_ref[0].astype(jnp.bfloat16)
            wd_b[...] = wd_ref[0].astype(jnp.bfloat16)

        xb = x_ref[...].astype(jnp.bfloat16)
        gate = jnp.dot(xb, wg_b[...], preferred_element_type=jnp.float32)
        up = jnp.dot(xb, wu_b[...], preferred_element_type=jnp.float32)
        hid = (gate * jax.nn.sigmoid(gate) * up).astype(jnp.bfloat16)
        y_ref[...] = jnp.dot(hid, wd_b[...], preferred_element_type=jnp.float32)

    @pl.when(i >= nact_ref[0])
    def _():
        y_ref[...] = jnp.zeros_like(y_ref)


def _experts(block_expert, n_active, xb, w_gate, w_up, w_down, bm):
    p_rows, d = xb.shape
    de = w_gate.shape[-1]

    def row_map(i, be, nact):
        return (jnp.minimum(i, nact[0] - 1), 0)

    def w_map(i, be, nact):
        return (be[i], 0, 0)

    grid_spec = pltpu.PrefetchScalarGridSpec(
        num_scalar_prefetch=2,
        grid=(p_rows // bm,),
        in_specs=[pl.BlockSpec((bm, d), row_map),
                  pl.BlockSpec((1, d, de), w_map),
                  pl.BlockSpec((1, d, de), w_map),
                  pl.BlockSpec((1, de, d), w_map)],
        out_specs=pl.BlockSpec((bm, d), lambda i, be, nact: (i, 0)),
        scratch_shapes=[pltpu.VMEM((d, de), jnp.bfloat16), pltpu.VMEM((d, de), jnp.bfloat16),
                        pltpu.VMEM((de, d), jnp.bfloat16)],
    )
    return pl.pallas_call(
        _expert_kernel,
        grid_spec=grid_spec,
        out_shape=jax.ShapeDtypeStruct((p_rows, d), jnp.float32),
        compiler_params=_params("arbitrary"),
        name="moe_experts",
    )(block_expert, n_active, xb, w_gate, w_up, w_down)


def _combine_kernel(dest_ref, h_ref, wts_ref, g_ref, y_ref, o_ref, ybuf, sem):
    tm = h_ref.shape[0]

    def row_copy(r, k):
        return pltpu.make_async_copy(y_ref.at[pl.ds(dest_ref[0, 2 * r + k], 1)], ybuf.at[k, pl.ds(r, 1)], sem)

    def issue(r, carry):
        row_copy(r, 0).start()
        row_copy(r, 1).start()
        return carry

    lax.fori_loop(0, tm, issue, 0)

    def drain(r, carry):
        row_copy(r, 0).wait()
        row_copy(r, 1).wait()
        return carry

    lax.fori_loop(0, tm, drain, 0)

    w = wts_ref[...]
    h3 = h_ref[...] + w[:, 0:1] * ybuf[0] + w[:, 1:2] * ybuf[1]
    o_ref[...] = h3 * _rms_scale(h3, NORM_EPS) * g_ref[...]


def _combine(dest3, h2, wts, g_final, y, tm):
    n_tok, d = h2.shape
    return pl.pallas_call(
        _combine_kernel,
        grid=(n_tok // tm,),
        in_specs=[pl.BlockSpec((None, 1, 2 * tm), lambda i: (i, 0, 0), memory_space=pltpu.SMEM),
                  pl.BlockSpec((tm, d), lambda i: (i, 0)),
                  pl.BlockSpec((tm, 2), lambda i: (i, 0)),
                  _resident(g_final.shape),
                  pl.BlockSpec(memory_space=pl.ANY)],
        out_specs=pl.BlockSpec((tm, d), lambda i: (i, 0)),
        out_shape=jax.ShapeDtypeStruct((n_tok, d), jnp.float32),
        scratch_shapes=[pltpu.VMEM((2, tm, d), jnp.float32), pltpu.SemaphoreType.DMA(())],
        compiler_params=_params("arbitrary"),
        name="moe_combine",
    )(dest3, h2, wts, g_final, y)


def _split_w_in(w_in):
    bf = jnp.bfloat16
    c0 = QKV_COLS
    c_q0, c_kv0 = c0, c0 + MLA_Q_RANK
    kpe0 = c_kv0 + MLA_KV_RANK
    g0 = kpe0 + MLA_ROPE_DIM
    half = MLA_ROPE_DIM // 2
    kpe = w_in[:, kpe0:g0]
    kpe_swapped = jnp.concatenate([kpe[:, half:], kpe[:, :half]], axis=1)
    pad = jnp.zeros((w_in.shape[0], LATENT_COLS - (g0 - c0) - MLA_ROPE_DIM), w_in.dtype)
    w_lat = jnp.concatenate([w_in[:, c_q0:g0], kpe_swapped, pad], axis=1)
    return w_in[:, :c0].astype(bf), w_lat.astype(bf), w_in[:, g0:].astype(bf)


def _split_w_uq(w_uq):
    half = MLA_ROPE_DIM // 2
    w = w_uq.reshape(MLA_Q_RANK, MLA_HEADS, MLA_QK_DIM).transpose(1, 0, 2)
    pe = w[:, :, MLA_NOPE_DIM:]
    pe_swapped = jnp.concatenate([pe[:, :, half:], pe[:, :, :half]], axis=2)
    return jnp.concatenate([w, pe_swapped], axis=2).astype(jnp.bfloat16)


def _split_w_ukv(w_ukv):
    w = w_ukv.reshape(MLA_KV_RANK, MLA_HEADS, MLA_NOPE_DIM + MLA_V_DIM)
    wuk = w[:, :, :MLA_NOPE_DIM].transpose(1, 0, 2).astype(jnp.bfloat16)
    wuv = w[:, :, MLA_NOPE_DIM:].reshape(MLA_KV_RANK, MLA_HEADS * MLA_V_DIM).astype(jnp.bfloat16)
    return wuk, wuv


def kernel(x, mem, positions, attn_norm_g, w_in, diff_lambda_q1, diff_lambda_k1, diff_lambda_q2, diff_lambda_k2, diff_subln_g, w_o_diff, mla_q_norm_g, w_uq, mla_kv_norm_g, w_ukv, w_o_mla, w_out, cross_norm_g, mem_norm_g, w_cq, w_ckv, w_co, ffn_norm_g, w_router_group, b_router_group, w_router_expert, b_router_expert, w_expert_gate, w_expert_up, w_expert_down, final_norm_g):
    batch, seq, d = x.shape
    assert d == D_MODEL and w_in.shape[0] == 1, "single-layer kernel"
    n_tok = batch * seq
    bf = jnp.bfloat16
    x2 = x.reshape(n_tok, d)

    tm_proj = min(1024, seq)
    tm_row = min(256, seq)
    tq = min(256, seq)

    cos_t, sin_t = _rope_tables(positions, n_tok, tm_proj)

    w_qkv, w_lat, w_gates = _split_w_in(w_in[0])
    g_attn = attn_norm_g[0].reshape(1, d)
    qkv = _inproj(x2, g_attn, w_qkv, "qkv", bf, tm_proj, 512, (cos_t, sin_t))
    latent = _inproj(x2, g_attn, w_lat, "latent", jnp.float32, tm_proj, 512)
    gates = _inproj(x2, g_attn, w_gates, "gates", bf, tm_proj, 512)

    o_a = _diff_attn(qkv.reshape(batch, seq, QKV_COLS),
                     diff_lambda_q1[0].reshape(1, -1), diff_lambda_k1[0].reshape(1, -1),
                     diff_lambda_q2[0].reshape(1, -1), diff_lambda_k2[0].reshape(1, -1),
                     diff_subln_g[0].reshape(1, -1), tq)

    wuk, wuv = _split_w_ukv(w_ukv[0])
    q_cat, k_cat, v_mla = _mla_proj(latent, mla_q_norm_g[0].reshape(1, -1), mla_kv_norm_g[0].reshape(1, -1),
                                    _split_w_uq(w_uq[0]), wuk, wuv, cos_t, sin_t, batch, seq, min(512, seq))
    o_b = _mla_attn(q_cat, k_cat, v_mla.reshape(batch, seq, MLA_HEADS * MLA_V_DIM), tq)

    h1 = _merge_out(o_a.reshape(n_tok, -1), o_b.reshape(n_tok, -1), gates, x2,
                    w_o_diff[0].astype(bf), w_o_mla[0].astype(bf), w_out[0].astype(bf), tm_row)

    kv_mem = _mem_kv(mem, mem_norm_g[0].reshape(1, d), w_ckv[0].astype(bf))
    w_r = jnp.zeros((d, LANES), jnp.float32)
    w_r = w_r.at[:, :N_GROUPS].set(w_router_group[0].astype(jnp.float32))
    w_r = w_r.at[:, ROUTER_EXPERT_LANE0:ROUTER_EXPERT_LANE0 + N_EXPERTS].set(w_router_expert[0].astype(jnp.float32))
    b_r = jnp.zeros((1, LANES), jnp.float32)
    b_r = b_r.at[0, :N_GROUPS].set(b_router_group[0].astype(jnp.float32))
    b_r = b_r.at[0, ROUTER_EXPERT_LANE0:ROUTER_EXPERT_LANE0 + N_EXPERTS].set(b_router_expert[0].astype(jnp.float32))
    g_ffn = ffn_norm_g[0].reshape(1, d)
    h2, eid, rank, wts, cnt = _cross_router(h1, cross_norm_g[0].reshape(1, d), w_cq[0].astype(bf), kv_mem,
                                            w_co[0].astype(bf), g_ffn, w_r, b_r, seq, tm_row)

    bm = MOE_ROWS_PER_BLOCK
    counts = cnt[0, ROUTER_EXPERT_LANE0:ROUTER_EXPERT_LANE0 + N_EXPERTS].astype(jnp.int32)
    padded = ((counts + bm - 1) // bm) * bm
    padded_end = jnp.cumsum(padded)
    padded_off = padded_end - padded
    dest = padded_off[eid] + rank
    p_rows = ((2 * n_tok + bm - 1) // bm) * bm + N_EXPERTS * bm
    n_blocks = p_rows // bm
    n_active = (padded_end[-1] // bm).astype(jnp.int32)
    blk = jnp.minimum(jnp.arange(n_blocks, dtype=jnp.int32), n_active - 1)
    block_expert = jnp.clip(jnp.searchsorted(padded_end, blk * bm, side="right"), 0, N_EXPERTS - 1).astype(jnp.int32)
    dest3 = dest.reshape(n_tok // tm_row, 1, 2 * tm_row)

    xb = _dispatch(dest3, h2, g_ffn, jnp.zeros((p_rows, d), jnp.float32), tm_row)
    y = _experts(block_expert, n_active.reshape(1), xb, w_expert_gate[0], w_expert_up[0], w_expert_down[0], bm)
    out = _combine(dest3, h2, wts, final_norm_g.reshape(1, d), y, tm_row)
    return out.reshape(batch, seq, d)
```

```python
import functools
import math

import jax
import jax.numpy as jnp
from jax import lax
from jax.experimental import pallas as pl
from jax.experimental.pallas import tpu as pltpu

D_MODEL = 2048
ROPE_THETA = 500000.0
NORM_EPS = 1e-6

DIFF_HEADS = 8
DIFF_HEAD_DIM = 64
DIFF_V_DIM = 2 * DIFF_HEAD_DIM
DIFF_ROT = DIFF_HEAD_DIM // 4
DIFF_SUBLN_EPS = 1e-5
DIFF_LAMBDA_INIT = 0.8 - 0.6 * math.exp(-0.3 * 0)

MLA_HEADS = 8
MLA_Q_RANK = 512
MLA_KV_RANK = 256
MLA_NOPE_DIM = 128
MLA_ROPE_DIM = 64
MLA_V_DIM = 128
MLA_QK_DIM = MLA_NOPE_DIM + MLA_ROPE_DIM

CROSS_HEADS = 4
CROSS_HEAD_DIM = 128

N_GROUPS = 4
EXPERTS_PER_GROUP = 8
N_EXPERTS = N_GROUPS * EXPERTS_PER_GROUP
D_EXPERT = 512

LANES = 128
LOG2E = 1.4426950408889634
VMEM_LIMIT_BYTES = 56 * 1024 * 1024

ROUTER_EXPERT_LANE0 = N_GROUPS

QKV_COLS = 3 * DIFF_HEADS * DIFF_V_DIM
LATENT_COLS = 1024
GATE_COLS = 2 * D_MODEL
KPE_COL0 = MLA_Q_RANK + MLA_KV_RANK

MOE_ROWS_PER_BLOCK = 256


def _params(*semantics):
    return pltpu.CompilerParams(dimension_semantics=semantics, vmem_limit_bytes=VMEM_LIMIT_BYTES)


def _resident(shape):
    zeros = (0,) * len(shape)
    return pl.BlockSpec(shape, lambda *_: zeros, pipeline_mode=pl.Buffered(1))


def _rms_scale(xf, eps):
    return lax.rsqrt(jnp.mean(xf * xf, axis=-1, keepdims=True) + eps)


def _sigmoid(x):
    return 0.5 * jnp.tanh(0.5 * x) + 0.5


def _lane_iota(shape):
    return lax.broadcasted_iota(jnp.int32, shape, len(shape) - 1)


def _trig_kernel(pos_ref, invf_ref, cos_ref, sin_ref):
    ang = pos_ref[...].astype(jnp.float32) * invf_ref[...]
    cos_ref[...] = jnp.cos(ang)
    sin_ref[...] = jnp.sin(ang)


def _rope_tables(positions, n_tok, tm):
    half_m = MLA_ROPE_DIM // 2
    half_d = DIFF_ROT // 2
    inv_m = jnp.float32(ROPE_THETA) ** (-jnp.arange(half_m, dtype=jnp.float32) * 2.0 / MLA_ROPE_DIM)
    inv_d = jnp.float32(ROPE_THETA) ** (-jnp.arange(half_d, dtype=jnp.float32) * 2.0 / DIFF_ROT)
    invf = jnp.concatenate([inv_m, inv_m, inv_d, inv_d,
                            jnp.zeros((DIFF_HEAD_DIM - DIFF_ROT,), jnp.float32)]).reshape(1, LANES)
    pos = positions.reshape(n_tok, 1)
    return pl.pallas_call(
        _trig_kernel,
        grid=(n_tok // tm,),
        in_specs=[pl.BlockSpec((tm, 1), lambda i: (i, 0)), _resident((1, LANES))],
        out_specs=[pl.BlockSpec((tm, LANES), lambda i: (i, 0))] * 2,
        out_shape=[jax.ShapeDtypeStruct((n_tok, LANES), jnp.float32)] * 2,
        compiler_params=_params("parallel"),
        name="rope_tables",
    )(pos, invf)


def _diff_rope_coeffs(cos_t, sin_t):
    lane = _lane_iota(cos_t.shape)
    upper = lane >= DIFF_HEAD_DIM
    cos_d = jnp.where(upper, cos_t, pltpu.roll(cos_t, DIFF_HEAD_DIM, 1))
    sin_d = jnp.where(upper, sin_t, pltpu.roll(sin_t, DIFF_HEAD_DIM, 1))
    in_head = lane % DIFF_HEAD_DIM
    half = DIFF_ROT // 2
    s_next = jnp.where(in_head < half, -sin_d, 0.0)
    s_prev = jnp.where((in_head >= half) & (in_head < DIFF_ROT), sin_d, 0.0)
    return cos_d, s_next, s_prev


def _mla_rope(pair, cos_t, sin_t):
    lane = _lane_iota(pair.shape)
    sin_signed = jnp.where(lane < MLA_ROPE_DIM // 2, -sin_t, sin_t)
    return pair * cos_t + pltpu.roll(pair, MLA_ROPE_DIM, 1) * sin_signed


def _inproj_kernel(*refs, mode, tn):
    if mode == "qkv":
        x_ref, g_ref, w_ref, cos_ref, sin_ref, o_ref, xn_ref = refs
    else:
        x_ref, g_ref, w_ref, o_ref, xn_ref = refs
    j = pl.program_id(1)

    @pl.when(j == 0)
    def _():
        xf = x_ref[...]
        xn_ref[...] = (xf * _rms_scale(xf, NORM_EPS) * g_ref[...]).astype(jnp.bfloat16)

    acc = jnp.dot(xn_ref[...], w_ref[...], preferred_element_type=jnp.float32)

    if mode == "latent":
        o_ref[...] = acc
    elif mode == "gates":
        o_ref[...] = _sigmoid(acc).astype(o_ref.dtype)
    else:
        q_tiles = DIFF_HEADS * DIFF_V_DIM // tn

        @pl.when(j < 2 * q_tiles)
        def _():
            cos_d, s_next, s_prev = _diff_rope_coeffs(cos_ref[...], sin_ref[...])
            qscale = jnp.where(j < q_tiles, DIFF_HEAD_DIM ** -0.5 * LOG2E, 1.0).astype(jnp.float32)
            for c in range(tn // LANES):
                xc = acc[:, c * LANES:(c + 1) * LANES]
                rot = (xc * cos_d + pltpu.roll(xc, LANES - DIFF_ROT // 2, 1) * s_next
                       + pltpu.roll(xc, DIFF_ROT // 2, 1) * s_prev)
                o_ref[:, c * LANES:(c + 1) * LANES] = (rot * qscale).astype(o_ref.dtype)

        @pl.when(j >= 2 * q_tiles)
        def _():
            o_ref[...] = acc.astype(o_ref.dtype)


def _inproj(x2, g, w, mode, out_dtype, tm, tn, tables=None):
    n_tok, d = x2.shape
    ncols = w.shape[1]
    in_specs = [pl.BlockSpec((tm, d), lambda i, j: (i, 0)),
                _resident((1, d)),
                pl.BlockSpec((d, tn), lambda i, j: (0, j))]
    args = [x2, g, w]
    if mode == "qkv":
        in_specs += [pl.BlockSpec((tm, LANES), lambda i, j: (i, 0))] * 2
        args += list(tables)
    return pl.pallas_call(
        functools.partial(_inproj_kernel, mode=mode, tn=tn),
        grid=(n_tok // tm, ncols // tn),
        in_specs=in_specs,
        out_specs=pl.BlockSpec((tm, tn), lambda i, j: (i, j)),
        out_shape=jax.ShapeDtypeStruct((n_tok, ncols), out_dtype),
        scratch_shapes=[pltpu.VMEM((tm, d), jnp.bfloat16)],
        compiler_params=_params("parallel", "arbitrary"),
        name="inproj_" + mode,
    )(*args)


def _mla_proj_kernel(c_ref, gq_ref, gkv_ref, wuq_ref, wuk_ref, wuv_ref, cos_ref, sin_ref,
                     q_ref, k_ref, v_ref):
    cos_t = cos_ref[...]
    sin_t = sin_ref[...]
    cq = c_ref[:, :MLA_Q_RANK]
    cqn = (cq * _rms_scale(cq, NORM_EPS) * gq_ref[...]).astype(jnp.bfloat16)
    ckv = c_ref[:, MLA_Q_RANK:KPE_COL0]
    ckvn = (ckv * _rms_scale(ckv, NORM_EPS) * gkv_ref[...]).astype(jnp.bfloat16)
    kpe = _mla_rope(c_ref[:, KPE_COL0:KPE_COL0 + LANES], cos_t, sin_t)[:, :MLA_ROPE_DIM].astype(k_ref.dtype)
    qscale = MLA_QK_DIM ** -0.5 * LOG2E
    for h in range(MLA_HEADS):
        r = jnp.dot(cqn, wuq_ref[h], preferred_element_type=jnp.float32)
        q_ref[0, h, :, :MLA_NOPE_DIM] = (r[:, :MLA_NOPE_DIM] * qscale).astype(q_ref.dtype)
        qpe = _mla_rope(r[:, MLA_NOPE_DIM:], cos_t, sin_t)[:, :MLA_ROPE_DIM]
        q_ref[0, h, :, MLA_NOPE_DIM:] = (qpe * qscale).astype(q_ref.dtype)
        kn = jnp.dot(ckvn, wuk_ref[h], preferred_element_type=jnp.float32)
        k_ref[0, h, :, :MLA_NOPE_DIM] = kn.astype(k_ref.dtype)
        k_ref[0, h, :, MLA_NOPE_DIM:] = kpe
    v_ref[...] = jnp.dot(ckvn, wuv_ref[...], preferred_element_type=jnp.float32).astype(v_ref.dtype)


def _mla_proj(latent, gq, gkv, wuq, wuk, wuv, cos_t, sin_t, batch, seq, tm):
    n_tok = latent.shape[0]
    per_b = seq // tm
    head_spec = pl.BlockSpec((1, MLA_HEADS, tm, MLA_QK_DIM), lambda i: (i // per_b, 0, i % per_b, 0))
    head_shape = jax.ShapeDtypeStruct((batch, MLA_HEADS, seq, MLA_QK_DIM), jnp.bfloat16)
    return pl.pallas_call(
        _mla_proj_kernel,
        grid=(n_tok // tm,),
        in_specs=[pl.BlockSpec((tm, LATENT_COLS), lambda i: (i, 0)),
                  _resident(gq.shape), _resident(gkv.shape),
                  _resident(wuq.shape), _resident(wuk.shape), _resident(wuv.shape),
                  pl.BlockSpec((tm, LANES), lambda i: (i, 0)),
                  pl.BlockSpec((tm, LANES), lambda i: (i, 0))],
        out_specs=[head_spec, head_spec,
                   pl.BlockSpec((tm, MLA_HEADS * MLA_V_DIM), lambda i: (i, 0))],
        out_shape=[head_shape, head_shape,
                   jax.ShapeDtypeStruct((n_tok, MLA_HEADS * MLA_V_DIM), jnp.bfloat16)],
        compiler_params=_params("parallel"),
        name="mla_proj",
    )(latent, gq, gkv, wuq, wuk, wuv, cos_t, sin_t)


def _with_ones(v):
    return jnp.concatenate([v, jnp.ones((v.shape[0], LANES), v.dtype)], axis=-1)


def _softmax_pv(s, v_ones):
    m = jnp.max(s, axis=-1, keepdims=True)
    p = jnp.exp2(s - m).astype(v_ones.dtype)
    pv = jnp.dot(p, v_ones, preferred_element_type=jnp.float32)
    dv = v_ones.shape[1] - LANES
    return pv[:, :dv] / pv[:, dv:]


def _diff_attn_kernel(q_ref, k_ref, v_ref, lq1_ref, lk1_ref, lq2_ref, lk2_ref, g_ref, o_ref, v1_ref, *, rg):
    @pl.when(pl.program_id(2) == 0)
    def _():
        v1_ref[...] = _with_ones(v_ref[0])

    k = k_ref[0]
    v = v1_ref[...]
    lam = (jnp.exp(jnp.sum(lq1_ref[...] * lk1_ref[...], axis=-1, keepdims=True))
           - jnp.exp(jnp.sum(lq2_ref[...] * lk2_ref[...], axis=-1, keepdims=True))
           + DIFF_LAMBDA_INIT)
    lane = _lane_iota((rg, LANES))
    for g in range(q_ref.shape[1] // rg):
        q = q_ref[0, g * rg:(g + 1) * rg]
        zero = jnp.zeros_like(q)
        q12 = jnp.concatenate([jnp.where(lane < DIFF_HEAD_DIM, q, zero),
                               jnp.where(lane >= DIFF_HEAD_DIM, q, zero)], axis=0)
        s = lax.dot_general(q12, k, (((1,), (1,)), ((), ())), preferred_element_type=jnp.float32)
        a = _softmax_pv(s, v)
        o = a[:rg] - lam * a[rg:]
        o = o * _rms_scale(o, DIFF_SUBLN_EPS) * g_ref[...] * (1.0 - DIFF_LAMBDA_INIT)
        o_ref[0, g * rg:(g + 1) * rg] = o.astype(o_ref.dtype)


def _diff_attn(qkv3, lq1, lk1, lq2, lk2, subln_g, tq, rg):
    batch, seq, _ = qkv3.shape
    h = DIFF_HEADS
    return pl.pallas_call(
        functools.partial(_diff_attn_kernel, rg=rg),
        grid=(batch, h, seq // tq),
        in_specs=[pl.BlockSpec((1, tq, LANES), lambda b, hh, i: (b, i, hh)),
                  pl.BlockSpec((1, seq, LANES), lambda b, hh, i: (b, 0, h + hh)),
                  pl.BlockSpec((1, seq, LANES), lambda b, hh, i: (b, 0, 2 * h + hh)),
                  _resident(lq1.shape), _resident(lk1.shape), _resident(lq2.shape), _resident(lk2.shape),
                  _resident(subln_g.shape)],
        out_specs=pl.BlockSpec((1, tq, DIFF_V_DIM), lambda b, hh, i: (b, i, hh)),
        out_shape=jax.ShapeDtypeStruct((batch, seq, h * DIFF_V_DIM), jnp.bfloat16),
        scratch_shapes=[pltpu.VMEM((seq, DIFF_V_DIM + LANES), jnp.bfloat16)],
        compiler_params=_params("parallel", "parallel", "arbitrary"),
        name="diff_attn",
    )(qkv3, qkv3, qkv3, lq1, lk1, lq2, lk2, subln_g)


def _mla_attn_kernel(q_ref, k_ref, v_ref, o_ref, v1_ref, *, rg):
    @pl.when(pl.program_id(2) == 0)
    def _():
        v1_ref[...] = _with_ones(v_ref[0])

    k = k_ref[0, 0]
    v = v1_ref[...]
    for g in range(q_ref.shape[2] // rg):
        s = lax.dot_general(q_ref[0, 0, g * rg:(g + 1) * rg], k, (((1,), (1,)), ((), ())),
                            preferred_element_type=jnp.float32)
        o_ref[0, g * rg:(g + 1) * rg] = _softmax_pv(s, v).astype(o_ref.dtype)


def _mla_attn(q_cat, k_cat, v3, tq, rg):
    batch, heads, seq, dqk = q_cat.shape
    return pl.pallas_call(
        functools.partial(_mla_attn_kernel, rg=rg),
        grid=(batch, heads, seq // tq),
        in_specs=[pl.BlockSpec((1, 1, tq, dqk), lambda b, h, i: (b, h, i, 0)),
                  pl.BlockSpec((1, 1, seq, dqk), lambda b, h, i: (b, h, 0, 0)),
                  pl.BlockSpec((1, seq, MLA_V_DIM), lambda b, h, i: (b, 0, h))],
        out_specs=pl.BlockSpec((1, tq, MLA_V_DIM), lambda b, h, i: (b, i, h)),
        out_shape=jax.ShapeDtypeStruct((batch, seq, heads * MLA_V_DIM), jnp.bfloat16),
        scratch_shapes=[pltpu.VMEM((seq, MLA_V_DIM + LANES), jnp.bfloat16)],
        compiler_params=_params("parallel", "parallel", "arbitrary"),
        name="mla_attn",
    )(q_cat, k_cat, v3)


def _merge_out_kernel(oa_ref, ob_ref, sga_ref, sgb_ref, x_ref, woa_ref, wob_ref, wout_ref, h_ref):
    ya = jnp.dot(oa_ref[...], woa_ref[...], preferred_element_type=jnp.float32)
    yb = jnp.dot(ob_ref[...], wob_ref[...], preferred_element_type=jnp.float32)
    merged = sga_ref[...].astype(jnp.float32) * ya + sgb_ref[...].astype(jnp.float32) * yb
    h_ref[...] = x_ref[...] + jnp.dot(merged.astype(jnp.bfloat16), wout_ref[...],
                                       preferred_element_type=jnp.float32)


def _merge_out(o_a, o_b, gates, x2, w_oa, w_ob, w_out, tm):
    n_tok, d = x2.shape
    return pl.pallas_call(
        _merge_out_kernel,
        grid=(n_tok // tm,),
        in_specs=[pl.BlockSpec((tm, o_a.shape[1]), lambda i: (i, 0)),
                  pl.BlockSpec((tm, o_b.shape[1]), lambda i: (i, 0)),
                  pl.BlockSpec((tm, d), lambda i: (i, 0)),
                  pl.BlockSpec((tm, d), lambda i: (i, 1)),
                  pl.BlockSpec((tm, d), lambda i: (i, 0)),
                  _resident(w_oa.shape), _resident(w_ob.shape), _resident(w_out.shape)],
        out_specs=pl.BlockSpec((tm, d), lambda i: (i, 0)),
        out_shape=jax.ShapeDtypeStruct((n_tok, d), jnp.float32),
        compiler_params=_params("parallel"),
        name="merge_out",
    )(o_a, o_b, gates, gates, x2, w_oa, w_ob, w_out)


def _mem_kv_kernel(mem_ref, g_ref, w_ref, kv_ref):
    mf = mem_ref[0]
    mn = (mf * _rms_scale(mf, NORM_EPS) * g_ref[...]).astype(jnp.bfloat16)
    kv_ref[0] = jnp.dot(mn, w_ref[...], preferred_element_type=jnp.float32).astype(kv_ref.dtype)


def _mem_kv(mem, g, w_ckv):
    batch, m, d = mem.shape
    return pl.pallas_call(
        _mem_kv_kernel,
        grid=(batch,),
        in_specs=[pl.BlockSpec((1, m, d), lambda b: (b, 0, 0)), _resident(g.shape), _resident(w_ckv.shape)],
        out_specs=pl.BlockSpec((1, m, w_ckv.shape[1]), lambda b: (b, 0, 0)),
        out_shape=jax.ShapeDtypeStruct((batch, m, w_ckv.shape[1]), jnp.bfloat16),
        compiler_params=_params("parallel"),
        name="mem_kv",
    )(mem, g, w_ckv)


def _cross_router_kernel(h_ref, gc_ref, wcq_ref, kv_ref, wco_ref, gf_ref, wr_ref, br_ref,
                         h2_ref, eid_ref, rank_ref, wts_ref, cnt_ref, carry_ref):
    i = pl.program_id(0)

    @pl.when(i == 0)
    def _():
        carry_ref[...] = jnp.zeros_like(carry_ref)

    h1 = h_ref[...]
    tm = h1.shape[0]
    hn = (h1 * _rms_scale(h1, NORM_EPS) * gc_ref[...]).astype(jnp.bfloat16)
    q = jnp.dot(hn, wcq_ref[...], preferred_element_type=jnp.float32) * (CROSS_HEAD_DIM ** -0.5 * LOG2E)
    q = q.astype(jnp.bfloat16)
    kv_cols = CROSS_HEADS * CROSS_HEAD_DIM
    heads = []
    for hd in range(CROSS_HEADS):
        lo = hd * CROSS_HEAD_DIM
        kh = kv_ref[0, :, lo:lo + CROSS_HEAD_DIM]
        vh = kv_ref[0, :, kv_cols + lo:kv_cols + lo + CROSS_HEAD_DIM]
        s = lax.dot_general(q[:, lo:lo + CROSS_HEAD_DIM], kh, (((1,), (1,)), ((), ())),
                            preferred_element_type=jnp.float32)
        heads.append(_softmax_pv(s, _with_ones(vh)).astype(jnp.bfloat16))
    o = jnp.concatenate(heads, axis=-1)
    h2 = h1 + jnp.dot(o, wco_ref[...], preferred_element_type=jnp.float32)
    h2_ref[...] = h2

    t = h2 * _rms_scale(h2, NORM_EPS) * gf_ref[...]
    t_hi = t.astype(jnp.bfloat16)
    t_lo = (t - t_hi.astype(jnp.float32)).astype(jnp.bfloat16)
    hi = jnp.dot(t_hi, wr_ref[...], preferred_element_type=jnp.float32)
    lo = jnp.dot(t_lo, wr_ref[:, :LANES], preferred_element_type=jnp.float32)
    logits = hi[:, :LANES] + (hi[:, LANES:] + lo) + br_ref[...]
    lane = _lane_iota(logits.shape)
    neg = jnp.float32(-jnp.inf)
    big = jnp.int32(2 * LANES)
    is_group = lane < N_GROUPS
    lg = jnp.where(is_group, logits, neg)
    mg = jnp.max(lg, axis=-1, keepdims=True)
    g_idx = jnp.min(jnp.where(is_group & (logits == mg), lane, big), axis=-1, keepdims=True)
    g_p = 1.0 / jnp.sum(jnp.exp(lg - mg), axis=-1, keepdims=True)
    lo_lane = ROUTER_EXPERT_LANE0 + EXPERTS_PER_GROUP * g_idx
    in_grp = (lane >= lo_lane) & (lane < lo_lane + EXPERTS_PER_GROUP)
    l1 = jnp.max(jnp.where(in_grp, logits, neg), axis=-1, keepdims=True)
    i1 = jnp.min(jnp.where(in_grp & (logits == l1), lane, big), axis=-1, keepdims=True)
    rest = in_grp & (lane != i1)
    l2 = jnp.max(jnp.where(rest, logits, neg), axis=-1, keepdims=True)
    i2 = jnp.min(jnp.where(rest & (logits == l2), lane, big), axis=-1, keepdims=True)
    d = jnp.exp(l2 - l1)
    w1 = g_p / (1.0 + d)
    w2 = w1 * d

    oh1 = lane == i1
    oh2 = lane == i2
    cnt = (oh1 | oh2).astype(jnp.bfloat16)
    row = lax.broadcasted_iota(jnp.int32, (tm, tm), 0)
    col = lax.broadcasted_iota(jnp.int32, (tm, tm), 1)
    before = (col < row).astype(jnp.bfloat16)
    slot = jnp.dot(before, cnt, preferred_element_type=jnp.float32) + carry_ref[...]
    r1 = jnp.sum(jnp.where(oh1, slot, 0.0), axis=-1, keepdims=True)
    r2 = jnp.sum(jnp.where(oh2, slot, 0.0), axis=-1, keepdims=True)
    carry_ref[...] += jnp.sum(cnt.astype(jnp.float32), axis=0, keepdims=True)
    cnt_ref[...] = carry_ref[...]

    pair = _lane_iota((tm, 2))
    eid_ref[...] = jnp.where(pair == 0, i1, i2) - ROUTER_EXPERT_LANE0
    rank_ref[...] = jnp.where(pair == 0, r1, r2).astype(jnp.int32)
    wts_ref[...] = jnp.where(pair == 0, w1, w2)


def _cross_router(h1, gc, w_cq, kv_mem, w_co, gf, w_r, b_r, seq, tm):
    n_tok, d = h1.shape
    per_b = seq // tm
    row2 = pl.BlockSpec((tm, 2), lambda i: (i, 0))
    return pl.pallas_call(
        _cross_router_kernel,
        grid=(n_tok // tm,),
        in_specs=[pl.BlockSpec((tm, d), lambda i: (i, 0)),
                  _resident(gc.shape), _resident(w_cq.shape),
                  pl.BlockSpec((1,) + kv_mem.shape[1:], lambda i: (i // per_b, 0, 0)),
                  _resident(w_co.shape), _resident(gf.shape), _resident(w_r.shape), _resident(b_r.shape)],
        out_specs=[pl.BlockSpec((tm, d), lambda i: (i, 0)), row2, row2, row2,
                   pl.BlockSpec((1, LANES), lambda i: (0, 0))],
        out_shape=[jax.ShapeDtypeStruct((n_tok, d), jnp.float32),
                   jax.ShapeDtypeStruct((n_tok, 2), jnp.int32),
                   jax.ShapeDtypeStruct((n_tok, 2), jnp.int32),
                   jax.ShapeDtypeStruct((n_tok, 2), jnp.float32),
                   jax.ShapeDtypeStruct((1, LANES), jnp.float32)],
        scratch_shapes=[pltpu.VMEM((1, LANES), jnp.float32)],
        compiler_params=_params("arbitrary"),
        name="cross_router",
    )(h1, gc, w_cq, kv_mem, w_co, gf, w_r, b_r)


def _dispatch_kernel(dest_ref, h_ref, g_ref, xb_in_ref, xb_ref, t_ref, sem):
    del xb_in_ref
    h2 = h_ref[...]
    tm = h2.shape[0]
    t_ref[...] = h2 * _rms_scale(h2, NORM_EPS) * g_ref[...]

    def row_copy(r, k):
        return pltpu.make_async_copy(t_ref.at[pl.ds(r, 1)], xb_ref.at[pl.ds(dest_ref[0, 2 * r + k], 1)], sem)

    def issue(r, carry):
        row_copy(r, 0).start()
        row_copy(r, 1).start()
        return carry

    lax.fori_loop(0, tm, issue, 0, unroll=8)
    for _ in range(2):
        pltpu.make_async_copy(t_ref, xb_ref.at[pl.ds(0, tm)], sem).wait()


def _dispatch(dest3, h2, gf, xb_init, tm):
    n_tok, d = h2.shape
    return pl.pallas_call(
        _dispatch_kernel,
        grid=(n_tok // tm,),
        in_specs=[pl.BlockSpec((None, 1, 2 * tm), lambda i: (i, 0, 0), memory_space=pltpu.SMEM),
                  pl.BlockSpec((tm, d), lambda i: (i, 0)),
                  _resident(gf.shape),
                  pl.BlockSpec(memory_space=pl.ANY)],
        out_specs=pl.BlockSpec(memory_space=pl.ANY),
        out_shape=jax.ShapeDtypeStruct(xb_init.shape, xb_init.dtype),
        scratch_shapes=[pltpu.VMEM((tm, d), jnp.float32), pltpu.SemaphoreType.DMA(())],
        input_output_aliases={3: 0},
        compiler_params=_params("arbitrary"),
        name="moe_dispatch",
    )(dest3, h2, gf, xb_init)


def _expert_kernel(be_ref, nact_ref, x_ref, wg_ref, wu_ref, wd_ref, y_ref, wg_b, wu_b, wd_b):
    i = pl.program_id(0)

    @pl.when(i < nact_ref[0])
    def _():
        prev = be_ref[jnp.maximum(i - 1, 0)]

        @pl.when((i == 0) | (be_ref[i] != prev))
        def _():
            wg_b[...] = wg_ref[0].astype(jnp.bfloat16)
            wu_b[...] = wu_ref[0].astype(jnp.bfloat16)
            wd_b[...] = wd_ref[0].astype(jnp.bfloat16)

        xb = x_ref[...].astype(jnp.bfloat16)
        gate = jnp.dot(xb, wg_b[...], preferred_element_type=jnp.float32)
        up = jnp.dot(xb, wu_b[...], preferred_element_type=jnp.float32)
        hid = (gate * _sigmoid(gate) * up).astype(jnp.bfloat16)
        y_ref[...] = jnp.dot(hid, wd_b[...], preferred_element_type=jnp.float32)

    @pl.when(i >= nact_ref[0])
    def _():
        y_ref[...] = jnp.zeros_like(y_ref)


def _experts(block_expert, n_active, xb, w_gate, w_up, w_down, bm):
    p_rows, d = xb.shape
    de = w_gate.shape[-1]

    def row_map(i, be, nact):
        return (jnp.minimum(i, nact[0] - 1), 0)

    def w_map(i, be, nact):
        return (be[i], 0, 0)

    grid_spec = pltpu.PrefetchScalarGridSpec(
        num_scalar_prefetch=2,
        grid=(p_rows // bm,),
        in_specs=[pl.BlockSpec((bm, d), row_map),
                  pl.BlockSpec((1, d, de), w_map),
                  pl.BlockSpec((1, d, de), w_map),
                  pl.BlockSpec((1, de, d), w_map)],
        out_specs=pl.BlockSpec((bm, d), lambda i, be, nact: (i, 0)),
        scratch_shapes=[pltpu.VMEM((d, de), jnp.bfloat16), pltpu.VMEM((d, de), jnp.bfloat16),
                        pltpu.VMEM((de, d), jnp.bfloat16)],
    )
    return pl.pallas_call(
        _expert_kernel,
        grid_spec=grid_spec,
        out_shape=jax.ShapeDtypeStruct((p_rows, d), jnp.float32),
        compiler_params=_params("arbitrary"),
        name="moe_experts",
    )(block_expert, n_active, xb, w_gate, w_up, w_down)


def _combine_kernel(dest_ref, h_ref, wts_ref, g_ref, y_ref, o_ref, ybuf, sem):
    tm = h_ref.shape[0]

    def row_copy(r, k):
        return pltpu.make_async_copy(y_ref.at[pl.ds(dest_ref[0, 2 * r + k], 1)], ybuf.at[k, pl.ds(r, 1)], sem)

    def issue(r, carry):
        row_copy(r, 0).start()
        row_copy(r, 1).start()
        return carry

    lax.fori_loop(0, tm, issue, 0, unroll=8)
    for k in range(2):
        pltpu.make_async_copy(y_ref.at[pl.ds(0, tm)], ybuf.at[k], sem).wait()

    w = wts_ref[...]
    h3 = h_ref[...] + w[:, 0:1] * ybuf[0] + w[:, 1:2] * ybuf[1]
    o_ref[...] = h3 * _rms_scale(h3, NORM_EPS) * g_ref[...]


def _combine(dest3, h2, wts, g_final, y, tm):
    n_tok, d = h2.shape
    return pl.pallas_call(
        _combine_kernel,
        grid=(n_tok // tm,),
        in_specs=[pl.BlockSpec((None, 1, 2 * tm), lambda i: (i, 0, 0), memory_space=pltpu.SMEM),
                  pl.BlockSpec((tm, d), lambda i: (i, 0)),
                  pl.BlockSpec((tm, 2), lambda i: (i, 0)),
                  _resident(g_final.shape),
                  pl.BlockSpec(memory_space=pl.ANY)],
        out_specs=pl.BlockSpec((tm, d), lambda i: (i, 0)),
        out_shape=jax.ShapeDtypeStruct((n_tok, d), jnp.float32),
        scratch_shapes=[pltpu.VMEM((2, tm, d), jnp.float32), pltpu.SemaphoreType.DMA(())],
        compiler_params=_params("arbitrary"),
        name="moe_combine",
    )(dest3, h2, wts, g_final, y)


def _split_w_in(w_in):
    bf = jnp.bfloat16
    c0 = QKV_COLS
    c_q0, c_kv0 = c0, c0 + MLA_Q_RANK
    kpe0 = c_kv0 + MLA_KV_RANK
    g0 = kpe0 + MLA_ROPE_DIM
    half = MLA_ROPE_DIM // 2
    kpe = w_in[:, kpe0:g0]
    kpe_swapped = jnp.concatenate([kpe[:, half:], kpe[:, :half]], axis=1)
    pad = jnp.zeros((w_in.shape[0], LATENT_COLS - (g0 - c0) - MLA_ROPE_DIM), w_in.dtype)
    w_lat = jnp.concatenate([w_in[:, c_q0:g0], kpe_swapped, pad], axis=1)
    return w_in[:, :c0].astype(bf), w_lat.astype(bf), w_in[:, g0:].astype(bf)


def _split_w_uq(w_uq):
    half = MLA_ROPE_DIM // 2
    w = w_uq.reshape(MLA_Q_RANK, MLA_HEADS, MLA_QK_DIM).transpose(1, 0, 2)
    pe = w[:, :, MLA_NOPE_DIM:]
    pe_swapped = jnp.concatenate([pe[:, :, half:], pe[:, :, :half]], axis=2)
    return jnp.concatenate([w, pe_swapped], axis=2).astype(jnp.bfloat16)


def _split_w_ukv(w_ukv):
    w = w_ukv.reshape(MLA_KV_RANK, MLA_HEADS, MLA_NOPE_DIM + MLA_V_DIM)
    wuk = w[:, :, :MLA_NOPE_DIM].transpose(1, 0, 2).astype(jnp.bfloat16)
    wuv = w[:, :, MLA_NOPE_DIM:].reshape(MLA_KV_RANK, MLA_HEADS * MLA_V_DIM).astype(jnp.bfloat16)
    return wuk, wuv


def kernel(x, mem, positions, attn_norm_g, w_in, diff_lambda_q1, diff_lambda_k1, diff_lambda_q2, diff_lambda_k2, diff_subln_g, w_o_diff, mla_q_norm_g, w_uq, mla_kv_norm_g, w_ukv, w_o_mla, w_out, cross_norm_g, mem_norm_g, w_cq, w_ckv, w_co, ffn_norm_g, w_router_group, b_router_group, w_router_expert, b_router_expert, w_expert_gate, w_expert_up, w_expert_down, final_norm_g):
    batch, seq, d = x.shape
    assert d == D_MODEL and w_in.shape[0] == 1, "single-layer kernel"
    n_tok = batch * seq
    bf = jnp.bfloat16
    x2 = x.reshape(n_tok, d)

    tm_proj = min(1024, seq)
    tm_row = min(256, seq)
    tq = min(2048, seq)
    rg_diff = 128
    rg_mla = 256

    cos_t, sin_t = _rope_tables(positions, n_tok, tm_proj)

    w_qkv, w_lat, w_gates = _split_w_in(w_in[0])
    g_attn = attn_norm_g[0].reshape(1, d)
    qkv = _inproj(x2, g_attn, w_qkv, "qkv", bf, tm_proj, 512, (cos_t, sin_t))
    latent = _inproj(x2, g_attn, w_lat, "latent", jnp.float32, tm_proj, 512)
    gates = _inproj(x2, g_attn, w_gates, "gates", bf, tm_proj, 512)

    o_a = _diff_attn(qkv.reshape(batch, seq, QKV_COLS),
                     diff_lambda_q1[0].reshape(1, -1), diff_lambda_k1[0].reshape(1, -1),
                     diff_lambda_q2[0].reshape(1, -1), diff_lambda_k2[0].reshape(1, -1),
                     diff_subln_g[0].reshape(1, -1), tq, rg_diff)

    wuk, wuv = _split_w_ukv(w_ukv[0])
    q_cat, k_cat, v_mla = _mla_proj(latent, mla_q_norm_g[0].reshape(1, -1), mla_kv_norm_g[0].reshape(1, -1),
                                    _split_w_uq(w_uq[0]), wuk, wuv, cos_t, sin_t, batch, seq, min(512, seq))
    o_b = _mla_attn(q_cat, k_cat, v_mla.reshape(batch, seq, MLA_HEADS * MLA_V_DIM), tq, rg_mla)

    h1 = _merge_out(o_a.reshape(n_tok, -1), o_b.reshape(n_tok, -1), gates, x2,
                    w_o_diff[0].astype(bf), w_o_mla[0].astype(bf), w_out[0].astype(bf), tm_row)

    kv_mem = _mem_kv(mem, mem_norm_g[0].reshape(1, d), w_ckv[0].astype(bf))
    n_router = N_GROUPS + N_EXPERTS
    w_r = jnp.concatenate([w_router_group[0].astype(jnp.float32), w_router_expert[0].astype(jnp.float32),
                           jnp.zeros((d, LANES - n_router), jnp.float32)], axis=1)
    w_r_hi = w_r.astype(bf)
    w_r_lo = (w_r - w_r_hi.astype(jnp.float32)).astype(bf)
    w_r = jnp.concatenate([w_r_hi, w_r_lo], axis=1)
    b_r = jnp.concatenate([b_router_group[0].astype(jnp.float32), b_router_expert[0].astype(jnp.float32),
                           jnp.zeros((LANES - n_router,), jnp.float32)]).reshape(1, LANES)
    g_ffn = ffn_norm_g[0].reshape(1, d)
    h2, eid, rank, wts, cnt = _cross_router(h1, cross_norm_g[0].reshape(1, d), w_cq[0].astype(bf), kv_mem,
                                            w_co[0].astype(bf), g_ffn, w_r, b_r, seq, min(512, seq))

    bm = MOE_ROWS_PER_BLOCK
    counts = cnt[0, ROUTER_EXPERT_LANE0:ROUTER_EXPERT_LANE0 + N_EXPERTS].astype(jnp.int32)
    padded = ((counts + bm - 1) // bm) * bm
    padded_end = jnp.cumsum(padded)
    padded_off = padded_end - padded
    dest = padded_off[eid] + rank
    p_rows = ((2 * n_tok + bm - 1) // bm) * bm + N_EXPERTS * bm
    n_blocks = p_rows // bm
    n_active = (padded_end[-1] // bm).astype(jnp.int32)
    blk = jnp.minimum(jnp.arange(n_blocks, dtype=jnp.int32), n_active - 1)
    block_expert = jnp.sum((padded_end[None, :] <= (blk * bm)[:, None]).astype(jnp.int32), axis=1)
    block_expert = jnp.minimum(block_expert, N_EXPERTS - 1)
    dest3 = dest.reshape(n_tok // tm_row, 1, 2 * tm_row)

    xb = _dispatch(dest3, h2, g_ffn, jnp.zeros((p_rows, d), jnp.float32), tm_row)
    y = _experts(block_expert, n_active.reshape(1), xb, w_expert_gate[0], w_expert_up[0], w_expert_down[0], bm)
    out = _combine(dest3, h2, wts, final_norm_g.reshape(1, d), y, tm_row)
    return out.reshape(batch, seq, d)
```

```python
import functools
import math

import jax
import jax.numpy as jnp
from jax import lax
from jax.experimental import pallas as pl
from jax.experimental.pallas import tpu as pltpu

D_MODEL = 2048
ROPE_THETA = 500000.0
NORM_EPS = 1e-6

DIFF_HEADS = 8
DIFF_HEAD_DIM = 64
DIFF_V_DIM = 2 * DIFF_HEAD_DIM
DIFF_ROT = DIFF_HEAD_DIM // 4
DIFF_SUBLN_EPS = 1e-5
DIFF_LAMBDA_INIT = 0.8 - 0.6 * math.exp(-0.3 * 0)

MLA_HEADS = 8
MLA_Q_RANK = 512
MLA_KV_RANK = 256
MLA_NOPE_DIM = 128
MLA_ROPE_DIM = 64
MLA_V_DIM = 128
MLA_QK_DIM = MLA_NOPE_DIM + MLA_ROPE_DIM

CROSS_HEADS = 4
CROSS_HEAD_DIM = 128

N_GROUPS = 4
EXPERTS_PER_GROUP = 8
N_EXPERTS = N_GROUPS * EXPERTS_PER_GROUP
D_EXPERT = 512

LANES = 128
LOG2E = 1.4426950408889634
VMEM_LIMIT_BYTES = 56 * 1024 * 1024

ROUTER_EXPERT_LANE0 = N_GROUPS

QKV_COLS = 3 * DIFF_HEADS * DIFF_V_DIM
LATENT_COLS = 1024
GATE_COLS = 2 * D_MODEL
KPE_COL0 = MLA_Q_RANK + MLA_KV_RANK

MOE_ROWS_PER_BLOCK = 256


def _params(*semantics):
    return pltpu.CompilerParams(dimension_semantics=semantics, vmem_limit_bytes=VMEM_LIMIT_BYTES)


def _resident(shape):
    zeros = (0,) * len(shape)
    return pl.BlockSpec(shape, lambda *_: zeros, pipeline_mode=pl.Buffered(1))


def _rms_scale(xf, eps):
    return lax.rsqrt(jnp.mean(xf * xf, axis=-1, keepdims=True) + eps)


def _sigmoid(x):
    return 0.5 * jnp.tanh(0.5 * x) + 0.5


def _lane_iota(shape):
    return lax.broadcasted_iota(jnp.int32, shape, len(shape) - 1)


def _trig_kernel(pos_ref, invf_ref, cos_ref, sin_ref):
    ang = pos_ref[...].astype(jnp.float32) * invf_ref[...]
    cos_ref[...] = jnp.cos(ang)
    sin_ref[...] = jnp.sin(ang)


def _rope_tables(positions, n_tok, tm):
    half_m = MLA_ROPE_DIM // 2
    half_d = DIFF_ROT // 2
    inv_m = jnp.float32(ROPE_THETA) ** (-jnp.arange(half_m, dtype=jnp.float32) * 2.0 / MLA_ROPE_DIM)
    inv_d = jnp.float32(ROPE_THETA) ** (-jnp.arange(half_d, dtype=jnp.float32) * 2.0 / DIFF_ROT)
    invf = jnp.concatenate([inv_m, inv_m, inv_d, inv_d,
                            jnp.zeros((DIFF_HEAD_DIM - DIFF_ROT,), jnp.float32)]).reshape(1, LANES)
    pos = positions.reshape(n_tok, 1)
    return pl.pallas_call(
        _trig_kernel,
        grid=(n_tok // tm,),
        in_specs=[pl.BlockSpec((tm, 1), lambda i: (i, 0)), _resident((1, LANES))],
        out_specs=[pl.BlockSpec((tm, LANES), lambda i: (i, 0))] * 2,
        out_shape=[jax.ShapeDtypeStruct((n_tok, LANES), jnp.float32)] * 2,
        compiler_params=_params("parallel"),
        name="rope_tables",
    )(pos, invf)


def _diff_rope_coeffs(cos_t, sin_t):
    lane = _lane_iota(cos_t.shape)
    upper = lane >= DIFF_HEAD_DIM
    cos_d = jnp.where(upper, cos_t, pltpu.roll(cos_t, DIFF_HEAD_DIM, 1))
    sin_d = jnp.where(upper, sin_t, pltpu.roll(sin_t, DIFF_HEAD_DIM, 1))
    in_head = lane % DIFF_HEAD_DIM
    half = DIFF_ROT // 2
    s_next = jnp.where(in_head < half, -sin_d, 0.0)
    s_prev = jnp.where((in_head >= half) & (in_head < DIFF_ROT), sin_d, 0.0)
    return cos_d, s_next, s_prev


def _mla_rope(pair, cos_t, sin_t):
    lane = _lane_iota(pair.shape)
    sin_signed = jnp.where(lane < MLA_ROPE_DIM // 2, -sin_t, sin_t)
    return pair * cos_t + pltpu.roll(pair, MLA_ROPE_DIM, 1) * sin_signed


INPROJ_TN = 1024
INPROJ_PIECE = 256
Q_TILES = DIFF_HEADS * DIFF_V_DIM // INPROJ_TN
ROPE_TILES = 2 * Q_TILES
QKV_TILES = QKV_COLS // INPROJ_TN
LATENT_TILES = LATENT_COLS // INPROJ_TN
GATE_TILES = GATE_COLS // INPROJ_TN
INPROJ_TILES = QKV_TILES + LATENT_TILES + GATE_TILES


def _inproj_kernel(x_ref, g_ref, w_ref, cos_ref, sin_ref, qkv_ref, lat_ref, gate_ref, xn_ref):
    j = pl.program_id(1)

    @pl.when(j == 0)
    def _():
        xf = x_ref[...]
        xn_ref[...] = (xf * _rms_scale(xf, NORM_EPS) * g_ref[...]).astype(jnp.bfloat16)

    def pieces(epilogue):
        for c in range(INPROJ_TN // INPROJ_PIECE):
            cols = slice(c * INPROJ_PIECE, (c + 1) * INPROJ_PIECE)
            epilogue(jnp.dot(xn_ref[...], w_ref[:, cols], preferred_element_type=jnp.float32), cols)

    @pl.when(j < ROPE_TILES)
    def _():
        cos_d, s_next, s_prev = _diff_rope_coeffs(cos_ref[...], sin_ref[...])
        qscale = jnp.where(j < Q_TILES, DIFF_HEAD_DIM ** -0.5 * LOG2E, 1.0).astype(jnp.float32)

        def rope(acc, cols):
            for c in range(INPROJ_PIECE // LANES):
                xc = acc[:, c * LANES:(c + 1) * LANES]
                rot = (xc * cos_d + pltpu.roll(xc, LANES - DIFF_ROT // 2, 1) * s_next
                       + pltpu.roll(xc, DIFF_ROT // 2, 1) * s_prev)
                lo = cols.start + c * LANES
                qkv_ref[:, lo:lo + LANES] = (rot * qscale).astype(qkv_ref.dtype)

        pieces(rope)

    @pl.when((j >= ROPE_TILES) & (j < QKV_TILES))
    def _():
        def value(acc, cols):
            qkv_ref[:, cols] = acc.astype(qkv_ref.dtype)

        pieces(value)

    @pl.when((j >= QKV_TILES) & (j < QKV_TILES + LATENT_TILES))
    def _():
        def latent(acc, cols):
            lat_ref[:, cols] = acc

        pieces(latent)

    @pl.when(j >= QKV_TILES + LATENT_TILES)
    def _():
        def gate(acc, cols):
            gate_ref[:, cols] = _sigmoid(acc).astype(gate_ref.dtype)

        pieces(gate)


def _inproj(x2, g, w_all, cos_t, sin_t, tm):
    n_tok, d = x2.shape
    tn = INPROJ_TN
    lat0 = QKV_TILES
    gate0 = QKV_TILES + LATENT_TILES
    return pl.pallas_call(
        _inproj_kernel,
        grid=(n_tok // tm, INPROJ_TILES),
        in_specs=[pl.BlockSpec((tm, d), lambda i, j: (i, 0)),
                  _resident((1, d)),
                  pl.BlockSpec((d, tn), lambda i, j: (0, j)),
                  pl.BlockSpec((tm, LANES), lambda i, j: (i, 0)),
                  pl.BlockSpec((tm, LANES), lambda i, j: (i, 0))],
        out_specs=[pl.BlockSpec((tm, tn), lambda i, j: (i, jnp.clip(j, 0, QKV_TILES - 1))),
                   pl.BlockSpec((tm, tn), lambda i, j: (i, jnp.clip(j - lat0, 0, LATENT_TILES - 1))),
                   pl.BlockSpec((tm, tn), lambda i, j: (i, jnp.clip(j - gate0, 0, GATE_TILES - 1)))],
        out_shape=[jax.ShapeDtypeStruct((n_tok, QKV_COLS), jnp.bfloat16),
                   jax.ShapeDtypeStruct((n_tok, LATENT_COLS), jnp.float32),
                   jax.ShapeDtypeStruct((n_tok, GATE_COLS), jnp.bfloat16)],
        scratch_shapes=[pltpu.VMEM((tm, d), jnp.bfloat16)],
        compiler_params=_params("parallel", "arbitrary"),
        name="inproj",
    )(x2, g, w_all, cos_t, sin_t)


def _mla_proj_kernel(c_ref, gq_ref, gkv_ref, wuq_ref, wuk_ref, wuv_ref, cos_ref, sin_ref,
                     q_ref, k_ref, v_ref):
    cos_t = cos_ref[...]
    sin_t = sin_ref[...]
    cq = c_ref[:, :MLA_Q_RANK]
    cqn = (cq * _rms_scale(cq, NORM_EPS) * gq_ref[...]).astype(jnp.bfloat16)
    ckv = c_ref[:, MLA_Q_RANK:KPE_COL0]
    ckvn = (ckv * _rms_scale(ckv, NORM_EPS) * gkv_ref[...]).astype(jnp.bfloat16)
    kpe = _mla_rope(c_ref[:, KPE_COL0:KPE_COL0 + LANES], cos_t, sin_t)[:, :MLA_ROPE_DIM].astype(k_ref.dtype)
    qscale = MLA_QK_DIM ** -0.5 * LOG2E
    for h in range(MLA_HEADS):
        r = jnp.dot(cqn, wuq_ref[h], preferred_element_type=jnp.float32)
        q_ref[0, h, :, :MLA_NOPE_DIM] = (r[:, :MLA_NOPE_DIM] * qscale).astype(q_ref.dtype)
        qpe = _mla_rope(r[:, MLA_NOPE_DIM:], cos_t, sin_t)[:, :MLA_ROPE_DIM]
        q_ref[0, h, :, MLA_NOPE_DIM:] = (qpe * qscale).astype(q_ref.dtype)
        kn = jnp.dot(ckvn, wuk_ref[h], preferred_element_type=jnp.float32)
        k_ref[0, h, :, :MLA_NOPE_DIM] = kn.astype(k_ref.dtype)
        k_ref[0, h, :, MLA_NOPE_DIM:] = kpe
    v_ref[...] = jnp.dot(ckvn, wuv_ref[...], preferred_element_type=jnp.float32).astype(v_ref.dtype)


def _mla_proj(latent, gq, gkv, wuq, wuk, wuv, cos_t, sin_t, batch, seq, tm):
    n_tok = latent.shape[0]
    per_b = seq // tm
    head_spec = pl.BlockSpec((1, MLA_HEADS, tm, MLA_QK_DIM), lambda i: (i // per_b, 0, i % per_b, 0))
    head_shape = jax.ShapeDtypeStruct((batch, MLA_HEADS, seq, MLA_QK_DIM), jnp.bfloat16)
    return pl.pallas_call(
        _mla_proj_kernel,
        grid=(n_tok // tm,),
        in_specs=[pl.BlockSpec((tm, LATENT_COLS), lambda i: (i, 0)),
                  _resident(gq.shape), _resident(gkv.shape),
                  _resident(wuq.shape), _resident(wuk.shape), _resident(wuv.shape),
                  pl.BlockSpec((tm, LANES), lambda i: (i, 0)),
                  pl.BlockSpec((tm, LANES), lambda i: (i, 0))],
        out_specs=[head_spec, head_spec,
                   pl.BlockSpec((tm, MLA_HEADS * MLA_V_DIM), lambda i: (i, 0))],
        out_shape=[head_shape, head_shape,
                   jax.ShapeDtypeStruct((n_tok, MLA_HEADS * MLA_V_DIM), jnp.bfloat16)],
        compiler_params=_params("parallel"),
        name="mla_proj",
    )(latent, gq, gkv, wuq, wuk, wuv, cos_t, sin_t)


def _with_ones(v):
    return jnp.concatenate([v, jnp.ones((v.shape[0], LANES), v.dtype)], axis=-1)


def _softmax_pv(s, v_ones):
    m = jnp.max(s, axis=-1, keepdims=True)
    p = jnp.exp2(s - m).astype(v_ones.dtype)
    pv = jnp.dot(p, v_ones, preferred_element_type=jnp.float32)
    dv = v_ones.shape[1] - LANES
    return pv[:, :dv] / pv[:, dv:]


def _diff_attn_kernel(q_ref, k_ref, v_ref, lq1_ref, lk1_ref, lq2_ref, lk2_ref, g_ref, o_ref, v1_ref, *, rg):
    @pl.when(pl.program_id(2) == 0)
    def _():
        v1_ref[...] = _with_ones(v_ref[0])

    k = k_ref[0]
    v = v1_ref[...]
    lam = (jnp.exp(jnp.sum(lq1_ref[...] * lk1_ref[...], axis=-1, keepdims=True))
           - jnp.exp(jnp.sum(lq2_ref[...] * lk2_ref[...], axis=-1, keepdims=True))
           + DIFF_LAMBDA_INIT)
    lane = _lane_iota((rg, LANES))
    for g in range(q_ref.shape[1] // rg):
        q = q_ref[0, g * rg:(g + 1) * rg]
        zero = jnp.zeros_like(q)
        q12 = jnp.concatenate([jnp.where(lane < DIFF_HEAD_DIM, q, zero),
                               jnp.where(lane >= DIFF_HEAD_DIM, q, zero)], axis=0)
        s = lax.dot_general(q12, k, (((1,), (1,)), ((), ())), preferred_element_type=jnp.float32)
        a = _softmax_pv(s, v)
        o = a[:rg] - lam * a[rg:]
        o = o * _rms_scale(o, DIFF_SUBLN_EPS) * g_ref[...] * (1.0 - DIFF_LAMBDA_INIT)
        o_ref[0, g * rg:(g + 1) * rg] = o.astype(o_ref.dtype)


def _diff_attn(qkv3, lq1, lk1, lq2, lk2, subln_g, tq, rg):
    batch, seq, _ = qkv3.shape
    h = DIFF_HEADS
    return pl.pallas_call(
        functools.partial(_diff_attn_kernel, rg=rg),
        grid=(batch, h, seq // tq),
        in_specs=[pl.BlockSpec((1, tq, LANES), lambda b, hh, i: (b, i, hh)),
                  pl.BlockSpec((1, seq, LANES), lambda b, hh, i: (b, 0, h + hh)),
                  pl.BlockSpec((1, seq, LANES), lambda b, hh, i: (b, 0, 2 * h + hh)),
                  _resident(lq1.shape), _resident(lk1.shape), _resident(lq2.shape), _resident(lk2.shape),
                  _resident(subln_g.shape)],
        out_specs=pl.BlockSpec((1, tq, DIFF_V_DIM), lambda b, hh, i: (b, i, hh)),
        out_shape=jax.ShapeDtypeStruct((batch, seq, h * DIFF_V_DIM), jnp.bfloat16),
        scratch_shapes=[pltpu.VMEM((seq, DIFF_V_DIM + LANES), jnp.bfloat16)],
        compiler_params=_params("parallel", "parallel", "arbitrary"),
        name="diff_attn",
    )(qkv3, qkv3, qkv3, lq1, lk1, lq2, lk2, subln_g)


def _mla_attn_kernel(q_ref, k_ref, v_ref, o_ref, v1_ref, *, rg):
    @pl.when(pl.program_id(2) == 0)
    def _():
        v1_ref[...] = _with_ones(v_ref[0])

    k = k_ref[0, 0]
    v = v1_ref[...]
    for g in range(q_ref.shape[2] // rg):
        s = lax.dot_general(q_ref[0, 0, g * rg:(g + 1) * rg], k, (((1,), (1,)), ((), ())),
                            preferred_element_type=jnp.float32)
        o_ref[0, g * rg:(g + 1) * rg] = _softmax_pv(s, v).astype(o_ref.dtype)


def _mla_attn(q_cat, k_cat, v3, tq, rg):
    batch, heads, seq, dqk = q_cat.shape
    return pl.pallas_call(
        functools.partial(_mla_attn_kernel, rg=rg),
        grid=(batch, heads, seq // tq),
        in_specs=[pl.BlockSpec((1, 1, tq, dqk), lambda b, h, i: (b, h, i, 0)),
                  pl.BlockSpec((1, 1, seq, dqk), lambda b, h, i: (b, h, 0, 0)),
                  pl.BlockSpec((1, seq, MLA_V_DIM), lambda b, h, i: (b, 0, h))],
        out_specs=pl.BlockSpec((1, tq, MLA_V_DIM), lambda b, h, i: (b, i, h)),
        out_shape=jax.ShapeDtypeStruct((batch, seq, heads * MLA_V_DIM), jnp.bfloat16),
        scratch_shapes=[pltpu.VMEM((seq, MLA_V_DIM + LANES), jnp.bfloat16)],
        compiler_params=_params("parallel", "parallel", "arbitrary"),
        name="mla_attn",
    )(q_cat, k_cat, v3)


def _merge_out_kernel(oa_ref, ob_ref, sga_ref, sgb_ref, x_ref, woa_ref, wob_ref, wout_ref, h_ref):
    ya = jnp.dot(oa_ref[...], woa_ref[...], preferred_element_type=jnp.float32)
    yb = jnp.dot(ob_ref[...], wob_ref[...], preferred_element_type=jnp.float32)
    merged = sga_ref[...].astype(jnp.float32) * ya + sgb_ref[...].astype(jnp.float32) * yb
    h_ref[...] = x_ref[...] + jnp.dot(merged.astype(jnp.bfloat16), wout_ref[...],
                                       preferred_element_type=jnp.float32)


def _merge_out(o_a, o_b, gates, x2, w_oa, w_ob, w_out, tm):
    n_tok, d = x2.shape
    return pl.pallas_call(
        _merge_out_kernel,
        grid=(n_tok // tm,),
        in_specs=[pl.BlockSpec((tm, o_a.shape[1]), lambda i: (i, 0)),
                  pl.BlockSpec((tm, o_b.shape[1]), lambda i: (i, 0)),
                  pl.BlockSpec((tm, d), lambda i: (i, 0)),
                  pl.BlockSpec((tm, d), lambda i: (i, 1)),
                  pl.BlockSpec((tm, d), lambda i: (i, 0)),
                  _resident(w_oa.shape), _resident(w_ob.shape), _resident(w_out.shape)],
        out_specs=pl.BlockSpec((tm, d), lambda i: (i, 0)),
        out_shape=jax.ShapeDtypeStruct((n_tok, d), jnp.float32),
        compiler_params=_params("parallel"),
        name="merge_out",
    )(o_a, o_b, gates, gates, x2, w_oa, w_ob, w_out)


def _mem_kv_kernel(mem_ref, g_ref, w_ref, kv_ref):
    mf = mem_ref[0]
    mn = (mf * _rms_scale(mf, NORM_EPS) * g_ref[...]).astype(jnp.bfloat16)
    kv_ref[0] = jnp.dot(mn, w_ref[...], preferred_element_type=jnp.float32).astype(kv_ref.dtype)


def _mem_kv(mem, g, w_ckv):
    batch, m, d = mem.shape
    return pl.pallas_call(
        _mem_kv_kernel,
        grid=(batch,),
        in_specs=[pl.BlockSpec((1, m, d), lambda b: (b, 0, 0)), _resident(g.shape), _resident(w_ckv.shape)],
        out_specs=pl.BlockSpec((1, m, w_ckv.shape[1]), lambda b: (b, 0, 0)),
        out_shape=jax.ShapeDtypeStruct((batch, m, w_ckv.shape[1]), jnp.bfloat16),
        compiler_params=_params("parallel"),
        name="mem_kv",
    )(mem, g, w_ckv)


def _cross_router_kernel(h_ref, gc_ref, wcq_ref, kv_ref, wco_ref, gf_ref, wr_ref, br_ref,
                         h2_ref, eid_ref, rank_ref, wts_ref, cnt_ref, carry_ref):
    i = pl.program_id(0)

    @pl.when(i == 0)
    def _():
        carry_ref[...] = jnp.zeros_like(carry_ref)

    h1 = h_ref[...]
    tm = h1.shape[0]
    hn = (h1 * _rms_scale(h1, NORM_EPS) * gc_ref[...]).astype(jnp.bfloat16)
    q = jnp.dot(hn, wcq_ref[...], preferred_element_type=jnp.float32) * (CROSS_HEAD_DIM ** -0.5 * LOG2E)
    q = q.astype(jnp.bfloat16)
    kv_cols = CROSS_HEADS * CROSS_HEAD_DIM
    heads = []
    for hd in range(CROSS_HEADS):
        lo = hd * CROSS_HEAD_DIM
        kh = kv_ref[0, :, lo:lo + CROSS_HEAD_DIM]
        vh = kv_ref[0, :, kv_cols + lo:kv_cols + lo + CROSS_HEAD_DIM]
        s = lax.dot_general(q[:, lo:lo + CROSS_HEAD_DIM], kh, (((1,), (1,)), ((), ())),
                            preferred_element_type=jnp.float32)
        heads.append(_softmax_pv(s, _with_ones(vh)).astype(jnp.bfloat16))
    o = jnp.concatenate(heads, axis=-1)
    h2 = h1 + jnp.dot(o, wco_ref[...], preferred_element_type=jnp.float32)
    h2_ref[...] = h2

    t = h2 * _rms_scale(h2, NORM_EPS) * gf_ref[...]
    t_hi = t.astype(jnp.bfloat16)
    t_lo = (t - t_hi.astype(jnp.float32)).astype(jnp.bfloat16)
    hi = jnp.dot(t_hi, wr_ref[...], preferred_element_type=jnp.float32)
    lo = jnp.dot(t_lo, wr_ref[:, :LANES], preferred_element_type=jnp.float32)
    logits = hi[:, :LANES] + (hi[:, LANES:] + lo) + br_ref[...]
    lane = _lane_iota(logits.shape)
    neg = jnp.float32(-jnp.inf)
    big = jnp.int32(2 * LANES)
    is_group = lane < N_GROUPS
    lg = jnp.where(is_group, logits, neg)
    mg = jnp.max(lg, axis=-1, keepdims=True)
    g_idx = jnp.min(jnp.where(is_group & (logits == mg), lane, big), axis=-1, keepdims=True)
    g_p = 1.0 / jnp.sum(jnp.exp(lg - mg), axis=-1, keepdims=True)
    lo_lane = ROUTER_EXPERT_LANE0 + EXPERTS_PER_GROUP * g_idx
    in_grp = (lane >= lo_lane) & (lane < lo_lane + EXPERTS_PER_GROUP)
    l1 = jnp.max(jnp.where(in_grp, logits, neg), axis=-1, keepdims=True)
    i1 = jnp.min(jnp.where(in_grp & (logits == l1), lane, big), axis=-1, keepdims=True)
    rest = in_grp & (lane != i1)
    l2 = jnp.max(jnp.where(rest, logits, neg), axis=-1, keepdims=True)
    i2 = jnp.min(jnp.where(rest & (logits == l2), lane, big), axis=-1, keepdims=True)
    d = jnp.exp(l2 - l1)
    w1 = g_p / (1.0 + d)
    w2 = w1 * d

    oh1 = lane == i1
    oh2 = lane == i2
    cnt = (oh1 | oh2).astype(jnp.bfloat16)
    row = lax.broadcasted_iota(jnp.int32, (tm, tm), 0)
    col = lax.broadcasted_iota(jnp.int32, (tm, tm), 1)
    before = (col < row).astype(jnp.bfloat16)
    slot = jnp.dot(before, cnt, preferred_element_type=jnp.float32) + carry_ref[...]
    r1 = jnp.sum(jnp.where(oh1, slot, 0.0), axis=-1, keepdims=True)
    r2 = jnp.sum(jnp.where(oh2, slot, 0.0), axis=-1, keepdims=True)
    carry_ref[...] += jnp.sum(cnt.astype(jnp.float32), axis=0, keepdims=True)
    cnt_ref[...] = carry_ref[...]

    eye = row == col

    def to_row(c, dtype):
        return jnp.sum(jnp.where(eye, c.astype(jnp.float32), 0.0), axis=0, keepdims=True).astype(dtype)

    eid_ref[0] = jnp.concatenate([to_row(i1 - ROUTER_EXPERT_LANE0, jnp.int32),
                                  to_row(i2 - ROUTER_EXPERT_LANE0, jnp.int32)], axis=0)
    rank_ref[0] = jnp.concatenate([to_row(r1, jnp.int32), to_row(r2, jnp.int32)], axis=0)
    wts_ref[...] = jnp.where(_lane_iota((tm, 2)) == 0, w1, w2)


def _cross_router(h1, gc, w_cq, kv_mem, w_co, gf, w_r, b_r, seq, tm):
    n_tok, d = h1.shape
    per_b = seq // tm
    row2 = pl.BlockSpec((tm, 2), lambda i: (i, 0))
    lane2 = pl.BlockSpec((1, 2, tm), lambda i: (i, 0, 0))
    return pl.pallas_call(
        _cross_router_kernel,
        grid=(n_tok // tm,),
        in_specs=[pl.BlockSpec((tm, d), lambda i: (i, 0)),
                  _resident(gc.shape), _resident(w_cq.shape),
                  pl.BlockSpec((1,) + kv_mem.shape[1:], lambda i: (i // per_b, 0, 0)),
                  _resident(w_co.shape), _resident(gf.shape), _resident(w_r.shape), _resident(b_r.shape)],
        out_specs=[pl.BlockSpec((tm, d), lambda i: (i, 0)), lane2, lane2, row2,
                   pl.BlockSpec((1, LANES), lambda i: (0, 0))],
        out_shape=[jax.ShapeDtypeStruct((n_tok, d), jnp.float32),
                   jax.ShapeDtypeStruct((n_tok // tm, 2, tm), jnp.int32),
                   jax.ShapeDtypeStruct((n_tok // tm, 2, tm), jnp.int32),
                   jax.ShapeDtypeStruct((n_tok, 2), jnp.float32),
                   jax.ShapeDtypeStruct((1, LANES), jnp.float32)],
        scratch_shapes=[pltpu.VMEM((1, LANES), jnp.float32)],
        compiler_params=_params("arbitrary"),
        name="cross_router",
    )(h1, gc, w_cq, kv_mem, w_co, gf, w_r, b_r)


def _dispatch_kernel(dest_ref, ztail_ref, h_ref, g_ref, xb_ref, t_ref, sem, zsem):
    i = pl.program_id(0)
    tm = h_ref.shape[0]
    slot = i % 2

    @pl.when(i == 0)
    def _():
        t_ref[1] = jnp.zeros(t_ref.shape[1:], t_ref.dtype)

        def zero_copy(e):
            return pltpu.make_async_copy(t_ref.at[1], xb_ref.at[pl.ds(pl.multiple_of(ztail_ref[0, e], tm), tm)],
                                         zsem)

        for e in range(ztail_ref.shape[1]):
            @pl.when(ztail_ref[1, e] > 0)
            def _():
                zero_copy(e).start()
        for e in range(ztail_ref.shape[1]):
            @pl.when(ztail_ref[1, e] > 0)
            def _():
                zero_copy(e).wait()

    h2 = h_ref[...]
    t_ref[slot] = h2 * _rms_scale(h2, NORM_EPS) * g_ref[...]

    for r in range(tm):
        for k in range(2):
            pltpu.make_async_copy(t_ref.at[slot, pl.ds(r, 1)],
                                  xb_ref.at[pl.ds(dest_ref[0, k * tm + r], 1)], sem.at[slot]).start()

    def wait_tile(which):
        for _ in range(2):
            pltpu.make_async_copy(t_ref.at[which], xb_ref.at[pl.ds(0, tm)], sem.at[which]).wait()

    @pl.when(i > 0)
    def _():
        wait_tile(1 - slot)

    @pl.when(i == pl.num_programs(0) - 1)
    def _():
        wait_tile(slot)


def _dispatch(dest3, ztail, h2, gf, p_rows, tm):
    n_tok, d = h2.shape
    return pl.pallas_call(
        _dispatch_kernel,
        grid=(n_tok // tm,),
        in_specs=[pl.BlockSpec((None, 1, 2 * tm), lambda i: (i, 0, 0), memory_space=pltpu.SMEM),
                  pl.BlockSpec(memory_space=pltpu.SMEM),
                  pl.BlockSpec((tm, d), lambda i: (i, 0)),
                  _resident(gf.shape)],
        out_specs=pl.BlockSpec(memory_space=pl.ANY),
        out_shape=jax.ShapeDtypeStruct((p_rows, d), jnp.float32),
        scratch_shapes=[pltpu.VMEM((2, tm, d), jnp.float32), pltpu.SemaphoreType.DMA((2,)),
                        pltpu.SemaphoreType.DMA(())],
        compiler_params=_params("arbitrary"),
        name="moe_dispatch",
    )(dest3, ztail, h2, gf)


def _expert_kernel(be_ref, wsel_ref, nact_ref, x_ref, wg_ref, wu_ref, wd_ref, y_ref, wg_b, wu_b, wd_b):
    del wsel_ref
    i = pl.program_id(0)
    blk = i - 1
    nact = nact_ref[0]
    last_blk = pl.num_programs(0) - 2

    def load_weights():
        wg_b[...] = wg_ref[0].astype(jnp.bfloat16)
        wu_b[...] = wu_ref[0].astype(jnp.bfloat16)
        wd_b[...] = wd_ref[0].astype(jnp.bfloat16)

    @pl.when(i == 0)
    def _():
        load_weights()

    @pl.when((i > 0) & (blk < nact))
    def _():
        xb = x_ref[...].astype(jnp.bfloat16)
        gate = jnp.dot(xb, wg_b[...], preferred_element_type=jnp.float32)
        up = jnp.dot(xb, wu_b[...], preferred_element_type=jnp.float32)
        hid = (gate * _sigmoid(gate) * up).astype(jnp.bfloat16)
        y_ref[...] = jnp.dot(hid, wd_b[...], preferred_element_type=jnp.float32)

        nxt = jnp.minimum(blk + 1, last_blk)

        @pl.when((blk + 1 < nact) & (be_ref[nxt] != be_ref[blk]))
        def _():
            load_weights()

    @pl.when((i > 0) & (blk >= nact))
    def _():
        y_ref[...] = jnp.zeros_like(y_ref)


def _experts(block_expert, weight_sel, n_active, xb, w_gate, w_up, w_down, bm):
    p_rows, d = xb.shape
    de = w_gate.shape[-1]

    def x_map(i, be, wsel, nact):
        return (jnp.clip(i - 1, 0, nact[0] - 1), 0)

    def w_map(i, be, wsel, nact):
        return (wsel[i], 0, 0)

    grid_spec = pltpu.PrefetchScalarGridSpec(
        num_scalar_prefetch=3,
        grid=(p_rows // bm + 1,),
        in_specs=[pl.BlockSpec((bm, d), x_map),
                  pl.BlockSpec((1, d, de), w_map),
                  pl.BlockSpec((1, d, de), w_map),
                  pl.BlockSpec((1, de, d), w_map)],
        out_specs=pl.BlockSpec((bm, d), lambda i, be, wsel, nact: (jnp.maximum(i - 1, 0), 0)),
        scratch_shapes=[pltpu.VMEM((d, de), jnp.bfloat16), pltpu.VMEM((d, de), jnp.bfloat16),
                        pltpu.VMEM((de, d), jnp.bfloat16)],
    )
    return pl.pallas_call(
        _expert_kernel,
        grid_spec=grid_spec,
        out_shape=jax.ShapeDtypeStruct((p_rows, d), jnp.float32),
        compiler_params=_params("arbitrary"),
        name="moe_experts",
    )(block_expert, weight_sel, n_active, xb, w_gate, w_up, w_down)


def _combine_kernel(dest_ref, dest_next_ref, h_ref, wts_ref, g_ref, y_ref, o_ref, ybuf, sem):
    i = pl.program_id(0)
    tm = h_ref.shape[0]
    slot = i % 2

    def gather(idx_ref, which):
        for r in range(tm):
            for k in range(2):
                pltpu.make_async_copy(y_ref.at[pl.ds(idx_ref[0, k * tm + r], 1)],
                                      ybuf.at[which, k, pl.ds(r, 1)], sem.at[which]).start()

    def wait_tile(which):
        for k in range(2):
            pltpu.make_async_copy(y_ref.at[pl.ds(0, tm)], ybuf.at[which, k], sem.at[which]).wait()

    @pl.when(i == 0)
    def _():
        gather(dest_ref, slot)

    wait_tile(slot)
    gather(dest_next_ref, 1 - slot)

    w = wts_ref[...]
    h3 = h_ref[...] + w[:, 0:1] * ybuf[slot, 0] + w[:, 1:2] * ybuf[slot, 1]
    o_ref[...] = h3 * _rms_scale(h3, NORM_EPS) * g_ref[...]

    @pl.when(i == pl.num_programs(0) - 1)
    def _():
        wait_tile(1 - slot)


def _combine(dest3, h2, wts, g_final, y, tm):
    n_tok, d = h2.shape
    last = n_tok // tm - 1
    return pl.pallas_call(
        _combine_kernel,
        grid=(n_tok // tm,),
        in_specs=[pl.BlockSpec((None, 1, 2 * tm), lambda i: (i, 0, 0), memory_space=pltpu.SMEM),
                  pl.BlockSpec((None, 1, 2 * tm), lambda i: (jnp.minimum(i + 1, last), 0, 0),
                               memory_space=pltpu.SMEM),
                  pl.BlockSpec((tm, d), lambda i: (i, 0)),
                  pl.BlockSpec((tm, 2), lambda i: (i, 0)),
                  _resident(g_final.shape),
                  pl.BlockSpec(memory_space=pl.ANY)],
        out_specs=pl.BlockSpec((tm, d), lambda i: (i, 0)),
        out_shape=jax.ShapeDtypeStruct((n_tok, d), jnp.float32),
        scratch_shapes=[pltpu.VMEM((2, 2, tm, d), jnp.float32), pltpu.SemaphoreType.DMA((2,))],
        compiler_params=_params("arbitrary"),
        name="moe_combine",
    )(dest3, dest3, h2, wts, g_final, y)


def _relayout_w_in(w_in):
    c0 = QKV_COLS
    kpe0 = c0 + KPE_COL0
    g0 = kpe0 + MLA_ROPE_DIM
    half = MLA_ROPE_DIM // 2
    kpe = w_in[:, kpe0:g0]
    kpe_swapped = jnp.concatenate([kpe[:, half:], kpe[:, :half]], axis=1)
    pad = jnp.zeros((w_in.shape[0], LATENT_COLS - (g0 - c0) - MLA_ROPE_DIM), w_in.dtype)
    return jnp.concatenate([w_in[:, :g0], kpe_swapped, pad, w_in[:, g0:]], axis=1).astype(jnp.bfloat16)


def _split_w_uq(w_uq):
    half = MLA_ROPE_DIM // 2
    w = w_uq.reshape(MLA_Q_RANK, MLA_HEADS, MLA_QK_DIM).transpose(1, 0, 2)
    pe = w[:, :, MLA_NOPE_DIM:]
    pe_swapped = jnp.concatenate([pe[:, :, half:], pe[:, :, :half]], axis=2)
    return jnp.concatenate([w, pe_swapped], axis=2).astype(jnp.bfloat16)


def _split_w_ukv(w_ukv):
    w = w_ukv.reshape(MLA_KV_RANK, MLA_HEADS, MLA_NOPE_DIM + MLA_V_DIM)
    wuk = w[:, :, :MLA_NOPE_DIM].transpose(1, 0, 2).astype(jnp.bfloat16)
    wuv = w[:, :, MLA_NOPE_DIM:].reshape(MLA_KV_RANK, MLA_HEADS * MLA_V_DIM).astype(jnp.bfloat16)
    return wuk, wuv


def kernel(x, mem, positions, attn_norm_g, w_in, diff_lambda_q1, diff_lambda_k1, diff_lambda_q2, diff_lambda_k2, diff_subln_g, w_o_diff, mla_q_norm_g, w_uq, mla_kv_norm_g, w_ukv, w_o_mla, w_out, cross_norm_g, mem_norm_g, w_cq, w_ckv, w_co, ffn_norm_g, w_router_group, b_router_group, w_router_expert, b_router_expert, w_expert_gate, w_expert_up, w_expert_down, final_norm_g):
    batch, seq, d = x.shape
    assert d == D_MODEL and w_in.shape[0] == 1, "single-layer kernel"
    n_tok = batch * seq
    bf = jnp.bfloat16
    x2 = x.reshape(n_tok, d)

    tm_proj = min(1024, seq)
    tm_row = min(256, seq)
    tm_cross = min(512, seq)
    tq = min(2048, seq)
    rg_diff = 128
    rg_mla = 256

    cos_t, sin_t = _rope_tables(positions, n_tok, tm_proj)

    g_attn = attn_norm_g[0].reshape(1, d)
    qkv, latent, gates = _inproj(x2, g_attn, _relayout_w_in(w_in[0]), cos_t, sin_t, tm_proj)

    o_a = _diff_attn(qkv.reshape(batch, seq, QKV_COLS),
                     diff_lambda_q1[0].reshape(1, -1), diff_lambda_k1[0].reshape(1, -1),
                     diff_lambda_q2[0].reshape(1, -1), diff_lambda_k2[0].reshape(1, -1),
                     diff_subln_g[0].reshape(1, -1), tq, rg_diff)

    wuk, wuv = _split_w_ukv(w_ukv[0])
    q_cat, k_cat, v_mla = _mla_proj(latent, mla_q_norm_g[0].reshape(1, -1), mla_kv_norm_g[0].reshape(1, -1),
                                    _split_w_uq(w_uq[0]), wuk, wuv, cos_t, sin_t, batch, seq, min(512, seq))
    o_b = _mla_attn(q_cat, k_cat, v_mla.reshape(batch, seq, MLA_HEADS * MLA_V_DIM), tq, rg_mla)

    h1 = _merge_out(o_a.reshape(n_tok, -1), o_b.reshape(n_tok, -1), gates, x2,
                    w_o_diff[0].astype(bf), w_o_mla[0].astype(bf), w_out[0].astype(bf), tm_row)

    kv_mem = _mem_kv(mem, mem_norm_g[0].reshape(1, d), w_ckv[0].astype(bf))
    n_router = N_GROUPS + N_EXPERTS
    w_r = jnp.concatenate([w_router_group[0].astype(jnp.float32), w_router_expert[0].astype(jnp.float32),
                           jnp.zeros((d, LANES - n_router), jnp.float32)], axis=1)
    w_r_hi = w_r.astype(bf)
    w_r_lo = (w_r - w_r_hi.astype(jnp.float32)).astype(bf)
    w_r = jnp.concatenate([w_r_hi, w_r_lo], axis=1)
    b_r = jnp.concatenate([b_router_group[0].astype(jnp.float32), b_router_expert[0].astype(jnp.float32),
                           jnp.zeros((LANES - n_router,), jnp.float32)]).reshape(1, LANES)
    g_ffn = ffn_norm_g[0].reshape(1, d)
    h2, eid, rank, wts, cnt = _cross_router(h1, cross_norm_g[0].reshape(1, d), w_cq[0].astype(bf), kv_mem,
                                            w_co[0].astype(bf), g_ffn, w_r, b_r, seq, tm_cross)

    bm = MOE_ROWS_PER_BLOCK
    assert tm_row == bm, "dispatch zeroes whole row blocks from its token-tile scratch"
    counts = cnt[0, ROUTER_EXPERT_LANE0:ROUTER_EXPERT_LANE0 + N_EXPERTS].astype(jnp.int32)
    padded = ((counts + bm - 1) // bm) * bm
    padded_end = jnp.cumsum(padded)
    padded_off = padded_end - padded
    dest = padded_off[eid] + rank
    p_rows = ((2 * n_tok + bm - 1) // bm) * bm + N_EXPERTS * bm
    n_blocks = p_rows // bm
    n_active = (padded_end[-1] // bm).astype(jnp.int32)
    blk = jnp.minimum(jnp.arange(n_blocks, dtype=jnp.int32), n_active - 1)
    block_expert = jnp.sum((padded_end[None, :] <= (blk * bm)[:, None]).astype(jnp.int32), axis=1)
    block_expert = jnp.minimum(block_expert, N_EXPERTS - 1)
    dest3 = dest.reshape(-1, 2, tm_cross // tm_row, tm_row).transpose(0, 2, 1, 3).reshape(
        n_tok // tm_row, 1, 2 * tm_row)
    unused = n_active + jnp.arange(N_EXPERTS, dtype=jnp.int32)
    ztail = jnp.stack([jnp.concatenate([jnp.maximum(padded_end - bm, 0), jnp.minimum(unused, n_blocks - 1) * bm]),
                       jnp.concatenate([padded > 0, unused < n_blocks]).astype(jnp.int32)]).astype(jnp.int32)
    eidx = jnp.arange(N_EXPERTS, dtype=jnp.int32)
    later = (eidx[None, :] > eidx[:, None]) & (padded > 0)[None, :]
    next_expert = jnp.min(jnp.where(later, eidx[None, :], N_EXPERTS), axis=1)
    next_expert = jnp.where(next_expert == N_EXPERTS, eidx, next_expert)
    weight_sel = jnp.concatenate([block_expert[:1], next_expert[block_expert]]).astype(jnp.int32)

    xb = _dispatch(dest3, ztail, h2, g_ffn, p_rows, tm_row)
    y = _experts(block_expert, weight_sel, n_active.reshape(1), xb,
                 w_expert_gate[0], w_expert_up[0], w_expert_down[0], bm)
    out = _combine(dest3, h2, wts, final_norm_g.reshape(1, d), y, tm_row)
    return out.reshape(batch, seq, d)
```

```python
import functools
import math

import jax
import jax.numpy as jnp
from jax import lax
from jax.experimental import pallas as pl
from jax.experimental.pallas import tpu as pltpu

D_MODEL = 2048
ROPE_THETA = 500000.0
NORM_EPS = 1e-6

DIFF_HEADS = 8
DIFF_HEAD_DIM = 64
DIFF_V_DIM = 2 * DIFF_HEAD_DIM
DIFF_ROT = DIFF_HEAD_DIM // 4
DIFF_SUBLN_EPS = 1e-5
DIFF_LAMBDA_INIT = 0.8 - 0.6 * math.exp(-0.3 * 0)

MLA_HEADS = 8
MLA_Q_RANK = 512
MLA_KV_RANK = 256
MLA_NOPE_DIM = 128
MLA_ROPE_DIM = 64
MLA_V_DIM = 128
MLA_QK_DIM = MLA_NOPE_DIM + MLA_ROPE_DIM

CROSS_HEADS = 4
CROSS_HEAD_DIM = 128

N_GROUPS = 4
EXPERTS_PER_GROUP = 8
N_EXPERTS = N_GROUPS * EXPERTS_PER_GROUP
D_EXPERT = 512

LANES = 128
LOG2E = 1.4426950408889634
VMEM_LIMIT_BYTES = 56 * 1024 * 1024

ROUTER_EXPERT_LANE0 = N_GROUPS

QKV_COLS = 3 * DIFF_HEADS * DIFF_V_DIM
LATENT_COLS = 1024
GATE_COLS = 2 * D_MODEL
KPE_COL0 = MLA_Q_RANK + MLA_KV_RANK

MOE_ROWS_PER_BLOCK = 256


def _params(*semantics):
    return pltpu.CompilerParams(dimension_semantics=semantics, vmem_limit_bytes=VMEM_LIMIT_BYTES)


def _resident(shape):
    zeros = (0,) * len(shape)
    return pl.BlockSpec(shape, lambda *_: zeros, pipeline_mode=pl.Buffered(1))


def _rms_scale(xf, eps):
    return lax.rsqrt(jnp.mean(xf * xf, axis=-1, keepdims=True) + eps)


def _sigmoid(x):
    return 0.5 * jnp.tanh(0.5 * x) + 0.5


def _pack_bf16_pair(a, b):
    hi = lax.bitcast_convert_type(a.astype(jnp.bfloat16).astype(jnp.float32), jnp.uint32)
    lo = lax.bitcast_convert_type(b.astype(jnp.bfloat16).astype(jnp.float32), jnp.uint32)
    return hi | (lo >> 16)


def _unpack_bf16_pair(w):
    a = lax.bitcast_convert_type(w & jnp.uint32(0xFFFF0000), jnp.float32)
    b = lax.bitcast_convert_type(w << 16, jnp.float32)
    return a, b


def _lane_iota(shape):
    return lax.broadcasted_iota(jnp.int32, shape, len(shape) - 1)


def _trig_kernel(pos_ref, invf_ref, cos_ref, sin_ref):
    ang = pos_ref[...].astype(jnp.float32) * invf_ref[...]
    cos_ref[...] = jnp.cos(ang)
    sin_ref[...] = jnp.sin(ang)


def _rope_tables(positions, n_tok, tm):
    half_m = MLA_ROPE_DIM // 2
    half_d = DIFF_ROT // 2
    inv_m = jnp.float32(ROPE_THETA) ** (-jnp.arange(half_m, dtype=jnp.float32) * 2.0 / MLA_ROPE_DIM)
    inv_d = jnp.float32(ROPE_THETA) ** (-jnp.arange(half_d, dtype=jnp.float32) * 2.0 / DIFF_ROT)
    invf = jnp.concatenate([inv_m, inv_m, inv_d, inv_d,
                            jnp.zeros((DIFF_HEAD_DIM - DIFF_ROT,), jnp.float32)]).reshape(1, LANES)
    pos = positions.reshape(n_tok, 1)
    return pl.pallas_call(
        _trig_kernel,
        grid=(n_tok // tm,),
        in_specs=[pl.BlockSpec((tm, 1), lambda i: (i, 0)), _resident((1, LANES))],
        out_specs=[pl.BlockSpec((tm, LANES), lambda i: (i, 0))] * 2,
        out_shape=[jax.ShapeDtypeStruct((n_tok, LANES), jnp.float32)] * 2,
        compiler_params=_params("parallel"),
        name="rope_tables",
    )(pos, invf)


def _diff_rope_coeffs(cos_t, sin_t):
    lane = _lane_iota(cos_t.shape)
    upper = lane >= DIFF_HEAD_DIM
    cos_d = jnp.where(upper, cos_t, pltpu.roll(cos_t, DIFF_HEAD_DIM, 1))
    sin_d = jnp.where(upper, sin_t, pltpu.roll(sin_t, DIFF_HEAD_DIM, 1))
    in_head = lane % DIFF_HEAD_DIM
    half = DIFF_ROT // 2
    s_next = jnp.where(in_head < half, -sin_d, 0.0)
    s_prev = jnp.where((in_head >= half) & (in_head < DIFF_ROT), sin_d, 0.0)
    return cos_d, s_next, s_prev


def _mla_rope(pair, cos_t, sin_t):
    lane = _lane_iota(pair.shape)
    sin_signed = jnp.where(lane < MLA_ROPE_DIM // 2, -sin_t, sin_t)
    return pair * cos_t + pltpu.roll(pair, MLA_ROPE_DIM, 1) * sin_signed


INPROJ_TN = 1024
INPROJ_PIECE = 256
Q_TILES = DIFF_HEADS * DIFF_V_DIM // INPROJ_TN
ROPE_TILES = 2 * Q_TILES
QKV_TILES = QKV_COLS // INPROJ_TN
LATENT_TILES = LATENT_COLS // INPROJ_TN
GATE_TILES = GATE_COLS // INPROJ_TN
INPROJ_TILES = QKV_TILES + LATENT_TILES + GATE_TILES


def _inproj_kernel(x_ref, g_ref, w_ref, cos_ref, sin_ref, qkv_ref, lat_ref, gate_ref, xn_ref):
    j = pl.program_id(1)

    @pl.when(j == 0)
    def _():
        xf = x_ref[...]
        xn_ref[...] = (xf * _rms_scale(xf, NORM_EPS) * g_ref[...]).astype(jnp.bfloat16)

    def pieces(epilogue):
        for c in range(INPROJ_TN // INPROJ_PIECE):
            cols = slice(c * INPROJ_PIECE, (c + 1) * INPROJ_PIECE)
            epilogue(jnp.dot(xn_ref[...], w_ref[:, cols], preferred_element_type=jnp.float32), cols)

    @pl.when(j < ROPE_TILES)
    def _():
        cos_d, s_next, s_prev = _diff_rope_coeffs(cos_ref[...], sin_ref[...])
        qscale = jnp.where(j < Q_TILES, DIFF_HEAD_DIM ** -0.5 * LOG2E, 1.0).astype(jnp.float32)

        def rope(acc, cols):
            for c in range(INPROJ_PIECE // LANES):
                xc = acc[:, c * LANES:(c + 1) * LANES]
                rot = (xc * cos_d + pltpu.roll(xc, LANES - DIFF_ROT // 2, 1) * s_next
                       + pltpu.roll(xc, DIFF_ROT // 2, 1) * s_prev)
                lo = cols.start + c * LANES
                qkv_ref[:, lo:lo + LANES] = (rot * qscale).astype(qkv_ref.dtype)

        pieces(rope)

    @pl.when((j >= ROPE_TILES) & (j < QKV_TILES))
    def _():
        def value(acc, cols):
            qkv_ref[:, cols] = acc.astype(qkv_ref.dtype)

        pieces(value)

    @pl.when((j >= QKV_TILES) & (j < QKV_TILES + LATENT_TILES))
    def _():
        def latent(acc, cols):
            lat_ref[:, cols] = acc

        pieces(latent)

    @pl.when(j >= QKV_TILES + LATENT_TILES)
    def _():
        def gate(acc, cols):
            gate_ref[:, cols] = _sigmoid(acc).astype(gate_ref.dtype)

        pieces(gate)


def _inproj(x2, g, w_all, cos_t, sin_t, tm):
    n_tok, d = x2.shape
    tn = INPROJ_TN
    lat0 = QKV_TILES
    gate0 = QKV_TILES + LATENT_TILES
    return pl.pallas_call(
        _inproj_kernel,
        grid=(n_tok // tm, INPROJ_TILES),
        in_specs=[pl.BlockSpec((tm, d), lambda i, j: (i, 0)),
                  _resident((1, d)),
                  pl.BlockSpec((d, tn), lambda i, j: (0, j)),
                  pl.BlockSpec((tm, LANES), lambda i, j: (i, 0)),
                  pl.BlockSpec((tm, LANES), lambda i, j: (i, 0))],
        out_specs=[pl.BlockSpec((tm, tn), lambda i, j: (i, jnp.clip(j, 0, QKV_TILES - 1))),
                   pl.BlockSpec((tm, tn), lambda i, j: (i, jnp.clip(j - lat0, 0, LATENT_TILES - 1))),
                   pl.BlockSpec((tm, tn), lambda i, j: (i, jnp.clip(j - gate0, 0, GATE_TILES - 1)))],
        out_shape=[jax.ShapeDtypeStruct((n_tok, QKV_COLS), jnp.bfloat16),
                   jax.ShapeDtypeStruct((n_tok, LATENT_COLS), jnp.float32),
                   jax.ShapeDtypeStruct((n_tok, GATE_COLS), jnp.bfloat16)],
        scratch_shapes=[pltpu.VMEM((tm, d), jnp.bfloat16)],
        compiler_params=_params("parallel", "arbitrary"),
        name="inproj",
    )(x2, g, w_all, cos_t, sin_t)


def _mla_proj_kernel(c_ref, gq_ref, gkv_ref, wuq_ref, wuk_ref, wuv_ref, cos_ref, sin_ref,
                     q_ref, k_ref, v_ref):
    cos_t = cos_ref[...]
    sin_t = sin_ref[...]
    cq = c_ref[:, :MLA_Q_RANK]
    cqn = (cq * _rms_scale(cq, NORM_EPS) * gq_ref[...]).astype(jnp.bfloat16)
    ckv = c_ref[:, MLA_Q_RANK:KPE_COL0]
    ckvn = (ckv * _rms_scale(ckv, NORM_EPS) * gkv_ref[...]).astype(jnp.bfloat16)
    kpe = _mla_rope(c_ref[:, KPE_COL0:KPE_COL0 + LANES], cos_t, sin_t)[:, :MLA_ROPE_DIM].astype(k_ref.dtype)
    qscale = MLA_QK_DIM ** -0.5 * LOG2E
    for h in range(MLA_HEADS):
        r = jnp.dot(cqn, wuq_ref[h], preferred_element_type=jnp.float32)
        q_ref[0, h, :, :MLA_NOPE_DIM] = (r[:, :MLA_NOPE_DIM] * qscale).astype(q_ref.dtype)
        qpe = _mla_rope(r[:, MLA_NOPE_DIM:], cos_t, sin_t)[:, :MLA_ROPE_DIM]
        q_ref[0, h, :, MLA_NOPE_DIM:] = (qpe * qscale).astype(q_ref.dtype)
        kn = jnp.dot(ckvn, wuk_ref[h], preferred_element_type=jnp.float32)
        k_ref[0, h, :, :MLA_NOPE_DIM] = kn.astype(k_ref.dtype)
        k_ref[0, h, :, MLA_NOPE_DIM:] = kpe
    v_ref[...] = jnp.dot(ckvn, wuv_ref[...], preferred_element_type=jnp.float32).astype(v_ref.dtype)


def _mla_proj(latent, gq, gkv, wuq, wuk, wuv, cos_t, sin_t, batch, seq, tm):
    n_tok = latent.shape[0]
    per_b = seq // tm
    head_spec = pl.BlockSpec((1, MLA_HEADS, tm, MLA_QK_DIM), lambda i: (i // per_b, 0, i % per_b, 0))
    head_shape = jax.ShapeDtypeStruct((batch, MLA_HEADS, seq, MLA_QK_DIM), jnp.bfloat16)
    return pl.pallas_call(
        _mla_proj_kernel,
        grid=(n_tok // tm,),
        in_specs=[pl.BlockSpec((tm, LATENT_COLS), lambda i: (i, 0)),
                  _resident(gq.shape), _resident(gkv.shape),
                  _resident(wuq.shape), _resident(wuk.shape), _resident(wuv.shape),
                  pl.BlockSpec((tm, LANES), lambda i: (i, 0)),
                  pl.BlockSpec((tm, LANES), lambda i: (i, 0))],
        out_specs=[head_spec, head_spec,
                   pl.BlockSpec((tm, MLA_HEADS * MLA_V_DIM), lambda i: (i, 0))],
        out_shape=[head_shape, head_shape,
                   jax.ShapeDtypeStruct((n_tok, MLA_HEADS * MLA_V_DIM), jnp.bfloat16)],
        compiler_params=_params("parallel"),
        name="mla_proj",
    )(latent, gq, gkv, wuq, wuk, wuv, cos_t, sin_t)


def _with_ones(v):
    return jnp.concatenate([v, jnp.ones((v.shape[0], LANES), v.dtype)], axis=-1)


def _softmax_pv(s, v_ones):
    m = jnp.max(s, axis=-1, keepdims=True)
    p = jnp.exp2(s - m).astype(v_ones.dtype)
    pv = jnp.dot(p, v_ones, preferred_element_type=jnp.float32)
    dv = v_ones.shape[1] - LANES
    return pv[:, :dv] / pv[:, dv:]


def _diff_attn_kernel(q_ref, k_ref, v_ref, lq1_ref, lk1_ref, lq2_ref, lk2_ref, g_ref, o_ref, v1_ref, *, rg):
    @pl.when(pl.program_id(2) == 0)
    def _():
        v1_ref[...] = _with_ones(v_ref[0])

    k = k_ref[0]
    v = v1_ref[...]
    lam = (jnp.exp(jnp.sum(lq1_ref[...] * lk1_ref[...], axis=-1, keepdims=True))
           - jnp.exp(jnp.sum(lq2_ref[...] * lk2_ref[...], axis=-1, keepdims=True))
           + DIFF_LAMBDA_INIT)
    lane = _lane_iota((rg, LANES))
    for g in range(q_ref.shape[1] // rg):
        q = q_ref[0, g * rg:(g + 1) * rg]
        zero = jnp.zeros_like(q)
        q12 = jnp.concatenate([jnp.where(lane < DIFF_HEAD_DIM, q, zero),
                               jnp.where(lane >= DIFF_HEAD_DIM, q, zero)], axis=0)
        s = lax.dot_general(q12, k, (((1,), (1,)), ((), ())), preferred_element_type=jnp.float32)
        a = _softmax_pv(s, v)
        o = a[:rg] - lam * a[rg:]
        o = o * _rms_scale(o, DIFF_SUBLN_EPS) * g_ref[...] * (1.0 - DIFF_LAMBDA_INIT)
        o_ref[0, g * rg:(g + 1) * rg] = o.astype(o_ref.dtype)


def _diff_attn(qkv3, lq1, lk1, lq2, lk2, subln_g, tq, rg):
    batch, seq, _ = qkv3.shape
    h = DIFF_HEADS
    return pl.pallas_call(
        functools.partial(_diff_attn_kernel, rg=rg),
        grid=(batch, h, seq // tq),
        in_specs=[pl.BlockSpec((1, tq, LANES), lambda b, hh, i: (b, i, hh)),
                  pl.BlockSpec((1, seq, LANES), lambda b, hh, i: (b, 0, h + hh)),
                  pl.BlockSpec((1, seq, LANES), lambda b, hh, i: (b, 0, 2 * h + hh)),
                  _resident(lq1.shape), _resident(lk1.shape), _resident(lq2.shape), _resident(lk2.shape),
                  _resident(subln_g.shape)],
        out_specs=pl.BlockSpec((1, tq, DIFF_V_DIM), lambda b, hh, i: (b, i, hh)),
        out_shape=jax.ShapeDtypeStruct((batch, seq, h * DIFF_V_DIM), jnp.bfloat16),
        scratch_shapes=[pltpu.VMEM((seq, DIFF_V_DIM + LANES), jnp.bfloat16)],
        compiler_params=_params("parallel", "parallel", "arbitrary"),
        name="diff_attn",
    )(qkv3, qkv3, qkv3, lq1, lk1, lq2, lk2, subln_g)


def _mla_attn_kernel(q_ref, k_ref, v_ref, o_ref, v1_ref, *, rg):
    @pl.when(pl.program_id(2) == 0)
    def _():
        v1_ref[...] = _with_ones(v_ref[0])

    k = k_ref[0, 0]
    v = v1_ref[...]
    for g in range(q_ref.shape[2] // rg):
        s = lax.dot_general(q_ref[0, 0, g * rg:(g + 1) * rg], k, (((1,), (1,)), ((), ())),
                            preferred_element_type=jnp.float32)
        o_ref[0, g * rg:(g + 1) * rg] = _softmax_pv(s, v).astype(o_ref.dtype)


def _mla_attn(q_cat, k_cat, v3, tq, rg):
    batch, heads, seq, dqk = q_cat.shape
    return pl.pallas_call(
        functools.partial(_mla_attn_kernel, rg=rg),
        grid=(batch, heads, seq // tq),
        in_specs=[pl.BlockSpec((1, 1, tq, dqk), lambda b, h, i: (b, h, i, 0)),
                  pl.BlockSpec((1, 1, seq, dqk), lambda b, h, i: (b, h, 0, 0)),
                  pl.BlockSpec((1, seq, MLA_V_DIM), lambda b, h, i: (b, 0, h))],
        out_specs=pl.BlockSpec((1, tq, MLA_V_DIM), lambda b, h, i: (b, i, h)),
        out_shape=jax.ShapeDtypeStruct((batch, seq, heads * MLA_V_DIM), jnp.bfloat16),
        scratch_shapes=[pltpu.VMEM((seq, MLA_V_DIM + LANES), jnp.bfloat16)],
        compiler_params=_params("parallel", "parallel", "arbitrary"),
        name="mla_attn",
    )(q_cat, k_cat, v3)


def _merge_out_kernel(oa_ref, ob_ref, sga_ref, sgb_ref, x_ref, woa_ref, wob_ref, wout_ref, h_ref):
    ya = jnp.dot(oa_ref[...], woa_ref[...], preferred_element_type=jnp.float32)
    yb = jnp.dot(ob_ref[...], wob_ref[...], preferred_element_type=jnp.float32)
    merged = sga_ref[...].astype(jnp.float32) * ya + sgb_ref[...].astype(jnp.float32) * yb
    h_ref[...] = x_ref[...] + jnp.dot(merged.astype(jnp.bfloat16), wout_ref[...],
                                       preferred_element_type=jnp.float32)


def _merge_out(o_a, o_b, gates, x2, w_oa, w_ob, w_out, tm):
    n_tok, d = x2.shape
    return pl.pallas_call(
        _merge_out_kernel,
        grid=(n_tok // tm,),
        in_specs=[pl.BlockSpec((tm, o_a.shape[1]), lambda i: (i, 0)),
                  pl.BlockSpec((tm, o_b.shape[1]), lambda i: (i, 0)),
                  pl.BlockSpec((tm, d), lambda i: (i, 0)),
                  pl.BlockSpec((tm, d), lambda i: (i, 1)),
                  pl.BlockSpec((tm, d), lambda i: (i, 0)),
                  _resident(w_oa.shape), _resident(w_ob.shape), _resident(w_out.shape)],
        out_specs=pl.BlockSpec((tm, d), lambda i: (i, 0)),
        out_shape=jax.ShapeDtypeStruct((n_tok, d), jnp.float32),
        compiler_params=_params("parallel"),
        name="merge_out",
    )(o_a, o_b, gates, gates, x2, w_oa, w_ob, w_out)


def _mem_kv_kernel(mem_ref, g_ref, w_ref, kv_ref):
    mf = mem_ref[0]
    mn = (mf * _rms_scale(mf, NORM_EPS) * g_ref[...]).astype(jnp.bfloat16)
    kv_ref[0] = jnp.dot(mn, w_ref[...], preferred_element_type=jnp.float32).astype(kv_ref.dtype)


def _mem_kv(mem, g, w_ckv):
    batch, m, d = mem.shape
    return pl.pallas_call(
        _mem_kv_kernel,
        grid=(batch,),
        in_specs=[pl.BlockSpec((1, m, d), lambda b: (b, 0, 0)), _resident(g.shape), _resident(w_ckv.shape)],
        out_specs=pl.BlockSpec((1, m, w_ckv.shape[1]), lambda b: (b, 0, 0)),
        out_shape=jax.ShapeDtypeStruct((batch, m, w_ckv.shape[1]), jnp.bfloat16),
        compiler_params=_params("parallel"),
        name="mem_kv",
    )(mem, g, w_ckv)


def _cross_router_kernel(h_ref, gc_ref, wcq_ref, kv_ref, wco_ref, gf_ref, wr_ref, br_ref,
                         h2_ref, eid_ref, rank_ref, wts_ref, cnt_ref, carry_ref):
    i = pl.program_id(0)

    @pl.when(i == 0)
    def _():
        carry_ref[...] = jnp.zeros_like(carry_ref)

    h1 = h_ref[...]
    tm = h1.shape[0]
    hn = (h1 * _rms_scale(h1, NORM_EPS) * gc_ref[...]).astype(jnp.bfloat16)
    q = jnp.dot(hn, wcq_ref[...], preferred_element_type=jnp.float32) * (CROSS_HEAD_DIM ** -0.5 * LOG2E)
    q = q.astype(jnp.bfloat16)
    kv_cols = CROSS_HEADS * CROSS_HEAD_DIM
    heads = []
    for hd in range(CROSS_HEADS):
        lo = hd * CROSS_HEAD_DIM
        kh = kv_ref[0, :, lo:lo + CROSS_HEAD_DIM]
        vh = kv_ref[0, :, kv_cols + lo:kv_cols + lo + CROSS_HEAD_DIM]
        s = lax.dot_general(q[:, lo:lo + CROSS_HEAD_DIM], kh, (((1,), (1,)), ((), ())),
                            preferred_element_type=jnp.float32)
        heads.append(_softmax_pv(s, _with_ones(vh)).astype(jnp.bfloat16))
    o = jnp.concatenate(heads, axis=-1)
    h2 = h1 + jnp.dot(o, wco_ref[...], preferred_element_type=jnp.float32)
    h2_ref[...] = h2

    t = h2 * _rms_scale(h2, NORM_EPS) * gf_ref[...]
    t_hi = t.astype(jnp.bfloat16)
    t_lo = (t - t_hi.astype(jnp.float32)).astype(jnp.bfloat16)
    hi = jnp.dot(t_hi, wr_ref[...], preferred_element_type=jnp.float32)
    lo = jnp.dot(t_lo, wr_ref[:, :LANES], preferred_element_type=jnp.float32)
    logits = hi[:, :LANES] + (hi[:, LANES:] + lo) + br_ref[...]
    lane = _lane_iota(logits.shape)
    neg = jnp.float32(-jnp.inf)
    big = jnp.int32(2 * LANES)
    is_group = lane < N_GROUPS
    lg = jnp.where(is_group, logits, neg)
    mg = jnp.max(lg, axis=-1, keepdims=True)
    g_idx = jnp.min(jnp.where(is_group & (logits == mg), lane, big), axis=-1, keepdims=True)
    g_p = 1.0 / jnp.sum(jnp.exp(lg - mg), axis=-1, keepdims=True)
    lo_lane = ROUTER_EXPERT_LANE0 + EXPERTS_PER_GROUP * g_idx
    in_grp = (lane >= lo_lane) & (lane < lo_lane + EXPERTS_PER_GROUP)
    l1 = jnp.max(jnp.where(in_grp, logits, neg), axis=-1, keepdims=True)
    i1 = jnp.min(jnp.where(in_grp & (logits == l1), lane, big), axis=-1, keepdims=True)
    rest = in_grp & (lane != i1)
    l2 = jnp.max(jnp.where(rest, logits, neg), axis=-1, keepdims=True)
    i2 = jnp.min(jnp.where(rest & (logits == l2), lane, big), axis=-1, keepdims=True)
    d = jnp.exp(l2 - l1)
    w1 = g_p / (1.0 + d)
    w2 = w1 * d

    oh1 = lane == i1
    oh2 = lane == i2
    cnt = (oh1 | oh2).astype(jnp.bfloat16)
    row = lax.broadcasted_iota(jnp.int32, (tm, tm), 0)
    col = lax.broadcasted_iota(jnp.int32, (tm, tm), 1)
    before = (col < row).astype(jnp.bfloat16)
    slot = jnp.dot(before, cnt, preferred_element_type=jnp.float32) + carry_ref[...]
    r1 = jnp.sum(jnp.where(oh1, slot, 0.0), axis=-1, keepdims=True)
    r2 = jnp.sum(jnp.where(oh2, slot, 0.0), axis=-1, keepdims=True)
    carry_ref[...] += jnp.sum(cnt.astype(jnp.float32), axis=0, keepdims=True)
    cnt_ref[...] = carry_ref[...]

    eye = row == col

    def to_row(c, dtype):
        return jnp.sum(jnp.where(eye, c.astype(jnp.float32), 0.0), axis=0, keepdims=True).astype(dtype)

    eid_ref[0] = jnp.concatenate([to_row(i1 - ROUTER_EXPERT_LANE0, jnp.int32),
                                  to_row(i2 - ROUTER_EXPERT_LANE0, jnp.int32)], axis=0)
    rank_ref[0] = jnp.concatenate([to_row(r1, jnp.int32), to_row(r2, jnp.int32)], axis=0)
    wts_ref[...] = jnp.where(_lane_iota((tm, 2)) == 0, w1, w2)


def _cross_router(h1, gc, w_cq, kv_mem, w_co, gf, w_r, b_r, seq, tm):
    n_tok, d = h1.shape
    per_b = seq // tm
    row2 = pl.BlockSpec((tm, 2), lambda i: (i, 0))
    lane2 = pl.BlockSpec((1, 2, tm), lambda i: (i, 0, 0))
    return pl.pallas_call(
        _cross_router_kernel,
        grid=(n_tok // tm,),
        in_specs=[pl.BlockSpec((tm, d), lambda i: (i, 0)),
                  _resident(gc.shape), _resident(w_cq.shape),
                  pl.BlockSpec((1,) + kv_mem.shape[1:], lambda i: (i // per_b, 0, 0)),
                  _resident(w_co.shape), _resident(gf.shape), _resident(w_r.shape), _resident(b_r.shape)],
        out_specs=[pl.BlockSpec((tm, d), lambda i: (i, 0)), lane2, lane2, row2,
                   pl.BlockSpec((1, LANES), lambda i: (0, 0))],
        out_shape=[jax.ShapeDtypeStruct((n_tok, d), jnp.float32),
                   jax.ShapeDtypeStruct((n_tok // tm, 2, tm), jnp.int32),
                   jax.ShapeDtypeStruct((n_tok // tm, 2, tm), jnp.int32),
                   jax.ShapeDtypeStruct((n_tok, 2), jnp.float32),
                   jax.ShapeDtypeStruct((1, LANES), jnp.float32)],
        scratch_shapes=[pltpu.VMEM((1, LANES), jnp.float32)],
        compiler_params=_params("arbitrary"),
        name="cross_router",
    )(h1, gc, w_cq, kv_mem, w_co, gf, w_r, b_r)


def _dispatch_kernel(dest_ref, ztail_ref, h_ref, g_ref, xb_ref, t_ref, sem, zsem):
    i = pl.program_id(0)
    tm = h_ref.shape[0]
    slot = i % 2

    @pl.when(i == 0)
    def _():
        t_ref[1] = jnp.zeros(t_ref.shape[1:], t_ref.dtype)

        def zero_copy(e):
            return pltpu.make_async_copy(t_ref.at[1], xb_ref.at[pl.ds(pl.multiple_of(ztail_ref[0, e], tm), tm)],
                                         zsem)

        for e in range(ztail_ref.shape[1]):
            @pl.when(ztail_ref[1, e] > 0)
            def _():
                zero_copy(e).start()
        for e in range(ztail_ref.shape[1]):
            @pl.when(ztail_ref[1, e] > 0)
            def _():
                zero_copy(e).wait()

    h2 = h_ref[...]
    t = h2 * _rms_scale(h2, NORM_EPS) * g_ref[...]
    half = t.shape[1] // 2
    t_ref[slot] = _pack_bf16_pair(t[:, :half], t[:, half:])

    for r in range(tm):
        for k in range(2):
            pltpu.make_async_copy(t_ref.at[slot, pl.ds(r, 1)],
                                  xb_ref.at[pl.ds(dest_ref[0, k * tm + r], 1)], sem.at[slot]).start()

    def wait_tile(which):
        for _ in range(2):
            pltpu.make_async_copy(t_ref.at[which], xb_ref.at[pl.ds(0, tm)], sem.at[which]).wait()

    @pl.when(i > 0)
    def _():
        wait_tile(1 - slot)

    @pl.when(i == pl.num_programs(0) - 1)
    def _():
        wait_tile(slot)


def _dispatch(dest3, ztail, h2, gf, p_rows, tm):
    n_tok, d = h2.shape
    return pl.pallas_call(
        _dispatch_kernel,
        grid=(n_tok // tm,),
        in_specs=[pl.BlockSpec((None, 1, 2 * tm), lambda i: (i, 0, 0), memory_space=pltpu.SMEM),
                  pl.BlockSpec(memory_space=pltpu.SMEM),
                  pl.BlockSpec((tm, d), lambda i: (i, 0)),
                  _resident(gf.shape)],
        out_specs=pl.BlockSpec(memory_space=pl.ANY),
        out_shape=jax.ShapeDtypeStruct((p_rows, d // 2), jnp.uint32),
        scratch_shapes=[pltpu.VMEM((2, tm, d // 2), jnp.uint32), pltpu.SemaphoreType.DMA((2,)),
                        pltpu.SemaphoreType.DMA(())],
        compiler_params=_params("arbitrary"),
        name="moe_dispatch",
    )(dest3, ztail, h2, gf)


def _expert_kernel(be_ref, wsel_ref, nact_ref, x_ref, wg_ref, wu_ref, wd_ref, y_ref, wg_b, wu_b, wd_b):
    del wsel_ref
    i = pl.program_id(0)
    blk = i - 1
    nact = nact_ref[0]
    last_blk = pl.num_programs(0) - 2

    def load_weights():
        wg_b[...] = wg_ref[0].astype(jnp.bfloat16)
        wu_b[...] = wu_ref[0].astype(jnp.bfloat16)
        wd_b[...] = wd_ref[0].astype(jnp.bfloat16)

    @pl.when(i == 0)
    def _():
        load_weights()

    @pl.when((i > 0) & (blk < nact))
    def _():
        x_a, x_b = _unpack_bf16_pair(x_ref[...])
        xb = jnp.concatenate([x_a.astype(jnp.bfloat16), x_b.astype(jnp.bfloat16)], axis=1)
        gate = jnp.dot(xb, wg_b[...], preferred_element_type=jnp.float32)
        up = jnp.dot(xb, wu_b[...], preferred_element_type=jnp.float32)
        hid = (gate * _sigmoid(gate) * up).astype(jnp.bfloat16)
        y = jnp.dot(hid, wd_b[...], preferred_element_type=jnp.float32)
        half = y.shape[1] // 2
        y_ref[...] = _pack_bf16_pair(y[:, :half], y[:, half:])

        nxt = jnp.minimum(blk + 1, last_blk)

        @pl.when((blk + 1 < nact) & (be_ref[nxt] != be_ref[blk]))
        def _():
            load_weights()

    @pl.when((i > 0) & (blk >= nact))
    def _():
        y_ref[...] = jnp.zeros_like(y_ref)


def _experts(block_expert, weight_sel, n_active, xb, w_gate, w_up, w_down, bm):
    p_rows, dp = xb.shape
    d = w_gate.shape[1]
    de = w_gate.shape[-1]

    def x_map(i, be, wsel, nact):
        return (jnp.clip(i - 1, 0, nact[0] - 1), 0)

    def w_map(i, be, wsel, nact):
        return (wsel[i], 0, 0)

    grid_spec = pltpu.PrefetchScalarGridSpec(
        num_scalar_prefetch=3,
        grid=(p_rows // bm + 1,),
        in_specs=[pl.BlockSpec((bm, dp), x_map),
                  pl.BlockSpec((1, d, de), w_map),
                  pl.BlockSpec((1, d, de), w_map),
                  pl.BlockSpec((1, de, d), w_map)],
        out_specs=pl.BlockSpec((bm, dp), lambda i, be, wsel, nact: (jnp.maximum(i - 1, 0), 0)),
        scratch_shapes=[pltpu.VMEM((d, de), jnp.bfloat16), pltpu.VMEM((d, de), jnp.bfloat16),
                        pltpu.VMEM((de, d), jnp.bfloat16)],
    )
    return pl.pallas_call(
        _expert_kernel,
        grid_spec=grid_spec,
        out_shape=jax.ShapeDtypeStruct((p_rows, dp), jnp.uint32),
        compiler_params=_params("arbitrary"),
        name="moe_experts",
    )(block_expert, weight_sel, n_active, xb, w_gate, w_up, w_down)


def _combine_kernel(dest_ref, dest_next_ref, h_ref, wts_ref, g_ref, y_ref, o_ref, ybuf, sem):
    i = pl.program_id(0)
    tm = h_ref.shape[0]
    slot = i % 2

    def gather(idx_ref, which):
        for r in range(tm):
            for k in range(2):
                pltpu.make_async_copy(y_ref.at[pl.ds(idx_ref[0, k * tm + r], 1)],
                                      ybuf.at[which, k, pl.ds(r, 1)], sem.at[which]).start()

    def wait_tile(which):
        for k in range(2):
            pltpu.make_async_copy(y_ref.at[pl.ds(0, tm)], ybuf.at[which, k], sem.at[which]).wait()

    @pl.when(i == 0)
    def _():
        gather(dest_ref, slot)

    wait_tile(slot)
    gather(dest_next_ref, 1 - slot)

    w = wts_ref[...]
    y0 = jnp.concatenate(_unpack_bf16_pair(ybuf[slot, 0]), axis=1)
    y1 = jnp.concatenate(_unpack_bf16_pair(ybuf[slot, 1]), axis=1)
    h3 = h_ref[...] + w[:, 0:1] * y0 + w[:, 1:2] * y1
    o_ref[...] = h3 * _rms_scale(h3, NORM_EPS) * g_ref[...]

    @pl.when(i == pl.num_programs(0) - 1)
    def _():
        wait_tile(1 - slot)


def _combine(dest3, h2, wts, g_final, y, tm):
    n_tok, d = h2.shape
    last = n_tok // tm - 1
    return pl.pallas_call(
        _combine_kernel,
        grid=(n_tok // tm,),
        in_specs=[pl.BlockSpec((None, 1, 2 * tm), lambda i: (i, 0, 0), memory_space=pltpu.SMEM),
                  pl.BlockSpec((None, 1, 2 * tm), lambda i: (jnp.minimum(i + 1, last), 0, 0),
                               memory_space=pltpu.SMEM),
                  pl.BlockSpec((tm, d), lambda i: (i, 0)),
                  pl.BlockSpec((tm, 2), lambda i: (i, 0)),
                  _resident(g_final.shape),
                  pl.BlockSpec(memory_space=pl.ANY)],
        out_specs=pl.BlockSpec((tm, d), lambda i: (i, 0)),
        out_shape=jax.ShapeDtypeStruct((n_tok, d), jnp.float32),
        scratch_shapes=[pltpu.VMEM((2, 2, tm, d // 2), jnp.uint32), pltpu.SemaphoreType.DMA((2,))],
        compiler_params=_params("arbitrary"),
        name="moe_combine",
    )(dest3, dest3, h2, wts, g_final, y)


def _relayout_w_in(w_in):
    c0 = QKV_COLS
    kpe0 = c0 + KPE_COL0
    g0 = kpe0 + MLA_ROPE_DIM
    half = MLA_ROPE_DIM // 2
    kpe = w_in[:, kpe0:g0]
    kpe_swapped = jnp.concatenate([kpe[:, half:], kpe[:, :half]], axis=1)
    pad = jnp.zeros((w_in.shape[0], LATENT_COLS - (g0 - c0) - MLA_ROPE_DIM), w_in.dtype)
    return jnp.concatenate([w_in[:, :g0], kpe_swapped, pad, w_in[:, g0:]], axis=1).astype(jnp.bfloat16)


def _split_w_uq(w_uq):
    half = MLA_ROPE_DIM // 2
    w = w_uq.reshape(MLA_Q_RANK, MLA_HEADS, MLA_QK_DIM).transpose(1, 0, 2)
    pe = w[:, :, MLA_NOPE_DIM:]
    pe_swapped = jnp.concatenate([pe[:, :, half:], pe[:, :, :half]], axis=2)
    return jnp.concatenate([w, pe_swapped], axis=2).astype(jnp.bfloat16)


def _split_w_ukv(w_ukv):
    w = w_ukv.reshape(MLA_KV_RANK, MLA_HEADS, MLA_NOPE_DIM + MLA_V_DIM)
    wuk = w[:, :, :MLA_NOPE_DIM].transpose(1, 0, 2).astype(jnp.bfloat16)
    wuv = w[:, :, MLA_NOPE_DIM:].reshape(MLA_KV_RANK, MLA_HEADS * MLA_V_DIM).astype(jnp.bfloat16)
    return wuk, wuv


def kernel(x, mem, positions, attn_norm_g, w_in, diff_lambda_q1, diff_lambda_k1, diff_lambda_q2, diff_lambda_k2, diff_subln_g, w_o_diff, mla_q_norm_g, w_uq, mla_kv_norm_g, w_ukv, w_o_mla, w_out, cross_norm_g, mem_norm_g, w_cq, w_ckv, w_co, ffn_norm_g, w_router_group, b_router_group, w_router_expert, b_router_expert, w_expert_gate, w_expert_up, w_expert_down, final_norm_g):
    batch, seq, d = x.shape
    assert d == D_MODEL and w_in.shape[0] == 1, "single-layer kernel"
    n_tok = batch * seq
    bf = jnp.bfloat16
    x2 = x.reshape(n_tok, d)

    tm_proj = min(1024, seq)
    tm_row = min(256, seq)
    tm_cross = min(512, seq)
    tq = min(2048, seq)
    rg_diff = 128
    rg_mla = 256

    cos_t, sin_t = _rope_tables(positions, n_tok, tm_proj)

    g_attn = attn_norm_g[0].reshape(1, d)
    qkv, latent, gates = _inproj(x2, g_attn, _relayout_w_in(w_in[0]), cos_t, sin_t, tm_proj)

    o_a = _diff_attn(qkv.reshape(batch, seq, QKV_COLS),
                     diff_lambda_q1[0].reshape(1, -1), diff_lambda_k1[0].reshape(1, -1),
                     diff_lambda_q2[0].reshape(1, -1), diff_lambda_k2[0].reshape(1, -1),
                     diff_subln_g[0].reshape(1, -1), tq, rg_diff)

    wuk, wuv = _split_w_ukv(w_ukv[0])
    q_cat, k_cat, v_mla = _mla_proj(latent, mla_q_norm_g[0].reshape(1, -1), mla_kv_norm_g[0].reshape(1, -1),
                                    _split_w_uq(w_uq[0]), wuk, wuv, cos_t, sin_t, batch, seq, min(512, seq))
    o_b = _mla_attn(q_cat, k_cat, v_mla.reshape(batch, seq, MLA_HEADS * MLA_V_DIM), tq, rg_mla)

    h1 = _merge_out(o_a.reshape(n_tok, -1), o_b.reshape(n_tok, -1), gates, x2,
                    w_o_diff[0].astype(bf), w_o_mla[0].astype(bf), w_out[0].astype(bf), tm_row)

    kv_mem = _mem_kv(mem, mem_norm_g[0].reshape(1, d), w_ckv[0].astype(bf))
    n_router = N_GROUPS + N_EXPERTS
    w_r = jnp.concatenate([w_router_group[0].astype(jnp.float32), w_router_expert[0].astype(jnp.float32),
                           jnp.zeros((d, LANES - n_router), jnp.float32)], axis=1)
    w_r_hi = w_r.astype(bf)
    w_r_lo = (w_r - w_r_hi.astype(jnp.float32)).astype(bf)
    w_r = jnp.concatenate([w_r_hi, w_r_lo], axis=1)
    b_r = jnp.concatenate([b_router_group[0].astype(jnp.float32), b_router_expert[0].astype(jnp.float32),
                           jnp.zeros((LANES - n_router,), jnp.float32)]).reshape(1, LANES)
    g_ffn = ffn_norm_g[0].reshape(1, d)
    h2, eid, rank, wts, cnt = _cross_router(h1, cross_norm_g[0].reshape(1, d), w_cq[0].astype(bf), kv_mem,
                                            w_co[0].astype(bf), g_ffn, w_r, b_r, seq, tm_cross)

    bm = MOE_ROWS_PER_BLOCK
    assert tm_row == bm, "dispatch zeroes whole row blocks from its token-tile scratch"
    counts = cnt[0, ROUTER_EXPERT_LANE0:ROUTER_EXPERT_LANE0 + N_EXPERTS].astype(jnp.int32)
    padded = ((counts + bm - 1) // bm) * bm
    padded_end = jnp.cumsum(padded)
    padded_off = padded_end - padded
    seg_start = jnp.sum(jnp.where(eid[..., None] == jnp.arange(N_EXPERTS, dtype=jnp.int32), padded_off, 0), axis=-1)
    dest = seg_start + rank
    p_rows = ((2 * n_tok + bm - 1) // bm) * bm + N_EXPERTS * bm
    n_blocks = p_rows // bm
    n_active = (padded_end[-1] // bm).astype(jnp.int32)
    blk = jnp.minimum(jnp.arange(n_blocks, dtype=jnp.int32), n_active - 1)
    block_expert = jnp.sum((padded_end[None, :] <= (blk * bm)[:, None]).astype(jnp.int32), axis=1)
    block_expert = jnp.minimum(block_expert, N_EXPERTS - 1)
    dest3 = dest.reshape(-1, 2, tm_cross // tm_row, tm_row).transpose(0, 2, 1, 3).reshape(
        n_tok // tm_row, 1, 2 * tm_row)
    unused = n_active + jnp.arange(N_EXPERTS, dtype=jnp.int32)
    ztail = jnp.stack([jnp.concatenate([jnp.maximum(padded_end - bm, 0), jnp.minimum(unused, n_blocks - 1) * bm]),
                       jnp.concatenate([padded > 0, unused < n_blocks]).astype(jnp.int32)]).astype(jnp.int32)
    eidx = jnp.arange(N_EXPERTS, dtype=jnp.int32)
    later = (eidx[None, :] > eidx[:, None]) & (padded > 0)[None, :]
    next_expert = jnp.min(jnp.where(later, eidx[None, :], N_EXPERTS), axis=1)
    next_expert = jnp.where(next_expert == N_EXPERTS, eidx, next_expert)
    weight_sel = jnp.concatenate([block_expert[:1], next_expert[block_expert]]).astype(jnp.int32)

    xb = _dispatch(dest3, ztail, h2, g_ffn, p_rows, tm_row)
    y = _experts(block_expert, weight_sel, n_active.reshape(1), xb,
                 w_expert_gate[0], w_expert_up[0], w_expert_down[0], bm)
    out = _combine(dest3, h2, wts, final_norm_g.reshape(1, d), y, tm_row)
    return out.reshape(batch, seq, d)
```

```python
import functools
import math

import jax
import jax.numpy as jnp
from jax import lax
from jax.experimental import pallas as pl
from jax.experimental.pallas import tpu as pltpu

D_MODEL = 2048
ROPE_THETA = 500000.0
NORM_EPS = 1e-6

DIFF_HEADS = 8
DIFF_HEAD_DIM = 64
DIFF_V_DIM = 2 * DIFF_HEAD_DIM
DIFF_ROT = DIFF_HEAD_DIM // 4
DIFF_SUBLN_EPS = 1e-5
DIFF_LAMBDA_INIT = 0.8 - 0.6 * math.exp(-0.3 * 0)

MLA_HEADS = 8
MLA_Q_RANK = 512
MLA_KV_RANK = 256
MLA_NOPE_DIM = 128
MLA_ROPE_DIM = 64
MLA_V_DIM = 128
MLA_QK_DIM = MLA_NOPE_DIM + MLA_ROPE_DIM

CROSS_HEADS = 4
CROSS_HEAD_DIM = 128

N_GROUPS = 4
EXPERTS_PER_GROUP = 8
N_EXPERTS = N_GROUPS * EXPERTS_PER_GROUP
D_EXPERT = 512

LANES = 128
LOG2E = 1.4426950408889634
VMEM_LIMIT_BYTES = 56 * 1024 * 1024

ROUTER_EXPERT_LANE0 = N_GROUPS

QKV_COLS = 3 * DIFF_HEADS * DIFF_V_DIM
LATENT_COLS = 1024
GATE_COLS = 2 * D_MODEL
KPE_COL0 = MLA_Q_RANK + MLA_KV_RANK

MOE_ROWS_PER_BLOCK = 256


def _params(*semantics):
    return pltpu.CompilerParams(dimension_semantics=semantics, vmem_limit_bytes=VMEM_LIMIT_BYTES)


def _resident(shape):
    zeros = (0,) * len(shape)
    return pl.BlockSpec(shape, lambda *_: zeros, pipeline_mode=pl.Buffered(1))


def _rms_scale(xf, eps):
    return lax.rsqrt(jnp.mean(xf * xf, axis=-1, keepdims=True) + eps)


def _sigmoid(x):
    return 0.5 * jnp.tanh(0.5 * x) + 0.5


def _pack_bf16_pair(a, b):
    hi = lax.bitcast_convert_type(a.astype(jnp.bfloat16).astype(jnp.float32), jnp.uint32)
    lo = lax.bitcast_convert_type(b.astype(jnp.bfloat16).astype(jnp.float32), jnp.uint32)
    return hi | (lo >> 16)


def _unpack_bf16_pair(w):
    a = lax.bitcast_convert_type(w & jnp.uint32(0xFFFF0000), jnp.float32)
    b = lax.bitcast_convert_type(w << 16, jnp.float32)
    return a, b


def _lane_iota(shape):
    return lax.broadcasted_iota(jnp.int32, shape, len(shape) - 1)


def _trig_kernel(pos_ref, invf_ref, cos_ref, sin_ref):
    ang = pos_ref[...].astype(jnp.float32) * invf_ref[...]
    cos_ref[...] = jnp.cos(ang)
    sin_ref[...] = jnp.sin(ang)


def _rope_tables(positions, n_tok, tm):
    half_m = MLA_ROPE_DIM // 2
    half_d = DIFF_ROT // 2
    inv_m = jnp.float32(ROPE_THETA) ** (-jnp.arange(half_m, dtype=jnp.float32) * 2.0 / MLA_ROPE_DIM)
    inv_d = jnp.float32(ROPE_THETA) ** (-jnp.arange(half_d, dtype=jnp.float32) * 2.0 / DIFF_ROT)
    invf = jnp.concatenate([inv_m, inv_m, inv_d, inv_d,
                            jnp.zeros((DIFF_HEAD_DIM - DIFF_ROT,), jnp.float32)]).reshape(1, LANES)
    pos = positions.reshape(n_tok, 1)
    return pl.pallas_call(
        _trig_kernel,
        grid=(n_tok // tm,),
        in_specs=[pl.BlockSpec((tm, 1), lambda i: (i, 0)), _resident((1, LANES))],
        out_specs=[pl.BlockSpec((tm, LANES), lambda i: (i, 0))] * 2,
        out_shape=[jax.ShapeDtypeStruct((n_tok, LANES), jnp.float32)] * 2,
        compiler_params=_params("parallel"),
        name="rope_tables",
    )(pos, invf)


def _diff_rope_coeffs(cos_t, sin_t):
    lane = _lane_iota(cos_t.shape)
    upper = lane >= DIFF_HEAD_DIM
    cos_d = jnp.where(upper, cos_t, pltpu.roll(cos_t, DIFF_HEAD_DIM, 1))
    sin_d = jnp.where(upper, sin_t, pltpu.roll(sin_t, DIFF_HEAD_DIM, 1))
    in_head = lane % DIFF_HEAD_DIM
    half = DIFF_ROT // 2
    s_next = jnp.where(in_head < half, -sin_d, 0.0)
    s_prev = jnp.where((in_head >= half) & (in_head < DIFF_ROT), sin_d, 0.0)
    return cos_d, s_next, s_prev


def _mla_rope(pair, cos_t, sin_t):
    lane = _lane_iota(pair.shape)
    sin_signed = jnp.where(lane < MLA_ROPE_DIM // 2, -sin_t, sin_t)
    return pair * cos_t + pltpu.roll(pair, MLA_ROPE_DIM, 1) * sin_signed


INPROJ_TN = 1024
INPROJ_PIECE = 256
Q_TILES = DIFF_HEADS * DIFF_V_DIM // INPROJ_TN
ROPE_TILES = 2 * Q_TILES
QKV_TILES = QKV_COLS // INPROJ_TN
LATENT_TILES = LATENT_COLS // INPROJ_TN
GATE_TILES = GATE_COLS // INPROJ_TN
INPROJ_TILES = QKV_TILES + LATENT_TILES + GATE_TILES
GATE_ROW0 = QKV_COLS + KPE_COL0 + MLA_ROPE_DIM


def _inproj_kernel(x_ref, g_ref, w_ref, cos_ref, sin_ref, qkv_ref, lat_ref, gate_ref, xn_ref):
    j = pl.program_id(1)

    @pl.when(j == 0)
    def _():
        xf = x_ref[...]
        xn_ref[...] = (xf * _rms_scale(xf, NORM_EPS) * g_ref[...]).astype(jnp.bfloat16)

    def pieces(epilogue):
        for c in range(INPROJ_TN // INPROJ_PIECE):
            cols = slice(c * INPROJ_PIECE, (c + 1) * INPROJ_PIECE)
            acc = lax.dot_general(xn_ref[...], w_ref[cols, :], (((1,), (1,)), ((), ())),
                                  preferred_element_type=jnp.float32)
            epilogue(acc, cols)

    @pl.when(j < ROPE_TILES)
    def _():
        cos_d, s_next, s_prev = _diff_rope_coeffs(cos_ref[...], sin_ref[...])
        qscale = jnp.where(j < Q_TILES, DIFF_HEAD_DIM ** -0.5 * LOG2E, 1.0).astype(jnp.float32)

        def rope(acc, cols):
            for c in range(INPROJ_PIECE // LANES):
                xc = acc[:, c * LANES:(c + 1) * LANES]
                rot = (xc * cos_d + pltpu.roll(xc, LANES - DIFF_ROT // 2, 1) * s_next
                       + pltpu.roll(xc, DIFF_ROT // 2, 1) * s_prev)
                lo = cols.start + c * LANES
                qkv_ref[:, lo:lo + LANES] = (rot * qscale).astype(qkv_ref.dtype)

        pieces(rope)

    @pl.when((j >= ROPE_TILES) & (j < QKV_TILES))
    def _():
        def value(acc, cols):
            qkv_ref[:, cols] = acc.astype(qkv_ref.dtype)

        pieces(value)

    @pl.when((j >= QKV_TILES) & (j < QKV_TILES + LATENT_TILES))
    def _():
        def latent(acc, cols):
            if cols.start <= KPE_COL0 < cols.stop:
                c0 = KPE_COL0 - cols.start
                v = acc[:, c0:c0 + LANES]
                lane = _lane_iota(v.shape)
                half = MLA_ROPE_DIM // 2
                swapped = jnp.where(lane < MLA_ROPE_DIM + half, pltpu.roll(v, half, 1),
                                    pltpu.roll(v, MLA_ROPE_DIM + half, 1))
                parts = [acc[:, :c0], jnp.where(lane < MLA_ROPE_DIM, v, swapped), acc[:, c0 + LANES:]]
                acc = jnp.concatenate([p for p in parts if p.shape[1]], axis=1)
            lat_ref[:, cols] = acc

        pieces(latent)

    @pl.when(j >= QKV_TILES + LATENT_TILES)
    def _():
        def gate(acc, cols):
            gate_ref[:, cols] = _sigmoid(acc).astype(gate_ref.dtype)

        pieces(gate)


def _inproj(x2, g, w_all, cos_t, sin_t, tm):
    n_tok, d = x2.shape
    tn = INPROJ_TN
    lat0 = QKV_TILES
    gate0 = QKV_TILES + LATENT_TILES
    return pl.pallas_call(
        _inproj_kernel,
        grid=(n_tok // tm, INPROJ_TILES),
        in_specs=[pl.BlockSpec((tm, d), lambda i, j: (i, 0)),
                  _resident((1, d)),
                  pl.BlockSpec((pl.Element(tn), pl.Element(d)),
                               lambda i, j: (pl.multiple_of(jnp.where(j < gate0, j * tn, GATE_ROW0 + (j - gate0) * tn), 16), 0)),
                  pl.BlockSpec((tm, LANES), lambda i, j: (i, 0)),
                  pl.BlockSpec((tm, LANES), lambda i, j: (i, 0))],
        out_specs=[pl.BlockSpec((tm, tn), lambda i, j: (i, jnp.clip(j, 0, QKV_TILES - 1))),
                   pl.BlockSpec((tm, tn), lambda i, j: (i, jnp.clip(j - lat0, 0, LATENT_TILES - 1))),
                   pl.BlockSpec((tm, tn), lambda i, j: (i, jnp.clip(j - gate0, 0, GATE_TILES - 1)))],
        out_shape=[jax.ShapeDtypeStruct((n_tok, QKV_COLS), jnp.bfloat16),
                   jax.ShapeDtypeStruct((n_tok, LATENT_COLS), jnp.float32),
                   jax.ShapeDtypeStruct((n_tok, GATE_COLS), jnp.bfloat16)],
        scratch_shapes=[pltpu.VMEM((tm, d), jnp.bfloat16)],
        compiler_params=_params("parallel", "arbitrary"),
        name="inproj",
    )(x2, g, w_all, cos_t, sin_t)


def _mla_proj_kernel(c_ref, gq_ref, gkv_ref, wuq_ref, wuk_ref, wuv_ref, cos_ref, sin_ref,
                     q_ref, k_ref, v_ref):
    cos_t = cos_ref[...]
    sin_t = sin_ref[...]
    cq = c_ref[:, :MLA_Q_RANK]
    cqn = (cq * _rms_scale(cq, NORM_EPS) * gq_ref[...]).astype(jnp.bfloat16)
    ckv = c_ref[:, MLA_Q_RANK:KPE_COL0]
    ckvn = (ckv * _rms_scale(ckv, NORM_EPS) * gkv_ref[...]).astype(jnp.bfloat16)
    kpe = _mla_rope(c_ref[:, KPE_COL0:KPE_COL0 + LANES], cos_t, sin_t)[:, :MLA_ROPE_DIM].astype(k_ref.dtype)
    qscale = MLA_QK_DIM ** -0.5 * LOG2E
    for h in range(MLA_HEADS):
        r = jnp.dot(cqn, wuq_ref[h], preferred_element_type=jnp.float32)
        q_ref[0, h, :, :MLA_NOPE_DIM] = (r[:, :MLA_NOPE_DIM] * qscale).astype(q_ref.dtype)
        qpe = _mla_rope(r[:, MLA_NOPE_DIM:], cos_t, sin_t)[:, :MLA_ROPE_DIM]
        q_ref[0, h, :, MLA_NOPE_DIM:] = (qpe * qscale).astype(q_ref.dtype)
        kn = jnp.dot(ckvn, wuk_ref[h], preferred_element_type=jnp.float32)
        k_ref[0, h, :, :MLA_NOPE_DIM] = kn.astype(k_ref.dtype)
        k_ref[0, h, :, MLA_NOPE_DIM:] = kpe
    v_ref[...] = jnp.dot(ckvn, wuv_ref[...], preferred_element_type=jnp.float32).astype(v_ref.dtype)


def _mla_proj(latent, gq, gkv, wuq, wuk, wuv, cos_t, sin_t, batch, seq, tm):
    n_tok = latent.shape[0]
    per_b = seq // tm
    head_spec = pl.BlockSpec((1, MLA_HEADS, tm, MLA_QK_DIM), lambda i: (i // per_b, 0, i % per_b, 0))
    head_shape = jax.ShapeDtypeStruct((batch, MLA_HEADS, seq, MLA_QK_DIM), jnp.bfloat16)
    return pl.pallas_call(
        _mla_proj_kernel,
        grid=(n_tok // tm,),
        in_specs=[pl.BlockSpec((tm, LATENT_COLS), lambda i: (i, 0)),
                  _resident(gq.shape), _resident(gkv.shape),
                  _resident(wuq.shape), _resident(wuk.shape), _resident(wuv.shape),
                  pl.BlockSpec((tm, LANES), lambda i: (i, 0)),
                  pl.BlockSpec((tm, LANES), lambda i: (i, 0))],
        out_specs=[head_spec, head_spec,
                   pl.BlockSpec((tm, MLA_HEADS * MLA_V_DIM), lambda i: (i, 0))],
        out_shape=[head_shape, head_shape,
                   jax.ShapeDtypeStruct((n_tok, MLA_HEADS * MLA_V_DIM), jnp.bfloat16)],
        compiler_params=_params("parallel"),
        name="mla_proj",
    )(latent, gq, gkv, wuq, wuk, wuv, cos_t, sin_t)


def _with_ones(v):
    return jnp.concatenate([v, jnp.ones((v.shape[0], LANES), v.dtype)], axis=-1)


def _softmax_pv(s, v_ones):
    m = jnp.max(s, axis=-1, keepdims=True)
    p = jnp.exp2(s - m).astype(v_ones.dtype)
    pv = jnp.dot(p, v_ones, preferred_element_type=jnp.float32)
    dv = v_ones.shape[1] - LANES
    return pv[:, :dv] / pv[:, dv:]


def _diff_attn_kernel(q_ref, k_ref, v_ref, lq1_ref, lk1_ref, lq2_ref, lk2_ref, g_ref, o_ref, v1_ref, *, rg):
    @pl.when(pl.program_id(2) == 0)
    def _():
        v1_ref[...] = _with_ones(v_ref[0])

    k = k_ref[0]
    v = v1_ref[...]
    lam = (jnp.exp(jnp.sum(lq1_ref[...] * lk1_ref[...], axis=-1, keepdims=True))
           - jnp.exp(jnp.sum(lq2_ref[...] * lk2_ref[...], axis=-1, keepdims=True))
           + DIFF_LAMBDA_INIT)
    lane = _lane_iota((rg, LANES))
    for g in range(q_ref.shape[1] // rg):
        q = q_ref[0, g * rg:(g + 1) * rg]
        zero = jnp.zeros_like(q)
        q12 = jnp.concatenate([jnp.where(lane < DIFF_HEAD_DIM, q, zero),
                               jnp.where(lane >= DIFF_HEAD_DIM, q, zero)], axis=0)
        s = lax.dot_general(q12, k, (((1,), (1,)), ((), ())), preferred_element_type=jnp.float32)
        a = _softmax_pv(s, v)
        o = a[:rg] - lam * a[rg:]
        o = o * _rms_scale(o, DIFF_SUBLN_EPS) * g_ref[...] * (1.0 - DIFF_LAMBDA_INIT)
        o_ref[0, g * rg:(g + 1) * rg] = o.astype(o_ref.dtype)


def _diff_attn(qkv3, lq1, lk1, lq2, lk2, subln_g, tq, rg):
    batch, seq, _ = qkv3.shape
    h = DIFF_HEADS
    return pl.pallas_call(
        functools.partial(_diff_attn_kernel, rg=rg),
        grid=(batch, h, seq // tq),
        in_specs=[pl.BlockSpec((1, tq, LANES), lambda b, hh, i: (b, i, hh)),
                  pl.BlockSpec((1, seq, LANES), lambda b, hh, i: (b, 0, h + hh)),
                  pl.BlockSpec((1, seq, LANES), lambda b, hh, i: (b, 0, 2 * h + hh)),
                  _resident(lq1.shape), _resident(lk1.shape), _resident(lq2.shape), _resident(lk2.shape),
                  _resident(subln_g.shape)],
        out_specs=pl.BlockSpec((1, tq, DIFF_V_DIM), lambda b, hh, i: (b, i, hh)),
        out_shape=jax.ShapeDtypeStruct((batch, seq, h * DIFF_V_DIM), jnp.bfloat16),
        scratch_shapes=[pltpu.VMEM((seq, DIFF_V_DIM + LANES), jnp.bfloat16)],
        compiler_params=_params("parallel", "parallel", "arbitrary"),
        name="diff_attn",
    )(qkv3, qkv3, qkv3, lq1, lk1, lq2, lk2, subln_g)


def _mla_attn_kernel(q_ref, k_ref, v_ref, o_ref, v1_ref, *, rg):
    @pl.when(pl.program_id(2) == 0)
    def _():
        v1_ref[...] = _with_ones(v_ref[0])

    k = k_ref[0, 0]
    v = v1_ref[...]
    for g in range(q_ref.shape[2] // rg):
        s = lax.dot_general(q_ref[0, 0, g * rg:(g + 1) * rg], k, (((1,), (1,)), ((), ())),
                            preferred_element_type=jnp.float32)
        o_ref[0, g * rg:(g + 1) * rg] = _softmax_pv(s, v).astype(o_ref.dtype)


def _mla_attn(q_cat, k_cat, v3, tq, rg):
    batch, heads, seq, dqk = q_cat.shape
    return pl.pallas_call(
        functools.partial(_mla_attn_kernel, rg=rg),
        grid=(batch, heads, seq // tq),
        in_specs=[pl.BlockSpec((1, 1, tq, dqk), lambda b, h, i: (b, h, i, 0)),
                  pl.BlockSpec((1, 1, seq, dqk), lambda b, h, i: (b, h, 0, 0)),
                  pl.BlockSpec((1, seq, MLA_V_DIM), lambda b, h, i: (b, 0, h))],
        out_specs=pl.BlockSpec((1, tq, MLA_V_DIM), lambda b, h, i: (b, i, h)),
        out_shape=jax.ShapeDtypeStruct((batch, seq, heads * MLA_V_DIM), jnp.bfloat16),
        scratch_shapes=[pltpu.VMEM((seq, MLA_V_DIM + LANES), jnp.bfloat16)],
        compiler_params=_params("parallel", "parallel", "arbitrary"),
        name="mla_attn",
    )(q_cat, k_cat, v3)


def _merge_out_kernel(oa_ref, ob_ref, sga_ref, sgb_ref, x_ref, woa_ref, wob_ref, wout_ref, h_ref):
    ya = jnp.dot(oa_ref[...], woa_ref[...], preferred_element_type=jnp.float32)
    yb = jnp.dot(ob_ref[...], wob_ref[...], preferred_element_type=jnp.float32)
    merged = sga_ref[...].astype(jnp.float32) * ya + sgb_ref[...].astype(jnp.float32) * yb
    h_ref[...] = x_ref[...] + jnp.dot(merged.astype(jnp.bfloat16), wout_ref[...],
                                       preferred_element_type=jnp.float32)


def _merge_out(o_a, o_b, gates, x2, w_oa, w_ob, w_out, tm):
    n_tok, d = x2.shape
    return pl.pallas_call(
        _merge_out_kernel,
        grid=(n_tok // tm,),
        in_specs=[pl.BlockSpec((tm, o_a.shape[1]), lambda i: (i, 0)),
                  pl.BlockSpec((tm, o_b.shape[1]), lambda i: (i, 0)),
                  pl.BlockSpec((tm, d), lambda i: (i, 0)),
                  pl.BlockSpec((tm, d), lambda i: (i, 1)),
                  pl.BlockSpec((tm, d), lambda i: (i, 0)),
                  _resident(w_oa.shape), _resident(w_ob.shape), _resident(w_out.shape)],
        out_specs=pl.BlockSpec((tm, d), lambda i: (i, 0)),
        out_shape=jax.ShapeDtypeStruct((n_tok, d), jnp.float32),
        compiler_params=_params("parallel"),
        name="merge_out",
    )(o_a, o_b, gates, gates, x2, w_oa, w_ob, w_out)


def _mem_kv_kernel(mem_ref, g_ref, w_ref, kv_ref):
    mf = mem_ref[0]
    mn = (mf * _rms_scale(mf, NORM_EPS) * g_ref[...]).astype(jnp.bfloat16)
    kv_ref[0] = jnp.dot(mn, w_ref[...], preferred_element_type=jnp.float32).astype(kv_ref.dtype)


def _mem_kv(mem, g, w_ckv):
    batch, m, d = mem.shape
    return pl.pallas_call(
        _mem_kv_kernel,
        grid=(batch,),
        in_specs=[pl.BlockSpec((1, m, d), lambda b: (b, 0, 0)), _resident(g.shape), _resident(w_ckv.shape)],
        out_specs=pl.BlockSpec((1, m, w_ckv.shape[1]), lambda b: (b, 0, 0)),
        out_shape=jax.ShapeDtypeStruct((batch, m, w_ckv.shape[1]), jnp.bfloat16),
        compiler_params=_params("parallel"),
        name="mem_kv",
    )(mem, g, w_ckv)


def _cross_router_kernel(h_ref, gc_ref, wcq_ref, kv_ref, wco_ref, gf_ref, wr_ref, br_ref,
                         h2_ref, eid_ref, rank_ref, wts_ref, cnt_ref, carry_ref):
    i = pl.program_id(0)

    @pl.when(i == 0)
    def _():
        carry_ref[...] = jnp.zeros_like(carry_ref)

    h1 = h_ref[...]
    tm = h1.shape[0]
    hn = (h1 * _rms_scale(h1, NORM_EPS) * gc_ref[...]).astype(jnp.bfloat16)
    q = jnp.dot(hn, wcq_ref[...], preferred_element_type=jnp.float32) * (CROSS_HEAD_DIM ** -0.5 * LOG2E)
    q = q.astype(jnp.bfloat16)
    kv_cols = CROSS_HEADS * CROSS_HEAD_DIM
    heads = []
    for hd in range(CROSS_HEADS):
        lo = hd * CROSS_HEAD_DIM
        kh = kv_ref[0, :, lo:lo + CROSS_HEAD_DIM]
        vh = kv_ref[0, :, kv_cols + lo:kv_cols + lo + CROSS_HEAD_DIM]
        s = lax.dot_general(q[:, lo:lo + CROSS_HEAD_DIM], kh, (((1,), (1,)), ((), ())),
                            preferred_element_type=jnp.float32)
        heads.append(_softmax_pv(s, _with_ones(vh)).astype(jnp.bfloat16))
    o = jnp.concatenate(heads, axis=-1)
    h2 = h1 + jnp.dot(o, wco_ref[...], preferred_element_type=jnp.float32)
    h2_ref[...] = h2

    t = h2 * _rms_scale(h2, NORM_EPS) * gf_ref[...]
    t_hi = t.astype(jnp.bfloat16)
    t_lo = (t - t_hi.astype(jnp.float32)).astype(jnp.bfloat16)
    hi = jnp.dot(t_hi, wr_ref[...], preferred_element_type=jnp.float32)
    lo = jnp.dot(t_lo, wr_ref[:, :LANES], preferred_element_type=jnp.float32)
    logits = hi[:, :LANES] + (hi[:, LANES:] + lo) + br_ref[...]
    lane = _lane_iota(logits.shape)
    neg = jnp.float32(-jnp.inf)
    big = jnp.int32(2 * LANES)
    is_group = lane < N_GROUPS
    lg = jnp.where(is_group, logits, neg)
    mg = jnp.max(lg, axis=-1, keepdims=True)
    g_idx = jnp.min(jnp.where(is_group & (logits == mg), lane, big), axis=-1, keepdims=True)
    g_p = 1.0 / jnp.sum(jnp.exp(lg - mg), axis=-1, keepdims=True)
    lo_lane = ROUTER_EXPERT_LANE0 + EXPERTS_PER_GROUP * g_idx
    in_grp = (lane >= lo_lane) & (lane < lo_lane + EXPERTS_PER_GROUP)
    l1 = jnp.max(jnp.where(in_grp, logits, neg), axis=-1, keepdims=True)
    i1 = jnp.min(jnp.where(in_grp & (logits == l1), lane, big), axis=-1, keepdims=True)
    rest = in_grp & (lane != i1)
    l2 = jnp.max(jnp.where(rest, logits, neg), axis=-1, keepdims=True)
    i2 = jnp.min(jnp.where(rest & (logits == l2), lane, big), axis=-1, keepdims=True)
    d = jnp.exp(l2 - l1)
    w1 = g_p / (1.0 + d)
    w2 = w1 * d

    oh1 = lane == i1
    oh2 = lane == i2
    cnt = (oh1 | oh2).astype(jnp.bfloat16)
    row = lax.broadcasted_iota(jnp.int32, (tm, tm), 0)
    col = lax.broadcasted_iota(jnp.int32, (tm, tm), 1)
    before = (col < row).astype(jnp.bfloat16)
    slot = jnp.dot(before, cnt, preferred_element_type=jnp.float32) + carry_ref[...]
    r1 = jnp.sum(jnp.where(oh1, slot, 0.0), axis=-1, keepdims=True)
    r2 = jnp.sum(jnp.where(oh2, slot, 0.0), axis=-1, keepdims=True)
    carry_ref[...] += jnp.sum(cnt.astype(jnp.float32), axis=0, keepdims=True)
    cnt_ref[...] = carry_ref[...]

    eye = row == col

    def to_row(c, dtype):
        return jnp.sum(jnp.where(eye, c.astype(jnp.float32), 0.0), axis=0, keepdims=True).astype(dtype)

    eid_ref[0] = jnp.concatenate([to_row(i1 - ROUTER_EXPERT_LANE0, jnp.int32),
                                  to_row(i2 - ROUTER_EXPERT_LANE0, jnp.int32)], axis=0)
    rank_ref[0] = jnp.concatenate([to_row(r1, jnp.int32), to_row(r2, jnp.int32)], axis=0)
    wts_ref[...] = jnp.where(_lane_iota((tm, 2)) == 0, w1, w2)


def _cross_router(h1, gc, w_cq, kv_mem, w_co, gf, w_r, b_r, seq, tm):
    n_tok, d = h1.shape
    per_b = seq // tm
    row2 = pl.BlockSpec((tm, 2), lambda i: (i, 0))
    lane2 = pl.BlockSpec((1, 2, tm), lambda i: (i, 0, 0))
    return pl.pallas_call(
        _cross_router_kernel,
        grid=(n_tok // tm,),
        in_specs=[pl.BlockSpec((tm, d), lambda i: (i, 0)),
                  _resident(gc.shape), _resident(w_cq.shape),
                  pl.BlockSpec((1,) + kv_mem.shape[1:], lambda i: (i // per_b, 0, 0)),
                  _resident(w_co.shape), _resident(gf.shape), _resident(w_r.shape), _resident(b_r.shape)],
        out_specs=[pl.BlockSpec((tm, d), lambda i: (i, 0)), lane2, lane2, row2,
                   pl.BlockSpec((1, LANES), lambda i: (0, 0))],
        out_shape=[jax.ShapeDtypeStruct((n_tok, d), jnp.float32),
                   jax.ShapeDtypeStruct((n_tok // tm, 2, tm), jnp.int32),
                   jax.ShapeDtypeStruct((n_tok // tm, 2, tm), jnp.int32),
                   jax.ShapeDtypeStruct((n_tok, 2), jnp.float32),
                   jax.ShapeDtypeStruct((1, LANES), jnp.float32)],
        scratch_shapes=[pltpu.VMEM((1, LANES), jnp.float32)],
        compiler_params=_params("arbitrary"),
        name="cross_router",
    )(h1, gc, w_cq, kv_mem, w_co, gf, w_r, b_r)


def _dispatch_kernel(dest_ref, ztail_ref, h_ref, g_ref, xb_ref, t_ref, sem, zsem):
    i = pl.program_id(0)
    tm = h_ref.shape[0]
    slot = i % 2

    @pl.when(i == 0)
    def _():
        t_ref[1] = jnp.zeros(t_ref.shape[1:], t_ref.dtype)

        def zero_copy(e):
            return pltpu.make_async_copy(t_ref.at[1], xb_ref.at[pl.ds(pl.multiple_of(ztail_ref[0, e], tm), tm)],
                                         zsem)

        for e in range(ztail_ref.shape[1]):
            @pl.when(ztail_ref[1, e] > 0)
            def _():
                zero_copy(e).start()
        for e in range(ztail_ref.shape[1]):
            @pl.when(ztail_ref[1, e] > 0)
            def _():
                zero_copy(e).wait()

    h2 = h_ref[...]
    t = h2 * _rms_scale(h2, NORM_EPS) * g_ref[...]
    half = t.shape[1] // 2
    t_ref[slot] = _pack_bf16_pair(t[:, :half], t[:, half:])

    for r in range(tm):
        for k in range(2):
            pltpu.make_async_copy(t_ref.at[slot, pl.ds(r, 1)],
                                  xb_ref.at[pl.ds(dest_ref[0, k * tm + r], 1)], sem.at[slot]).start()

    def wait_tile(which):
        for _ in range(2):
            pltpu.make_async_copy(t_ref.at[which], xb_ref.at[pl.ds(0, tm)], sem.at[which]).wait()

    @pl.when(i > 0)
    def _():
        wait_tile(1 - slot)

    @pl.when(i == pl.num_programs(0) - 1)
    def _():
        wait_tile(slot)


def _dispatch(dest3, ztail, h2, gf, p_rows, tm):
    n_tok, d = h2.shape
    return pl.pallas_call(
        _dispatch_kernel,
        grid=(n_tok // tm,),
        in_specs=[pl.BlockSpec((None, 1, 2 * tm), lambda i: (i, 0, 0), memory_space=pltpu.SMEM),
                  pl.BlockSpec(memory_space=pltpu.SMEM),
                  pl.BlockSpec((tm, d), lambda i: (i, 0)),
                  _resident(gf.shape)],
        out_specs=pl.BlockSpec(memory_space=pl.ANY),
        out_shape=jax.ShapeDtypeStruct((p_rows, d // 2), jnp.uint32),
        scratch_shapes=[pltpu.VMEM((2, tm, d // 2), jnp.uint32), pltpu.SemaphoreType.DMA((2,)),
                        pltpu.SemaphoreType.DMA(())],
        compiler_params=_params("arbitrary"),
        name="moe_dispatch",
    )(dest3, ztail, h2, gf)


def _expert_kernel(be_ref, nact_ref, x_ref, wg_hbm, wu_hbm, wd_hbm, y_ref,
                   wg_f, wu_f, wd_f, wg_b, wu_b, wd_b, sem):
    i = pl.program_id(0)
    blk = i - 1
    nact = nact_ref[0]
    last_blk = pl.num_programs(0) - 2

    def weight_copies(e):
        return (pltpu.make_async_copy(wg_hbm.at[e], wg_f, sem),
                pltpu.make_async_copy(wu_hbm.at[e], wu_f, sem),
                pltpu.make_async_copy(wd_hbm.at[e], wd_f, sem))

    def fetch(e):
        for c in weight_copies(e):
            c.start()

    def land(e):
        for c in weight_copies(e):
            c.wait()
        wg_b[...] = wg_f[...].astype(jnp.bfloat16)
        wu_b[...] = wu_f[...].astype(jnp.bfloat16)
        wd_b[...] = wd_f[...].astype(jnp.bfloat16)

    @pl.when(i == 0)
    def _():
        fetch(be_ref[0])
        land(be_ref[0])

    @pl.when((i > 0) & (blk < nact))
    def _():
        here = be_ref[blk]
        nxt = be_ref[jnp.minimum(blk + 1, last_blk)]
        prv = be_ref[jnp.maximum(blk - 1, 0)]
        seg_end = nact_ref[1 + N_EXPERTS + here] + nact_ref[1 + here]
        has_next = seg_end < nact
        after = be_ref[jnp.minimum(seg_end, last_blk)]
        is_first = (blk == 0) | (prv != here)
        is_last = (blk + 1 >= nact) | (nxt != here)

        @pl.when(is_first & has_next)
        def _():
            fetch(after)

        x_a, x_b = _unpack_bf16_pair(x_ref[...])
        xb = jnp.concatenate([x_a.astype(jnp.bfloat16), x_b.astype(jnp.bfloat16)], axis=1)
        gate = jnp.dot(xb, wg_b[...], preferred_element_type=jnp.float32)
        up = jnp.dot(xb, wu_b[...], preferred_element_type=jnp.float32)
        hid = (gate * _sigmoid(gate) * up).astype(jnp.bfloat16)
        y = jnp.dot(hid, wd_b[...], preferred_element_type=jnp.float32)
        half = y.shape[1] // 2
        y_ref[...] = _pack_bf16_pair(y[:, :half], y[:, half:])

        @pl.when(is_last & has_next)
        def _():
            land(after)

    @pl.when((i > 0) & (blk >= nact))
    def _():
        y_ref[...] = jnp.zeros_like(y_ref)


def _experts(block_expert, sched, xb, w_gate, w_up, w_down, bm):
    p_rows, dp = xb.shape
    d = w_gate.shape[1]
    de = w_gate.shape[-1]

    def x_map(i, be, sc):
        return (jnp.clip(i - 1, 0, sc[0] - 1), 0)

    grid_spec = pltpu.PrefetchScalarGridSpec(
        num_scalar_prefetch=2,
        grid=(p_rows // bm + 1,),
        in_specs=[pl.BlockSpec((bm, dp), x_map),
                  pl.BlockSpec(memory_space=pl.ANY),
                  pl.BlockSpec(memory_space=pl.ANY),
                  pl.BlockSpec(memory_space=pl.ANY)],
        out_specs=pl.BlockSpec((bm, dp), lambda i, be, sc: (jnp.maximum(i - 1, 0), 0)),
        scratch_shapes=[pltpu.VMEM((d, de), jnp.float32), pltpu.VMEM((d, de), jnp.float32),
                        pltpu.VMEM((de, d), jnp.float32),
                        pltpu.VMEM((d, de), jnp.bfloat16), pltpu.VMEM((d, de), jnp.bfloat16),
                        pltpu.VMEM((de, d), jnp.bfloat16),
                        pltpu.SemaphoreType.DMA(())],
    )
    return pl.pallas_call(
        _expert_kernel,
        grid_spec=grid_spec,
        out_shape=jax.ShapeDtypeStruct((p_rows, dp), jnp.uint32),
        compiler_params=_params("arbitrary"),
        name="moe_experts",
    )(block_expert, sched, xb, w_gate, w_up, w_down)


def _combine_kernel(dest_ref, dest_next_ref, h_ref, wts_ref, g_ref, y_ref, o_ref, ybuf, sem):
    i = pl.program_id(0)
    tm = h_ref.shape[0]
    slot = i % 2

    def gather(idx_ref, which):
        for r in range(tm):
            for k in range(2):
                pltpu.make_async_copy(y_ref.at[pl.ds(idx_ref[0, k * tm + r], 1)],
                                      ybuf.at[which, k, pl.ds(r, 1)], sem.at[which]).start()

    def wait_tile(which):
        for k in range(2):
            pltpu.make_async_copy(y_ref.at[pl.ds(0, tm)], ybuf.at[which, k], sem.at[which]).wait()

    @pl.when(i == 0)
    def _():
        gather(dest_ref, slot)

    wait_tile(slot)
    gather(dest_next_ref, 1 - slot)

    w = wts_ref[...]
    y0 = jnp.concatenate(_unpack_bf16_pair(ybuf[slot, 0]), axis=1)
    y1 = jnp.concatenate(_unpack_bf16_pair(ybuf[slot, 1]), axis=1)
    h3 = h_ref[...] + w[:, 0:1] * y0 + w[:, 1:2] * y1
    o_ref[...] = h3 * _rms_scale(h3, NORM_EPS) * g_ref[...]

    @pl.when(i == pl.num_programs(0) - 1)
    def _():
        wait_tile(1 - slot)


def _combine(dest3, h2, wts, g_final, y, tm):
    n_tok, d = h2.shape
    last = n_tok // tm - 1
    return pl.pallas_call(
        _combine_kernel,
        grid=(n_tok // tm,),
        in_specs=[pl.BlockSpec((None, 1, 2 * tm), lambda i: (i, 0, 0), memory_space=pltpu.SMEM),
                  pl.BlockSpec((None, 1, 2 * tm), lambda i: (jnp.minimum(i + 1, last), 0, 0),
                               memory_space=pltpu.SMEM),
                  pl.BlockSpec((tm, d), lambda i: (i, 0)),
                  pl.BlockSpec((tm, 2), lambda i: (i, 0)),
                  _resident(g_final.shape),
                  pl.BlockSpec(memory_space=pl.ANY)],
        out_specs=pl.BlockSpec((tm, d), lambda i: (i, 0)),
        out_shape=jax.ShapeDtypeStruct((n_tok, d), jnp.float32),
        scratch_shapes=[pltpu.VMEM((2, 2, tm, d // 2), jnp.uint32), pltpu.SemaphoreType.DMA((2,))],
        compiler_params=_params("arbitrary"),
        name="moe_combine",
    )(dest3, dest3, h2, wts, g_final, y)


def _transpose_w_in(w_in):
    return jnp.swapaxes(w_in, 0, 1).astype(jnp.bfloat16)


def _split_w_uq(w_uq):
    half = MLA_ROPE_DIM // 2
    w = w_uq.reshape(MLA_Q_RANK, MLA_HEADS, MLA_QK_DIM).transpose(1, 0, 2)
    pe = w[:, :, MLA_NOPE_DIM:]
    pe_swapped = jnp.concatenate([pe[:, :, half:], pe[:, :, :half]], axis=2)
    return jnp.concatenate([w, pe_swapped], axis=2).astype(jnp.bfloat16)


def _split_w_ukv(w_ukv):
    w = w_ukv.reshape(MLA_KV_RANK, MLA_HEADS, MLA_NOPE_DIM + MLA_V_DIM)
    wuk = w[:, :, :MLA_NOPE_DIM].transpose(1, 0, 2).astype(jnp.bfloat16)
    wuv = w[:, :, MLA_NOPE_DIM:].reshape(MLA_KV_RANK, MLA_HEADS * MLA_V_DIM).astype(jnp.bfloat16)
    return wuk, wuv


def kernel(x, mem, positions, attn_norm_g, w_in, diff_lambda_q1, diff_lambda_k1, diff_lambda_q2, diff_lambda_k2, diff_subln_g, w_o_diff, mla_q_norm_g, w_uq, mla_kv_norm_g, w_ukv, w_o_mla, w_out, cross_norm_g, mem_norm_g, w_cq, w_ckv, w_co, ffn_norm_g, w_router_group, b_router_group, w_router_expert, b_router_expert, w_expert_gate, w_expert_up, w_expert_down, final_norm_g):
    batch, seq, d = x.shape
    assert d == D_MODEL and w_in.shape[0] == 1, "single-layer kernel"
    n_tok = batch * seq
    bf = jnp.bfloat16
    x2 = x.reshape(n_tok, d)

    tm_proj = min(1024, seq)
    tm_row = min(256, seq)
    tm_cross = min(512, seq)
    tq = min(2048, seq)
    rg_diff = 128
    rg_mla = 256

    cos_t, sin_t = _rope_tables(positions, n_tok, tm_proj)

    g_attn = attn_norm_g[0].reshape(1, d)
    qkv, latent, gates = _inproj(x2, g_attn, _transpose_w_in(w_in[0]), cos_t, sin_t, tm_proj)

    o_a = _diff_attn(qkv.reshape(batch, seq, QKV_COLS),
                     diff_lambda_q1[0].reshape(1, -1), diff_lambda_k1[0].reshape(1, -1),
                     diff_lambda_q2[0].reshape(1, -1), diff_lambda_k2[0].reshape(1, -1),
                     diff_subln_g[0].reshape(1, -1), tq, rg_diff)

    wuk, wuv = _split_w_ukv(w_ukv[0])
    q_cat, k_cat, v_mla = _mla_proj(latent, mla_q_norm_g[0].reshape(1, -1), mla_kv_norm_g[0].reshape(1, -1),
                                    _split_w_uq(w_uq[0]), wuk, wuv, cos_t, sin_t, batch, seq, min(512, seq))
    o_b = _mla_attn(q_cat, k_cat, v_mla.reshape(batch, seq, MLA_HEADS * MLA_V_DIM), tq, rg_mla)

    h1 = _merge_out(o_a.reshape(n_tok, -1), o_b.reshape(n_tok, -1), gates, x2,
                    w_o_diff[0].astype(bf), w_o_mla[0].astype(bf), w_out[0].astype(bf), tm_row)

    kv_mem = _mem_kv(mem, mem_norm_g[0].reshape(1, d), w_ckv[0].astype(bf))
    n_router = N_GROUPS + N_EXPERTS
    w_r = jnp.concatenate([w_router_group[0].astype(jnp.float32), w_router_expert[0].astype(jnp.float32),
                           jnp.zeros((d, LANES - n_router), jnp.float32)], axis=1)
    w_r_hi = w_r.astype(bf)
    w_r_lo = (w_r - w_r_hi.astype(jnp.float32)).astype(bf)
    w_r = jnp.concatenate([w_r_hi, w_r_lo], axis=1)
    b_r = jnp.concatenate([b_router_group[0].astype(jnp.float32), b_router_expert[0].astype(jnp.float32),
                           jnp.zeros((LANES - n_router,), jnp.float32)]).reshape(1, LANES)
    g_ffn = ffn_norm_g[0].reshape(1, d)
    h2, eid, rank, wts, cnt = _cross_router(h1, cross_norm_g[0].reshape(1, d), w_cq[0].astype(bf), kv_mem,
                                            w_co[0].astype(bf), g_ffn, w_r, b_r, seq, tm_cross)

    bm = MOE_ROWS_PER_BLOCK
    assert tm_row == bm, "dispatch zeroes whole row blocks from its token-tile scratch"
    counts = cnt[0, ROUTER_EXPERT_LANE0:ROUTER_EXPERT_LANE0 + N_EXPERTS].astype(jnp.int32)
    padded = ((counts + bm - 1) // bm) * bm
    padded_end = jnp.cumsum(padded)
    padded_off = padded_end - padded
    seg_start = jnp.sum(jnp.where(eid[..., None] == jnp.arange(N_EXPERTS, dtype=jnp.int32), padded_off, 0), axis=-1)
    dest = seg_start + rank
    p_rows = ((2 * n_tok + bm - 1) // bm) * bm + N_EXPERTS * bm
    n_blocks = p_rows // bm
    n_active = (padded_end[-1] // bm).astype(jnp.int32)
    blk = jnp.minimum(jnp.arange(n_blocks, dtype=jnp.int32), n_active - 1)
    block_expert = jnp.sum((padded_end[None, :] <= (blk * bm)[:, None]).astype(jnp.int32), axis=1)
    block_expert = jnp.minimum(block_expert, N_EXPERTS - 1)
    dest3 = dest.reshape(-1, 2, tm_cross // tm_row, tm_row).transpose(0, 2, 1, 3).reshape(
        n_tok // tm_row, 1, 2 * tm_row)
    unused = n_active + jnp.arange(N_EXPERTS, dtype=jnp.int32)
    ztail = jnp.stack([jnp.concatenate([jnp.maximum(padded_end - bm, 0), jnp.minimum(unused, n_blocks - 1) * bm]),
                       jnp.concatenate([padded > 0, unused < n_blocks]).astype(jnp.int32)]).astype(jnp.int32)
    sched = jnp.concatenate([n_active.reshape(1), padded // bm, padded_off // bm]).astype(jnp.int32)

    xb = _dispatch(dest3, ztail, h2, g_ffn, p_rows, tm_row)
    y = _experts(block_expert, sched, xb, w_expert_gate[0], w_expert_up[0], w_expert_down[0], bm)
    out = _combine(dest3, h2, wts, final_norm_g.reshape(1, d), y, tm_row)
    return out.reshape(batch, seq, d)
```

```python
import functools
import math

import jax
import jax.numpy as jnp
from jax import lax
from jax.experimental import pallas as pl
from jax.experimental.pallas import tpu as pltpu

D_MODEL = 2048
ROPE_THETA = 500000.0
NORM_EPS = 1e-6

DIFF_HEADS = 8
DIFF_HEAD_DIM = 64
DIFF_V_DIM = 2 * DIFF_HEAD_DIM
DIFF_ROT = DIFF_HEAD_DIM // 4
DIFF_SUBLN_EPS = 1e-5
DIFF_LAMBDA_INIT = 0.8 - 0.6 * math.exp(-0.3 * 0)

MLA_HEADS = 8
MLA_Q_RANK = 512
MLA_KV_RANK = 256
MLA_NOPE_DIM = 128
MLA_ROPE_DIM = 64
MLA_V_DIM = 128
MLA_QK_DIM = MLA_NOPE_DIM + MLA_ROPE_DIM

CROSS_HEADS = 4
CROSS_HEAD_DIM = 128

N_GROUPS = 4
EXPERTS_PER_GROUP = 8
N_EXPERTS = N_GROUPS * EXPERTS_PER_GROUP
D_EXPERT = 512

LANES = 128
LOG2E = 1.4426950408889634
VMEM_LIMIT_BYTES = 56 * 1024 * 1024

ROUTER_EXPERT_LANE0 = N_GROUPS

QKV_COLS = 3 * DIFF_HEADS * DIFF_V_DIM
LATENT_COLS = 1024
GATE_COLS = 2 * D_MODEL
KPE_COL0 = MLA_Q_RANK + MLA_KV_RANK

MOE_ROWS_PER_BLOCK = 256
SUBLANES = 8


def _params(*semantics):
    return pltpu.CompilerParams(dimension_semantics=semantics, vmem_limit_bytes=VMEM_LIMIT_BYTES)


def _resident(shape):
    zeros = (0,) * len(shape)
    return pl.BlockSpec(shape, lambda *_: zeros, pipeline_mode=pl.Buffered(1))


def _rms_scale(xf, eps):
    return lax.rsqrt(jnp.mean(xf * xf, axis=-1, keepdims=True) + eps)


def _sigmoid(x):
    return 0.5 * jnp.tanh(0.5 * x) + 0.5


def _pack_bf16_pair(a, b):
    hi = lax.bitcast_convert_type(a.astype(jnp.bfloat16).astype(jnp.float32), jnp.uint32)
    lo = lax.bitcast_convert_type(b.astype(jnp.bfloat16).astype(jnp.float32), jnp.uint32)
    return hi | (lo >> 16)


def _unpack_bf16_pair(w):
    a = lax.bitcast_convert_type(w & jnp.uint32(0xFFFF0000), jnp.float32)
    b = lax.bitcast_convert_type(w << 16, jnp.float32)
    return a, b


def _rows_to_tiles(ref_view, packed):
    rows = packed.shape[0]
    for c in range(SUBLANES):
        ref_view[pl.ds(c, rows, stride=SUBLANES), :] = packed[:, c * LANES:(c + 1) * LANES]


def _tiles_to_row_chunks(ref_view, rows):
    return [ref_view[pl.ds(c, rows, stride=SUBLANES), :] for c in range(SUBLANES)]


def _lane_iota(shape):
    return lax.broadcasted_iota(jnp.int32, shape, len(shape) - 1)


def _trig_kernel(pos_ref, invf_ref, cos_ref, sin_ref):
    ang = pos_ref[...].astype(jnp.float32) * invf_ref[...]
    cos_ref[...] = jnp.cos(ang)
    sin_ref[...] = jnp.sin(ang)


def _rope_tables(positions, n_tok, tm):
    half_m = MLA_ROPE_DIM // 2
    half_d = DIFF_ROT // 2
    inv_m = jnp.float32(ROPE_THETA) ** (-jnp.arange(half_m, dtype=jnp.float32) * 2.0 / MLA_ROPE_DIM)
    inv_d = jnp.float32(ROPE_THETA) ** (-jnp.arange(half_d, dtype=jnp.float32) * 2.0 / DIFF_ROT)
    invf = jnp.concatenate([inv_m, inv_m, inv_d, inv_d,
                            jnp.zeros((DIFF_HEAD_DIM - DIFF_ROT,), jnp.float32)]).reshape(1, LANES)
    pos = positions.reshape(n_tok, 1)
    return pl.pallas_call(
        _trig_kernel,
        grid=(n_tok // tm,),
        in_specs=[pl.BlockSpec((tm, 1), lambda i: (i, 0)), _resident((1, LANES))],
        out_specs=[pl.BlockSpec((tm, LANES), lambda i: (i, 0))] * 2,
        out_shape=[jax.ShapeDtypeStruct((n_tok, LANES), jnp.float32)] * 2,
        compiler_params=_params("parallel"),
        name="rope_tables",
    )(pos, invf)


def _diff_rope_coeffs(cos_t, sin_t):
    lane = _lane_iota(cos_t.shape)
    upper = lane >= DIFF_HEAD_DIM
    cos_d = jnp.where(upper, cos_t, pltpu.roll(cos_t, DIFF_HEAD_DIM, 1))
    sin_d = jnp.where(upper, sin_t, pltpu.roll(sin_t, DIFF_HEAD_DIM, 1))
    in_head = lane % DIFF_HEAD_DIM
    half = DIFF_ROT // 2
    s_next = jnp.where(in_head < half, -sin_d, 0.0)
    s_prev = jnp.where((in_head >= half) & (in_head < DIFF_ROT), sin_d, 0.0)
    return cos_d, s_next, s_prev


def _mla_rope(pair, cos_t, sin_t):
    lane = _lane_iota(pair.shape)
    sin_signed = jnp.where(lane < MLA_ROPE_DIM // 2, -sin_t, sin_t)
    return pair * cos_t + pltpu.roll(pair, MLA_ROPE_DIM, 1) * sin_signed


INPROJ_TN = 1024
INPROJ_PIECE = 256
Q_TILES = DIFF_HEADS * DIFF_V_DIM // INPROJ_TN
ROPE_TILES = 2 * Q_TILES
QKV_TILES = QKV_COLS // INPROJ_TN
LATENT_TILES = LATENT_COLS // INPROJ_TN
GATE_TILES = GATE_COLS // INPROJ_TN
INPROJ_TILES = QKV_TILES + LATENT_TILES + GATE_TILES
GATE_ROW0 = QKV_COLS + KPE_COL0 + MLA_ROPE_DIM


def _inproj_kernel(x_ref, g_ref, w_ref, cos_ref, sin_ref, qkv_ref, lat_ref, gate_ref, xn_ref):
    j = pl.program_id(1)

    @pl.when(j == 0)
    def _():
        xf = x_ref[...]
        xn_ref[...] = (xf * _rms_scale(xf, NORM_EPS) * g_ref[...]).astype(jnp.bfloat16)

    def pieces(epilogue):
        for c in range(INPROJ_TN // INPROJ_PIECE):
            cols = slice(c * INPROJ_PIECE, (c + 1) * INPROJ_PIECE)
            acc = lax.dot_general(xn_ref[...], w_ref[cols, :], (((1,), (1,)), ((), ())),
                                  preferred_element_type=jnp.float32)
            epilogue(acc, cols)

    @pl.when(j < ROPE_TILES)
    def _():
        cos_d, s_next, s_prev = _diff_rope_coeffs(cos_ref[...], sin_ref[...])
        qscale = jnp.where(j < Q_TILES, DIFF_HEAD_DIM ** -0.5 * LOG2E, 1.0).astype(jnp.float32)

        def rope(acc, cols):
            for c in range(INPROJ_PIECE // LANES):
                xc = acc[:, c * LANES:(c + 1) * LANES]
                rot = (xc * cos_d + pltpu.roll(xc, LANES - DIFF_ROT // 2, 1) * s_next
                       + pltpu.roll(xc, DIFF_ROT // 2, 1) * s_prev)
                lo = cols.start + c * LANES
                qkv_ref[:, lo:lo + LANES] = (rot * qscale).astype(qkv_ref.dtype)

        pieces(rope)

    @pl.when((j >= ROPE_TILES) & (j < QKV_TILES))
    def _():
        def value(acc, cols):
            qkv_ref[:, cols] = acc.astype(qkv_ref.dtype)

        pieces(value)

    @pl.when((j >= QKV_TILES) & (j < QKV_TILES + LATENT_TILES))
    def _():
        def latent(acc, cols):
            if cols.start <= KPE_COL0 < cols.stop:
                c0 = KPE_COL0 - cols.start
                v = acc[:, c0:c0 + LANES]
                lane = _lane_iota(v.shape)
                half = MLA_ROPE_DIM // 2
                swapped = jnp.where(lane < MLA_ROPE_DIM + half, pltpu.roll(v, half, 1),
                                    pltpu.roll(v, MLA_ROPE_DIM + half, 1))
                parts = [acc[:, :c0], jnp.where(lane < MLA_ROPE_DIM, v, swapped), acc[:, c0 + LANES:]]
                acc = jnp.concatenate([p for p in parts if p.shape[1]], axis=1)
            lat_ref[:, cols] = acc

        pieces(latent)

    @pl.when(j >= QKV_TILES + LATENT_TILES)
    def _():
        def gate(acc, cols):
            gate_ref[:, cols] = _sigmoid(acc).astype(gate_ref.dtype)

        pieces(gate)


def _inproj(x2, g, w_all, cos_t, sin_t, tm):
    n_tok, d = x2.shape
    tn = INPROJ_TN
    lat0 = QKV_TILES
    gate0 = QKV_TILES + LATENT_TILES
    return pl.pallas_call(
        _inproj_kernel,
        grid=(n_tok // tm, INPROJ_TILES),
        in_specs=[pl.BlockSpec((tm, d), lambda i, j: (i, 0)),
                  _resident((1, d)),
                  pl.BlockSpec((pl.Element(tn), pl.Element(d)),
                               lambda i, j: (pl.multiple_of(jnp.where(j < gate0, j * tn, GATE_ROW0 + (j - gate0) * tn), 16), 0)),
                  pl.BlockSpec((tm, LANES), lambda i, j: (i, 0)),
                  pl.BlockSpec((tm, LANES), lambda i, j: (i, 0))],
        out_specs=[pl.BlockSpec((tm, tn), lambda i, j: (i, jnp.clip(j, 0, QKV_TILES - 1))),
                   pl.BlockSpec((tm, tn), lambda i, j: (i, jnp.clip(j - lat0, 0, LATENT_TILES - 1))),
                   pl.BlockSpec((tm, tn), lambda i, j: (i, jnp.clip(j - gate0, 0, GATE_TILES - 1)))],
        out_shape=[jax.ShapeDtypeStruct((n_tok, QKV_COLS), jnp.bfloat16),
                   jax.ShapeDtypeStruct((n_tok, LATENT_COLS), jnp.float32),
                   jax.ShapeDtypeStruct((n_tok, GATE_COLS), jnp.bfloat16)],
        scratch_shapes=[pltpu.VMEM((tm, d), jnp.bfloat16)],
        compiler_params=_params("parallel", "arbitrary"),
        name="inproj",
    )(x2, g, w_all, cos_t, sin_t)


def _mla_proj_kernel(c_ref, gq_ref, gkv_ref, wuq_ref, wuk_ref, wuv_ref, cos_ref, sin_ref,
                     q_ref, k_ref, v_ref):
    cos_t = cos_ref[...]
    sin_t = sin_ref[...]
    cq = c_ref[:, :MLA_Q_RANK]
    cqn = (cq * _rms_scale(cq, NORM_EPS) * gq_ref[...]).astype(jnp.bfloat16)
    ckv = c_ref[:, MLA_Q_RANK:KPE_COL0]
    ckvn = (ckv * _rms_scale(ckv, NORM_EPS) * gkv_ref[...]).astype(jnp.bfloat16)
    kpe = _mla_rope(c_ref[:, KPE_COL0:KPE_COL0 + LANES], cos_t, sin_t)[:, :MLA_ROPE_DIM].astype(k_ref.dtype)
    qscale = MLA_QK_DIM ** -0.5 * LOG2E
    for h in range(MLA_HEADS):
        r = jnp.dot(cqn, wuq_ref[h], preferred_element_type=jnp.float32)
        q_ref[0, h, :, :MLA_NOPE_DIM] = (r[:, :MLA_NOPE_DIM] * qscale).astype(q_ref.dtype)
        qpe = _mla_rope(r[:, MLA_NOPE_DIM:], cos_t, sin_t)[:, :MLA_ROPE_DIM]
        q_ref[0, h, :, MLA_NOPE_DIM:] = (qpe * qscale).astype(q_ref.dtype)
        kn = jnp.dot(ckvn, wuk_ref[h], preferred_element_type=jnp.float32)
        k_ref[0, h, :, :MLA_NOPE_DIM] = kn.astype(k_ref.dtype)
        k_ref[0, h, :, MLA_NOPE_DIM:] = kpe
    v_ref[...] = jnp.dot(ckvn, wuv_ref[...], preferred_element_type=jnp.float32).astype(v_ref.dtype)


def _mla_proj(latent, gq, gkv, wuq, wuk, wuv, cos_t, sin_t, batch, seq, tm):
    n_tok = latent.shape[0]
    per_b = seq // tm
    head_spec = pl.BlockSpec((1, MLA_HEADS, tm, MLA_QK_DIM), lambda i: (i // per_b, 0, i % per_b, 0))
    head_shape = jax.ShapeDtypeStruct((batch, MLA_HEADS, seq, MLA_QK_DIM), jnp.bfloat16)
    return pl.pallas_call(
        _mla_proj_kernel,
        grid=(n_tok // tm,),
        in_specs=[pl.BlockSpec((tm, LATENT_COLS), lambda i: (i, 0)),
                  _resident(gq.shape), _resident(gkv.shape),
                  _resident(wuq.shape), _resident(wuk.shape), _resident(wuv.shape),
                  pl.BlockSpec((tm, LANES), lambda i: (i, 0)),
                  pl.BlockSpec((tm, LANES), lambda i: (i, 0))],
        out_specs=[head_spec, head_spec,
                   pl.BlockSpec((tm, MLA_HEADS * MLA_V_DIM), lambda i: (i, 0))],
        out_shape=[head_shape, head_shape,
                   jax.ShapeDtypeStruct((n_tok, MLA_HEADS * MLA_V_DIM), jnp.bfloat16)],
        compiler_params=_params("parallel"),
        name="mla_proj",
    )(latent, gq, gkv, wuq, wuk, wuv, cos_t, sin_t)


def _with_ones(v):
    return jnp.concatenate([v, jnp.ones((v.shape[0], LANES), v.dtype)], axis=-1)


def _softmax_pv(s, v_ones):
    m = jnp.max(s, axis=-1, keepdims=True)
    p = jnp.exp2(s - m).astype(v_ones.dtype)
    pv = jnp.dot(p, v_ones, preferred_element_type=jnp.float32)
    dv = v_ones.shape[1] - LANES
    return pv[:, :dv] / pv[:, dv:]


def _diff_attn_kernel(q_ref, k_ref, v_ref, lq1_ref, lk1_ref, lq2_ref, lk2_ref, g_ref, o_ref, v1_ref, *, rg):
    @pl.when(pl.program_id(2) == 0)
    def _():
        v1_ref[...] = _with_ones(v_ref[0])

    k = k_ref[0]
    v = v1_ref[...]
    lam = (jnp.exp(jnp.sum(lq1_ref[...] * lk1_ref[...], axis=-1, keepdims=True))
           - jnp.exp(jnp.sum(lq2_ref[...] * lk2_ref[...], axis=-1, keepdims=True))
           + DIFF_LAMBDA_INIT)
    lane = _lane_iota((rg, LANES))
    for g in range(q_ref.shape[1] // rg):
        q = q_ref[0, g * rg:(g + 1) * rg]
        zero = jnp.zeros_like(q)
        q12 = jnp.concatenate([jnp.where(lane < DIFF_HEAD_DIM, q, zero),
                               jnp.where(lane >= DIFF_HEAD_DIM, q, zero)], axis=0)
        s = lax.dot_general(q12, k, (((1,), (1,)), ((), ())), preferred_element_type=jnp.float32)
        a = _softmax_pv(s, v)
        o = a[:rg] - lam * a[rg:]
        o = o * _rms_scale(o, DIFF_SUBLN_EPS) * g_ref[...] * (1.0 - DIFF_LAMBDA_INIT)
        o_ref[0, g * rg:(g + 1) * rg] = o.astype(o_ref.dtype)


def _diff_attn(qkv3, lq1, lk1, lq2, lk2, subln_g, tq, rg):
    batch, seq, _ = qkv3.shape
    h = DIFF_HEADS
    return pl.pallas_call(
        functools.partial(_diff_attn_kernel, rg=rg),
        grid=(batch, h, seq // tq),
        in_specs=[pl.BlockSpec((1, tq, LANES), lambda b, hh, i: (b, i, hh)),
                  pl.BlockSpec((1, seq, LANES), lambda b, hh, i: (b, 0, h + hh)),
                  pl.BlockSpec((1, seq, LANES), lambda b, hh, i: (b, 0, 2 * h + hh)),
                  _resident(lq1.shape), _resident(lk1.shape), _resident(lq2.shape), _resident(lk2.shape),
                  _resident(subln_g.shape)],
        out_specs=pl.BlockSpec((1, tq, DIFF_V_DIM), lambda b, hh, i: (b, i, hh)),
        out_shape=jax.ShapeDtypeStruct((batch, seq, h * DIFF_V_DIM), jnp.bfloat16),
        scratch_shapes=[pltpu.VMEM((seq, DIFF_V_DIM + LANES), jnp.bfloat16)],
        compiler_params=_params("parallel", "parallel", "arbitrary"),
        name="diff_attn",
    )(qkv3, qkv3, qkv3, lq1, lk1, lq2, lk2, subln_g)


def _mla_attn_kernel(q_ref, k_ref, v_ref, o_ref, v1_ref, *, rg):
    @pl.when(pl.program_id(2) == 0)
    def _():
        v1_ref[...] = _with_ones(v_ref[0])

    k = k_ref[0, 0]
    v = v1_ref[...]
    for g in range(q_ref.shape[2] // rg):
        s = lax.dot_general(q_ref[0, 0, g * rg:(g + 1) * rg], k, (((1,), (1,)), ((), ())),
                            preferred_element_type=jnp.float32)
        o_ref[0, g * rg:(g + 1) * rg] = _softmax_pv(s, v).astype(o_ref.dtype)


def _mla_attn(q_cat, k_cat, v3, tq, rg):
    batch, heads, seq, dqk = q_cat.shape
    return pl.pallas_call(
        functools.partial(_mla_attn_kernel, rg=rg),
        grid=(batch, heads, seq // tq),
        in_specs=[pl.BlockSpec((1, 1, tq, dqk), lambda b, h, i: (b, h, i, 0)),
                  pl.BlockSpec((1, 1, seq, dqk), lambda b, h, i: (b, h, 0, 0)),
                  pl.BlockSpec((1, seq, MLA_V_DIM), lambda b, h, i: (b, 0, h))],
        out_specs=pl.BlockSpec((1, tq, MLA_V_DIM), lambda b, h, i: (b, i, h)),
        out_shape=jax.ShapeDtypeStruct((batch, seq, heads * MLA_V_DIM), jnp.bfloat16),
        scratch_shapes=[pltpu.VMEM((seq, MLA_V_DIM + LANES), jnp.bfloat16)],
        compiler_params=_params("parallel", "parallel", "arbitrary"),
        name="mla_attn",
    )(q_cat, k_cat, v3)


def _merge_out_kernel(oa_ref, ob_ref, sga_ref, sgb_ref, x_ref, woa_ref, wob_ref, wout_ref, h_ref):
    ya = jnp.dot(oa_ref[...], woa_ref[...], preferred_element_type=jnp.float32)
    yb = jnp.dot(ob_ref[...], wob_ref[...], preferred_element_type=jnp.float32)
    merged = sga_ref[...].astype(jnp.float32) * ya + sgb_ref[...].astype(jnp.float32) * yb
    h_ref[...] = x_ref[...] + jnp.dot(merged.astype(jnp.bfloat16), wout_ref[...],
                                       preferred_element_type=jnp.float32)


def _merge_out(o_a, o_b, gates, x2, w_oa, w_ob, w_out, tm):
    n_tok, d = x2.shape
    return pl.pallas_call(
        _merge_out_kernel,
        grid=(n_tok // tm,),
        in_specs=[pl.BlockSpec((tm, o_a.shape[1]), lambda i: (i, 0)),
                  pl.BlockSpec((tm, o_b.shape[1]), lambda i: (i, 0)),
                  pl.BlockSpec((tm, d), lambda i: (i, 0)),
                  pl.BlockSpec((tm, d), lambda i: (i, 1)),
                  pl.BlockSpec((tm, d), lambda i: (i, 0)),
                  _resident(w_oa.shape), _resident(w_ob.shape), _resident(w_out.shape)],
        out_specs=pl.BlockSpec((tm, d), lambda i: (i, 0)),
        out_shape=jax.ShapeDtypeStruct((n_tok, d), jnp.float32),
        compiler_params=_params("parallel"),
        name="merge_out",
    )(o_a, o_b, gates, gates, x2, w_oa, w_ob, w_out)


def _mem_kv_kernel(mem_ref, g_ref, w_ref, kv_ref):
    mf = mem_ref[0]
    mn = (mf * _rms_scale(mf, NORM_EPS) * g_ref[...]).astype(jnp.bfloat16)
    kv_ref[0] = jnp.dot(mn, w_ref[...], preferred_element_type=jnp.float32).astype(kv_ref.dtype)


def _mem_kv(mem, g, w_ckv):
    batch, m, d = mem.shape
    return pl.pallas_call(
        _mem_kv_kernel,
        grid=(batch,),
        in_specs=[pl.BlockSpec((1, m, d), lambda b: (b, 0, 0)), _resident(g.shape), _resident(w_ckv.shape)],
        out_specs=pl.BlockSpec((1, m, w_ckv.shape[1]), lambda b: (b, 0, 0)),
        out_shape=jax.ShapeDtypeStruct((batch, m, w_ckv.shape[1]), jnp.bfloat16),
        compiler_params=_params("parallel"),
        name="mem_kv",
    )(mem, g, w_ckv)


def _cross_router_kernel(h_ref, gc_ref, wcq_ref, kv_ref, wco_ref, gf_ref, wr_ref, br_ref,
                         h2_ref, eid_ref, rank_ref, wts_ref, cnt_ref, carry_ref):
    i = pl.program_id(0)

    @pl.when(i == 0)
    def _():
        carry_ref[...] = jnp.zeros_like(carry_ref)

    h1 = h_ref[...]
    tm = h1.shape[0]
    hn = (h1 * _rms_scale(h1, NORM_EPS) * gc_ref[...]).astype(jnp.bfloat16)
    q = jnp.dot(hn, wcq_ref[...], preferred_element_type=jnp.float32) * (CROSS_HEAD_DIM ** -0.5 * LOG2E)
    q = q.astype(jnp.bfloat16)
    kv_cols = CROSS_HEADS * CROSS_HEAD_DIM
    heads = []
    for hd in range(CROSS_HEADS):
        lo = hd * CROSS_HEAD_DIM
        kh = kv_ref[0, :, lo:lo + CROSS_HEAD_DIM]
        vh = kv_ref[0, :, kv_cols + lo:kv_cols + lo + CROSS_HEAD_DIM]
        s = lax.dot_general(q[:, lo:lo + CROSS_HEAD_DIM], kh, (((1,), (1,)), ((), ())),
                            preferred_element_type=jnp.float32)
        heads.append(_softmax_pv(s, _with_ones(vh)).astype(jnp.bfloat16))
    o = jnp.concatenate(heads, axis=-1)
    h2 = h1 + jnp.dot(o, wco_ref[...], preferred_element_type=jnp.float32)
    h2_ref[...] = h2

    t = h2 * _rms_scale(h2, NORM_EPS) * gf_ref[...]
    t_hi = t.astype(jnp.bfloat16)
    t_lo = (t - t_hi.astype(jnp.float32)).astype(jnp.bfloat16)
    hi = jnp.dot(t_hi, wr_ref[...], preferred_element_type=jnp.float32)
    lo = jnp.dot(t_lo, wr_ref[:, :LANES], preferred_element_type=jnp.float32)
    logits = hi[:, :LANES] + (hi[:, LANES:] + lo) + br_ref[...]
    lane = _lane_iota(logits.shape)
    neg = jnp.float32(-jnp.inf)
    big = jnp.int32(2 * LANES)
    is_group = lane < N_GROUPS
    lg = jnp.where(is_group, logits, neg)
    mg = jnp.max(lg, axis=-1, keepdims=True)
    g_idx = jnp.min(jnp.where(is_group & (logits == mg), lane, big), axis=-1, keepdims=True)
    g_p = 1.0 / jnp.sum(jnp.exp(lg - mg), axis=-1, keepdims=True)
    lo_lane = ROUTER_EXPERT_LANE0 + EXPERTS_PER_GROUP * g_idx
    in_grp = (lane >= lo_lane) & (lane < lo_lane + EXPERTS_PER_GROUP)
    l1 = jnp.max(jnp.where(in_grp, logits, neg), axis=-1, keepdims=True)
    i1 = jnp.min(jnp.where(in_grp & (logits == l1), lane, big), axis=-1, keepdims=True)
    rest = in_grp & (lane != i1)
    l2 = jnp.max(jnp.where(rest, logits, neg), axis=-1, keepdims=True)
    i2 = jnp.min(jnp.where(rest & (logits == l2), lane, big), axis=-1, keepdims=True)
    d = jnp.exp(l2 - l1)
    w1 = g_p / (1.0 + d)
    w2 = w1 * d

    oh1 = lane == i1
    oh2 = lane == i2
    cnt = (oh1 | oh2).astype(jnp.bfloat16)
    row = lax.broadcasted_iota(jnp.int32, (tm, tm), 0)
    col = lax.broadcasted_iota(jnp.int32, (tm, tm), 1)
    before = (col < row).astype(jnp.bfloat16)
    slot = jnp.dot(before, cnt, preferred_element_type=jnp.float32) + carry_ref[...]
    r1 = jnp.sum(jnp.where(oh1, slot, 0.0), axis=-1, keepdims=True)
    r2 = jnp.sum(jnp.where(oh2, slot, 0.0), axis=-1, keepdims=True)
    carry_ref[...] += jnp.sum(cnt.astype(jnp.float32), axis=0, keepdims=True)
    cnt_ref[...] = carry_ref[...]

    eye = row == col

    def to_row(c, dtype):
        return jnp.sum(jnp.where(eye, c.astype(jnp.float32), 0.0), axis=0, keepdims=True).astype(dtype)

    eid_ref[0] = jnp.concatenate([to_row(i1 - ROUTER_EXPERT_LANE0, jnp.int32),
                                  to_row(i2 - ROUTER_EXPERT_LANE0, jnp.int32)], axis=0)
    rank_ref[0] = jnp.concatenate([to_row(r1, jnp.int32), to_row(r2, jnp.int32)], axis=0)
    wts_ref[...] = jnp.where(_lane_iota((tm, 2)) == 0, w1, w2)


def _cross_router(h1, gc, w_cq, kv_mem, w_co, gf, w_r, b_r, seq, tm):
    n_tok, d = h1.shape
    per_b = seq // tm
    row2 = pl.BlockSpec((tm, 2), lambda i: (i, 0))
    lane2 = pl.BlockSpec((1, 2, tm), lambda i: (i, 0, 0))
    return pl.pallas_call(
        _cross_router_kernel,
        grid=(n_tok // tm,),
        in_specs=[pl.BlockSpec((tm, d), lambda i: (i, 0)),
                  _resident(gc.shape), _resident(w_cq.shape),
                  pl.BlockSpec((1,) + kv_mem.shape[1:], lambda i: (i // per_b, 0, 0)),
                  _resident(w_co.shape), _resident(gf.shape), _resident(w_r.shape), _resident(b_r.shape)],
        out_specs=[pl.BlockSpec((tm, d), lambda i: (i, 0)), lane2, lane2, row2,
                   pl.BlockSpec((1, LANES), lambda i: (0, 0))],
        out_shape=[jax.ShapeDtypeStruct((n_tok, d), jnp.float32),
                   jax.ShapeDtypeStruct((n_tok // tm, 2, tm), jnp.int32),
                   jax.ShapeDtypeStruct((n_tok // tm, 2, tm), jnp.int32),
                   jax.ShapeDtypeStruct((n_tok, 2), jnp.float32),
                   jax.ShapeDtypeStruct((1, LANES), jnp.float32)],
        scratch_shapes=[pltpu.VMEM((1, LANES), jnp.float32)],
        compiler_params=_params("arbitrary"),
        name="cross_router",
    )(h1, gc, w_cq, kv_mem, w_co, gf, w_r, b_r)


def _dispatch_kernel(dest_ref, ztail_ref, h_ref, g_ref, xb_ref, t_ref, sem, zsem):
    i = pl.program_id(0)
    tm = h_ref.shape[0]
    slot = i % 2

    @pl.when(i == 0)
    def _():
        t_ref[1] = jnp.zeros(t_ref.shape[1:], t_ref.dtype)
        tile_rows = tm * SUBLANES

        def zero_copy(e):
            start = pl.multiple_of(ztail_ref[0, e] * SUBLANES, tile_rows)
            return pltpu.make_async_copy(t_ref.at[1], xb_ref.at[pl.ds(start, tile_rows), :], zsem)

        for e in range(ztail_ref.shape[1]):
            @pl.when(ztail_ref[1, e] > 0)
            def _():
                zero_copy(e).start()
        for e in range(ztail_ref.shape[1]):
            @pl.when(ztail_ref[1, e] > 0)
            def _():
                zero_copy(e).wait()

    h2 = h_ref[...]
    t = h2 * _rms_scale(h2, NORM_EPS) * g_ref[...]
    half = t.shape[1] // 2
    _rows_to_tiles(t_ref.at[slot], _pack_bf16_pair(t[:, :half], t[:, half:]))

    for r in range(tm):
        for k in range(2):
            dst = pl.multiple_of(dest_ref[0, k * tm + r] * SUBLANES, SUBLANES)
            pltpu.make_async_copy(t_ref.at[slot, pl.ds(r * SUBLANES, SUBLANES), :],
                                  xb_ref.at[pl.ds(dst, SUBLANES), :], sem.at[slot]).start()

    def wait_tile(which):
        for _ in range(2):
            pltpu.make_async_copy(t_ref.at[which], xb_ref.at[pl.ds(0, tm * SUBLANES), :], sem.at[which]).wait()

    @pl.when(i > 0)
    def _():
        wait_tile(1 - slot)

    @pl.when(i == pl.num_programs(0) - 1)
    def _():
        wait_tile(slot)


def _dispatch(dest3, ztail, h2, gf, p_rows, tm):
    n_tok, d = h2.shape
    return pl.pallas_call(
        _dispatch_kernel,
        grid=(n_tok // tm,),
        in_specs=[pl.BlockSpec((None, 1, 2 * tm), lambda i: (i, 0, 0), memory_space=pltpu.SMEM),
                  pl.BlockSpec(memory_space=pltpu.SMEM),
                  pl.BlockSpec((tm, d), lambda i: (i, 0)),
                  _resident(gf.shape)],
        out_specs=pl.BlockSpec(memory_space=pl.ANY),
        out_shape=jax.ShapeDtypeStruct((p_rows * SUBLANES, LANES), jnp.uint32),
        scratch_shapes=[pltpu.VMEM((2, tm * SUBLANES, LANES), jnp.uint32), pltpu.SemaphoreType.DMA((2,)),
                        pltpu.SemaphoreType.DMA(())],
        compiler_params=_params("arbitrary"),
        name="moe_dispatch",
    )(dest3, ztail, h2, gf)


def _expert_kernel(be_ref, nact_ref, x_ref, wg_hbm, wu_hbm, wd_hbm, y_ref,
                   wg_f, wu_f, wd_f, wg_b, wu_b, wd_b, sem):
    i = pl.program_id(0)
    blk = i - 1
    nact = nact_ref[0]
    last_blk = pl.num_programs(0) - 2

    def weight_copies(e):
        return (pltpu.make_async_copy(wg_hbm.at[e], wg_f, sem),
                pltpu.make_async_copy(wu_hbm.at[e], wu_f, sem),
                pltpu.make_async_copy(wd_hbm.at[e], wd_f, sem))

    def fetch(e):
        for c in weight_copies(e):
            c.start()

    def land(e):
        for c in weight_copies(e):
            c.wait()
        half = wg_f.shape[0] // 2
        for c in range(SUBLANES):
            for part, src0 in enumerate((c * LANES, half + c * LANES)):
                dst0 = (2 * c + part) * LANES
                wg_b[dst0:dst0 + LANES, :] = wg_f[src0:src0 + LANES, :].astype(jnp.bfloat16)
                wu_b[dst0:dst0 + LANES, :] = wu_f[src0:src0 + LANES, :].astype(jnp.bfloat16)
        wd_b[...] = wd_f[...].astype(jnp.bfloat16)

    @pl.when(i == 0)
    def _():
        fetch(be_ref[0])
        land(be_ref[0])

    @pl.when((i > 0) & (blk < nact))
    def _():
        here = be_ref[blk]
        nxt = be_ref[jnp.minimum(blk + 1, last_blk)]
        prv = be_ref[jnp.maximum(blk - 1, 0)]
        seg_end = nact_ref[1 + N_EXPERTS + here] + nact_ref[1 + here]
        has_next = seg_end < nact
        after = be_ref[jnp.minimum(seg_end, last_blk)]
        is_first = (blk == 0) | (prv != here)
        is_last = (blk + 1 >= nact) | (nxt != here)

        @pl.when(is_first & has_next)
        def _():
            fetch(after)

        bm = x_ref.shape[0] // SUBLANES
        parts = []
        for chunk in _tiles_to_row_chunks(x_ref, bm):
            x_a, x_b = _unpack_bf16_pair(chunk)
            parts += [x_a.astype(jnp.bfloat16), x_b.astype(jnp.bfloat16)]
        xb = jnp.concatenate(parts, axis=1)
        gate = jnp.dot(xb, wg_b[...], preferred_element_type=jnp.float32)
        up = jnp.dot(xb, wu_b[...], preferred_element_type=jnp.float32)
        hid = (gate * _sigmoid(gate) * up).astype(jnp.bfloat16)
        y = jnp.dot(hid, wd_b[...], preferred_element_type=jnp.float32)
        half = y.shape[1] // 2
        _rows_to_tiles(y_ref, _pack_bf16_pair(y[:, :half], y[:, half:]))

        @pl.when(is_last & has_next)
        def _():
            land(after)

    @pl.when((i > 0) & (blk >= nact))
    def _():
        y_ref[...] = jnp.zeros_like(y_ref)


def _experts(block_expert, sched, xb, w_gate, w_up, w_down, bm):
    p_rows = xb.shape[0] // SUBLANES
    d = w_gate.shape[1]
    de = w_gate.shape[-1]

    def x_map(i, be, sc):
        return (jnp.clip(i - 1, 0, sc[0] - 1), 0)

    grid_spec = pltpu.PrefetchScalarGridSpec(
        num_scalar_prefetch=2,
        grid=(p_rows // bm + 1,),
        in_specs=[pl.BlockSpec((bm * SUBLANES, LANES), x_map),
                  pl.BlockSpec(memory_space=pl.ANY),
                  pl.BlockSpec(memory_space=pl.ANY),
                  pl.BlockSpec(memory_space=pl.ANY)],
        out_specs=pl.BlockSpec((bm * SUBLANES, LANES), lambda i, be, sc: (jnp.maximum(i - 1, 0), 0)),
        scratch_shapes=[pltpu.VMEM((d, de), jnp.float32), pltpu.VMEM((d, de), jnp.float32),
                        pltpu.VMEM((de, d), jnp.float32),
                        pltpu.VMEM((d, de), jnp.bfloat16), pltpu.VMEM((d, de), jnp.bfloat16),
                        pltpu.VMEM((de, d), jnp.bfloat16),
                        pltpu.SemaphoreType.DMA(())],
    )
    return pl.pallas_call(
        _expert_kernel,
        grid_spec=grid_spec,
        out_shape=jax.ShapeDtypeStruct(xb.shape, jnp.uint32),
        compiler_params=_params("arbitrary"),
        name="moe_experts",
    )(block_expert, sched, xb, w_gate, w_up, w_down)


def _combine_kernel(dest_ref, dest_next_ref, h_ref, wts_ref, g_ref, y_ref, o_ref, ybuf, sem):
    i = pl.program_id(0)
    tm = h_ref.shape[0]
    slot = i % 2

    def gather(idx_ref, which):
        for r in range(tm):
            for k in range(2):
                src = pl.multiple_of(idx_ref[0, k * tm + r] * SUBLANES, SUBLANES)
                pltpu.make_async_copy(y_ref.at[pl.ds(src, SUBLANES), :],
                                      ybuf.at[which, k, pl.ds(r * SUBLANES, SUBLANES), :], sem.at[which]).start()

    def wait_tile(which):
        for k in range(2):
            pltpu.make_async_copy(y_ref.at[pl.ds(0, tm * SUBLANES), :], ybuf.at[which, k], sem.at[which]).wait()

    @pl.when(i == 0)
    def _():
        gather(dest_ref, slot)

    wait_tile(slot)
    gather(dest_next_ref, 1 - slot)

    w = wts_ref[...]
    half = h_ref.shape[1] // 2
    lo_parts, hi_parts = [], []
    for c, (c0, c1) in enumerate(zip(_tiles_to_row_chunks(ybuf.at[slot, 0], tm),
                                      _tiles_to_row_chunks(ybuf.at[slot, 1], tm))):
        a0, b0 = _unpack_bf16_pair(c0)
        a1, b1 = _unpack_bf16_pair(c1)
        lo_parts.append(h_ref[:, c * LANES:(c + 1) * LANES] + w[:, 0:1] * a0 + w[:, 1:2] * a1)
        hi_parts.append(h_ref[:, half + c * LANES:half + (c + 1) * LANES] + w[:, 0:1] * b0 + w[:, 1:2] * b1)
    h3 = jnp.concatenate(lo_parts + hi_parts, axis=1)
    o_ref[...] = h3 * _rms_scale(h3, NORM_EPS) * g_ref[...]

    @pl.when(i == pl.num_programs(0) - 1)
    def _():
        wait_tile(1 - slot)


def _combine(dest3, h2, wts, g_final, y, tm):
    n_tok, d = h2.shape
    last = n_tok // tm - 1
    return pl.pallas_call(
        _combine_kernel,
        grid=(n_tok // tm,),
        in_specs=[pl.BlockSpec((None, 1, 2 * tm), lambda i: (i, 0, 0), memory_space=pltpu.SMEM),
                  pl.BlockSpec((None, 1, 2 * tm), lambda i: (jnp.minimum(i + 1, last), 0, 0),
                               memory_space=pltpu.SMEM),
                  pl.BlockSpec((tm, d), lambda i: (i, 0)),
                  pl.BlockSpec((tm, 2), lambda i: (i, 0)),
                  _resident(g_final.shape),
                  pl.BlockSpec(memory_space=pl.ANY)],
        out_specs=pl.BlockSpec((tm, d), lambda i: (i, 0)),
        out_shape=jax.ShapeDtypeStruct((n_tok, d), jnp.float32),
        scratch_shapes=[pltpu.VMEM((2, 2, tm * SUBLANES, LANES), jnp.uint32), pltpu.SemaphoreType.DMA((2,))],
        compiler_params=_params("arbitrary"),
        name="moe_combine",
    )(dest3, dest3, h2, wts, g_final, y)


def _transpose_w_in(w_in):
    return jnp.swapaxes(w_in, 0, 1).astype(jnp.bfloat16)


def _split_w_uq(w_uq):
    half = MLA_ROPE_DIM // 2
    w = w_uq.reshape(MLA_Q_RANK, MLA_HEADS, MLA_QK_DIM).transpose(1, 0, 2)
    pe = w[:, :, MLA_NOPE_DIM:]
    pe_swapped = jnp.concatenate([pe[:, :, half:], pe[:, :, :half]], axis=2)
    return jnp.concatenate([w, pe_swapped], axis=2).astype(jnp.bfloat16)


def _split_w_ukv(w_ukv):
    w = w_ukv.reshape(MLA_KV_RANK, MLA_HEADS, MLA_NOPE_DIM + MLA_V_DIM)
    wuk = w[:, :, :MLA_NOPE_DIM].transpose(1, 0, 2).astype(jnp.bfloat16)
    wuv = w[:, :, MLA_NOPE_DIM:].reshape(MLA_KV_RANK, MLA_HEADS * MLA_V_DIM).astype(jnp.bfloat16)
    return wuk, wuv


def kernel(x, mem, positions, attn_norm_g, w_in, diff_lambda_q1, diff_lambda_k1, diff_lambda_q2, diff_lambda_k2, diff_subln_g, w_o_diff, mla_q_norm_g, w_uq, mla_kv_norm_g, w_ukv, w_o_mla, w_out, cross_norm_g, mem_norm_g, w_cq, w_ckv, w_co, ffn_norm_g, w_router_group, b_router_group, w_router_expert, b_router_expert, w_expert_gate, w_expert_up, w_expert_down, final_norm_g):
    batch, seq, d = x.shape
    assert d == D_MODEL and w_in.shape[0] == 1, "single-layer kernel"
    n_tok = batch * seq
    bf = jnp.bfloat16
    x2 = x.reshape(n_tok, d)

    tm_proj = min(1024, seq)
    tm_row = min(256, seq)
    tm_cross = min(512, seq)
    tq = min(2048, seq)
    rg_diff = 128
    rg_mla = 256

    cos_t, sin_t = _rope_tables(positions, n_tok, tm_proj)

    g_attn = attn_norm_g[0].reshape(1, d)
    qkv, latent, gates = _inproj(x2, g_attn, _transpose_w_in(w_in[0]), cos_t, sin_t, tm_proj)

    o_a = _diff_attn(qkv.reshape(batch, seq, QKV_COLS),
                     diff_lambda_q1[0].reshape(1, -1), diff_lambda_k1[0].reshape(1, -1),
                     diff_lambda_q2[0].reshape(1, -1), diff_lambda_k2[0].reshape(1, -1),
                     diff_subln_g[0].reshape(1, -1), tq, rg_diff)

    wuk, wuv = _split_w_ukv(w_ukv[0])
    q_cat, k_cat, v_mla = _mla_proj(latent, mla_q_norm_g[0].reshape(1, -1), mla_kv_norm_g[0].reshape(1, -1),
                                    _split_w_uq(w_uq[0]), wuk, wuv, cos_t, sin_t, batch, seq, min(512, seq))
    o_b = _mla_attn(q_cat, k_cat, v_mla.reshape(batch, seq, MLA_HEADS * MLA_V_DIM), tq, rg_mla)

    h1 = _merge_out(o_a.reshape(n_tok, -1), o_b.reshape(n_tok, -1), gates, x2,
                    w_o_diff[0].astype(bf), w_o_mla[0].astype(bf), w_out[0].astype(bf), tm_row)

    kv_mem = _mem_kv(mem, mem_norm_g[0].reshape(1, d), w_ckv[0].astype(bf))
    n_router = N_GROUPS + N_EXPERTS
    w_r = jnp.concatenate([w_router_group[0].astype(jnp.float32), w_router_expert[0].astype(jnp.float32),
                           jnp.zeros((d, LANES - n_router), jnp.float32)], axis=1)
    w_r_hi = w_r.astype(bf)
    w_r_lo = (w_r - w_r_hi.astype(jnp.float32)).astype(bf)
    w_r = jnp.concatenate([w_r_hi, w_r_lo], axis=1)
    b_r = jnp.concatenate([b_router_group[0].astype(jnp.float32), b_router_expert[0].astype(jnp.float32),
                           jnp.zeros((LANES - n_router,), jnp.float32)]).reshape(1, LANES)
    g_ffn = ffn_norm_g[0].reshape(1, d)
    h2, eid, rank, wts, cnt = _cross_router(h1, cross_norm_g[0].reshape(1, d), w_cq[0].astype(bf), kv_mem,
                                            w_co[0].astype(bf), g_ffn, w_r, b_r, seq, tm_cross)

    bm = MOE_ROWS_PER_BLOCK
    assert tm_row == bm, "dispatch zeroes whole row blocks from its token-tile scratch"
    counts = cnt[0, ROUTER_EXPERT_LANE0:ROUTER_EXPERT_LANE0 + N_EXPERTS].astype(jnp.int32)
    padded = ((counts + bm - 1) // bm) * bm
    padded_end = jnp.cumsum(padded)
    padded_off = padded_end - padded
    seg_start = jnp.sum(jnp.where(eid[..., None] == jnp.arange(N_EXPERTS, dtype=jnp.int32), padded_off, 0), axis=-1)
    dest = seg_start + rank
    p_rows = ((2 * n_tok + bm - 1) // bm) * bm + N_EXPERTS * bm
    n_blocks = p_rows // bm
    n_active = (padded_end[-1] // bm).astype(jnp.int32)
    blk = jnp.minimum(jnp.arange(n_blocks, dtype=jnp.int32), n_active - 1)
    block_expert = jnp.sum((padded_end[None, :] <= (blk * bm)[:, None]).astype(jnp.int32), axis=1)
    block_expert = jnp.minimum(block_expert, N_EXPERTS - 1)
    dest3 = dest.reshape(-1, 2, tm_cross // tm_row, tm_row).transpose(0, 2, 1, 3).reshape(
        n_tok // tm_row, 1, 2 * tm_row)
    unused = n_active + jnp.arange(N_EXPERTS, dtype=jnp.int32)
    ztail = jnp.stack([jnp.concatenate([jnp.maximum(padded_end - bm, 0), jnp.minimum(unused, n_blocks - 1) * bm]),
                       jnp.concatenate([padded > 0, unused < n_blocks]).astype(jnp.int32)]).astype(jnp.int32)
    sched = jnp.concatenate([n_active.reshape(1), padded // bm, padded_off // bm]).astype(jnp.int32)

    xb = _dispatch(dest3, ztail, h2, g_ffn, p_rows, tm_row)
    y = _experts(block_expert, sched, xb, w_expert_gate[0], w_expert_up[0], w_expert_down[0], bm)
    out = _combine(dest3, h2, wts, final_norm_g.reshape(1, d), y, tm_row)
    return out.reshape(batch, seq, d)
```

```python
import functools
import math

import jax
import jax.numpy as jnp
from jax import lax
from jax.experimental import pallas as pl
from jax.experimental.pallas import tpu as pltpu

D_MODEL = 2048
ROPE_THETA = 500000.0
NORM_EPS = 1e-6

DIFF_HEADS = 8
DIFF_HEAD_DIM = 64
DIFF_V_DIM = 2 * DIFF_HEAD_DIM
DIFF_ROT = DIFF_HEAD_DIM // 4
DIFF_SUBLN_EPS = 1e-5
DIFF_LAMBDA_INIT = 0.8 - 0.6 * math.exp(-0.3 * 0)

MLA_HEADS = 8
MLA_Q_RANK = 512
MLA_KV_RANK = 256
MLA_NOPE_DIM = 128
MLA_ROPE_DIM = 64
MLA_V_DIM = 128
MLA_QK_DIM = MLA_NOPE_DIM + MLA_ROPE_DIM

CROSS_HEADS = 4
CROSS_HEAD_DIM = 128

N_GROUPS = 4
EXPERTS_PER_GROUP = 8
N_EXPERTS = N_GROUPS * EXPERTS_PER_GROUP
D_EXPERT = 512

LANES = 128
LOG2E = 1.4426950408889634
VMEM_LIMIT_BYTES = 56 * 1024 * 1024

ROUTER_EXPERT_LANE0 = N_GROUPS

QKV_COLS = 3 * DIFF_HEADS * DIFF_V_DIM
LATENT_COLS = 1024
GATE_COLS = 2 * D_MODEL
KPE_COL0 = MLA_Q_RANK + MLA_KV_RANK

MOE_ROWS_PER_BLOCK = 256
SUBLANES = 8


def _params(*semantics):
    return pltpu.CompilerParams(dimension_semantics=semantics, vmem_limit_bytes=VMEM_LIMIT_BYTES)


def _resident(shape):
    zeros = (0,) * len(shape)
    return pl.BlockSpec(shape, lambda *_: zeros, pipeline_mode=pl.Buffered(1))


def _rms_scale(xf, eps):
    return lax.rsqrt(jnp.mean(xf * xf, axis=-1, keepdims=True) + eps)


def _sigmoid(x):
    return 0.5 * jnp.tanh(0.5 * x) + 0.5


def _pack_bf16_pair(a, b):
    hi = lax.bitcast_convert_type(a.astype(jnp.bfloat16).astype(jnp.float32), jnp.uint32)
    lo = lax.bitcast_convert_type(b.astype(jnp.bfloat16).astype(jnp.float32), jnp.uint32)
    return hi | (lo >> 16)


def _unpack_bf16_pair(w):
    a = lax.bitcast_convert_type(w & jnp.uint32(0xFFFF0000), jnp.float32)
    b = lax.bitcast_convert_type(w << 16, jnp.float32)
    return a, b


def _rows_to_tiles(ref_view, packed):
    rows = packed.shape[0]
    for c in range(SUBLANES):
        ref_view[pl.ds(c, rows, stride=SUBLANES), :] = packed[:, c * LANES:(c + 1) * LANES]


def _tiles_to_row_chunks(ref_view, rows):
    return [ref_view[pl.ds(c, rows, stride=SUBLANES), :] for c in range(SUBLANES)]


def _lane_iota(shape):
    return lax.broadcasted_iota(jnp.int32, shape, len(shape) - 1)


def _trig_kernel(pos_ref, invf_ref, cos_ref, sin_ref):
    ang = pos_ref[...].astype(jnp.float32) * invf_ref[...]
    cos_ref[...] = jnp.cos(ang)
    sin_ref[...] = jnp.sin(ang)


def _rope_tables(positions, n_tok, tm):
    half_m = MLA_ROPE_DIM // 2
    half_d = DIFF_ROT // 2
    inv_m = jnp.float32(ROPE_THETA) ** (-jnp.arange(half_m, dtype=jnp.float32) * 2.0 / MLA_ROPE_DIM)
    inv_d = jnp.float32(ROPE_THETA) ** (-jnp.arange(half_d, dtype=jnp.float32) * 2.0 / DIFF_ROT)
    invf = jnp.concatenate([inv_m, inv_m, inv_d, inv_d,
                            jnp.zeros((DIFF_HEAD_DIM - DIFF_ROT,), jnp.float32)]).reshape(1, LANES)
    pos = positions.reshape(n_tok, 1)
    return pl.pallas_call(
        _trig_kernel,
        grid=(n_tok // tm,),
        in_specs=[pl.BlockSpec((tm, 1), lambda i: (i, 0)), _resident((1, LANES))],
        out_specs=[pl.BlockSpec((tm, LANES), lambda i: (i, 0))] * 2,
        out_shape=[jax.ShapeDtypeStruct((n_tok, LANES), jnp.float32)] * 2,
        compiler_params=_params("parallel"),
        name="rope_tables",
    )(pos, invf)


def _diff_rope_coeffs(cos_t, sin_t):
    lane = _lane_iota(cos_t.shape)
    upper = lane >= DIFF_HEAD_DIM
    cos_d = jnp.where(upper, cos_t, pltpu.roll(cos_t, DIFF_HEAD_DIM, 1))
    sin_d = jnp.where(upper, sin_t, pltpu.roll(sin_t, DIFF_HEAD_DIM, 1))
    in_head = lane % DIFF_HEAD_DIM
    half = DIFF_ROT // 2
    s_next = jnp.where(in_head < half, -sin_d, 0.0)
    s_prev = jnp.where((in_head >= half) & (in_head < DIFF_ROT), sin_d, 0.0)
    return cos_d, s_next, s_prev


def _mla_rope(pair, cos_t, sin_t):
    lane = _lane_iota(pair.shape)
    sin_signed = jnp.where(lane < MLA_ROPE_DIM // 2, -sin_t, sin_t)
    return pair * cos_t + pltpu.roll(pair, MLA_ROPE_DIM, 1) * sin_signed


INPROJ_TN = 1024
INPROJ_PIECE = 256
Q_TILES = DIFF_HEADS * DIFF_V_DIM // INPROJ_TN
ROPE_TILES = 2 * Q_TILES
QKV_TILES = QKV_COLS // INPROJ_TN
LATENT_TILES = LATENT_COLS // INPROJ_TN
GATE_TILES = GATE_COLS // INPROJ_TN
INPROJ_TILES = QKV_TILES + LATENT_TILES + GATE_TILES
GATE_ROW0 = QKV_COLS + KPE_COL0 + MLA_ROPE_DIM


def _inproj_kernel(x_ref, g_ref, w_ref, cos_ref, sin_ref, qkv_ref, lat_ref, gate_ref, xn_ref):
    j = pl.program_id(1)

    @pl.when(j == 0)
    def _():
        xf = x_ref[...]
        xn_ref[...] = (xf * _rms_scale(xf, NORM_EPS) * g_ref[...]).astype(jnp.bfloat16)

    def pieces(epilogue):
        for c in range(INPROJ_TN // INPROJ_PIECE):
            cols = slice(c * INPROJ_PIECE, (c + 1) * INPROJ_PIECE)
            acc = lax.dot_general(xn_ref[...], w_ref[cols, :], (((1,), (1,)), ((), ())),
                                  preferred_element_type=jnp.float32)
            epilogue(acc, cols)

    @pl.when(j < ROPE_TILES)
    def _():
        cos_d, s_next, s_prev = _diff_rope_coeffs(cos_ref[...], sin_ref[...])
        qscale = jnp.where(j < Q_TILES, DIFF_HEAD_DIM ** -0.5 * LOG2E, 1.0).astype(jnp.float32)

        def rope(acc, cols):
            for c in range(INPROJ_PIECE // LANES):
                xc = acc[:, c * LANES:(c + 1) * LANES]
                rot = (xc * cos_d + pltpu.roll(xc, LANES - DIFF_ROT // 2, 1) * s_next
                       + pltpu.roll(xc, DIFF_ROT // 2, 1) * s_prev)
                lo = cols.start + c * LANES
                qkv_ref[:, lo:lo + LANES] = (rot * qscale).astype(qkv_ref.dtype)

        pieces(rope)

    @pl.when((j >= ROPE_TILES) & (j < QKV_TILES))
    def _():
        def value(acc, cols):
            qkv_ref[:, cols] = acc.astype(qkv_ref.dtype)

        pieces(value)

    @pl.when((j >= QKV_TILES) & (j < QKV_TILES + LATENT_TILES))
    def _():
        def latent(acc, cols):
            if cols.start <= KPE_COL0 < cols.stop:
                c0 = KPE_COL0 - cols.start
                v = acc[:, c0:c0 + LANES]
                lane = _lane_iota(v.shape)
                half = MLA_ROPE_DIM // 2
                swapped = jnp.where(lane < MLA_ROPE_DIM + half, pltpu.roll(v, half, 1),
                                    pltpu.roll(v, MLA_ROPE_DIM + half, 1))
                parts = [acc[:, :c0], jnp.where(lane < MLA_ROPE_DIM, v, swapped), acc[:, c0 + LANES:]]
                acc = jnp.concatenate([p for p in parts if p.shape[1]], axis=1)
            lat_ref[:, cols] = acc

        pieces(latent)

    @pl.when(j >= QKV_TILES + LATENT_TILES)
    def _():
        def gate(acc, cols):
            gate_ref[:, cols] = _sigmoid(acc).astype(gate_ref.dtype)

        pieces(gate)


def _inproj(x2, g, w_all, cos_t, sin_t, tm):
    n_tok, d = x2.shape
    tn = INPROJ_TN
    lat0 = QKV_TILES
    gate0 = QKV_TILES + LATENT_TILES
    return pl.pallas_call(
        _inproj_kernel,
        grid=(n_tok // tm, INPROJ_TILES),
        in_specs=[pl.BlockSpec((tm, d), lambda i, j: (i, 0)),
                  _resident((1, d)),
                  pl.BlockSpec((pl.Element(tn), pl.Element(d)),
                               lambda i, j: (pl.multiple_of(jnp.where(j < gate0, j * tn, GATE_ROW0 + (j - gate0) * tn), 16), 0)),
                  pl.BlockSpec((tm, LANES), lambda i, j: (i, 0)),
                  pl.BlockSpec((tm, LANES), lambda i, j: (i, 0))],
        out_specs=[pl.BlockSpec((tm, tn), lambda i, j: (i, jnp.clip(j, 0, QKV_TILES - 1))),
                   pl.BlockSpec((tm, tn), lambda i, j: (i, jnp.clip(j - lat0, 0, LATENT_TILES - 1))),
                   pl.BlockSpec((tm, tn), lambda i, j: (i, jnp.clip(j - gate0, 0, GATE_TILES - 1)))],
        out_shape=[jax.ShapeDtypeStruct((n_tok, QKV_COLS), jnp.bfloat16),
                   jax.ShapeDtypeStruct((n_tok, LATENT_COLS), jnp.float32),
                   jax.ShapeDtypeStruct((n_tok, GATE_COLS), jnp.bfloat16)],
        scratch_shapes=[pltpu.VMEM((tm, d), jnp.bfloat16)],
        compiler_params=_params("parallel", "arbitrary"),
        name="inproj",
    )(x2, g, w_all, cos_t, sin_t)


def _mla_proj_kernel(c_ref, gq_ref, gkv_ref, wuq_ref, wuk_ref, wuv_ref, cos_ref, sin_ref,
                     q_ref, k_ref, v_ref):
    cos_t = cos_ref[...]
    sin_t = sin_ref[...]
    cq = c_ref[:, :MLA_Q_RANK]
    cqn = (cq * _rms_scale(cq, NORM_EPS) * gq_ref[...]).astype(jnp.bfloat16)
    ckv = c_ref[:, MLA_Q_RANK:KPE_COL0]
    ckvn = (ckv * _rms_scale(ckv, NORM_EPS) * gkv_ref[...]).astype(jnp.bfloat16)
    kpe = _mla_rope(c_ref[:, KPE_COL0:KPE_COL0 + LANES], cos_t, sin_t)[:, :MLA_ROPE_DIM].astype(k_ref.dtype)
    qscale = MLA_QK_DIM ** -0.5 * LOG2E
    for h in range(MLA_HEADS):
        r = jnp.dot(cqn, wuq_ref[h], preferred_element_type=jnp.float32)
        q_ref[0, h, :, :MLA_NOPE_DIM] = (r[:, :MLA_NOPE_DIM] * qscale).astype(q_ref.dtype)
        qpe = _mla_rope(r[:, MLA_NOPE_DIM:], cos_t, sin_t)[:, :MLA_ROPE_DIM]
        q_ref[0, h, :, MLA_NOPE_DIM:] = (qpe * qscale).astype(q_ref.dtype)
        kn = jnp.dot(ckvn, wuk_ref[h], preferred_element_type=jnp.float32)
        k_ref[0, h, :, :MLA_NOPE_DIM] = kn.astype(k_ref.dtype)
        k_ref[0, h, :, MLA_NOPE_DIM:] = kpe
    v_ref[...] = jnp.dot(ckvn, wuv_ref[...], preferred_element_type=jnp.float32).astype(v_ref.dtype)


def _mla_proj(latent, gq, gkv, wuq, wuk, wuv, cos_t, sin_t, batch, seq, tm):
    n_tok = latent.shape[0]
    per_b = seq // tm
    head_spec = pl.BlockSpec((1, MLA_HEADS, tm, MLA_QK_DIM), lambda i: (i // per_b, 0, i % per_b, 0))
    head_shape = jax.ShapeDtypeStruct((batch, MLA_HEADS, seq, MLA_QK_DIM), jnp.bfloat16)
    return pl.pallas_call(
        _mla_proj_kernel,
        grid=(n_tok // tm,),
        in_specs=[pl.BlockSpec((tm, LATENT_COLS), lambda i: (i, 0)),
                  _resident(gq.shape), _resident(gkv.shape),
                  _resident(wuq.shape), _resident(wuk.shape), _resident(wuv.shape),
                  pl.BlockSpec((tm, LANES), lambda i: (i, 0)),
                  pl.BlockSpec((tm, LANES), lambda i: (i, 0))],
        out_specs=[head_spec, head_spec,
                   pl.BlockSpec((tm, MLA_HEADS * MLA_V_DIM), lambda i: (i, 0))],
        out_shape=[head_shape, head_shape,
                   jax.ShapeDtypeStruct((n_tok, MLA_HEADS * MLA_V_DIM), jnp.bfloat16)],
        compiler_params=_params("parallel"),
        name="mla_proj",
    )(latent, gq, gkv, wuq, wuk, wuv, cos_t, sin_t)


def _with_ones(v):
    return jnp.concatenate([v, jnp.ones((v.shape[0], LANES), v.dtype)], axis=-1)


def _softmax_pv(s, v_ones):
    m = jnp.max(s, axis=-1, keepdims=True)
    p = jnp.exp2(s - m).astype(v_ones.dtype)
    pv = jnp.dot(p, v_ones, preferred_element_type=jnp.float32)
    dv = v_ones.shape[1] - LANES
    return pv[:, :dv] / pv[:, dv:]


ATTN_HEADS_PER_STEP = 2


def _diff_attn_kernel(q_ref, k_ref, v_ref, lq1_ref, lk1_ref, lq2_ref, lk2_ref, g_ref, o_ref, v1_ref, *, rg):
    heads = q_ref.shape[2] // LANES

    @pl.when(pl.program_id(2) == 0)
    def _():
        for h in range(heads):
            v1_ref[h] = _with_ones(v_ref[0, :, h * DIFF_V_DIM:(h + 1) * DIFF_V_DIM])

    lam = (jnp.exp(jnp.sum(lq1_ref[...] * lk1_ref[...], axis=-1, keepdims=True))
           - jnp.exp(jnp.sum(lq2_ref[...] * lk2_ref[...], axis=-1, keepdims=True))
           + DIFF_LAMBDA_INIT)
    lane = _lane_iota((rg, LANES))
    for g in range(q_ref.shape[1] // rg):
        for h in range(heads):
            cols = slice(h * LANES, (h + 1) * LANES)
            q = q_ref[0, g * rg:(g + 1) * rg, cols]
            k = k_ref[0, :, cols]
            zero = jnp.zeros_like(q)
            q12 = jnp.concatenate([jnp.where(lane < DIFF_HEAD_DIM, q, zero),
                                   jnp.where(lane >= DIFF_HEAD_DIM, q, zero)], axis=0)
            s = lax.dot_general(q12, k, (((1,), (1,)), ((), ())), preferred_element_type=jnp.float32)
            a = _softmax_pv(s, v1_ref[h])
            o = a[:rg] - lam * a[rg:]
            o = o * _rms_scale(o, DIFF_SUBLN_EPS) * g_ref[...] * (1.0 - DIFF_LAMBDA_INIT)
            o_ref[0, g * rg:(g + 1) * rg, cols] = o.astype(o_ref.dtype)


def _diff_attn(qkv3, lq1, lk1, lq2, lk2, subln_g, tq, rg):
    batch, seq, _ = qkv3.shape
    hp = ATTN_HEADS_PER_STEP
    steps = DIFF_HEADS // hp
    width = hp * DIFF_V_DIM
    return pl.pallas_call(
        functools.partial(_diff_attn_kernel, rg=rg),
        grid=(batch, steps, seq // tq),
        in_specs=[pl.BlockSpec((1, tq, width), lambda b, hh, i: (b, i, hh)),
                  pl.BlockSpec((1, seq, width), lambda b, hh, i: (b, 0, steps + hh)),
                  pl.BlockSpec((1, seq, width), lambda b, hh, i: (b, 0, 2 * steps + hh)),
                  _resident(lq1.shape), _resident(lk1.shape), _resident(lq2.shape), _resident(lk2.shape),
                  _resident(subln_g.shape)],
        out_specs=pl.BlockSpec((1, tq, width), lambda b, hh, i: (b, i, hh)),
        out_shape=jax.ShapeDtypeStruct((batch, seq, DIFF_HEADS * DIFF_V_DIM), jnp.bfloat16),
        scratch_shapes=[pltpu.VMEM((hp, seq, DIFF_V_DIM + LANES), jnp.bfloat16)],
        compiler_params=_params("parallel", "parallel", "arbitrary"),
        name="diff_attn",
    )(qkv3, qkv3, qkv3, lq1, lk1, lq2, lk2, subln_g)


def _mla_attn_kernel(q_ref, k_ref, v_ref, o_ref, v1_ref, *, rg):
    heads = q_ref.shape[1]

    @pl.when(pl.program_id(2) == 0)
    def _():
        for h in range(heads):
            v1_ref[h] = _with_ones(v_ref[0, :, h * MLA_V_DIM:(h + 1) * MLA_V_DIM])

    for g in range(q_ref.shape[2] // rg):
        for h in range(heads):
            s = lax.dot_general(q_ref[0, h, g * rg:(g + 1) * rg], k_ref[0, h], (((1,), (1,)), ((), ())),
                                preferred_element_type=jnp.float32)
            o_ref[0, g * rg:(g + 1) * rg, h * MLA_V_DIM:(h + 1) * MLA_V_DIM] = (
                _softmax_pv(s, v1_ref[h]).astype(o_ref.dtype))


def _mla_attn(q_cat, k_cat, v3, tq, rg):
    batch, heads, seq, dqk = q_cat.shape
    hp = ATTN_HEADS_PER_STEP
    return pl.pallas_call(
        functools.partial(_mla_attn_kernel, rg=rg),
        grid=(batch, heads // hp, seq // tq),
        in_specs=[pl.BlockSpec((1, hp, tq, dqk), lambda b, h, i: (b, h, i, 0)),
                  pl.BlockSpec((1, hp, seq, dqk), lambda b, h, i: (b, h, 0, 0)),
                  pl.BlockSpec((1, seq, hp * MLA_V_DIM), lambda b, h, i: (b, 0, h))],
        out_specs=pl.BlockSpec((1, tq, hp * MLA_V_DIM), lambda b, h, i: (b, i, h)),
        out_shape=jax.ShapeDtypeStruct((batch, seq, heads * MLA_V_DIM), jnp.bfloat16),
        scratch_shapes=[pltpu.VMEM((hp, seq, MLA_V_DIM + LANES), jnp.bfloat16)],
        compiler_params=_params("parallel", "parallel", "arbitrary"),
        name="mla_attn",
    )(q_cat, k_cat, v3)


def _merge_out_kernel(oa_ref, ob_ref, sga_ref, sgb_ref, x_ref, woa_ref, wob_ref, wout_ref, h_ref):
    ya = jnp.dot(oa_ref[...], woa_ref[...], preferred_element_type=jnp.float32)
    yb = jnp.dot(ob_ref[...], wob_ref[...], preferred_element_type=jnp.float32)
    merged = sga_ref[...].astype(jnp.float32) * ya + sgb_ref[...].astype(jnp.float32) * yb
    h_ref[...] = x_ref[...] + jnp.dot(merged.astype(jnp.bfloat16), wout_ref[...],
                                       preferred_element_type=jnp.float32)


def _merge_out(o_a, o_b, gates, x2, w_oa, w_ob, w_out, tm):
    n_tok, d = x2.shape
    return pl.pallas_call(
        _merge_out_kernel,
        grid=(n_tok // tm,),
        in_specs=[pl.BlockSpec((tm, o_a.shape[1]), lambda i: (i, 0)),
                  pl.BlockSpec((tm, o_b.shape[1]), lambda i: (i, 0)),
                  pl.BlockSpec((tm, d), lambda i: (i, 0)),
                  pl.BlockSpec((tm, d), lambda i: (i, 1)),
                  pl.BlockSpec((tm, d), lambda i: (i, 0)),
                  _resident(w_oa.shape), _resident(w_ob.shape), _resident(w_out.shape)],
        out_specs=pl.BlockSpec((tm, d), lambda i: (i, 0)),
        out_shape=jax.ShapeDtypeStruct((n_tok, d), jnp.float32),
        compiler_params=_params("parallel"),
        name="merge_out",
    )(o_a, o_b, gates, gates, x2, w_oa, w_ob, w_out)


def _mem_kv_kernel(mem_ref, g_ref, w_ref, kv_ref):
    mf = mem_ref[0]
    mn = (mf * _rms_scale(mf, NORM_EPS) * g_ref[...]).astype(jnp.bfloat16)
    kv_ref[0] = jnp.dot(mn, w_ref[...], preferred_element_type=jnp.float32).astype(kv_ref.dtype)


def _mem_kv(mem, g, w_ckv):
    batch, m, d = mem.shape
    return pl.pallas_call(
        _mem_kv_kernel,
        grid=(batch,),
        in_specs=[pl.BlockSpec((1, m, d), lambda b: (b, 0, 0)), _resident(g.shape), _resident(w_ckv.shape)],
        out_specs=pl.BlockSpec((1, m, w_ckv.shape[1]), lambda b: (b, 0, 0)),
        out_shape=jax.ShapeDtypeStruct((batch, m, w_ckv.shape[1]), jnp.bfloat16),
        compiler_params=_params("parallel"),
        name="mem_kv",
    )(mem, g, w_ckv)


def _cross_router_kernel(h_ref, gc_ref, wcq_ref, kv_ref, wco_ref, gf_ref, wr_ref, br_ref,
                         h2_ref, eid_ref, rank_ref, wts_ref, cnt_ref, carry_ref):
    i = pl.program_id(0)

    @pl.when(i == 0)
    def _():
        carry_ref[...] = jnp.zeros_like(carry_ref)

    h1 = h_ref[...]
    tm = h1.shape[0]
    hn = (h1 * _rms_scale(h1, NORM_EPS) * gc_ref[...]).astype(jnp.bfloat16)
    q = jnp.dot(hn, wcq_ref[...], preferred_element_type=jnp.float32) * (CROSS_HEAD_DIM ** -0.5 * LOG2E)
    q = q.astype(jnp.bfloat16)
    kv_cols = CROSS_HEADS * CROSS_HEAD_DIM
    heads = []
    for hd in range(CROSS_HEADS):
        lo = hd * CROSS_HEAD_DIM
        kh = kv_ref[0, :, lo:lo + CROSS_HEAD_DIM]
        vh = kv_ref[0, :, kv_cols + lo:kv_cols + lo + CROSS_HEAD_DIM]
        s = lax.dot_general(q[:, lo:lo + CROSS_HEAD_DIM], kh, (((1,), (1,)), ((), ())),
                            preferred_element_type=jnp.float32)
        heads.append(_softmax_pv(s, _with_ones(vh)).astype(jnp.bfloat16))
    o = jnp.concatenate(heads, axis=-1)
    h2 = h1 + jnp.dot(o, wco_ref[...], preferred_element_type=jnp.float32)
    h2_ref[...] = h2

    t = h2 * _rms_scale(h2, NORM_EPS) * gf_ref[...]
    t_hi = t.astype(jnp.bfloat16)
    t_lo = (t - t_hi.astype(jnp.float32)).astype(jnp.bfloat16)
    hi = jnp.dot(t_hi, wr_ref[...], preferred_element_type=jnp.float32)
    lo = jnp.dot(t_lo, wr_ref[:, :LANES], preferred_element_type=jnp.float32)
    logits = hi[:, :LANES] + (hi[:, LANES:] + lo) + br_ref[...]
    lane = _lane_iota(logits.shape)
    neg = jnp.float32(-jnp.inf)
    big = jnp.int32(2 * LANES)
    is_group = lane < N_GROUPS
    lg = jnp.where(is_group, logits, neg)
    mg = jnp.max(lg, axis=-1, keepdims=True)
    g_idx = jnp.min(jnp.where(is_group & (logits == mg), lane, big), axis=-1, keepdims=True)
    g_p = 1.0 / jnp.sum(jnp.exp(lg - mg), axis=-1, keepdims=True)
    lo_lane = ROUTER_EXPERT_LANE0 + EXPERTS_PER_GROUP * g_idx
    in_grp = (lane >= lo_lane) & (lane < lo_lane + EXPERTS_PER_GROUP)
    l1 = jnp.max(jnp.where(in_grp, logits, neg), axis=-1, keepdims=True)
    i1 = jnp.min(jnp.where(in_grp & (logits == l1), lane, big), axis=-1, keepdims=True)
    rest = in_grp & (lane != i1)
    l2 = jnp.max(jnp.where(rest, logits, neg), axis=-1, keepdims=True)
    i2 = jnp.min(jnp.where(rest & (logits == l2), lane, big), axis=-1, keepdims=True)
    d = jnp.exp(l2 - l1)
    w1 = g_p / (1.0 + d)
    w2 = w1 * d

    oh1 = lane == i1
    oh2 = lane == i2
    cnt = (oh1 | oh2).astype(jnp.bfloat16)
    row = lax.broadcasted_iota(jnp.int32, (tm, tm), 0)
    col = lax.broadcasted_iota(jnp.int32, (tm, tm), 1)
    before = (col < row).astype(jnp.bfloat16)
    slot = jnp.dot(before, cnt, preferred_element_type=jnp.float32) + carry_ref[...]
    r1 = jnp.sum(jnp.where(oh1, slot, 0.0), axis=-1, keepdims=True)
    r2 = jnp.sum(jnp.where(oh2, slot, 0.0), axis=-1, keepdims=True)
    carry_ref[...] += jnp.sum(cnt.astype(jnp.float32), axis=0, keepdims=True)
    cnt_ref[...] = carry_ref[...]

    eye = row == col

    def to_row(c, dtype):
        return jnp.sum(jnp.where(eye, c.astype(jnp.float32), 0.0), axis=0, keepdims=True).astype(dtype)

    eid_ref[0] = jnp.concatenate([to_row(i1 - ROUTER_EXPERT_LANE0, jnp.int32),
                                  to_row(i2 - ROUTER_EXPERT_LANE0, jnp.int32)], axis=0)
    rank_ref[0] = jnp.concatenate([to_row(r1, jnp.int32), to_row(r2, jnp.int32)], axis=0)
    wts_ref[...] = jnp.where(_lane_iota((tm, 2)) == 0, w1, w2)


def _cross_router(h1, gc, w_cq, kv_mem, w_co, gf, w_r, b_r, seq, tm):
    n_tok, d = h1.shape
    per_b = seq // tm
    row2 = pl.BlockSpec((tm, 2), lambda i: (i, 0))
    lane2 = pl.BlockSpec((1, 2, tm), lambda i: (i, 0, 0))
    return pl.pallas_call(
        _cross_router_kernel,
        grid=(n_tok // tm,),
        in_specs=[pl.BlockSpec((tm, d), lambda i: (i, 0)),
                  _resident(gc.shape), _resident(w_cq.shape),
                  pl.BlockSpec((1,) + kv_mem.shape[1:], lambda i: (i // per_b, 0, 0)),
                  _resident(w_co.shape), _resident(gf.shape), _resident(w_r.shape), _resident(b_r.shape)],
        out_specs=[pl.BlockSpec((tm, d), lambda i: (i, 0)), lane2, lane2, row2,
                   pl.BlockSpec((1, LANES), lambda i: (0, 0))],
        out_shape=[jax.ShapeDtypeStruct((n_tok, d), jnp.float32),
                   jax.ShapeDtypeStruct((n_tok // tm, 2, tm), jnp.int32),
                   jax.ShapeDtypeStruct((n_tok // tm, 2, tm), jnp.int32),
                   jax.ShapeDtypeStruct((n_tok, 2), jnp.float32),
                   jax.ShapeDtypeStruct((1, LANES), jnp.float32)],
        scratch_shapes=[pltpu.VMEM((1, LANES), jnp.float32)],
        compiler_params=_params("arbitrary"),
        name="cross_router",
    )(h1, gc, w_cq, kv_mem, w_co, gf, w_r, b_r)


def _dispatch_kernel(dest_ref, ztail_ref, h_ref, g_ref, xb_ref, t_ref, sem, zsem):
    i = pl.program_id(0)
    tm = h_ref.shape[0]
    slot = i % 2

    @pl.when(i == 0)
    def _():
        t_ref[1] = jnp.zeros(t_ref.shape[1:], t_ref.dtype)
        tile_rows = tm * SUBLANES

        def zero_copy(e):
            start = pl.multiple_of(ztail_ref[0, e] * SUBLANES, tile_rows)
            return pltpu.make_async_copy(t_ref.at[1], xb_ref.at[pl.ds(start, tile_rows), :], zsem)

        for e in range(ztail_ref.shape[1]):
            @pl.when(ztail_ref[1, e] > 0)
            def _():
                zero_copy(e).start()
        for e in range(ztail_ref.shape[1]):
            @pl.when(ztail_ref[1, e] > 0)
            def _():
                zero_copy(e).wait()

    h2 = h_ref[...]
    t = h2 * _rms_scale(h2, NORM_EPS) * g_ref[...]
    half = t.shape[1] // 2
    _rows_to_tiles(t_ref.at[slot], _pack_bf16_pair(t[:, :half], t[:, half:]))

    for r in range(tm):
        for k in range(2):
            dst = pl.multiple_of(dest_ref[0, k * tm + r] * SUBLANES, SUBLANES)
            pltpu.make_async_copy(t_ref.at[slot, pl.ds(r * SUBLANES, SUBLANES), :],
                                  xb_ref.at[pl.ds(dst, SUBLANES), :], sem.at[slot]).start()

    def wait_tile(which):
        for _ in range(2):
            pltpu.make_async_copy(t_ref.at[which], xb_ref.at[pl.ds(0, tm * SUBLANES), :], sem.at[which]).wait()

    @pl.when(i > 0)
    def _():
        wait_tile(1 - slot)

    @pl.when(i == pl.num_programs(0) - 1)
    def _():
        wait_tile(slot)


def _dispatch(dest3, ztail, h2, gf, p_rows, tm):
    n_tok, d = h2.shape
    return pl.pallas_call(
        _dispatch_kernel,
        grid=(n_tok // tm,),
        in_specs=[pl.BlockSpec((None, 1, 2 * tm), lambda i: (i, 0, 0), memory_space=pltpu.SMEM),
                  pl.BlockSpec(memory_space=pltpu.SMEM),
                  pl.BlockSpec((tm, d), lambda i: (i, 0)),
                  _resident(gf.shape)],
        out_specs=pl.BlockSpec(memory_space=pl.ANY),
        out_shape=jax.ShapeDtypeStruct((p_rows * SUBLANES, LANES), jnp.uint32),
        scratch_shapes=[pltpu.VMEM((2, tm * SUBLANES, LANES), jnp.uint32), pltpu.SemaphoreType.DMA((2,)),
                        pltpu.SemaphoreType.DMA(())],
        compiler_params=_params("arbitrary"),
        name="moe_dispatch",
    )(dest3, ztail, h2, gf)


def _expert_kernel(be_ref, nact_ref, x_ref, wg_hbm, wu_hbm, wd_hbm, y_ref,
                   wg_f, wu_f, wd_f, wg_b, wu_b, wd_b, sem):
    i = pl.program_id(0)
    blk = i - 1
    nact = nact_ref[0]
    last_blk = pl.num_programs(0) - 2

    def weight_copies(e):
        return (pltpu.make_async_copy(wg_hbm.at[e], wg_f, sem),
                pltpu.make_async_copy(wu_hbm.at[e], wu_f, sem),
                pltpu.make_async_copy(wd_hbm.at[e], wd_f, sem))

    def fetch(e):
        for c in weight_copies(e):
            c.start()

    def land(e):
        for c in weight_copies(e):
            c.wait()
        half = wg_f.shape[0] // 2
        for c in range(SUBLANES):
            for part, src0 in enumerate((c * LANES, half + c * LANES)):
                dst0 = (2 * c + part) * LANES
                wg_b[dst0:dst0 + LANES, :] = wg_f[src0:src0 + LANES, :].astype(jnp.bfloat16)
                wu_b[dst0:dst0 + LANES, :] = wu_f[src0:src0 + LANES, :].astype(jnp.bfloat16)
        wd_b[...] = wd_f[...].astype(jnp.bfloat16)

    @pl.when(i == 0)
    def _():
        fetch(be_ref[0])
        land(be_ref[0])

    @pl.when((i > 0) & (blk < nact))
    def _():
        here = be_ref[blk]
        nxt = be_ref[jnp.minimum(blk + 1, last_blk)]
        prv = be_ref[jnp.maximum(blk - 1, 0)]
        seg_end = nact_ref[1 + N_EXPERTS + here] + nact_ref[1 + here]
        has_next = seg_end < nact
        after = be_ref[jnp.minimum(seg_end, last_blk)]
        is_first = (blk == 0) | (prv != here)
        is_last = (blk + 1 >= nact) | (nxt != here)

        @pl.when(is_first & has_next)
        def _():
            fetch(after)

        bm = x_ref.shape[0] // SUBLANES
        parts = []
        for chunk in _tiles_to_row_chunks(x_ref, bm):
            x_a, x_b = _unpack_bf16_pair(chunk)
            parts += [x_a.astype(jnp.bfloat16), x_b.astype(jnp.bfloat16)]
        xb = jnp.concatenate(parts, axis=1)
        gate = jnp.dot(xb, wg_b[...], preferred_element_type=jnp.float32)
        up = jnp.dot(xb, wu_b[...], preferred_element_type=jnp.float32)
        hid = (gate * _sigmoid(gate) * up).astype(jnp.bfloat16)
        y = jnp.dot(hid, wd_b[...], preferred_element_type=jnp.float32)
        half = y.shape[1] // 2
        _rows_to_tiles(y_ref, _pack_bf16_pair(y[:, :half], y[:, half:]))

        @pl.when(is_last & has_next)
        def _():
            land(after)

    @pl.when((i > 0) & (blk >= nact))
    def _():
        y_ref[...] = jnp.zeros_like(y_ref)


def _experts(block_expert, sched, xb, w_gate, w_up, w_down, bm):
    p_rows = xb.shape[0] // SUBLANES
    d = w_gate.shape[1]
    de = w_gate.shape[-1]

    def x_map(i, be, sc):
        return (jnp.clip(i - 1, 0, sc[0] - 1), 0)

    grid_spec = pltpu.PrefetchScalarGridSpec(
        num_scalar_prefetch=2,
        grid=(p_rows // bm + 1,),
        in_specs=[pl.BlockSpec((bm * SUBLANES, LANES), x_map),
                  pl.BlockSpec(memory_space=pl.ANY),
                  pl.BlockSpec(memory_space=pl.ANY),
                  pl.BlockSpec(memory_space=pl.ANY)],
        out_specs=pl.BlockSpec((bm * SUBLANES, LANES), lambda i, be, sc: (jnp.maximum(i - 1, 0), 0)),
        scratch_shapes=[pltpu.VMEM((d, de), jnp.float32), pltpu.VMEM((d, de), jnp.float32),
                        pltpu.VMEM((de, d), jnp.float32),
                        pltpu.VMEM((d, de), jnp.bfloat16), pltpu.VMEM((d, de), jnp.bfloat16),
                        pltpu.VMEM((de, d), jnp.bfloat16),
                        pltpu.SemaphoreType.DMA(())],
    )
    return pl.pallas_call(
        _expert_kernel,
        grid_spec=grid_spec,
        out_shape=jax.ShapeDtypeStruct(xb.shape, jnp.uint32),
        compiler_params=_params("arbitrary"),
        name="moe_experts",
    )(block_expert, sched, xb, w_gate, w_up, w_down)


def _combine_kernel(dest_ref, dest_next_ref, h_ref, wts_ref, g_ref, y_ref, o_ref, ybuf, sem):
    i = pl.program_id(0)
    tm = h_ref.shape[0]
    slot = i % 2

    def gather(idx_ref, which):
        for r in range(tm):
            for k in range(2):
                src = pl.multiple_of(idx_ref[0, k * tm + r] * SUBLANES, SUBLANES)
                pltpu.make_async_copy(y_ref.at[pl.ds(src, SUBLANES), :],
                                      ybuf.at[which, k, pl.ds(r * SUBLANES, SUBLANES), :], sem.at[which]).start()

    def wait_tile(which):
        for k in range(2):
            pltpu.make_async_copy(y_ref.at[pl.ds(0, tm * SUBLANES), :], ybuf.at[which, k], sem.at[which]).wait()

    @pl.when(i == 0)
    def _():
        gather(dest_ref, slot)

    wait_tile(slot)
    gather(dest_next_ref, 1 - slot)

    w = wts_ref[...]
    half = h_ref.shape[1] // 2
    lo_parts, hi_parts = [], []
    for c, (c0, c1) in enumerate(zip(_tiles_to_row_chunks(ybuf.at[slot, 0], tm),
                                      _tiles_to_row_chunks(ybuf.at[slot, 1], tm))):
        a0, b0 = _unpack_bf16_pair(c0)
        a1, b1 = _unpack_bf16_pair(c1)
        lo_parts.append(h_ref[:, c * LANES:(c + 1) * LANES] + w[:, 0:1] * a0 + w[:, 1:2] * a1)
        hi_parts.append(h_ref[:, half + c * LANES:half + (c + 1) * LANES] + w[:, 0:1] * b0 + w[:, 1:2] * b1)
    h3 = jnp.concatenate(lo_parts + hi_parts, axis=1)
    o_ref[...] = h3 * _rms_scale(h3, NORM_EPS) * g_ref[...]

    @pl.when(i == pl.num_programs(0) - 1)
    def _():
        wait_tile(1 - slot)


def _combine(dest3, h2, wts, g_final, y, tm):
    n_tok, d = h2.shape
    last = n_tok // tm - 1
    return pl.pallas_call(
        _combine_kernel,
        grid=(n_tok // tm,),
        in_specs=[pl.BlockSpec((None, 1, 2 * tm), lambda i: (i, 0, 0), memory_space=pltpu.SMEM),
                  pl.BlockSpec((None, 1, 2 * tm), lambda i: (jnp.minimum(i + 1, last), 0, 0),
                               memory_space=pltpu.SMEM),
                  pl.BlockSpec((tm, d), lambda i: (i, 0)),
                  pl.BlockSpec((tm, 2), lambda i: (i, 0)),
                  _resident(g_final.shape),
                  pl.BlockSpec(memory_space=pl.ANY)],
        out_specs=pl.BlockSpec((tm, d), lambda i: (i, 0)),
        out_shape=jax.ShapeDtypeStruct((n_tok, d), jnp.float32),
        scratch_shapes=[pltpu.VMEM((2, 2, tm * SUBLANES, LANES), jnp.uint32), pltpu.SemaphoreType.DMA((2,))],
        compiler_params=_params("arbitrary"),
        name="moe_combine",
    )(dest3, dest3, h2, wts, g_final, y)


def _transpose_w_in(w_in):
    return jnp.swapaxes(w_in, 0, 1).astype(jnp.bfloat16)


def _split_w_uq(w_uq):
    half = MLA_ROPE_DIM // 2
    w = w_uq.reshape(MLA_Q_RANK, MLA_HEADS, MLA_QK_DIM).transpose(1, 0, 2)
    pe = w[:, :, MLA_NOPE_DIM:]
    pe_swapped = jnp.concatenate([pe[:, :, half:], pe[:, :, :half]], axis=2)
    return jnp.concatenate([w, pe_swapped], axis=2).astype(jnp.bfloat16)


def _split_w_ukv(w_ukv):
    w = w_ukv.reshape(MLA_KV_RANK, MLA_HEADS, MLA_NOPE_DIM + MLA_V_DIM)
    wuk = w[:, :, :MLA_NOPE_DIM].transpose(1, 0, 2).astype(jnp.bfloat16)
    wuv = w[:, :, MLA_NOPE_DIM:].reshape(MLA_KV_RANK, MLA_HEADS * MLA_V_DIM).astype(jnp.bfloat16)
    return wuk, wuv


def kernel(x, mem, positions, attn_norm_g, w_in, diff_lambda_q1, diff_lambda_k1, diff_lambda_q2, diff_lambda_k2, diff_subln_g, w_o_diff, mla_q_norm_g, w_uq, mla_kv_norm_g, w_ukv, w_o_mla, w_out, cross_norm_g, mem_norm_g, w_cq, w_ckv, w_co, ffn_norm_g, w_router_group, b_router_group, w_router_expert, b_router_expert, w_expert_gate, w_expert_up, w_expert_down, final_norm_g):
    batch, seq, d = x.shape
    assert d == D_MODEL and w_in.shape[0] == 1, "single-layer kernel"
    n_tok = batch * seq
    bf = jnp.bfloat16
    x2 = x.reshape(n_tok, d)

    tm_proj = min(1024, seq)
    tm_row = min(256, seq)
    tm_cross = min(512, seq)
    tq = min(2048, seq)
    rg_diff = 128
    rg_mla = 256

    cos_t, sin_t = _rope_tables(positions, n_tok, tm_proj)

    g_attn = attn_norm_g[0].reshape(1, d)
    qkv, latent, gates = _inproj(x2, g_attn, _transpose_w_in(w_in[0]), cos_t, sin_t, tm_proj)

    o_a = _diff_attn(qkv.reshape(batch, seq, QKV_COLS),
                     diff_lambda_q1[0].reshape(1, -1), diff_lambda_k1[0].reshape(1, -1),
                     diff_lambda_q2[0].reshape(1, -1), diff_lambda_k2[0].reshape(1, -1),
                     diff_subln_g[0].reshape(1, -1), tq, rg_diff)

    wuk, wuv = _split_w_ukv(w_ukv[0])
    q_cat, k_cat, v_mla = _mla_proj(latent, mla_q_norm_g[0].reshape(1, -1), mla_kv_norm_g[0].reshape(1, -1),
                                    _split_w_uq(w_uq[0]), wuk, wuv, cos_t, sin_t, batch, seq, min(512, seq))
    o_b = _mla_attn(q_cat, k_cat, v_mla.reshape(batch, seq, MLA_HEADS * MLA_V_DIM), tq, rg_mla)

    h1 = _merge_out(o_a.reshape(n_tok, -1), o_b.reshape(n_tok, -1), gates, x2,
                    w_o_diff[0].astype(bf), w_o_mla[0].astype(bf), w_out[0].astype(bf), tm_row)

    kv_mem = _mem_kv(mem, mem_norm_g[0].reshape(1, d), w_ckv[0].astype(bf))
    n_router = N_GROUPS + N_EXPERTS
    w_r = jnp.concatenate([w_router_group[0].astype(jnp.float32), w_router_expert[0].astype(jnp.float32),
                           jnp.zeros((d, LANES - n_router), jnp.float32)], axis=1)
    w_r_hi = w_r.astype(bf)
    w_r_lo = (w_r - w_r_hi.astype(jnp.float32)).astype(bf)
    w_r = jnp.concatenate([w_r_hi, w_r_lo], axis=1)
    b_r = jnp.concatenate([b_router_group[0].astype(jnp.float32), b_router_expert[0].astype(jnp.float32),
                           jnp.zeros((LANES - n_router,), jnp.float32)]).reshape(1, LANES)
    g_ffn = ffn_norm_g[0].reshape(1, d)
    h2, eid, rank, wts, cnt = _cross_router(h1, cross_norm_g[0].reshape(1, d), w_cq[0].astype(bf), kv_mem,
                                            w_co[0].astype(bf), g_ffn, w_r, b_r, seq, tm_cross)

    bm = MOE_ROWS_PER_BLOCK
    assert tm_row == bm, "dispatch zeroes whole row blocks from its token-tile scratch"
    counts = cnt[0, ROUTER_EXPERT_LANE0:ROUTER_EXPERT_LANE0 + N_EXPERTS].astype(jnp.int32)
    padded = ((counts + bm - 1) // bm) * bm
    padded_end = jnp.cumsum(padded)
    padded_off = padded_end - padded
    seg_start = jnp.sum(jnp.where(eid[..., None] == jnp.arange(N_EXPERTS, dtype=jnp.int32), padded_off, 0), axis=-1)
    dest = seg_start + rank
    p_rows = ((2 * n_tok + bm - 1) // bm) * bm + N_EXPERTS * bm
    n_blocks = p_rows // bm
    n_active = (padded_end[-1] // bm).astype(jnp.int32)
    blk = jnp.minimum(jnp.arange(n_blocks, dtype=jnp.int32), n_active - 1)
    block_expert = jnp.sum((padded_end[None, :] <= (blk * bm)[:, None]).astype(jnp.int32), axis=1)
    block_expert = jnp.minimum(block_expert, N_EXPERTS - 1)
    dest3 = dest.reshape(-1, 2, tm_cross // tm_row, tm_row).transpose(0, 2, 1, 3).reshape(
        n_tok // tm_row, 1, 2 * tm_row)
    unused = n_active + jnp.arange(N_EXPERTS, dtype=jnp.int32)
    ztail = jnp.stack([jnp.concatenate([jnp.maximum(padded_end - bm, 0), jnp.minimum(unused, n_blocks - 1) * bm]),
                       jnp.concatenate([padded > 0, unused < n_blocks]).astype(jnp.int32)]).astype(jnp.int32)
    sched = jnp.concatenate([n_active.reshape(1), padded // bm, padded_off // bm]).astype(jnp.int32)

    xb = _dispatch(dest3, ztail, h2, g_ffn, p_rows, tm_row)
    y = _experts(block_expert, sched, xb, w_expert_gate[0], w_expert_up[0], w_expert_down[0], bm)
    out = _combine(dest3, h2, wts, final_norm_g.reshape(1, d), y, tm_row)
    return out.reshape(batch, seq, d)
```

```python
import functools
import math

import jax
import jax.numpy as jnp
from jax import lax
from jax.experimental import pallas as pl
from jax.experimental.pallas import tpu as pltpu

D_MODEL = 2048
ROPE_THETA = 500000.0
NORM_EPS = 1e-6

DIFF_HEADS = 8
DIFF_HEAD_DIM = 64
DIFF_V_DIM = 2 * DIFF_HEAD_DIM
DIFF_ROT = DIFF_HEAD_DIM // 4
DIFF_SUBLN_EPS = 1e-5
DIFF_LAMBDA_INIT = 0.8 - 0.6 * math.exp(-0.3 * 0)

MLA_HEADS = 8
MLA_Q_RANK = 512
MLA_KV_RANK = 256
MLA_NOPE_DIM = 128
MLA_ROPE_DIM = 64
MLA_V_DIM = 128
MLA_QK_DIM = MLA_NOPE_DIM + MLA_ROPE_DIM

CROSS_HEADS = 4
CROSS_HEAD_DIM = 128

N_GROUPS = 4
EXPERTS_PER_GROUP = 8
N_EXPERTS = N_GROUPS * EXPERTS_PER_GROUP
D_EXPERT = 512

LANES = 128
LOG2E = 1.4426950408889634
VMEM_LIMIT_BYTES = 56 * 1024 * 1024

ROUTER_EXPERT_LANE0 = N_GROUPS

QKV_COLS = 3 * DIFF_HEADS * DIFF_V_DIM
LATENT_COLS = 1024
GATE_COLS = 2 * D_MODEL
KPE_COL0 = MLA_Q_RANK + MLA_KV_RANK

MOE_ROWS_PER_BLOCK = 512
SUBLANES = 8


def _params(*semantics):
    return pltpu.CompilerParams(dimension_semantics=semantics, vmem_limit_bytes=VMEM_LIMIT_BYTES)


def _resident(shape):
    zeros = (0,) * len(shape)
    return pl.BlockSpec(shape, lambda *_: zeros, pipeline_mode=pl.Buffered(1))


def _rms_scale(xf, eps):
    return lax.rsqrt(jnp.mean(xf * xf, axis=-1, keepdims=True) + eps)


def _sigmoid(x):
    return 0.5 * jnp.tanh(0.5 * x) + 0.5


def _pack_bf16_pair(a, b):
    hi = lax.bitcast_convert_type(a.astype(jnp.bfloat16).astype(jnp.float32), jnp.uint32)
    lo = lax.bitcast_convert_type(b.astype(jnp.bfloat16).astype(jnp.float32), jnp.uint32)
    return hi | (lo >> 16)


def _unpack_bf16_pair(w):
    a = lax.bitcast_convert_type(w & jnp.uint32(0xFFFF0000), jnp.float32)
    b = lax.bitcast_convert_type(w << 16, jnp.float32)
    return a, b


def _rows_to_tiles(ref_view, packed):
    rows = packed.shape[0]
    for c in range(SUBLANES):
        ref_view[pl.ds(c, rows, stride=SUBLANES), :] = packed[:, c * LANES:(c + 1) * LANES]


def _tiles_to_row_chunks(ref_view, rows):
    return [ref_view[pl.ds(c, rows, stride=SUBLANES), :] for c in range(SUBLANES)]


def _lane_iota(shape):
    return lax.broadcasted_iota(jnp.int32, shape, len(shape) - 1)


def _trig_kernel(pos_ref, invf_ref, cos_ref, sin_ref):
    ang = pos_ref[...].astype(jnp.float32) * invf_ref[...]
    cos_ref[...] = jnp.cos(ang)
    sin_ref[...] = jnp.sin(ang)


def _rope_tables(positions, n_tok, tm):
    half_m = MLA_ROPE_DIM // 2
    half_d = DIFF_ROT // 2
    inv_m = jnp.float32(ROPE_THETA) ** (-jnp.arange(half_m, dtype=jnp.float32) * 2.0 / MLA_ROPE_DIM)
    inv_d = jnp.float32(ROPE_THETA) ** (-jnp.arange(half_d, dtype=jnp.float32) * 2.0 / DIFF_ROT)
    invf = jnp.concatenate([inv_m, inv_m, inv_d, inv_d,
                            jnp.zeros((DIFF_HEAD_DIM - DIFF_ROT,), jnp.float32)]).reshape(1, LANES)
    pos = positions.reshape(n_tok, 1)
    return pl.pallas_call(
        _trig_kernel,
        grid=(n_tok // tm,),
        in_specs=[pl.BlockSpec((tm, 1), lambda i: (i, 0)), _resident((1, LANES))],
        out_specs=[pl.BlockSpec((tm, LANES), lambda i: (i, 0))] * 2,
        out_shape=[jax.ShapeDtypeStruct((n_tok, LANES), jnp.float32)] * 2,
        compiler_params=_params("parallel"),
        name="rope_tables",
    )(pos, invf)


def _diff_rope_coeffs(cos_t, sin_t):
    lane = _lane_iota(cos_t.shape)
    upper = lane >= DIFF_HEAD_DIM
    cos_d = jnp.where(upper, cos_t, pltpu.roll(cos_t, DIFF_HEAD_DIM, 1))
    sin_d = jnp.where(upper, sin_t, pltpu.roll(sin_t, DIFF_HEAD_DIM, 1))
    in_head = lane % DIFF_HEAD_DIM
    half = DIFF_ROT // 2
    s_next = jnp.where(in_head < half, -sin_d, 0.0)
    s_prev = jnp.where((in_head >= half) & (in_head < DIFF_ROT), sin_d, 0.0)
    return cos_d, s_next, s_prev


def _mla_rope(pair, cos_t, sin_t):
    lane = _lane_iota(pair.shape)
    sin_signed = jnp.where(lane < MLA_ROPE_DIM // 2, -sin_t, sin_t)
    return pair * cos_t + pltpu.roll(pair, MLA_ROPE_DIM, 1) * sin_signed


INPROJ_TN = 1024
INPROJ_PIECE = 256
Q_TILES = DIFF_HEADS * DIFF_V_DIM // INPROJ_TN
ROPE_TILES = 2 * Q_TILES
QKV_TILES = QKV_COLS // INPROJ_TN
LATENT_TILES = LATENT_COLS // INPROJ_TN
GATE_TILES = GATE_COLS // INPROJ_TN
INPROJ_TILES = QKV_TILES + LATENT_TILES + GATE_TILES
GATE_ROW0 = QKV_COLS + KPE_COL0 + MLA_ROPE_DIM


def _inproj_kernel(x_ref, g_ref, w_ref, cos_ref, sin_ref, qkv_ref, lat_ref, gate_ref, xn_ref):
    j = pl.program_id(1)

    @pl.when(j == 0)
    def _():
        xf = x_ref[...]
        xn_ref[...] = (xf * _rms_scale(xf, NORM_EPS) * g_ref[...]).astype(jnp.bfloat16)

    def pieces(epilogue):
        for c in range(INPROJ_TN // INPROJ_PIECE):
            cols = slice(c * INPROJ_PIECE, (c + 1) * INPROJ_PIECE)
            acc = lax.dot_general(xn_ref[...], w_ref[cols, :], (((1,), (1,)), ((), ())),
                                  preferred_element_type=jnp.float32)
            epilogue(acc, cols)

    @pl.when(j < ROPE_TILES)
    def _():
        cos_d, s_next, s_prev = _diff_rope_coeffs(cos_ref[...], sin_ref[...])
        qscale = jnp.where(j < Q_TILES, DIFF_HEAD_DIM ** -0.5 * LOG2E, 1.0).astype(jnp.float32)

        def rope(acc, cols):
            for c in range(INPROJ_PIECE // LANES):
                xc = acc[:, c * LANES:(c + 1) * LANES]
                rot = (xc * cos_d + pltpu.roll(xc, LANES - DIFF_ROT // 2, 1) * s_next
                       + pltpu.roll(xc, DIFF_ROT // 2, 1) * s_prev)
                lo = cols.start + c * LANES
                qkv_ref[:, lo:lo + LANES] = (rot * qscale).astype(qkv_ref.dtype)

        pieces(rope)

    @pl.when((j >= ROPE_TILES) & (j < QKV_TILES))
    def _():
        def value(acc, cols):
            qkv_ref[:, cols] = acc.astype(qkv_ref.dtype)

        pieces(value)

    @pl.when((j >= QKV_TILES) & (j < QKV_TILES + LATENT_TILES))
    def _():
        def latent(acc, cols):
            if cols.start <= KPE_COL0 < cols.stop:
                c0 = KPE_COL0 - cols.start
                v = acc[:, c0:c0 + LANES]
                lane = _lane_iota(v.shape)
                half = MLA_ROPE_DIM // 2
                swapped = jnp.where(lane < MLA_ROPE_DIM + half, pltpu.roll(v, half, 1),
                                    pltpu.roll(v, MLA_ROPE_DIM + half, 1))
                parts = [acc[:, :c0], jnp.where(lane < MLA_ROPE_DIM, v, swapped), acc[:, c0 + LANES:]]
                acc = jnp.concatenate([p for p in parts if p.shape[1]], axis=1)
            lat_ref[:, cols] = acc

        pieces(latent)

    @pl.when(j >= QKV_TILES + LATENT_TILES)
    def _():
        def gate(acc, cols):
            gate_ref[:, cols] = _sigmoid(acc).astype(gate_ref.dtype)

        pieces(gate)


def _inproj(x2, g, w_all, cos_t, sin_t, tm):
    n_tok, d = x2.shape
    tn = INPROJ_TN
    lat0 = QKV_TILES
    gate0 = QKV_TILES + LATENT_TILES
    return pl.pallas_call(
        _inproj_kernel,
        grid=(n_tok // tm, INPROJ_TILES),
        in_specs=[pl.BlockSpec((tm, d), lambda i, j: (i, 0)),
                  _resident((1, d)),
                  pl.BlockSpec((pl.Element(tn), pl.Element(d)),
                               lambda i, j: (pl.multiple_of(jnp.where(j < gate0, j * tn, GATE_ROW0 + (j - gate0) * tn), 16), 0)),
                  pl.BlockSpec((tm, LANES), lambda i, j: (i, 0)),
                  pl.BlockSpec((tm, LANES), lambda i, j: (i, 0))],
        out_specs=[pl.BlockSpec((tm, tn), lambda i, j: (i, jnp.clip(j, 0, QKV_TILES - 1))),
                   pl.BlockSpec((tm, tn), lambda i, j: (i, jnp.clip(j - lat0, 0, LATENT_TILES - 1))),
                   pl.BlockSpec((tm, tn), lambda i, j: (i, jnp.clip(j - gate0, 0, GATE_TILES - 1)))],
        out_shape=[jax.ShapeDtypeStruct((n_tok, QKV_COLS), jnp.bfloat16),
                   jax.ShapeDtypeStruct((n_tok, LATENT_COLS), jnp.float32),
                   jax.ShapeDtypeStruct((n_tok, GATE_COLS), jnp.bfloat16)],
        scratch_shapes=[pltpu.VMEM((tm, d), jnp.bfloat16)],
        compiler_params=_params("parallel", "arbitrary"),
        name="inproj",
    )(x2, g, w_all, cos_t, sin_t)


def _mla_proj_kernel(c_ref, gq_ref, gkv_ref, wuq_ref, wuk_ref, wuv_ref, cos_ref, sin_ref,
                     q_ref, k_ref, v_ref):
    cos_t = cos_ref[...]
    sin_t = sin_ref[...]
    cq = c_ref[:, :MLA_Q_RANK]
    cqn = (cq * _rms_scale(cq, NORM_EPS) * gq_ref[...]).astype(jnp.bfloat16)
    ckv = c_ref[:, MLA_Q_RANK:KPE_COL0]
    ckvn = (ckv * _rms_scale(ckv, NORM_EPS) * gkv_ref[...]).astype(jnp.bfloat16)
    kpe = _mla_rope(c_ref[:, KPE_COL0:KPE_COL0 + LANES], cos_t, sin_t)[:, :MLA_ROPE_DIM].astype(k_ref.dtype)
    qscale = MLA_QK_DIM ** -0.5 * LOG2E
    for h in range(MLA_HEADS):
        r = jnp.dot(cqn, wuq_ref[h], preferred_element_type=jnp.float32)
        q_ref[0, h, :, :MLA_NOPE_DIM] = (r[:, :MLA_NOPE_DIM] * qscale).astype(q_ref.dtype)
        qpe = _mla_rope(r[:, MLA_NOPE_DIM:], cos_t, sin_t)[:, :MLA_ROPE_DIM]
        q_ref[0, h, :, MLA_NOPE_DIM:] = (qpe * qscale).astype(q_ref.dtype)
        kn = jnp.dot(ckvn, wuk_ref[h], preferred_element_type=jnp.float32)
        k_ref[0, h, :, :MLA_NOPE_DIM] = kn.astype(k_ref.dtype)
        k_ref[0, h, :, MLA_NOPE_DIM:] = kpe
    v_ref[...] = jnp.dot(ckvn, wuv_ref[...], preferred_element_type=jnp.float32).astype(v_ref.dtype)


def _mla_proj(latent, gq, gkv, wuq, wuk, wuv, cos_t, sin_t, batch, seq, tm):
    n_tok = latent.shape[0]
    per_b = seq // tm
    head_spec = pl.BlockSpec((1, MLA_HEADS, tm, MLA_QK_DIM), lambda i: (i // per_b, 0, i % per_b, 0))
    head_shape = jax.ShapeDtypeStruct((batch, MLA_HEADS, seq, MLA_QK_DIM), jnp.bfloat16)
    return pl.pallas_call(
        _mla_proj_kernel,
        grid=(n_tok // tm,),
        in_specs=[pl.BlockSpec((tm, LATENT_COLS), lambda i: (i, 0)),
                  _resident(gq.shape), _resident(gkv.shape),
                  _resident(wuq.shape), _resident(wuk.shape), _resident(wuv.shape),
                  pl.BlockSpec((tm, LANES), lambda i: (i, 0)),
                  pl.BlockSpec((tm, LANES), lambda i: (i, 0))],
        out_specs=[head_spec, head_spec,
                   pl.BlockSpec((tm, MLA_HEADS * MLA_V_DIM), lambda i: (i, 0))],
        out_shape=[head_shape, head_shape,
                   jax.ShapeDtypeStruct((n_tok, MLA_HEADS * MLA_V_DIM), jnp.bfloat16)],
        compiler_params=_params("parallel"),
        name="mla_proj",
    )(latent, gq, gkv, wuq, wuk, wuv, cos_t, sin_t)


def _with_ones(v):
    return jnp.concatenate([v, jnp.ones((v.shape[0], LANES), v.dtype)], axis=-1)


def _softmax_pv(s, v_ones):
    m = jnp.max(s, axis=-1, keepdims=True)
    p = jnp.exp2(s - m).astype(v_ones.dtype)
    pv = jnp.dot(p, v_ones, preferred_element_type=jnp.float32)
    dv = v_ones.shape[1] - LANES
    return pv[:, :dv] / pv[:, dv:]


ATTN_HEADS_PER_STEP = 2


def _diff_attn_kernel(q_ref, k_ref, v_ref, lq1_ref, lk1_ref, lq2_ref, lk2_ref, g_ref, o_ref, v1_ref, *, rg):
    heads = q_ref.shape[2] // LANES

    @pl.when(pl.program_id(2) == 0)
    def _():
        for h in range(heads):
            v1_ref[h] = _with_ones(v_ref[0, :, h * DIFF_V_DIM:(h + 1) * DIFF_V_DIM])

    lam = (jnp.exp(jnp.sum(lq1_ref[...] * lk1_ref[...], axis=-1, keepdims=True))
           - jnp.exp(jnp.sum(lq2_ref[...] * lk2_ref[...], axis=-1, keepdims=True))
           + DIFF_LAMBDA_INIT)
    lane = _lane_iota((rg, LANES))
    for g in range(q_ref.shape[1] // rg):
        for h in range(heads):
            cols = slice(h * LANES, (h + 1) * LANES)
            q = q_ref[0, g * rg:(g + 1) * rg, cols]
            k = k_ref[0, :, cols]
            zero = jnp.zeros_like(q)
            q12 = jnp.concatenate([jnp.where(lane < DIFF_HEAD_DIM, q, zero),
                                   jnp.where(lane >= DIFF_HEAD_DIM, q, zero)], axis=0)
            s = lax.dot_general(q12, k, (((1,), (1,)), ((), ())), preferred_element_type=jnp.float32)
            a = _softmax_pv(s, v1_ref[h])
            o = a[:rg] - lam * a[rg:]
            o = o * _rms_scale(o, DIFF_SUBLN_EPS) * g_ref[...] * (1.0 - DIFF_LAMBDA_INIT)
            o_ref[0, g * rg:(g + 1) * rg, cols] = o.astype(o_ref.dtype)


def _diff_attn(qkv3, lq1, lk1, lq2, lk2, subln_g, tq, rg):
    batch, seq, _ = qkv3.shape
    hp = ATTN_HEADS_PER_STEP
    steps = DIFF_HEADS // hp
    width = hp * DIFF_V_DIM
    return pl.pallas_call(
        functools.partial(_diff_attn_kernel, rg=rg),
        grid=(batch, steps, seq // tq),
        in_specs=[pl.BlockSpec((1, tq, width), lambda b, hh, i: (b, i, hh)),
                  pl.BlockSpec((1, seq, width), lambda b, hh, i: (b, 0, steps + hh)),
                  pl.BlockSpec((1, seq, width), lambda b, hh, i: (b, 0, 2 * steps + hh)),
                  _resident(lq1.shape), _resident(lk1.shape), _resident(lq2.shape), _resident(lk2.shape),
                  _resident(subln_g.shape)],
        out_specs=pl.BlockSpec((1, tq, width), lambda b, hh, i: (b, i, hh)),
        out_shape=jax.ShapeDtypeStruct((batch, seq, DIFF_HEADS * DIFF_V_DIM), jnp.bfloat16),
        scratch_shapes=[pltpu.VMEM((hp, seq, DIFF_V_DIM + LANES), jnp.bfloat16)],
        compiler_params=_params("parallel", "parallel", "arbitrary"),
        name="diff_attn",
    )(qkv3, qkv3, qkv3, lq1, lk1, lq2, lk2, subln_g)


def _mla_attn_kernel(q_ref, k_ref, v_ref, o_ref, v1_ref, *, rg):
    heads = q_ref.shape[1]

    @pl.when(pl.program_id(2) == 0)
    def _():
        for h in range(heads):
            v1_ref[h] = _with_ones(v_ref[0, :, h * MLA_V_DIM:(h + 1) * MLA_V_DIM])

    for g in range(q_ref.shape[2] // rg):
        for h in range(heads):
            s = lax.dot_general(q_ref[0, h, g * rg:(g + 1) * rg], k_ref[0, h], (((1,), (1,)), ((), ())),
                                preferred_element_type=jnp.float32)
            o_ref[0, g * rg:(g + 1) * rg, h * MLA_V_DIM:(h + 1) * MLA_V_DIM] = (
                _softmax_pv(s, v1_ref[h]).astype(o_ref.dtype))


def _mla_attn(q_cat, k_cat, v3, tq, rg):
    batch, heads, seq, dqk = q_cat.shape
    hp = ATTN_HEADS_PER_STEP
    return pl.pallas_call(
        functools.partial(_mla_attn_kernel, rg=rg),
        grid=(batch, heads // hp, seq // tq),
        in_specs=[pl.BlockSpec((1, hp, tq, dqk), lambda b, h, i: (b, h, i, 0)),
                  pl.BlockSpec((1, hp, seq, dqk), lambda b, h, i: (b, h, 0, 0)),
                  pl.BlockSpec((1, seq, hp * MLA_V_DIM), lambda b, h, i: (b, 0, h))],
        out_specs=pl.BlockSpec((1, tq, hp * MLA_V_DIM), lambda b, h, i: (b, i, h)),
        out_shape=jax.ShapeDtypeStruct((batch, seq, heads * MLA_V_DIM), jnp.bfloat16),
        scratch_shapes=[pltpu.VMEM((hp, seq, MLA_V_DIM + LANES), jnp.bfloat16)],
        compiler_params=_params("parallel", "parallel", "arbitrary"),
        name="mla_attn",
    )(q_cat, k_cat, v3)


def _merge_out_kernel(oa_ref, ob_ref, sga_ref, sgb_ref, x_ref, woa_ref, wob_ref, wout_ref, h_ref):
    ya = jnp.dot(oa_ref[...], woa_ref[...], preferred_element_type=jnp.float32)
    yb = jnp.dot(ob_ref[...], wob_ref[...], preferred_element_type=jnp.float32)
    merged = sga_ref[...].astype(jnp.float32) * ya + sgb_ref[...].astype(jnp.float32) * yb
    h_ref[...] = x_ref[...] + jnp.dot(merged.astype(jnp.bfloat16), wout_ref[...],
                                       preferred_element_type=jnp.float32)


def _merge_out(o_a, o_b, gates, x2, w_oa, w_ob, w_out, tm):
    n_tok, d = x2.shape
    return pl.pallas_call(
        _merge_out_kernel,
        grid=(n_tok // tm,),
        in_specs=[pl.BlockSpec((tm, o_a.shape[1]), lambda i: (i, 0)),
                  pl.BlockSpec((tm, o_b.shape[1]), lambda i: (i, 0)),
                  pl.BlockSpec((tm, d), lambda i: (i, 0)),
                  pl.BlockSpec((tm, d), lambda i: (i, 1)),
                  pl.BlockSpec((tm, d), lambda i: (i, 0)),
                  _resident(w_oa.shape), _resident(w_ob.shape), _resident(w_out.shape)],
        out_specs=pl.BlockSpec((tm, d), lambda i: (i, 0)),
        out_shape=jax.ShapeDtypeStruct((n_tok, d), jnp.float32),
        compiler_params=_params("parallel"),
        name="merge_out",
    )(o_a, o_b, gates, gates, x2, w_oa, w_ob, w_out)


def _mem_kv_kernel(mem_ref, g_ref, w_ref, kv_ref):
    mf = mem_ref[0]
    mn = (mf * _rms_scale(mf, NORM_EPS) * g_ref[...]).astype(jnp.bfloat16)
    kv_ref[0] = jnp.dot(mn, w_ref[...], preferred_element_type=jnp.float32).astype(kv_ref.dtype)


def _mem_kv(mem, g, w_ckv):
    batch, m, d = mem.shape
    return pl.pallas_call(
        _mem_kv_kernel,
        grid=(batch,),
        in_specs=[pl.BlockSpec((1, m, d), lambda b: (b, 0, 0)), _resident(g.shape), _resident(w_ckv.shape)],
        out_specs=pl.BlockSpec((1, m, w_ckv.shape[1]), lambda b: (b, 0, 0)),
        out_shape=jax.ShapeDtypeStruct((batch, m, w_ckv.shape[1]), jnp.bfloat16),
        compiler_params=_params("parallel"),
        name="mem_kv",
    )(mem, g, w_ckv)


def _cross_router_kernel(h_ref, gc_ref, wcq_ref, kv_ref, wco_ref, gf_ref, wr_ref, br_ref,
                         h2_ref, eid_ref, rank_ref, wts_ref, cnt_ref, carry_ref):
    i = pl.program_id(0)

    @pl.when(i == 0)
    def _():
        carry_ref[...] = jnp.zeros_like(carry_ref)

    h1 = h_ref[...]
    tm = h1.shape[0]
    hn = (h1 * _rms_scale(h1, NORM_EPS) * gc_ref[...]).astype(jnp.bfloat16)
    q = jnp.dot(hn, wcq_ref[...], preferred_element_type=jnp.float32) * (CROSS_HEAD_DIM ** -0.5 * LOG2E)
    q = q.astype(jnp.bfloat16)
    kv_cols = CROSS_HEADS * CROSS_HEAD_DIM
    heads = []
    for hd in range(CROSS_HEADS):
        lo = hd * CROSS_HEAD_DIM
        kh = kv_ref[0, :, lo:lo + CROSS_HEAD_DIM]
        vh = kv_ref[0, :, kv_cols + lo:kv_cols + lo + CROSS_HEAD_DIM]
        s = lax.dot_general(q[:, lo:lo + CROSS_HEAD_DIM], kh, (((1,), (1,)), ((), ())),
                            preferred_element_type=jnp.float32)
        heads.append(_softmax_pv(s, _with_ones(vh)).astype(jnp.bfloat16))
    o = jnp.concatenate(heads, axis=-1)
    h2 = h1 + jnp.dot(o, wco_ref[...], preferred_element_type=jnp.float32)
    h2_ref[...] = h2

    t = h2 * _rms_scale(h2, NORM_EPS) * gf_ref[...]
    t_hi = t.astype(jnp.bfloat16)
    t_lo = (t - t_hi.astype(jnp.float32)).astype(jnp.bfloat16)
    hi = jnp.dot(t_hi, wr_ref[...], preferred_element_type=jnp.float32)
    lo = jnp.dot(t_lo, wr_ref[:, :LANES], preferred_element_type=jnp.float32)
    logits = hi[:, :LANES] + (hi[:, LANES:] + lo) + br_ref[...]
    lane = _lane_iota(logits.shape)
    neg = jnp.float32(-jnp.inf)
    big = jnp.int32(2 * LANES)
    is_group = lane < N_GROUPS
    lg = jnp.where(is_group, logits, neg)
    mg = jnp.max(lg, axis=-1, keepdims=True)
    g_idx = jnp.min(jnp.where(is_group & (logits == mg), lane, big), axis=-1, keepdims=True)
    g_p = 1.0 / jnp.sum(jnp.exp(lg - mg), axis=-1, keepdims=True)
    lo_lane = ROUTER_EXPERT_LANE0 + EXPERTS_PER_GROUP * g_idx
    in_grp = (lane >= lo_lane) & (lane < lo_lane + EXPERTS_PER_GROUP)
    l1 = jnp.max(jnp.where(in_grp, logits, neg), axis=-1, keepdims=True)
    i1 = jnp.min(jnp.where(in_grp & (logits == l1), lane, big), axis=-1, keepdims=True)
    rest = in_grp & (lane != i1)
    l2 = jnp.max(jnp.where(rest, logits, neg), axis=-1, keepdims=True)
    i2 = jnp.min(jnp.where(rest & (logits == l2), lane, big), axis=-1, keepdims=True)
    d = jnp.exp(l2 - l1)
    w1 = g_p / (1.0 + d)
    w2 = w1 * d

    oh1 = lane == i1
    oh2 = lane == i2
    cnt = (oh1 | oh2).astype(jnp.bfloat16)
    row = lax.broadcasted_iota(jnp.int32, (tm, tm), 0)
    col = lax.broadcasted_iota(jnp.int32, (tm, tm), 1)
    before = (col < row).astype(jnp.bfloat16)
    slot = jnp.dot(before, cnt, preferred_element_type=jnp.float32) + carry_ref[...]
    r1 = jnp.sum(jnp.where(oh1, slot, 0.0), axis=-1, keepdims=True)
    r2 = jnp.sum(jnp.where(oh2, slot, 0.0), axis=-1, keepdims=True)
    carry_ref[...] += jnp.sum(cnt.astype(jnp.float32), axis=0, keepdims=True)
    cnt_ref[...] = carry_ref[...]

    eye = row == col

    def to_row(c, dtype):
        return jnp.sum(jnp.where(eye, c.astype(jnp.float32), 0.0), axis=0, keepdims=True).astype(dtype)

    eid_ref[0] = jnp.concatenate([to_row(i1 - ROUTER_EXPERT_LANE0, jnp.int32),
                                  to_row(i2 - ROUTER_EXPERT_LANE0, jnp.int32)], axis=0)
    rank_ref[0] = jnp.concatenate([to_row(r1, jnp.int32), to_row(r2, jnp.int32)], axis=0)
    wts_ref[...] = jnp.where(_lane_iota((tm, 2)) == 0, w1, w2)


def _cross_router(h1, gc, w_cq, kv_mem, w_co, gf, w_r, b_r, seq, tm):
    n_tok, d = h1.shape
    per_b = seq // tm
    row2 = pl.BlockSpec((tm, 2), lambda i: (i, 0))
    lane2 = pl.BlockSpec((1, 2, tm), lambda i: (i, 0, 0))
    return pl.pallas_call(
        _cross_router_kernel,
        grid=(n_tok // tm,),
        in_specs=[pl.BlockSpec((tm, d), lambda i: (i, 0)),
                  _resident(gc.shape), _resident(w_cq.shape),
                  pl.BlockSpec((1,) + kv_mem.shape[1:], lambda i: (i // per_b, 0, 0)),
                  _resident(w_co.shape), _resident(gf.shape), _resident(w_r.shape), _resident(b_r.shape)],
        out_specs=[pl.BlockSpec((tm, d), lambda i: (i, 0)), lane2, lane2, row2,
                   pl.BlockSpec((1, LANES), lambda i: (0, 0))],
        out_shape=[jax.ShapeDtypeStruct((n_tok, d), jnp.float32),
                   jax.ShapeDtypeStruct((n_tok // tm, 2, tm), jnp.int32),
                   jax.ShapeDtypeStruct((n_tok // tm, 2, tm), jnp.int32),
                   jax.ShapeDtypeStruct((n_tok, 2), jnp.float32),
                   jax.ShapeDtypeStruct((1, LANES), jnp.float32)],
        scratch_shapes=[pltpu.VMEM((1, LANES), jnp.float32)],
        compiler_params=_params("arbitrary"),
        name="cross_router",
    )(h1, gc, w_cq, kv_mem, w_co, gf, w_r, b_r)


def _dispatch_kernel(dest_ref, ztail_ref, h_ref, g_ref, xb_ref, t_ref, sem, zsem):
    i = pl.program_id(0)
    tm = h_ref.shape[0]
    slot = i % 2

    @pl.when(i == 0)
    def _():
        t_ref[1] = jnp.zeros(t_ref.shape[1:], t_ref.dtype)
        tile_rows = tm * SUBLANES

        def zero_copy(e):
            start = pl.multiple_of(ztail_ref[0, e] * SUBLANES, tile_rows)
            return pltpu.make_async_copy(t_ref.at[1], xb_ref.at[pl.ds(start, tile_rows), :], zsem)

        for e in range(ztail_ref.shape[1]):
            @pl.when(ztail_ref[1, e] > 0)
            def _():
                zero_copy(e).start()
        for e in range(ztail_ref.shape[1]):
            @pl.when(ztail_ref[1, e] > 0)
            def _():
                zero_copy(e).wait()

    h2 = h_ref[...]
    t = h2 * _rms_scale(h2, NORM_EPS) * g_ref[...]
    half = t.shape[1] // 2
    _rows_to_tiles(t_ref.at[slot], _pack_bf16_pair(t[:, :half], t[:, half:]))

    for r in range(tm):
        for k in range(2):
            dst = pl.multiple_of(dest_ref[0, k * tm + r] * SUBLANES, SUBLANES)
            pltpu.make_async_copy(t_ref.at[slot, pl.ds(r * SUBLANES, SUBLANES), :],
                                  xb_ref.at[pl.ds(dst, SUBLANES), :], sem.at[slot]).start()

    def wait_tile(which):
        for _ in range(2):
            pltpu.make_async_copy(t_ref.at[which], xb_ref.at[pl.ds(0, tm * SUBLANES), :], sem.at[which]).wait()

    @pl.when(i > 0)
    def _():
        wait_tile(1 - slot)

    @pl.when(i == pl.num_programs(0) - 1)
    def _():
        wait_tile(slot)


def _dispatch(dest3, ztail, h2, gf, p_rows, tm):
    n_tok, d = h2.shape
    return pl.pallas_call(
        _dispatch_kernel,
        grid=(n_tok // tm,),
        in_specs=[pl.BlockSpec((None, 1, 2 * tm), lambda i: (i, 0, 0), memory_space=pltpu.SMEM),
                  pl.BlockSpec(memory_space=pltpu.SMEM),
                  pl.BlockSpec((tm, d), lambda i: (i, 0)),
                  _resident(gf.shape)],
        out_specs=pl.BlockSpec(memory_space=pl.ANY),
        out_shape=jax.ShapeDtypeStruct((p_rows * SUBLANES, LANES), jnp.uint32),
        scratch_shapes=[pltpu.VMEM((2, tm * SUBLANES, LANES), jnp.uint32), pltpu.SemaphoreType.DMA((2,)),
                        pltpu.SemaphoreType.DMA(())],
        compiler_params=_params("arbitrary"),
        name="moe_dispatch",
    )(dest3, ztail, h2, gf)


def _expert_kernel(be_ref, nact_ref, x_ref, wg_hbm, wu_hbm, wd_hbm, y_ref,
                   wg_f, wu_f, wd_f, wg_b, wu_b, wd_b, sem):
    i = pl.program_id(0)
    blk = i - 1
    nact = nact_ref[0]
    last_blk = pl.num_programs(0) - 2

    def weight_copies(e):
        return (pltpu.make_async_copy(wg_hbm.at[e], wg_f, sem),
                pltpu.make_async_copy(wu_hbm.at[e], wu_f, sem),
                pltpu.make_async_copy(wd_hbm.at[e], wd_f, sem))

    def fetch(e):
        for c in weight_copies(e):
            c.start()

    def land(e):
        for c in weight_copies(e):
            c.wait()
        half = wg_f.shape[0] // 2
        for c in range(SUBLANES):
            for part, src0 in enumerate((c * LANES, half + c * LANES)):
                dst0 = (2 * c + part) * LANES
                wg_b[dst0:dst0 + LANES, :] = wg_f[src0:src0 + LANES, :].astype(jnp.bfloat16)
                wu_b[dst0:dst0 + LANES, :] = wu_f[src0:src0 + LANES, :].astype(jnp.bfloat16)
        wd_b[...] = wd_f[...].astype(jnp.bfloat16)

    @pl.when(i == 0)
    def _():
        fetch(be_ref[0])
        land(be_ref[0])

    @pl.when((i > 0) & (blk < nact))
    def _():
        here = be_ref[blk]
        nxt = be_ref[jnp.minimum(blk + 1, last_blk)]
        prv = be_ref[jnp.maximum(blk - 1, 0)]
        seg_end = nact_ref[1 + N_EXPERTS + here] + nact_ref[1 + here]
        has_next = seg_end < nact
        after = be_ref[jnp.minimum(seg_end, last_blk)]
        is_first = (blk == 0) | (prv != here)
        is_last = (blk + 1 >= nact) | (nxt != here)

        @pl.when(is_first & has_next)
        def _():
            fetch(after)

        bm = x_ref.shape[0] // SUBLANES
        parts = []
        for chunk in _tiles_to_row_chunks(x_ref, bm):
            x_a, x_b = _unpack_bf16_pair(chunk)
            parts += [x_a.astype(jnp.bfloat16), x_b.astype(jnp.bfloat16)]
        xb = jnp.concatenate(parts, axis=1)
        gate = jnp.dot(xb, wg_b[...], preferred_element_type=jnp.float32)
        up = jnp.dot(xb, wu_b[...], preferred_element_type=jnp.float32)
        hid = (gate * _sigmoid(gate) * up).astype(jnp.bfloat16)
        y = jnp.dot(hid, wd_b[...], preferred_element_type=jnp.float32)
        half = y.shape[1] // 2
        _rows_to_tiles(y_ref, _pack_bf16_pair(y[:, :half], y[:, half:]))

        @pl.when(is_last & has_next)
        def _():
            land(after)

    @pl.when((i > 0) & (blk >= nact))
    def _():
        y_ref[...] = jnp.zeros_like(y_ref)


def _experts(block_expert, sched, xb, w_gate, w_up, w_down, bm):
    p_rows = xb.shape[0] // SUBLANES
    d = w_gate.shape[1]
    de = w_gate.shape[-1]

    def x_map(i, be, sc):
        return (jnp.clip(i - 1, 0, sc[0] - 1), 0)

    grid_spec = pltpu.PrefetchScalarGridSpec(
        num_scalar_prefetch=2,
        grid=(p_rows // bm + 1,),
        in_specs=[pl.BlockSpec((bm * SUBLANES, LANES), x_map),
                  pl.BlockSpec(memory_space=pl.ANY),
                  pl.BlockSpec(memory_space=pl.ANY),
                  pl.BlockSpec(memory_space=pl.ANY)],
        out_specs=pl.BlockSpec((bm * SUBLANES, LANES), lambda i, be, sc: (jnp.maximum(i - 1, 0), 0)),
        scratch_shapes=[pltpu.VMEM((d, de), jnp.float32), pltpu.VMEM((d, de), jnp.float32),
                        pltpu.VMEM((de, d), jnp.float32),
                        pltpu.VMEM((d, de), jnp.bfloat16), pltpu.VMEM((d, de), jnp.bfloat16),
                        pltpu.VMEM((de, d), jnp.bfloat16),
                        pltpu.SemaphoreType.DMA(())],
    )
    return pl.pallas_call(
        _expert_kernel,
        grid_spec=grid_spec,
        out_shape=jax.ShapeDtypeStruct(xb.shape, jnp.uint32),
        compiler_params=_params("arbitrary"),
        name="moe_experts",
    )(block_expert, sched, xb, w_gate, w_up, w_down)


def _combine_kernel(dest_ref, dest_next_ref, h_ref, wts_ref, g_ref, y_ref, o_ref, ybuf, sem):
    i = pl.program_id(0)
    tm = h_ref.shape[0]
    slot = i % 2

    def gather(idx_ref, which):
        for r in range(tm):
            for k in range(2):
                src = pl.multiple_of(idx_ref[0, k * tm + r] * SUBLANES, SUBLANES)
                pltpu.make_async_copy(y_ref.at[pl.ds(src, SUBLANES), :],
                                      ybuf.at[which, k, pl.ds(r * SUBLANES, SUBLANES), :], sem.at[which]).start()

    def wait_tile(which):
        for k in range(2):
            pltpu.make_async_copy(y_ref.at[pl.ds(0, tm * SUBLANES), :], ybuf.at[which, k], sem.at[which]).wait()

    @pl.when(i == 0)
    def _():
        gather(dest_ref, slot)

    wait_tile(slot)
    gather(dest_next_ref, 1 - slot)

    w = wts_ref[...]
    half = h_ref.shape[1] // 2
    lo_parts, hi_parts = [], []
    for c, (c0, c1) in enumerate(zip(_tiles_to_row_chunks(ybuf.at[slot, 0], tm),
                                      _tiles_to_row_chunks(ybuf.at[slot, 1], tm))):
        a0, b0 = _unpack_bf16_pair(c0)
        a1, b1 = _unpack_bf16_pair(c1)
        lo_parts.append(h_ref[:, c * LANES:(c + 1) * LANES] + w[:, 0:1] * a0 + w[:, 1:2] * a1)
        hi_parts.append(h_ref[:, half + c * LANES:half + (c + 1) * LANES] + w[:, 0:1] * b0 + w[:, 1:2] * b1)
    h3 = jnp.concatenate(lo_parts + hi_parts, axis=1)
    o_ref[...] = h3 * _rms_scale(h3, NORM_EPS) * g_ref[...]

    @pl.when(i == pl.num_programs(0) - 1)
    def _():
        wait_tile(1 - slot)


def _combine(dest3, h2, wts, g_final, y, tm):
    n_tok, d = h2.shape
    last = n_tok // tm - 1
    return pl.pallas_call(
        _combine_kernel,
        grid=(n_tok // tm,),
        in_specs=[pl.BlockSpec((None, 1, 2 * tm), lambda i: (i, 0, 0), memory_space=pltpu.SMEM),
                  pl.BlockSpec((None, 1, 2 * tm), lambda i: (jnp.minimum(i + 1, last), 0, 0),
                               memory_space=pltpu.SMEM),
                  pl.BlockSpec((tm, d), lambda i: (i, 0)),
                  pl.BlockSpec((tm, 2), lambda i: (i, 0)),
                  _resident(g_final.shape),
                  pl.BlockSpec(memory_space=pl.ANY)],
        out_specs=pl.BlockSpec((tm, d), lambda i: (i, 0)),
        out_shape=jax.ShapeDtypeStruct((n_tok, d), jnp.float32),
        scratch_shapes=[pltpu.VMEM((2, 2, tm * SUBLANES, LANES), jnp.uint32), pltpu.SemaphoreType.DMA((2,))],
        compiler_params=_params("arbitrary"),
        name="moe_combine",
    )(dest3, dest3, h2, wts, g_final, y)


def _transpose_w_in(w_in):
    return jnp.swapaxes(w_in, 0, 1).astype(jnp.bfloat16)


def _split_w_uq(w_uq):
    half = MLA_ROPE_DIM // 2
    w = w_uq.reshape(MLA_Q_RANK, MLA_HEADS, MLA_QK_DIM).transpose(1, 0, 2)
    pe = w[:, :, MLA_NOPE_DIM:]
    pe_swapped = jnp.concatenate([pe[:, :, half:], pe[:, :, :half]], axis=2)
    return jnp.concatenate([w, pe_swapped], axis=2).astype(jnp.bfloat16)


def _split_w_ukv(w_ukv):
    w = w_ukv.reshape(MLA_KV_RANK, MLA_HEADS, MLA_NOPE_DIM + MLA_V_DIM)
    wuk = w[:, :, :MLA_NOPE_DIM].transpose(1, 0, 2).astype(jnp.bfloat16)
    wuv = w[:, :, MLA_NOPE_DIM:].reshape(MLA_KV_RANK, MLA_HEADS * MLA_V_DIM).astype(jnp.bfloat16)
    return wuk, wuv


def kernel(x, mem, positions, attn_norm_g, w_in, diff_lambda_q1, diff_lambda_k1, diff_lambda_q2, diff_lambda_k2, diff_subln_g, w_o_diff, mla_q_norm_g, w_uq, mla_kv_norm_g, w_ukv, w_o_mla, w_out, cross_norm_g, mem_norm_g, w_cq, w_ckv, w_co, ffn_norm_g, w_router_group, b_router_group, w_router_expert, b_router_expert, w_expert_gate, w_expert_up, w_expert_down, final_norm_g):
    batch, seq, d = x.shape
    assert d == D_MODEL and w_in.shape[0] == 1, "single-layer kernel"
    n_tok = batch * seq
    bf = jnp.bfloat16
    x2 = x.reshape(n_tok, d)

    tm_proj = min(1024, seq)
    tm_row = min(256, seq)
    tm_moe = min(MOE_ROWS_PER_BLOCK, seq)
    tm_cross = min(512, seq)
    tq = min(2048, seq)
    rg_diff = 128
    rg_mla = 256

    cos_t, sin_t = _rope_tables(positions, n_tok, tm_proj)

    g_attn = attn_norm_g[0].reshape(1, d)
    qkv, latent, gates = _inproj(x2, g_attn, _transpose_w_in(w_in[0]), cos_t, sin_t, tm_proj)

    o_a = _diff_attn(qkv.reshape(batch, seq, QKV_COLS),
                     diff_lambda_q1[0].reshape(1, -1), diff_lambda_k1[0].reshape(1, -1),
                     diff_lambda_q2[0].reshape(1, -1), diff_lambda_k2[0].reshape(1, -1),
                     diff_subln_g[0].reshape(1, -1), tq, rg_diff)

    wuk, wuv = _split_w_ukv(w_ukv[0])
    q_cat, k_cat, v_mla = _mla_proj(latent, mla_q_norm_g[0].reshape(1, -1), mla_kv_norm_g[0].reshape(1, -1),
                                    _split_w_uq(w_uq[0]), wuk, wuv, cos_t, sin_t, batch, seq, min(512, seq))
    o_b = _mla_attn(q_cat, k_cat, v_mla.reshape(batch, seq, MLA_HEADS * MLA_V_DIM), tq, rg_mla)

    h1 = _merge_out(o_a.reshape(n_tok, -1), o_b.reshape(n_tok, -1), gates, x2,
                    w_o_diff[0].astype(bf), w_o_mla[0].astype(bf), w_out[0].astype(bf), tm_row)

    kv_mem = _mem_kv(mem, mem_norm_g[0].reshape(1, d), w_ckv[0].astype(bf))
    n_router = N_GROUPS + N_EXPERTS
    w_r = jnp.concatenate([w_router_group[0].astype(jnp.float32), w_router_expert[0].astype(jnp.float32),
                           jnp.zeros((d, LANES - n_router), jnp.float32)], axis=1)
    w_r_hi = w_r.astype(bf)
    w_r_lo = (w_r - w_r_hi.astype(jnp.float32)).astype(bf)
    w_r = jnp.concatenate([w_r_hi, w_r_lo], axis=1)
    b_r = jnp.concatenate([b_router_group[0].astype(jnp.float32), b_router_expert[0].astype(jnp.float32),
                           jnp.zeros((LANES - n_router,), jnp.float32)]).reshape(1, LANES)
    g_ffn = ffn_norm_g[0].reshape(1, d)
    h2, eid, rank, wts, cnt = _cross_router(h1, cross_norm_g[0].reshape(1, d), w_cq[0].astype(bf), kv_mem,
                                            w_co[0].astype(bf), g_ffn, w_r, b_r, seq, tm_cross)

    bm = MOE_ROWS_PER_BLOCK
    assert tm_moe == bm, "dispatch zeroes whole row blocks from its token-tile scratch"
    counts = cnt[0, ROUTER_EXPERT_LANE0:ROUTER_EXPERT_LANE0 + N_EXPERTS].astype(jnp.int32)
    padded = ((counts + bm - 1) // bm) * bm
    padded_end = jnp.cumsum(padded)
    padded_off = padded_end - padded
    seg_start = jnp.sum(jnp.where(eid[..., None] == jnp.arange(N_EXPERTS, dtype=jnp.int32), padded_off, 0), axis=-1)
    dest = seg_start + rank
    p_rows = ((2 * n_tok + bm - 1) // bm) * bm + N_EXPERTS * bm
    n_blocks = p_rows // bm
    n_active = (padded_end[-1] // bm).astype(jnp.int32)
    blk = jnp.minimum(jnp.arange(n_blocks, dtype=jnp.int32), n_active - 1)
    block_expert = jnp.sum((padded_end[None, :] <= (blk * bm)[:, None]).astype(jnp.int32), axis=1)
    block_expert = jnp.minimum(block_expert, N_EXPERTS - 1)
    dest3 = dest.reshape(-1, 2, tm_cross // tm_moe, tm_moe).transpose(0, 2, 1, 3).reshape(
        n_tok // tm_moe, 1, 2 * tm_moe)
    unused = n_active + jnp.arange(N_EXPERTS, dtype=jnp.int32)
    ztail = jnp.stack([jnp.concatenate([jnp.maximum(padded_end - bm, 0), jnp.minimum(unused, n_blocks - 1) * bm]),
                       jnp.concatenate([padded > 0, unused < n_blocks]).astype(jnp.int32)]).astype(jnp.int32)
    sched = jnp.concatenate([n_active.reshape(1), padded // bm, padded_off // bm]).astype(jnp.int32)

    xb = _dispatch(dest3, ztail, h2, g_ffn, p_rows, tm_moe)
    y = _experts(block_expert, sched, xb, w_expert_gate[0], w_expert_up[0], w_expert_down[0], bm)
    out = _combine(dest3, h2, wts, final_norm_g.reshape(1, d), y, tm_moe)
    return out.reshape(batch, seq, d)
```

```python
import functools
import math

import jax
import jax.numpy as jnp
from jax import lax
from jax.experimental import pallas as pl
from jax.experimental.pallas import tpu as pltpu

D_MODEL = 2048
ROPE_THETA = 500000.0
NORM_EPS = 1e-6

DIFF_HEADS = 8
DIFF_HEAD_DIM = 64
DIFF_V_DIM = 2 * DIFF_HEAD_DIM
DIFF_ROT = DIFF_HEAD_DIM // 4
DIFF_SUBLN_EPS = 1e-5
DIFF_LAMBDA_INIT = 0.8 - 0.6 * math.exp(-0.3 * 0)

MLA_HEADS = 8
MLA_Q_RANK = 512
MLA_KV_RANK = 256
MLA_NOPE_DIM = 128
MLA_ROPE_DIM = 64
MLA_V_DIM = 128
MLA_QK_DIM = MLA_NOPE_DIM + MLA_ROPE_DIM

CROSS_HEADS = 4
CROSS_HEAD_DIM = 128

N_GROUPS = 4
EXPERTS_PER_GROUP = 8
N_EXPERTS = N_GROUPS * EXPERTS_PER_GROUP
D_EXPERT = 512

LANES = 128
LOG2E = 1.4426950408889634
VMEM_LIMIT_BYTES = 56 * 1024 * 1024

ROUTER_EXPERT_LANE0 = N_GROUPS

QKV_COLS = 3 * DIFF_HEADS * DIFF_V_DIM
LATENT_COLS = 1024
GATE_COLS = 2 * D_MODEL
KPE_COL0 = MLA_Q_RANK + MLA_KV_RANK

MOE_ROWS_PER_BLOCK = 512
SUBLANES = 8


def _params(*semantics):
    return pltpu.CompilerParams(dimension_semantics=semantics, vmem_limit_bytes=VMEM_LIMIT_BYTES)


def _resident(shape):
    zeros = (0,) * len(shape)
    return pl.BlockSpec(shape, lambda *_: zeros, pipeline_mode=pl.Buffered(1))


def _rms_scale(xf, eps):
    return lax.rsqrt(jnp.mean(xf * xf, axis=-1, keepdims=True) + eps)


def _sigmoid(x):
    return 0.5 * jnp.tanh(0.5 * x) + 0.5


def _pack_bf16_pair(a, b):
    hi = lax.bitcast_convert_type(a.astype(jnp.bfloat16).astype(jnp.float32), jnp.uint32)
    lo = lax.bitcast_convert_type(b.astype(jnp.bfloat16).astype(jnp.float32), jnp.uint32)
    return hi | (lo >> 16)


def _unpack_bf16_pair(w):
    a = lax.bitcast_convert_type(w & jnp.uint32(0xFFFF0000), jnp.float32)
    b = lax.bitcast_convert_type(w << 16, jnp.float32)
    return a, b


def _rows_to_tiles(ref_view, packed):
    rows = packed.shape[0]
    for c in range(SUBLANES):
        ref_view[pl.ds(c, rows, stride=SUBLANES), :] = packed[:, c * LANES:(c + 1) * LANES]


def _tiles_to_row_chunks(ref_view, rows):
    return [ref_view[pl.ds(c, rows, stride=SUBLANES), :] for c in range(SUBLANES)]


def _lane_iota(shape):
    return lax.broadcasted_iota(jnp.int32, shape, len(shape) - 1)


def _trig_kernel(pos_ref, invf_ref, cos_ref, sin_ref):
    ang = pos_ref[...].astype(jnp.float32) * invf_ref[...]
    cos_ref[...] = jnp.cos(ang)
    sin_ref[...] = jnp.sin(ang)


def _rope_tables(positions, n_tok, tm):
    half_m = MLA_ROPE_DIM // 2
    half_d = DIFF_ROT // 2
    inv_m = jnp.float32(ROPE_THETA) ** (-jnp.arange(half_m, dtype=jnp.float32) * 2.0 / MLA_ROPE_DIM)
    inv_d = jnp.float32(ROPE_THETA) ** (-jnp.arange(half_d, dtype=jnp.float32) * 2.0 / DIFF_ROT)
    invf = jnp.concatenate([inv_m, inv_m, inv_d, inv_d,
                            jnp.zeros((DIFF_HEAD_DIM - DIFF_ROT,), jnp.float32)]).reshape(1, LANES)
    pos = positions.reshape(n_tok, 1)
    return pl.pallas_call(
        _trig_kernel,
        grid=(n_tok // tm,),
        in_specs=[pl.BlockSpec((tm, 1), lambda i: (i, 0)), _resident((1, LANES))],
        out_specs=[pl.BlockSpec((tm, LANES), lambda i: (i, 0))] * 2,
        out_shape=[jax.ShapeDtypeStruct((n_tok, LANES), jnp.float32)] * 2,
        compiler_params=_params("parallel"),
        name="rope_tables",
    )(pos, invf)


def _diff_rope_coeffs(cos_t, sin_t):
    lane = _lane_iota(cos_t.shape)
    upper = lane >= DIFF_HEAD_DIM
    cos_d = jnp.where(upper, cos_t, pltpu.roll(cos_t, DIFF_HEAD_DIM, 1))
    sin_d = jnp.where(upper, sin_t, pltpu.roll(sin_t, DIFF_HEAD_DIM, 1))
    in_head = lane % DIFF_HEAD_DIM
    half = DIFF_ROT // 2
    s_next = jnp.where(in_head < half, -sin_d, 0.0)
    s_prev = jnp.where((in_head >= half) & (in_head < DIFF_ROT), sin_d, 0.0)
    return cos_d, s_next, s_prev


def _mla_rope(pair, cos_t, sin_t):
    lane = _lane_iota(pair.shape)
    sin_signed = jnp.where(lane < MLA_ROPE_DIM // 2, -sin_t, sin_t)
    return pair * cos_t + pltpu.roll(pair, MLA_ROPE_DIM, 1) * sin_signed


INPROJ_TN = 1024
INPROJ_PIECE = 256
Q_TILES = DIFF_HEADS * DIFF_V_DIM // INPROJ_TN
ROPE_TILES = 2 * Q_TILES
QKV_TILES = QKV_COLS // INPROJ_TN
LATENT_TILES = LATENT_COLS // INPROJ_TN
GATE_TILES = GATE_COLS // INPROJ_TN
INPROJ_TILES = QKV_TILES + LATENT_TILES + GATE_TILES
GATE_ROW0 = QKV_COLS + KPE_COL0 + MLA_ROPE_DIM


def _inproj_kernel(x_ref, g_ref, w_ref, cos_ref, sin_ref, qkv_ref, lat_ref, gate_ref, xn_ref):
    j = pl.program_id(1)

    @pl.when(j == 0)
    def _():
        xf = x_ref[...]
        xn_ref[...] = (xf * _rms_scale(xf, NORM_EPS) * g_ref[...]).astype(jnp.bfloat16)

    def pieces(epilogue):
        for c in range(INPROJ_TN // INPROJ_PIECE):
            cols = slice(c * INPROJ_PIECE, (c + 1) * INPROJ_PIECE)
            acc = lax.dot_general(xn_ref[...], w_ref[cols, :], (((1,), (1,)), ((), ())),
                                  preferred_element_type=jnp.float32)
            epilogue(acc, cols)

    @pl.when(j < ROPE_TILES)
    def _():
        cos_d, s_next, s_prev = _diff_rope_coeffs(cos_ref[...], sin_ref[...])
        qscale = jnp.where(j < Q_TILES, DIFF_HEAD_DIM ** -0.5 * LOG2E, 1.0).astype(jnp.float32)

        def rope(acc, cols):
            for c in range(INPROJ_PIECE // LANES):
                xc = acc[:, c * LANES:(c + 1) * LANES]
                rot = (xc * cos_d + pltpu.roll(xc, LANES - DIFF_ROT // 2, 1) * s_next
                       + pltpu.roll(xc, DIFF_ROT // 2, 1) * s_prev)
                lo = cols.start + c * LANES
                qkv_ref[:, lo:lo + LANES] = (rot * qscale).astype(qkv_ref.dtype)

        pieces(rope)

    @pl.when((j >= ROPE_TILES) & (j < QKV_TILES))
    def _():
        def value(acc, cols):
            qkv_ref[:, cols] = acc.astype(qkv_ref.dtype)

        pieces(value)

    @pl.when((j >= QKV_TILES) & (j < QKV_TILES + LATENT_TILES))
    def _():
        def latent(acc, cols):
            if cols.start <= KPE_COL0 < cols.stop:
                c0 = KPE_COL0 - cols.start
                v = acc[:, c0:c0 + LANES]
                lane = _lane_iota(v.shape)
                half = MLA_ROPE_DIM // 2
                swapped = jnp.where(lane < MLA_ROPE_DIM + half, pltpu.roll(v, half, 1),
                                    pltpu.roll(v, MLA_ROPE_DIM + half, 1))
                parts = [acc[:, :c0], jnp.where(lane < MLA_ROPE_DIM, v, swapped), acc[:, c0 + LANES:]]
                acc = jnp.concatenate([p for p in parts if p.shape[1]], axis=1)
            lat_ref[:, cols] = acc

        pieces(latent)

    @pl.when(j >= QKV_TILES + LATENT_TILES)
    def _():
        def gate(acc, cols):
            gate_ref[:, cols] = _sigmoid(acc).astype(gate_ref.dtype)

        pieces(gate)


def _inproj(x2, g, w_all, cos_t, sin_t, tm):
    n_tok, d = x2.shape
    tn = INPROJ_TN
    lat0 = QKV_TILES
    gate0 = QKV_TILES + LATENT_TILES
    return pl.pallas_call(
        _inproj_kernel,
        grid=(n_tok // tm, INPROJ_TILES),
        in_specs=[pl.BlockSpec((tm, d), lambda i, j: (i, 0)),
                  _resident((1, d)),
                  pl.BlockSpec((pl.Element(tn), pl.Element(d)),
                               lambda i, j: (pl.multiple_of(jnp.where(j < gate0, j * tn, GATE_ROW0 + (j - gate0) * tn), 16), 0)),
                  pl.BlockSpec((tm, LANES), lambda i, j: (i, 0)),
                  pl.BlockSpec((tm, LANES), lambda i, j: (i, 0))],
        out_specs=[pl.BlockSpec((tm, tn), lambda i, j: (i, jnp.clip(j, 0, QKV_TILES - 1))),
                   pl.BlockSpec((tm, tn), lambda i, j: (i, jnp.clip(j - lat0, 0, LATENT_TILES - 1))),
                   pl.BlockSpec((tm, tn), lambda i, j: (i, jnp.clip(j - gate0, 0, GATE_TILES - 1)))],
        out_shape=[jax.ShapeDtypeStruct((n_tok, QKV_COLS), jnp.bfloat16),
                   jax.ShapeDtypeStruct((n_tok, LATENT_COLS), jnp.float32),
                   jax.ShapeDtypeStruct((n_tok, GATE_COLS), jnp.bfloat16)],
        scratch_shapes=[pltpu.VMEM((tm, d), jnp.bfloat16)],
        compiler_params=_params("parallel", "arbitrary"),
        name="inproj",
    )(x2, g, w_all, cos_t, sin_t)


def _mla_proj_kernel(c_ref, gq_ref, gkv_ref, wuq_ref, wuk_ref, wuv_ref, cos_ref, sin_ref,
                     q_ref, k_ref, v_ref):
    cos_t = cos_ref[...]
    sin_t = sin_ref[...]
    cq = c_ref[:, :MLA_Q_RANK]
    cqn = (cq * _rms_scale(cq, NORM_EPS) * gq_ref[...]).astype(jnp.bfloat16)
    ckv = c_ref[:, MLA_Q_RANK:KPE_COL0]
    ckvn = (ckv * _rms_scale(ckv, NORM_EPS) * gkv_ref[...]).astype(jnp.bfloat16)
    kpe = _mla_rope(c_ref[:, KPE_COL0:KPE_COL0 + LANES], cos_t, sin_t)[:, :MLA_ROPE_DIM].astype(k_ref.dtype)
    qscale = MLA_QK_DIM ** -0.5 * LOG2E
    for h in range(MLA_HEADS):
        r = jnp.dot(cqn, wuq_ref[h], preferred_element_type=jnp.float32)
        q_ref[0, h, :, :MLA_NOPE_DIM] = (r[:, :MLA_NOPE_DIM] * qscale).astype(q_ref.dtype)
        qpe = _mla_rope(r[:, MLA_NOPE_DIM:], cos_t, sin_t)[:, :MLA_ROPE_DIM]
        q_ref[0, h, :, MLA_NOPE_DIM:] = (qpe * qscale).astype(q_ref.dtype)
        kn = jnp.dot(ckvn, wuk_ref[h], preferred_element_type=jnp.float32)
        k_ref[0, h, :, :MLA_NOPE_DIM] = kn.astype(k_ref.dtype)
        k_ref[0, h, :, MLA_NOPE_DIM:] = kpe
    v_ref[...] = jnp.dot(ckvn, wuv_ref[...], preferred_element_type=jnp.float32).astype(v_ref.dtype)


def _mla_proj(latent, gq, gkv, wuq, wuk, wuv, cos_t, sin_t, batch, seq, tm):
    n_tok = latent.shape[0]
    per_b = seq // tm
    head_spec = pl.BlockSpec((1, MLA_HEADS, tm, MLA_QK_DIM), lambda i: (i // per_b, 0, i % per_b, 0))
    head_shape = jax.ShapeDtypeStruct((batch, MLA_HEADS, seq, MLA_QK_DIM), jnp.bfloat16)
    return pl.pallas_call(
        _mla_proj_kernel,
        grid=(n_tok // tm,),
        in_specs=[pl.BlockSpec((tm, LATENT_COLS), lambda i: (i, 0)),
                  _resident(gq.shape), _resident(gkv.shape),
                  _resident(wuq.shape), _resident(wuk.shape), _resident(wuv.shape),
                  pl.BlockSpec((tm, LANES), lambda i: (i, 0)),
                  pl.BlockSpec((tm, LANES), lambda i: (i, 0))],
        out_specs=[head_spec, head_spec,
                   pl.BlockSpec((tm, MLA_HEADS * MLA_V_DIM), lambda i: (i, 0))],
        out_shape=[head_shape, head_shape,
                   jax.ShapeDtypeStruct((n_tok, MLA_HEADS * MLA_V_DIM), jnp.bfloat16)],
        compiler_params=_params("parallel"),
        name="mla_proj",
    )(latent, gq, gkv, wuq, wuk, wuv, cos_t, sin_t)


def _with_ones(v):
    return jnp.concatenate([v, jnp.ones((v.shape[0], LANES), v.dtype)], axis=-1)


def _softmax_pv(s, v_ones):
    m = jnp.max(s, axis=-1, keepdims=True)
    p = jnp.exp2(s - m).astype(v_ones.dtype)
    pv = jnp.dot(p, v_ones, preferred_element_type=jnp.float32)
    dv = v_ones.shape[1] - LANES
    return pv[:, :dv] / pv[:, dv:]


ATTN_HEADS_PER_STEP = 2


def _diff_attn_kernel(q_ref, k_ref, v_ref, lq1_ref, lk1_ref, lq2_ref, lk2_ref, g_ref, o_ref, v1_ref, *, rg):
    heads = q_ref.shape[2] // LANES

    @pl.when(pl.program_id(2) == 0)
    def _():
        for h in range(heads):
            v1_ref[h] = _with_ones(v_ref[0, :, h * DIFF_V_DIM:(h + 1) * DIFF_V_DIM])

    lam = (jnp.exp(jnp.sum(lq1_ref[...] * lk1_ref[...], axis=-1, keepdims=True))
           - jnp.exp(jnp.sum(lq2_ref[...] * lk2_ref[...], axis=-1, keepdims=True))
           + DIFF_LAMBDA_INIT)
    lane = _lane_iota((rg, LANES))
    for g in range(q_ref.shape[1] // rg):
        for h in range(heads):
            cols = slice(h * LANES, (h + 1) * LANES)
            q = q_ref[0, g * rg:(g + 1) * rg, cols]
            k = k_ref[0, :, cols]
            zero = jnp.zeros_like(q)
            q12 = jnp.concatenate([jnp.where(lane < DIFF_HEAD_DIM, q, zero),
                                   jnp.where(lane >= DIFF_HEAD_DIM, q, zero)], axis=0)
            s = lax.dot_general(q12, k, (((1,), (1,)), ((), ())), preferred_element_type=jnp.float32)
            a = _softmax_pv(s, v1_ref[h])
            o = a[:rg] - lam * a[rg:]
            o = o * _rms_scale(o, DIFF_SUBLN_EPS) * g_ref[...] * (1.0 - DIFF_LAMBDA_INIT)
            o_ref[0, g * rg:(g + 1) * rg, cols] = o.astype(o_ref.dtype)


def _diff_attn(qkv3, lq1, lk1, lq2, lk2, subln_g, tq, rg):
    batch, seq, _ = qkv3.shape
    hp = ATTN_HEADS_PER_STEP
    steps = DIFF_HEADS // hp
    width = hp * DIFF_V_DIM
    return pl.pallas_call(
        functools.partial(_diff_attn_kernel, rg=rg),
        grid=(batch, steps, seq // tq),
        in_specs=[pl.BlockSpec((1, tq, width), lambda b, hh, i: (b, i, hh)),
                  pl.BlockSpec((1, seq, width), lambda b, hh, i: (b, 0, steps + hh)),
                  pl.BlockSpec((1, seq, width), lambda b, hh, i: (b, 0, 2 * steps + hh)),
                  _resident(lq1.shape), _resident(lk1.shape), _resident(lq2.shape), _resident(lk2.shape),
                  _resident(subln_g.shape)],
        out_specs=pl.BlockSpec((1, tq, width), lambda b, hh, i: (b, i, hh)),
        out_shape=jax.ShapeDtypeStruct((batch, seq, DIFF_HEADS * DIFF_V_DIM), jnp.bfloat16),
        scratch_shapes=[pltpu.VMEM((hp, seq, DIFF_V_DIM + LANES), jnp.bfloat16)],
        compiler_params=_params("parallel", "parallel", "arbitrary"),
        name="diff_attn",
    )(qkv3, qkv3, qkv3, lq1, lk1, lq2, lk2, subln_g)


def _mla_attn_kernel(q_ref, k_ref, v_ref, o_ref, v1_ref, *, rg):
    heads = q_ref.shape[1]

    @pl.when(pl.program_id(2) == 0)
    def _():
        for h in range(heads):
            v1_ref[h] = _with_ones(v_ref[0, :, h * MLA_V_DIM:(h + 1) * MLA_V_DIM])

    for g in range(q_ref.shape[2] // rg):
        for h in range(heads):
            s = lax.dot_general(q_ref[0, h, g * rg:(g + 1) * rg], k_ref[0, h], (((1,), (1,)), ((), ())),
                                preferred_element_type=jnp.float32)
            o_ref[0, g * rg:(g + 1) * rg, h * MLA_V_DIM:(h + 1) * MLA_V_DIM] = (
                _softmax_pv(s, v1_ref[h]).astype(o_ref.dtype))


def _mla_attn(q_cat, k_cat, v3, tq, rg):
    batch, heads, seq, dqk = q_cat.shape
    hp = ATTN_HEADS_PER_STEP
    return pl.pallas_call(
        functools.partial(_mla_attn_kernel, rg=rg),
        grid=(batch, heads // hp, seq // tq),
        in_specs=[pl.BlockSpec((1, hp, tq, dqk), lambda b, h, i: (b, h, i, 0)),
                  pl.BlockSpec((1, hp, seq, dqk), lambda b, h, i: (b, h, 0, 0)),
                  pl.BlockSpec((1, seq, hp * MLA_V_DIM), lambda b, h, i: (b, 0, h))],
        out_specs=pl.BlockSpec((1, tq, hp * MLA_V_DIM), lambda b, h, i: (b, i, h)),
        out_shape=jax.ShapeDtypeStruct((batch, seq, heads * MLA_V_DIM), jnp.bfloat16),
        scratch_shapes=[pltpu.VMEM((hp, seq, MLA_V_DIM + LANES), jnp.bfloat16)],
        compiler_params=_params("parallel", "parallel", "arbitrary"),
        name="mla_attn",
    )(q_cat, k_cat, v3)


def _merge_out_kernel(oa_ref, ob_ref, sga_ref, sgb_ref, x_ref, woa_ref, wob_ref, wout_ref, h_ref):
    ya = jnp.dot(oa_ref[...], woa_ref[...], preferred_element_type=jnp.float32)
    yb = jnp.dot(ob_ref[...], wob_ref[...], preferred_element_type=jnp.float32)
    merged = sga_ref[...].astype(jnp.float32) * ya + sgb_ref[...].astype(jnp.float32) * yb
    h_ref[...] = x_ref[...] + jnp.dot(merged.astype(jnp.bfloat16), wout_ref[...],
                                       preferred_element_type=jnp.float32)


def _merge_out(o_a, o_b, gates, x2, w_oa, w_ob, w_out, tm):
    n_tok, d = x2.shape
    return pl.pallas_call(
        _merge_out_kernel,
        grid=(n_tok // tm,),
        in_specs=[pl.BlockSpec((tm, o_a.shape[1]), lambda i: (i, 0)),
                  pl.BlockSpec((tm, o_b.shape[1]), lambda i: (i, 0)),
                  pl.BlockSpec((tm, d), lambda i: (i, 0)),
                  pl.BlockSpec((tm, d), lambda i: (i, 1)),
                  pl.BlockSpec((tm, d), lambda i: (i, 0)),
                  _resident(w_oa.shape), _resident(w_ob.shape), _resident(w_out.shape)],
        out_specs=pl.BlockSpec((tm, d), lambda i: (i, 0)),
        out_shape=jax.ShapeDtypeStruct((n_tok, d), jnp.float32),
        compiler_params=_params("parallel"),
        name="merge_out",
    )(o_a, o_b, gates, gates, x2, w_oa, w_ob, w_out)


def _mem_kv_kernel(mem_ref, g_ref, w_ref, kv_ref):
    mf = mem_ref[0]
    mn = (mf * _rms_scale(mf, NORM_EPS) * g_ref[...]).astype(jnp.bfloat16)
    kv_ref[0] = jnp.dot(mn, w_ref[...], preferred_element_type=jnp.float32).astype(kv_ref.dtype)


def _mem_kv(mem, g, w_ckv):
    batch, m, d = mem.shape
    return pl.pallas_call(
        _mem_kv_kernel,
        grid=(batch,),
        in_specs=[pl.BlockSpec((1, m, d), lambda b: (b, 0, 0)), _resident(g.shape), _resident(w_ckv.shape)],
        out_specs=pl.BlockSpec((1, m, w_ckv.shape[1]), lambda b: (b, 0, 0)),
        out_shape=jax.ShapeDtypeStruct((batch, m, w_ckv.shape[1]), jnp.bfloat16),
        compiler_params=_params("parallel"),
        name="mem_kv",
    )(mem, g, w_ckv)


def _cross_router_kernel(h_ref, gc_ref, wcq_ref, kv_ref, wco_ref, gf_ref, wr_ref, br_ref,
                         h2_ref, eid_ref, rank_ref, wts_ref, cnt_ref, carry_ref):
    i = pl.program_id(0)

    @pl.when(i == 0)
    def _():
        carry_ref[...] = jnp.zeros_like(carry_ref)

    h1 = h_ref[...]
    tm = h1.shape[0]
    hn = (h1 * _rms_scale(h1, NORM_EPS) * gc_ref[...]).astype(jnp.bfloat16)
    q = jnp.dot(hn, wcq_ref[...], preferred_element_type=jnp.float32) * (CROSS_HEAD_DIM ** -0.5 * LOG2E)
    q = q.astype(jnp.bfloat16)
    kv_cols = CROSS_HEADS * CROSS_HEAD_DIM
    heads = []
    for hd in range(CROSS_HEADS):
        lo = hd * CROSS_HEAD_DIM
        kh = kv_ref[0, :, lo:lo + CROSS_HEAD_DIM]
        vh = kv_ref[0, :, kv_cols + lo:kv_cols + lo + CROSS_HEAD_DIM]
        s = lax.dot_general(q[:, lo:lo + CROSS_HEAD_DIM], kh, (((1,), (1,)), ((), ())),
                            preferred_element_type=jnp.float32)
        heads.append(_softmax_pv(s, _with_ones(vh)).astype(jnp.bfloat16))
    o = jnp.concatenate(heads, axis=-1)
    h2 = h1 + jnp.dot(o, wco_ref[...], preferred_element_type=jnp.float32)
    h2_ref[...] = h2

    t = h2 * _rms_scale(h2, NORM_EPS) * gf_ref[...]
    t_hi = t.astype(jnp.bfloat16)
    t_lo = (t - t_hi.astype(jnp.float32)).astype(jnp.bfloat16)
    hi = jnp.dot(t_hi, wr_ref[...], preferred_element_type=jnp.float32)
    lo = jnp.dot(t_lo, wr_ref[:, :LANES], preferred_element_type=jnp.float32)
    logits = hi[:, :LANES] + (hi[:, LANES:] + lo) + br_ref[...]
    lane = _lane_iota(logits.shape)
    neg = jnp.float32(-jnp.inf)
    big = jnp.int32(2 * LANES)
    is_group = lane < N_GROUPS
    lg = jnp.where(is_group, logits, neg)
    mg = jnp.max(lg, axis=-1, keepdims=True)
    g_idx = jnp.min(jnp.where(is_group & (logits == mg), lane, big), axis=-1, keepdims=True)
    g_p = 1.0 / jnp.sum(jnp.exp(lg - mg), axis=-1, keepdims=True)
    lo_lane = ROUTER_EXPERT_LANE0 + EXPERTS_PER_GROUP * g_idx
    in_grp = (lane >= lo_lane) & (lane < lo_lane + EXPERTS_PER_GROUP)
    l1 = jnp.max(jnp.where(in_grp, logits, neg), axis=-1, keepdims=True)
    i1 = jnp.min(jnp.where(in_grp & (logits == l1), lane, big), axis=-1, keepdims=True)
    rest = in_grp & (lane != i1)
    l2 = jnp.max(jnp.where(rest, logits, neg), axis=-1, keepdims=True)
    i2 = jnp.min(jnp.where(rest & (logits == l2), lane, big), axis=-1, keepdims=True)
    d = jnp.exp(l2 - l1)
    w1 = g_p / (1.0 + d)
    w2 = w1 * d

    oh1 = lane == i1
    oh2 = lane == i2
    cnt = (oh1 | oh2).astype(jnp.bfloat16)
    row = lax.broadcasted_iota(jnp.int32, (tm, tm), 0)
    col = lax.broadcasted_iota(jnp.int32, (tm, tm), 1)
    before = (col < row).astype(jnp.bfloat16)
    slot = jnp.dot(before, cnt, preferred_element_type=jnp.float32) + carry_ref[...]
    r1 = jnp.sum(jnp.where(oh1, slot, 0.0), axis=-1, keepdims=True)
    r2 = jnp.sum(jnp.where(oh2, slot, 0.0), axis=-1, keepdims=True)
    carry_ref[...] += jnp.sum(cnt.astype(jnp.float32), axis=0, keepdims=True)
    cnt_ref[...] = carry_ref[...]

    eye = row == col

    def to_row(c, dtype):
        return jnp.sum(jnp.where(eye, c.astype(jnp.float32), 0.0), axis=0, keepdims=True).astype(dtype)

    eid_ref[0] = jnp.concatenate([to_row(i1 - ROUTER_EXPERT_LANE0, jnp.int32),
                                  to_row(i2 - ROUTER_EXPERT_LANE0, jnp.int32)], axis=0)
    rank_ref[0] = jnp.concatenate([to_row(r1, jnp.int32), to_row(r2, jnp.int32)], axis=0)
    wts_ref[...] = jnp.where(_lane_iota((tm, 2)) == 0, w1, w2)


def _cross_router(h1, gc, w_cq, kv_mem, w_co, gf, w_r, b_r, seq, tm):
    n_tok, d = h1.shape
    per_b = seq // tm
    row2 = pl.BlockSpec((tm, 2), lambda i: (i, 0))
    lane2 = pl.BlockSpec((1, 2, tm), lambda i: (i, 0, 0))
    return pl.pallas_call(
        _cross_router_kernel,
        grid=(n_tok // tm,),
        in_specs=[pl.BlockSpec((tm, d), lambda i: (i, 0)),
                  _resident(gc.shape), _resident(w_cq.shape),
                  pl.BlockSpec((1,) + kv_mem.shape[1:], lambda i: (i // per_b, 0, 0)),
                  _resident(w_co.shape), _resident(gf.shape), _resident(w_r.shape), _resident(b_r.shape)],
        out_specs=[pl.BlockSpec((tm, d), lambda i: (i, 0)), lane2, lane2, row2,
                   pl.BlockSpec((1, LANES), lambda i: (0, 0))],
        out_shape=[jax.ShapeDtypeStruct((n_tok, d), jnp.float32),
                   jax.ShapeDtypeStruct((n_tok // tm, 2, tm), jnp.int32),
                   jax.ShapeDtypeStruct((n_tok // tm, 2, tm), jnp.int32),
                   jax.ShapeDtypeStruct((n_tok, 2), jnp.float32),
                   jax.ShapeDtypeStruct((1, LANES), jnp.float32)],
        scratch_shapes=[pltpu.VMEM((1, LANES), jnp.float32)],
        compiler_params=_params("arbitrary"),
        name="cross_router",
    )(h1, gc, w_cq, kv_mem, w_co, gf, w_r, b_r)


def _dispatch_kernel(dest_ref, ztail_ref, h_ref, g_ref, xb_ref, t_ref, sem, zsem):
    i = pl.program_id(0)
    tm = h_ref.shape[0]
    slot = i % 2

    @pl.when(i == 0)
    def _():
        t_ref[1] = jnp.zeros(t_ref.shape[1:], t_ref.dtype)
        tile_rows = tm * SUBLANES

        pieces = MOE_ROWS_PER_BLOCK // tm

        def zero_copy(e, piece):
            start = pl.multiple_of(ztail_ref[0, e] * SUBLANES + piece * tile_rows, tile_rows)
            return pltpu.make_async_copy(t_ref.at[1], xb_ref.at[pl.ds(start, tile_rows), :], zsem)

        for e in range(ztail_ref.shape[1]):
            @pl.when(ztail_ref[1, e] > 0)
            def _():
                for piece in range(pieces):
                    zero_copy(e, piece).start()
        for e in range(ztail_ref.shape[1]):
            @pl.when(ztail_ref[1, e] > 0)
            def _():
                for piece in range(pieces):
                    zero_copy(e, piece).wait()

    h2 = h_ref[...]
    t = h2 * _rms_scale(h2, NORM_EPS) * g_ref[...]
    half = t.shape[1] // 2
    _rows_to_tiles(t_ref.at[slot], _pack_bf16_pair(t[:, :half], t[:, half:]))

    for r in range(tm):
        for k in range(2):
            dst = pl.multiple_of(dest_ref[0, k * tm + r] * SUBLANES, SUBLANES)
            pltpu.make_async_copy(t_ref.at[slot, pl.ds(r * SUBLANES, SUBLANES), :],
                                  xb_ref.at[pl.ds(dst, SUBLANES), :], sem.at[slot]).start()

    def wait_tile(which):
        for _ in range(2):
            pltpu.make_async_copy(t_ref.at[which], xb_ref.at[pl.ds(0, tm * SUBLANES), :], sem.at[which]).wait()

    @pl.when(i > 0)
    def _():
        wait_tile(1 - slot)

    @pl.when(i == pl.num_programs(0) - 1)
    def _():
        wait_tile(slot)


def _dispatch(dest3, ztail, h2, gf, p_rows, tm):
    n_tok, d = h2.shape
    return pl.pallas_call(
        _dispatch_kernel,
        grid=(n_tok // tm,),
        in_specs=[pl.BlockSpec((None, 1, 2 * tm), lambda i: (i, 0, 0), memory_space=pltpu.SMEM),
                  pl.BlockSpec(memory_space=pltpu.SMEM),
                  pl.BlockSpec((tm, d), lambda i: (i, 0)),
                  _resident(gf.shape)],
        out_specs=pl.BlockSpec(memory_space=pl.ANY),
        out_shape=jax.ShapeDtypeStruct((p_rows * SUBLANES, LANES), jnp.uint32),
        scratch_shapes=[pltpu.VMEM((2, tm * SUBLANES, LANES), jnp.uint32), pltpu.SemaphoreType.DMA((2,)),
                        pltpu.SemaphoreType.DMA(())],
        compiler_params=_params("arbitrary"),
        name="moe_dispatch",
    )(dest3, ztail, h2, gf)


def _expert_kernel(be_ref, nact_ref, x_ref, wg_hbm, wu_hbm, wd_hbm, y_ref,
                   wg_f, wu_f, wd_f, wg_b, wu_b, wd_b, sem):
    i = pl.program_id(0)
    blk = i - 1
    nact = nact_ref[0]
    last_blk = pl.num_programs(0) - 2

    def weight_copies(e):
        return (pltpu.make_async_copy(wg_hbm.at[e], wg_f, sem),
                pltpu.make_async_copy(wu_hbm.at[e], wu_f, sem),
                pltpu.make_async_copy(wd_hbm.at[e], wd_f, sem))

    def fetch(e):
        for c in weight_copies(e):
            c.start()

    def land(e):
        for c in weight_copies(e):
            c.wait()
        half = wg_f.shape[0] // 2
        for c in range(SUBLANES):
            for part, src0 in enumerate((c * LANES, half + c * LANES)):
                dst0 = (2 * c + part) * LANES
                wg_b[dst0:dst0 + LANES, :] = wg_f[src0:src0 + LANES, :].astype(jnp.bfloat16)
                wu_b[dst0:dst0 + LANES, :] = wu_f[src0:src0 + LANES, :].astype(jnp.bfloat16)
        wd_b[...] = wd_f[...].astype(jnp.bfloat16)

    @pl.when(i == 0)
    def _():
        fetch(be_ref[0])
        land(be_ref[0])

    @pl.when((i > 0) & (blk < nact))
    def _():
        here = be_ref[blk]
        nxt = be_ref[jnp.minimum(blk + 1, last_blk)]
        prv = be_ref[jnp.maximum(blk - 1, 0)]
        seg_end = nact_ref[1 + N_EXPERTS + here] + nact_ref[1 + here]
        has_next = seg_end < nact
        after = be_ref[jnp.minimum(seg_end, last_blk)]
        is_first = (blk == 0) | (prv != here)
        is_last = (blk + 1 >= nact) | (nxt != here)

        @pl.when(is_first & has_next)
        def _():
            fetch(after)

        bm = x_ref.shape[0] // SUBLANES
        parts = []
        for chunk in _tiles_to_row_chunks(x_ref, bm):
            x_a, x_b = _unpack_bf16_pair(chunk)
            parts += [x_a.astype(jnp.bfloat16), x_b.astype(jnp.bfloat16)]
        xb = jnp.concatenate(parts, axis=1)
        gate = jnp.dot(xb, wg_b[...], preferred_element_type=jnp.float32)
        up = jnp.dot(xb, wu_b[...], preferred_element_type=jnp.float32)
        hid = (gate * _sigmoid(gate) * up).astype(jnp.bfloat16)
        y = jnp.dot(hid, wd_b[...], preferred_element_type=jnp.float32)
        half = y.shape[1] // 2
        _rows_to_tiles(y_ref, _pack_bf16_pair(y[:, :half], y[:, half:]))

        @pl.when(is_last & has_next)
        def _():
            land(after)

    @pl.when((i > 0) & (blk >= nact))
    def _():
        y_ref[...] = jnp.zeros_like(y_ref)


def _experts(block_expert, sched, xb, w_gate, w_up, w_down, bm):
    p_rows = xb.shape[0] // SUBLANES
    d = w_gate.shape[1]
    de = w_gate.shape[-1]

    def x_map(i, be, sc):
        return (jnp.clip(i - 1, 0, sc[0] - 1), 0)

    grid_spec = pltpu.PrefetchScalarGridSpec(
        num_scalar_prefetch=2,
        grid=(p_rows // bm + 1,),
        in_specs=[pl.BlockSpec((bm * SUBLANES, LANES), x_map),
                  pl.BlockSpec(memory_space=pl.ANY),
                  pl.BlockSpec(memory_space=pl.ANY),
                  pl.BlockSpec(memory_space=pl.ANY)],
        out_specs=pl.BlockSpec((bm * SUBLANES, LANES), lambda i, be, sc: (jnp.maximum(i - 1, 0), 0)),
        scratch_shapes=[pltpu.VMEM((d, de), jnp.float32), pltpu.VMEM((d, de), jnp.float32),
                        pltpu.VMEM((de, d), jnp.float32),
                        pltpu.VMEM((d, de), jnp.bfloat16), pltpu.VMEM((d, de), jnp.bfloat16),
                        pltpu.VMEM((de, d), jnp.bfloat16),
                        pltpu.SemaphoreType.DMA(())],
    )
    return pl.pallas_call(
        _expert_kernel,
        grid_spec=grid_spec,
        out_shape=jax.ShapeDtypeStruct(xb.shape, jnp.uint32),
        compiler_params=_params("arbitrary"),
        name="moe_experts",
    )(block_expert, sched, xb, w_gate, w_up, w_down)


def _combine_kernel(dest_ref, dest_next_ref, h_ref, wts_ref, g_ref, y_ref, o_ref, ybuf, sem):
    i = pl.program_id(0)
    tm = h_ref.shape[0]
    slot = i % 2

    def gather(idx_ref, which):
        for r in range(tm):
            for k in range(2):
                src = pl.multiple_of(idx_ref[0, k * tm + r] * SUBLANES, SUBLANES)
                pltpu.make_async_copy(y_ref.at[pl.ds(src, SUBLANES), :],
                                      ybuf.at[which, k, pl.ds(r * SUBLANES, SUBLANES), :], sem.at[which]).start()

    def wait_tile(which):
        for k in range(2):
            pltpu.make_async_copy(y_ref.at[pl.ds(0, tm * SUBLANES), :], ybuf.at[which, k], sem.at[which]).wait()

    @pl.when(i == 0)
    def _():
        gather(dest_ref, slot)

    wait_tile(slot)
    gather(dest_next_ref, 1 - slot)

    w = wts_ref[...]
    half = h_ref.shape[1] // 2
    lo_parts, hi_parts = [], []
    for c, (c0, c1) in enumerate(zip(_tiles_to_row_chunks(ybuf.at[slot, 0], tm),
                                      _tiles_to_row_chunks(ybuf.at[slot, 1], tm))):
        a0, b0 = _unpack_bf16_pair(c0)
        a1, b1 = _unpack_bf16_pair(c1)
        lo_parts.append(h_ref[:, c * LANES:(c + 1) * LANES] + w[:, 0:1] * a0 + w[:, 1:2] * a1)
        hi_parts.append(h_ref[:, half + c * LANES:half + (c + 1) * LANES] + w[:, 0:1] * b0 + w[:, 1:2] * b1)
    h3 = jnp.concatenate(lo_parts + hi_parts, axis=1)
    o_ref[...] = h3 * _rms_scale(h3, NORM_EPS) * g_ref[...]

    @pl.when(i == pl.num_programs(0) - 1)
    def _():
        wait_tile(1 - slot)


def _combine(dest3, h2, wts, g_final, y, tm):
    n_tok, d = h2.shape
    last = n_tok // tm - 1
    return pl.pallas_call(
        _combine_kernel,
        grid=(n_tok // tm,),
        in_specs=[pl.BlockSpec((None, 1, 2 * tm), lambda i: (i, 0, 0), memory_space=pltpu.SMEM),
                  pl.BlockSpec((None, 1, 2 * tm), lambda i: (jnp.minimum(i + 1, last), 0, 0),
                               memory_space=pltpu.SMEM),
                  pl.BlockSpec((tm, d), lambda i: (i, 0)),
                  pl.BlockSpec((tm, 2), lambda i: (i, 0)),
                  _resident(g_final.shape),
                  pl.BlockSpec(memory_space=pl.ANY)],
        out_specs=pl.BlockSpec((tm, d), lambda i: (i, 0)),
        out_shape=jax.ShapeDtypeStruct((n_tok, d), jnp.float32),
        scratch_shapes=[pltpu.VMEM((2, 2, tm * SUBLANES, LANES), jnp.uint32), pltpu.SemaphoreType.DMA((2,))],
        compiler_params=_params("arbitrary"),
        name="moe_combine",
    )(dest3, dest3, h2, wts, g_final, y)


def _transpose_w_in(w_in):
    return jnp.swapaxes(w_in, 0, 1).astype(jnp.bfloat16)


def _split_w_uq(w_uq):
    half = MLA_ROPE_DIM // 2
    w = w_uq.reshape(MLA_Q_RANK, MLA_HEADS, MLA_QK_DIM).transpose(1, 0, 2)
    pe = w[:, :, MLA_NOPE_DIM:]
    pe_swapped = jnp.concatenate([pe[:, :, half:], pe[:, :, :half]], axis=2)
    return jnp.concatenate([w, pe_swapped], axis=2).astype(jnp.bfloat16)


def _split_w_ukv(w_ukv):
    w = w_ukv.reshape(MLA_KV_RANK, MLA_HEADS, MLA_NOPE_DIM + MLA_V_DIM)
    wuk = w[:, :, :MLA_NOPE_DIM].transpose(1, 0, 2).astype(jnp.bfloat16)
    wuv = w[:, :, MLA_NOPE_DIM:].reshape(MLA_KV_RANK, MLA_HEADS * MLA_V_DIM).astype(jnp.bfloat16)
    return wuk, wuv


def kernel(x, mem, positions, attn_norm_g, w_in, diff_lambda_q1, diff_lambda_k1, diff_lambda_q2, diff_lambda_k2, diff_subln_g, w_o_diff, mla_q_norm_g, w_uq, mla_kv_norm_g, w_ukv, w_o_mla, w_out, cross_norm_g, mem_norm_g, w_cq, w_ckv, w_co, ffn_norm_g, w_router_group, b_router_group, w_router_expert, b_router_expert, w_expert_gate, w_expert_up, w_expert_down, final_norm_g):
    batch, seq, d = x.shape
    assert d == D_MODEL and w_in.shape[0] == 1, "single-layer kernel"
    n_tok = batch * seq
    bf = jnp.bfloat16
    x2 = x.reshape(n_tok, d)

    tm_proj = min(1024, seq)
    tm_row = min(256, seq)
    tm_moe = min(MOE_ROWS_PER_BLOCK, seq)
    tm_cross = min(512, seq)
    tq = min(2048, seq)
    rg_diff = 128
    rg_mla = 256

    cos_t, sin_t = _rope_tables(positions, n_tok, tm_proj)

    g_attn = attn_norm_g[0].reshape(1, d)
    qkv, latent, gates = _inproj(x2, g_attn, _transpose_w_in(w_in[0]), cos_t, sin_t, tm_proj)

    o_a = _diff_attn(qkv.reshape(batch, seq, QKV_COLS),
                     diff_lambda_q1[0].reshape(1, -1), diff_lambda_k1[0].reshape(1, -1),
                     diff_lambda_q2[0].reshape(1, -1), diff_lambda_k2[0].reshape(1, -1),
                     diff_subln_g[0].reshape(1, -1), tq, rg_diff)

    wuk, wuv = _split_w_ukv(w_ukv[0])
    q_cat, k_cat, v_mla = _mla_proj(latent, mla_q_norm_g[0].reshape(1, -1), mla_kv_norm_g[0].reshape(1, -1),
                                    _split_w_uq(w_uq[0]), wuk, wuv, cos_t, sin_t, batch, seq, min(512, seq))
    o_b = _mla_attn(q_cat, k_cat, v_mla.reshape(batch, seq, MLA_HEADS * MLA_V_DIM), tq, rg_mla)

    h1 = _merge_out(o_a.reshape(n_tok, -1), o_b.reshape(n_tok, -1), gates, x2,
                    w_o_diff[0].astype(bf), w_o_mla[0].astype(bf), w_out[0].astype(bf), tm_cross)

    kv_mem = _mem_kv(mem, mem_norm_g[0].reshape(1, d), w_ckv[0].astype(bf))
    n_router = N_GROUPS + N_EXPERTS
    w_r = jnp.concatenate([w_router_group[0].astype(jnp.float32), w_router_expert[0].astype(jnp.float32),
                           jnp.zeros((d, LANES - n_router), jnp.float32)], axis=1)
    w_r_hi = w_r.astype(bf)
    w_r_lo = (w_r - w_r_hi.astype(jnp.float32)).astype(bf)
    w_r = jnp.concatenate([w_r_hi, w_r_lo], axis=1)
    b_r = jnp.concatenate([b_router_group[0].astype(jnp.float32), b_router_expert[0].astype(jnp.float32),
                           jnp.zeros((LANES - n_router,), jnp.float32)]).reshape(1, LANES)
    g_ffn = ffn_norm_g[0].reshape(1, d)
    h2, eid, rank, wts, cnt = _cross_router(h1, cross_norm_g[0].reshape(1, d), w_cq[0].astype(bf), kv_mem,
                                            w_co[0].astype(bf), g_ffn, w_r, b_r, seq, tm_cross)

    bm = MOE_ROWS_PER_BLOCK
    assert bm % tm_row == 0 and tm_moe == bm
    counts = cnt[0, ROUTER_EXPERT_LANE0:ROUTER_EXPERT_LANE0 + N_EXPERTS].astype(jnp.int32)
    padded = ((counts + bm - 1) // bm) * bm
    padded_end = jnp.cumsum(padded)
    padded_off = padded_end - padded
    seg_start = jnp.sum(jnp.where(eid[..., None] == jnp.arange(N_EXPERTS, dtype=jnp.int32), padded_off, 0), axis=-1)
    dest = seg_start + rank
    p_rows = ((2 * n_tok + bm - 1) // bm) * bm + N_EXPERTS * bm
    n_blocks = p_rows // bm
    n_active = (padded_end[-1] // bm).astype(jnp.int32)
    blk = jnp.minimum(jnp.arange(n_blocks, dtype=jnp.int32), n_active - 1)
    block_expert = jnp.sum((padded_end[None, :] <= (blk * bm)[:, None]).astype(jnp.int32), axis=1)
    block_expert = jnp.minimum(block_expert, N_EXPERTS - 1)
    def tile_slots(tm):
        return dest.reshape(-1, 2, tm_cross // tm, tm).transpose(0, 2, 1, 3).reshape(n_tok // tm, 1, 2 * tm)
    unused = n_active + jnp.arange(N_EXPERTS, dtype=jnp.int32)
    ztail = jnp.stack([jnp.concatenate([jnp.maximum(padded_end - bm, 0), jnp.minimum(unused, n_blocks - 1) * bm]),
                       jnp.concatenate([padded > 0, unused < n_blocks]).astype(jnp.int32)]).astype(jnp.int32)
    sched = jnp.concatenate([n_active.reshape(1), padded // bm, padded_off // bm]).astype(jnp.int32)

    xb = _dispatch(tile_slots(tm_row), ztail, h2, g_ffn, p_rows, tm_row)
    y = _experts(block_expert, sched, xb, w_expert_gate[0], w_expert_up[0], w_expert_down[0], bm)
    out = _combine(tile_slots(tm_moe), h2, wts, final_norm_g.reshape(1, d), y, tm_moe)
    return out.reshape(batch, seq, d)
```

```python
import functools
import math

import jax
import jax.numpy as jnp
from jax import lax
from jax.experimental import pallas as pl
from jax.experimental.pallas import tpu as pltpu

D_MODEL = 2048
ROPE_THETA = 500000.0
NORM_EPS = 1e-6

DIFF_HEADS = 8
DIFF_HEAD_DIM = 64
DIFF_V_DIM = 2 * DIFF_HEAD_DIM
DIFF_ROT = DIFF_HEAD_DIM // 4
DIFF_SUBLN_EPS = 1e-5
DIFF_LAMBDA_INIT = 0.8 - 0.6 * math.exp(-0.3 * 0)

MLA_HEADS = 8
MLA_Q_RANK = 512
MLA_KV_RANK = 256
MLA_NOPE_DIM = 128
MLA_ROPE_DIM = 64
MLA_V_DIM = 128
MLA_QK_DIM = MLA_NOPE_DIM + MLA_ROPE_DIM

CROSS_HEADS = 4
CROSS_HEAD_DIM = 128

N_GROUPS = 4
EXPERTS_PER_GROUP = 8
N_EXPERTS = N_GROUPS * EXPERTS_PER_GROUP
D_EXPERT = 512

LANES = 128
LOG2E = 1.4426950408889634
VMEM_LIMIT_BYTES = 56 * 1024 * 1024

ROUTER_EXPERT_LANE0 = N_GROUPS

QKV_COLS = 3 * DIFF_HEADS * DIFF_V_DIM
LATENT_COLS = 1024
GATE_COLS = 2 * D_MODEL
KPE_COL0 = MLA_Q_RANK + MLA_KV_RANK

MOE_ROWS_PER_BLOCK = 512
SUBLANES = 8


def _params(*semantics):
    return pltpu.CompilerParams(dimension_semantics=semantics, vmem_limit_bytes=VMEM_LIMIT_BYTES)


def _resident(shape):
    zeros = (0,) * len(shape)
    return pl.BlockSpec(shape, lambda *_: zeros, pipeline_mode=pl.Buffered(1))


def _rms_scale(xf, eps):
    return lax.rsqrt(jnp.mean(xf * xf, axis=-1, keepdims=True) + eps)


def _sigmoid(x):
    return 0.5 * jnp.tanh(0.5 * x) + 0.5


def _pack_bf16_pair(a, b):
    hi = lax.bitcast_convert_type(a.astype(jnp.bfloat16).astype(jnp.float32), jnp.uint32)
    lo = lax.bitcast_convert_type(b.astype(jnp.bfloat16).astype(jnp.float32), jnp.uint32)
    return hi | (lo >> 16)


def _unpack_bf16_pair(w):
    a = lax.bitcast_convert_type(w & jnp.uint32(0xFFFF0000), jnp.float32)
    b = lax.bitcast_convert_type(w << 16, jnp.float32)
    return a, b


def _rows_to_tiles(ref_view, packed):
    rows = packed.shape[0]
    for c in range(SUBLANES):
        ref_view[pl.ds(c, rows, stride=SUBLANES), :] = packed[:, c * LANES:(c + 1) * LANES]


def _tiles_to_row_chunks(ref_view, rows):
    return [ref_view[pl.ds(c, rows, stride=SUBLANES), :] for c in range(SUBLANES)]


def _lane_iota(shape):
    return lax.broadcasted_iota(jnp.int32, shape, len(shape) - 1)


def _trig_kernel(pos_ref, invf_ref, cos_ref, sin_ref):
    ang = pos_ref[...].astype(jnp.float32) * invf_ref[...]
    cos_ref[...] = jnp.cos(ang)
    sin_ref[...] = jnp.sin(ang)


def _rope_tables(positions, n_tok, tm):
    half_m = MLA_ROPE_DIM // 2
    half_d = DIFF_ROT // 2
    inv_m = jnp.float32(ROPE_THETA) ** (-jnp.arange(half_m, dtype=jnp.float32) * 2.0 / MLA_ROPE_DIM)
    inv_d = jnp.float32(ROPE_THETA) ** (-jnp.arange(half_d, dtype=jnp.float32) * 2.0 / DIFF_ROT)
    invf = jnp.concatenate([inv_m, inv_m, inv_d, inv_d,
                            jnp.zeros((DIFF_HEAD_DIM - DIFF_ROT,), jnp.float32)]).reshape(1, LANES)
    pos = positions.reshape(n_tok, 1)
    return pl.pallas_call(
        _trig_kernel,
        grid=(n_tok // tm,),
        in_specs=[pl.BlockSpec((tm, 1), lambda i: (i, 0)), _resident((1, LANES))],
        out_specs=[pl.BlockSpec((tm, LANES), lambda i: (i, 0))] * 2,
        out_shape=[jax.ShapeDtypeStruct((n_tok, LANES), jnp.float32)] * 2,
        compiler_params=_params("parallel"),
        name="rope_tables",
    )(pos, invf)


def _diff_rope_coeffs(cos_t, sin_t):
    lane = _lane_iota(cos_t.shape)
    upper = lane >= DIFF_HEAD_DIM
    cos_d = jnp.where(upper, cos_t, pltpu.roll(cos_t, DIFF_HEAD_DIM, 1))
    sin_d = jnp.where(upper, sin_t, pltpu.roll(sin_t, DIFF_HEAD_DIM, 1))
    in_head = lane % DIFF_HEAD_DIM
    half = DIFF_ROT // 2
    s_next = jnp.where(in_head < half, -sin_d, 0.0)
    s_prev = jnp.where((in_head >= half) & (in_head < DIFF_ROT), sin_d, 0.0)
    return cos_d, s_next, s_prev


def _mla_rope(pair, cos_t, sin_t):
    lane = _lane_iota(pair.shape)
    sin_signed = jnp.where(lane < MLA_ROPE_DIM // 2, -sin_t, sin_t)
    return pair * cos_t + pltpu.roll(pair, MLA_ROPE_DIM, 1) * sin_signed


INPROJ_TN = 1024
INPROJ_PIECE = 256
Q_TILES = DIFF_HEADS * DIFF_V_DIM // INPROJ_TN
ROPE_TILES = 2 * Q_TILES
QKV_TILES = QKV_COLS // INPROJ_TN
LATENT_TILES = LATENT_COLS // INPROJ_TN
GATE_TILES = GATE_COLS // INPROJ_TN
INPROJ_TILES = QKV_TILES + LATENT_TILES + GATE_TILES
GATE_ROW0 = QKV_COLS + KPE_COL0 + MLA_ROPE_DIM


def _inproj_kernel(x_ref, g_ref, w_ref, cos_ref, sin_ref, qkv_ref, lat_ref, gate_ref, xn_ref):
    j = pl.program_id(1)

    @pl.when(j == 0)
    def _():
        xf = x_ref[...]
        xn_ref[...] = (xf * _rms_scale(xf, NORM_EPS) * g_ref[...]).astype(jnp.bfloat16)

    def pieces(epilogue):
        for c in range(INPROJ_TN // INPROJ_PIECE):
            cols = slice(c * INPROJ_PIECE, (c + 1) * INPROJ_PIECE)
            acc = lax.dot_general(xn_ref[...], w_ref[cols, :], (((1,), (1,)), ((), ())),
                                  preferred_element_type=jnp.float32)
            epilogue(acc, cols)

    @pl.when(j < ROPE_TILES)
    def _():
        cos_d, s_next, s_prev = _diff_rope_coeffs(cos_ref[...], sin_ref[...])
        qscale = jnp.where(j < Q_TILES, DIFF_HEAD_DIM ** -0.5 * LOG2E, 1.0).astype(jnp.float32)

        def rope(acc, cols):
            for c in range(INPROJ_PIECE // LANES):
                xc = acc[:, c * LANES:(c + 1) * LANES]
                rot = (xc * cos_d + pltpu.roll(xc, LANES - DIFF_ROT // 2, 1) * s_next
                       + pltpu.roll(xc, DIFF_ROT // 2, 1) * s_prev)
                lo = cols.start + c * LANES
                qkv_ref[:, lo:lo + LANES] = (rot * qscale).astype(qkv_ref.dtype)

        pieces(rope)

    @pl.when((j >= ROPE_TILES) & (j < QKV_TILES))
    def _():
        def value(acc, cols):
            qkv_ref[:, cols] = acc.astype(qkv_ref.dtype)

        pieces(value)

    @pl.when((j >= QKV_TILES) & (j < QKV_TILES + LATENT_TILES))
    def _():
        def latent(acc, cols):
            if cols.start <= KPE_COL0 < cols.stop:
                c0 = KPE_COL0 - cols.start
                v = acc[:, c0:c0 + LANES]
                lane = _lane_iota(v.shape)
                half = MLA_ROPE_DIM // 2
                swapped = jnp.where(lane < MLA_ROPE_DIM + half, pltpu.roll(v, half, 1),
                                    pltpu.roll(v, MLA_ROPE_DIM + half, 1))
                parts = [acc[:, :c0], jnp.where(lane < MLA_ROPE_DIM, v, swapped), acc[:, c0 + LANES:]]
                acc = jnp.concatenate([p for p in parts if p.shape[1]], axis=1)
            lat_ref[:, cols] = acc

        pieces(latent)

    @pl.when(j >= QKV_TILES + LATENT_TILES)
    def _():
        def gate(acc, cols):
            gate_ref[:, cols] = _sigmoid(acc).astype(gate_ref.dtype)

        pieces(gate)


def _inproj(x2, g, w_all, cos_t, sin_t, tm):
    n_tok, d = x2.shape
    tn = INPROJ_TN
    lat0 = QKV_TILES
    gate0 = QKV_TILES + LATENT_TILES
    return pl.pallas_call(
        _inproj_kernel,
        grid=(n_tok // tm, INPROJ_TILES),
        in_specs=[pl.BlockSpec((tm, d), lambda i, j: (i, 0)),
                  _resident((1, d)),
                  pl.BlockSpec((pl.Element(tn), pl.Element(d)),
                               lambda i, j: (pl.multiple_of(jnp.where(j < gate0, j * tn, GATE_ROW0 + (j - gate0) * tn), 16), 0)),
                  pl.BlockSpec((tm, LANES), lambda i, j: (i, 0)),
                  pl.BlockSpec((tm, LANES), lambda i, j: (i, 0))],
        out_specs=[pl.BlockSpec((tm, tn), lambda i, j: (i, jnp.clip(j, 0, QKV_TILES - 1))),
                   pl.BlockSpec((tm, tn), lambda i, j: (i, jnp.clip(j - lat0, 0, LATENT_TILES - 1))),
                   pl.BlockSpec((tm, tn), lambda i, j: (i, jnp.clip(j - gate0, 0, GATE_TILES - 1)))],
        out_shape=[jax.ShapeDtypeStruct((n_tok, QKV_COLS), jnp.bfloat16),
                   jax.ShapeDtypeStruct((n_tok, LATENT_COLS), jnp.float32),
                   jax.ShapeDtypeStruct((n_tok, GATE_COLS), jnp.bfloat16)],
        scratch_shapes=[pltpu.VMEM((tm, d), jnp.bfloat16)],
        compiler_params=_params("parallel", "arbitrary"),
        name="inproj",
    )(x2, g, w_all, cos_t, sin_t)


def _mla_proj_kernel(c_ref, gq_ref, gkv_ref, wuq_ref, wuk_ref, wuv_ref, cos_ref, sin_ref,
                     q_ref, k_ref, v_ref):
    cos_t = cos_ref[...]
    sin_t = sin_ref[...]
    cq = c_ref[:, :MLA_Q_RANK]
    cqn = (cq * _rms_scale(cq, NORM_EPS) * gq_ref[...]).astype(jnp.bfloat16)
    ckv = c_ref[:, MLA_Q_RANK:KPE_COL0]
    ckvn = (ckv * _rms_scale(ckv, NORM_EPS) * gkv_ref[...]).astype(jnp.bfloat16)
    kpe = _mla_rope(c_ref[:, KPE_COL0:KPE_COL0 + LANES], cos_t, sin_t)[:, :MLA_ROPE_DIM].astype(k_ref.dtype)
    qscale = MLA_QK_DIM ** -0.5 * LOG2E
    for h in range(MLA_HEADS):
        r = jnp.dot(cqn, wuq_ref[h], preferred_element_type=jnp.float32)
        q_ref[0, h, :, :MLA_NOPE_DIM] = (r[:, :MLA_NOPE_DIM] * qscale).astype(q_ref.dtype)
        qpe = _mla_rope(r[:, MLA_NOPE_DIM:], cos_t, sin_t)[:, :MLA_ROPE_DIM]
        q_ref[0, h, :, MLA_NOPE_DIM:] = (qpe * qscale).astype(q_ref.dtype)
        kn = jnp.dot(ckvn, wuk_ref[h], preferred_element_type=jnp.float32)
        k_ref[0, h, :, :MLA_NOPE_DIM] = kn.astype(k_ref.dtype)
        k_ref[0, h, :, MLA_NOPE_DIM:] = kpe
    v_ref[...] = jnp.dot(ckvn, wuv_ref[...], preferred_element_type=jnp.float32).astype(v_ref.dtype)


def _mla_proj(latent, gq, gkv, wuq, wuk, wuv, cos_t, sin_t, batch, seq, tm):
    n_tok = latent.shape[0]
    per_b = seq // tm
    head_spec = pl.BlockSpec((1, MLA_HEADS, tm, MLA_QK_DIM), lambda i: (i // per_b, 0, i % per_b, 0))
    head_shape = jax.ShapeDtypeStruct((batch, MLA_HEADS, seq, MLA_QK_DIM), jnp.bfloat16)
    return pl.pallas_call(
        _mla_proj_kernel,
        grid=(n_tok // tm,),
        in_specs=[pl.BlockSpec((tm, LATENT_COLS), lambda i: (i, 0)),
                  _resident(gq.shape), _resident(gkv.shape),
                  _resident(wuq.shape), _resident(wuk.shape), _resident(wuv.shape),
                  pl.BlockSpec((tm, LANES), lambda i: (i, 0)),
                  pl.BlockSpec((tm, LANES), lambda i: (i, 0))],
        out_specs=[head_spec, head_spec,
                   pl.BlockSpec((tm, MLA_HEADS * MLA_V_DIM), lambda i: (i, 0))],
        out_shape=[head_shape, head_shape,
                   jax.ShapeDtypeStruct((n_tok, MLA_HEADS * MLA_V_DIM), jnp.bfloat16)],
        compiler_params=_params("parallel"),
        name="mla_proj",
    )(latent, gq, gkv, wuq, wuk, wuv, cos_t, sin_t)


def _with_ones(v):
    return jnp.concatenate([v, jnp.ones((v.shape[0], LANES), v.dtype)], axis=-1)


def _softmax_pv(s, v_ones):
    m = jnp.max(s, axis=-1, keepdims=True)
    p = jnp.exp2(s - m).astype(v_ones.dtype)
    pv = jnp.dot(p, v_ones, preferred_element_type=jnp.float32)
    dv = v_ones.shape[1] - LANES
    return pv[:, :dv] / pv[:, dv:]


ATTN_HEADS_PER_STEP = 2


def _diff_attn_kernel(q_ref, k_ref, v_ref, lq1_ref, lk1_ref, lq2_ref, lk2_ref, g_ref, o_ref, v1_ref, *, rg):
    heads = q_ref.shape[2] // LANES

    @pl.when(pl.program_id(2) == 0)
    def _():
        for h in range(heads):
            v1_ref[h] = _with_ones(v_ref[0, :, h * DIFF_V_DIM:(h + 1) * DIFF_V_DIM])

    lam = (jnp.exp(jnp.sum(lq1_ref[...] * lk1_ref[...], axis=-1, keepdims=True))
           - jnp.exp(jnp.sum(lq2_ref[...] * lk2_ref[...], axis=-1, keepdims=True))
           + DIFF_LAMBDA_INIT)
    lane = _lane_iota((rg, LANES))
    for g in range(q_ref.shape[1] // rg):
        for h in range(heads):
            cols = slice(h * LANES, (h + 1) * LANES)
            q = q_ref[0, g * rg:(g + 1) * rg, cols]
            k = k_ref[0, :, cols]
            zero = jnp.zeros_like(q)
            q12 = jnp.concatenate([jnp.where(lane < DIFF_HEAD_DIM, q, zero),
                                   jnp.where(lane >= DIFF_HEAD_DIM, q, zero)], axis=0)
            s = lax.dot_general(q12, k, (((1,), (1,)), ((), ())), preferred_element_type=jnp.float32)
            a = _softmax_pv(s, v1_ref[h])
            o = a[:rg] - lam * a[rg:]
            o = o * _rms_scale(o, DIFF_SUBLN_EPS) * g_ref[...] * (1.0 - DIFF_LAMBDA_INIT)
            o_ref[0, g * rg:(g + 1) * rg, cols] = o.astype(o_ref.dtype)


def _diff_attn(qkv3, lq1, lk1, lq2, lk2, subln_g, tq, rg):
    batch, seq, _ = qkv3.shape
    hp = ATTN_HEADS_PER_STEP
    steps = DIFF_HEADS // hp
    width = hp * DIFF_V_DIM
    return pl.pallas_call(
        functools.partial(_diff_attn_kernel, rg=rg),
        grid=(batch, steps, seq // tq),
        in_specs=[pl.BlockSpec((1, tq, width), lambda b, hh, i: (b, i, hh)),
                  pl.BlockSpec((1, seq, width), lambda b, hh, i: (b, 0, steps + hh)),
                  pl.BlockSpec((1, seq, width), lambda b, hh, i: (b, 0, 2 * steps + hh)),
                  _resident(lq1.shape), _resident(lk1.shape), _resident(lq2.shape), _resident(lk2.shape),
                  _resident(subln_g.shape)],
        out_specs=pl.BlockSpec((1, tq, width), lambda b, hh, i: (b, i, hh)),
        out_shape=jax.ShapeDtypeStruct((batch, seq, DIFF_HEADS * DIFF_V_DIM), jnp.bfloat16),
        scratch_shapes=[pltpu.VMEM((hp, seq, DIFF_V_DIM + LANES), jnp.bfloat16)],
        compiler_params=_params("parallel", "parallel", "arbitrary"),
        name="diff_attn",
    )(qkv3, qkv3, qkv3, lq1, lk1, lq2, lk2, subln_g)


def _mla_attn_kernel(q_ref, k_ref, v_ref, o_ref, v1_ref, *, rg):
    heads = q_ref.shape[1]

    @pl.when(pl.program_id(2) == 0)
    def _():
        for h in range(heads):
            v1_ref[h] = _with_ones(v_ref[0, :, h * MLA_V_DIM:(h + 1) * MLA_V_DIM])

    for g in range(q_ref.shape[2] // rg):
        for h in range(heads):
            s = lax.dot_general(q_ref[0, h, g * rg:(g + 1) * rg], k_ref[0, h], (((1,), (1,)), ((), ())),
                                preferred_element_type=jnp.float32)
            o_ref[0, g * rg:(g + 1) * rg, h * MLA_V_DIM:(h + 1) * MLA_V_DIM] = (
                _softmax_pv(s, v1_ref[h]).astype(o_ref.dtype))


def _mla_attn(q_cat, k_cat, v3, tq, rg):
    batch, heads, seq, dqk = q_cat.shape
    hp = ATTN_HEADS_PER_STEP
    return pl.pallas_call(
        functools.partial(_mla_attn_kernel, rg=rg),
        grid=(batch, heads // hp, seq // tq),
        in_specs=[pl.BlockSpec((1, hp, tq, dqk), lambda b, h, i: (b, h, i, 0)),
                  pl.BlockSpec((1, hp, seq, dqk), lambda b, h, i: (b, h, 0, 0)),
                  pl.BlockSpec((1, seq, hp * MLA_V_DIM), lambda b, h, i: (b, 0, h))],
        out_specs=pl.BlockSpec((1, tq, hp * MLA_V_DIM), lambda b, h, i: (b, i, h)),
        out_shape=jax.ShapeDtypeStruct((batch, seq, heads * MLA_V_DIM), jnp.bfloat16),
        scratch_shapes=[pltpu.VMEM((hp, seq, MLA_V_DIM + LANES), jnp.bfloat16)],
        compiler_params=_params("parallel", "parallel", "arbitrary"),
        name="mla_attn",
    )(q_cat, k_cat, v3)


def _merge_out_kernel(oa_ref, ob_ref, sga_ref, sgb_ref, x_ref, woa_ref, wob_ref, wout_ref, h_ref):
    ya = jnp.dot(oa_ref[...], woa_ref[...], preferred_element_type=jnp.float32)
    yb = jnp.dot(ob_ref[...], wob_ref[...], preferred_element_type=jnp.float32)
    merged = sga_ref[...].astype(jnp.float32) * ya + sgb_ref[...].astype(jnp.float32) * yb
    h_ref[...] = x_ref[...] + jnp.dot(merged.astype(jnp.bfloat16), wout_ref[...],
                                       preferred_element_type=jnp.float32)


def _merge_out(o_a, o_b, gates, x2, w_oa, w_ob, w_out, tm):
    n_tok, d = x2.shape
    return pl.pallas_call(
        _merge_out_kernel,
        grid=(n_tok // tm,),
        in_specs=[pl.BlockSpec((tm, o_a.shape[1]), lambda i: (i, 0)),
                  pl.BlockSpec((tm, o_b.shape[1]), lambda i: (i, 0)),
                  pl.BlockSpec((tm, d), lambda i: (i, 0)),
                  pl.BlockSpec((tm, d), lambda i: (i, 1)),
                  pl.BlockSpec((tm, d), lambda i: (i, 0)),
                  _resident(w_oa.shape), _resident(w_ob.shape), _resident(w_out.shape)],
        out_specs=pl.BlockSpec((tm, d), lambda i: (i, 0)),
        out_shape=jax.ShapeDtypeStruct((n_tok, d), jnp.float32),
        compiler_params=_params("parallel"),
        name="merge_out",
    )(o_a, o_b, gates, gates, x2, w_oa, w_ob, w_out)


def _mem_kv_kernel(mem_ref, g_ref, w_ref, kv_ref):
    mf = mem_ref[0]
    mn = (mf * _rms_scale(mf, NORM_EPS) * g_ref[...]).astype(jnp.bfloat16)
    kv_ref[0] = jnp.dot(mn, w_ref[...], preferred_element_type=jnp.float32).astype(kv_ref.dtype)


def _mem_kv(mem, g, w_ckv):
    batch, m, d = mem.shape
    return pl.pallas_call(
        _mem_kv_kernel,
        grid=(batch,),
        in_specs=[pl.BlockSpec((1, m, d), lambda b: (b, 0, 0)), _resident(g.shape), _resident(w_ckv.shape)],
        out_specs=pl.BlockSpec((1, m, w_ckv.shape[1]), lambda b: (b, 0, 0)),
        out_shape=jax.ShapeDtypeStruct((batch, m, w_ckv.shape[1]), jnp.bfloat16),
        compiler_params=_params("parallel"),
        name="mem_kv",
    )(mem, g, w_ckv)


def _cross_router_kernel(h_ref, gc_ref, wcq_ref, kv_ref, wco_ref, gf_ref, wr_ref, br_ref,
                         h2_ref, eid_ref, rank_ref, wts_ref, cnt_ref, carry_ref):
    i = pl.program_id(0)

    @pl.when(i == 0)
    def _():
        carry_ref[...] = jnp.zeros_like(carry_ref)

    h1 = h_ref[...]
    tm = h1.shape[0]
    hn = (h1 * _rms_scale(h1, NORM_EPS) * gc_ref[...]).astype(jnp.bfloat16)
    q = jnp.dot(hn, wcq_ref[...], preferred_element_type=jnp.float32) * (CROSS_HEAD_DIM ** -0.5 * LOG2E)
    q = q.astype(jnp.bfloat16)
    kv_cols = CROSS_HEADS * CROSS_HEAD_DIM
    heads = []
    for hd in range(CROSS_HEADS):
        lo = hd * CROSS_HEAD_DIM
        kh = kv_ref[0, :, lo:lo + CROSS_HEAD_DIM]
        vh = kv_ref[0, :, kv_cols + lo:kv_cols + lo + CROSS_HEAD_DIM]
        s = lax.dot_general(q[:, lo:lo + CROSS_HEAD_DIM], kh, (((1,), (1,)), ((), ())),
                            preferred_element_type=jnp.float32)
        heads.append(_softmax_pv(s, _with_ones(vh)).astype(jnp.bfloat16))
    o = jnp.concatenate(heads, axis=-1)
    h2 = h1 + jnp.dot(o, wco_ref[...], preferred_element_type=jnp.float32)
    h2_ref[...] = h2

    t = h2 * _rms_scale(h2, NORM_EPS) * gf_ref[...]
    t_hi = t.astype(jnp.bfloat16)
    t_lo = (t - t_hi.astype(jnp.float32)).astype(jnp.bfloat16)
    hi = jnp.dot(t_hi, wr_ref[...], preferred_element_type=jnp.float32)
    lo = jnp.dot(t_lo, wr_ref[:, :LANES], preferred_element_type=jnp.float32)
    logits = hi[:, :LANES] + (hi[:, LANES:] + lo) + br_ref[...]
    lane = _lane_iota(logits.shape)
    neg = jnp.float32(-jnp.inf)
    big = jnp.int32(2 * LANES)
    is_group = lane < N_GROUPS
    lg = jnp.where(is_group, logits, neg)
    mg = jnp.max(lg, axis=-1, keepdims=True)
    g_idx = jnp.min(jnp.where(is_group & (logits == mg), lane, big), axis=-1, keepdims=True)
    g_p = 1.0 / jnp.sum(jnp.exp(lg - mg), axis=-1, keepdims=True)
    lo_lane = ROUTER_EXPERT_LANE0 + EXPERTS_PER_GROUP * g_idx
    in_grp = (lane >= lo_lane) & (lane < lo_lane + EXPERTS_PER_GROUP)
    l1 = jnp.max(jnp.where(in_grp, logits, neg), axis=-1, keepdims=True)
    i1 = jnp.min(jnp.where(in_grp & (logits == l1), lane, big), axis=-1, keepdims=True)
    rest = in_grp & (lane != i1)
    l2 = jnp.max(jnp.where(rest, logits, neg), axis=-1, keepdims=True)
    i2 = jnp.min(jnp.where(rest & (logits == l2), lane, big), axis=-1, keepdims=True)
    d = jnp.exp(l2 - l1)
    w1 = g_p / (1.0 + d)
    w2 = w1 * d

    oh1 = lane == i1
    oh2 = lane == i2
    cnt = (oh1 | oh2).astype(jnp.bfloat16)
    row = lax.broadcasted_iota(jnp.int32, (tm, tm), 0)
    col = lax.broadcasted_iota(jnp.int32, (tm, tm), 1)
    before = (col < row).astype(jnp.bfloat16)
    slot = jnp.dot(before, cnt, preferred_element_type=jnp.float32) + carry_ref[...]
    r1 = jnp.sum(jnp.where(oh1, slot, 0.0), axis=-1, keepdims=True)
    r2 = jnp.sum(jnp.where(oh2, slot, 0.0), axis=-1, keepdims=True)
    carry_ref[...] += jnp.sum(cnt.astype(jnp.float32), axis=0, keepdims=True)
    cnt_ref[...] = carry_ref[...]

    eye = row == col

    def to_row(c, dtype):
        return jnp.sum(jnp.where(eye, c.astype(jnp.float32), 0.0), axis=0, keepdims=True).astype(dtype)

    eid_ref[0] = jnp.concatenate([to_row(i1 - ROUTER_EXPERT_LANE0, jnp.int32),
                                  to_row(i2 - ROUTER_EXPERT_LANE0, jnp.int32)], axis=0)
    rank_ref[0] = jnp.concatenate([to_row(r1, jnp.int32), to_row(r2, jnp.int32)], axis=0)
    wts_ref[...] = jnp.where(_lane_iota((tm, 2)) == 0, w1, w2)


def _cross_router(h1, gc, w_cq, kv_mem, w_co, gf, w_r, b_r, seq, tm):
    n_tok, d = h1.shape
    per_b = seq // tm
    row2 = pl.BlockSpec((tm, 2), lambda i: (i, 0))
    lane2 = pl.BlockSpec((1, 2, tm), lambda i: (i, 0, 0))
    return pl.pallas_call(
        _cross_router_kernel,
        grid=(n_tok // tm,),
        in_specs=[pl.BlockSpec((tm, d), lambda i: (i, 0)),
                  _resident(gc.shape), _resident(w_cq.shape),
                  pl.BlockSpec((1,) + kv_mem.shape[1:], lambda i: (i // per_b, 0, 0)),
                  _resident(w_co.shape), _resident(gf.shape), _resident(w_r.shape), _resident(b_r.shape)],
        out_specs=[pl.BlockSpec((tm, d), lambda i: (i, 0)), lane2, lane2, row2,
                   pl.BlockSpec((1, LANES), lambda i: (0, 0))],
        out_shape=[jax.ShapeDtypeStruct((n_tok, d), jnp.float32),
                   jax.ShapeDtypeStruct((n_tok // tm, 2, tm), jnp.int32),
                   jax.ShapeDtypeStruct((n_tok // tm, 2, tm), jnp.int32),
                   jax.ShapeDtypeStruct((n_tok, 2), jnp.float32),
                   jax.ShapeDtypeStruct((1, LANES), jnp.float32)],
        scratch_shapes=[pltpu.VMEM((1, LANES), jnp.float32)],
        compiler_params=_params("arbitrary"),
        name="cross_router",
    )(h1, gc, w_cq, kv_mem, w_co, gf, w_r, b_r)


def _dispatch_kernel(dest_ref, ztail_ref, h_ref, g_ref, xb_ref, t_ref, sem, zsem):
    i = pl.program_id(0)
    tm = h_ref.shape[0]
    slot = i % 2

    @pl.when(i == 0)
    def _():
        t_ref[1] = jnp.zeros(t_ref.shape[1:], t_ref.dtype)
        tile_rows = tm * SUBLANES

        pieces = MOE_ROWS_PER_BLOCK // tm

        def zero_copy(e, piece):
            start = pl.multiple_of(ztail_ref[0, e] * SUBLANES + piece * tile_rows, tile_rows)
            return pltpu.make_async_copy(t_ref.at[1], xb_ref.at[pl.ds(start, tile_rows), :], zsem)

        for e in range(ztail_ref.shape[1]):
            @pl.when(ztail_ref[1, e] > 0)
            def _():
                for piece in range(pieces):
                    zero_copy(e, piece).start()
        for e in range(ztail_ref.shape[1]):
            @pl.when(ztail_ref[1, e] > 0)
            def _():
                for piece in range(pieces):
                    zero_copy(e, piece).wait()

    h2 = h_ref[...]
    t = h2 * _rms_scale(h2, NORM_EPS) * g_ref[...]
    half = t.shape[1] // 2
    _rows_to_tiles(t_ref.at[slot], _pack_bf16_pair(t[:, :half], t[:, half:]))

    for r in range(tm):
        for k in range(2):
            dst = pl.multiple_of(dest_ref[0, k * tm + r] * SUBLANES, SUBLANES)
            pltpu.make_async_copy(t_ref.at[slot, pl.ds(r * SUBLANES, SUBLANES), :],
                                  xb_ref.at[pl.ds(dst, SUBLANES), :], sem.at[slot]).start()

    def wait_tile(which):
        for _ in range(2):
            pltpu.make_async_copy(t_ref.at[which], xb_ref.at[pl.ds(0, tm * SUBLANES), :], sem.at[which]).wait()

    @pl.when(i > 0)
    def _():
        wait_tile(1 - slot)

    @pl.when(i == pl.num_programs(0) - 1)
    def _():
        wait_tile(slot)


def _dispatch(dest3, ztail, h2, gf, p_rows, tm):
    n_tok, d = h2.shape
    return pl.pallas_call(
        _dispatch_kernel,
        grid=(n_tok // tm,),
        in_specs=[pl.BlockSpec((None, 1, 2 * tm), lambda i: (i, 0, 0), memory_space=pltpu.SMEM),
                  pl.BlockSpec(memory_space=pltpu.SMEM),
                  pl.BlockSpec((tm, d), lambda i: (i, 0)),
                  _resident(gf.shape)],
        out_specs=pl.BlockSpec(memory_space=pl.ANY),
        out_shape=jax.ShapeDtypeStruct((p_rows * SUBLANES, LANES), jnp.uint32),
        scratch_shapes=[pltpu.VMEM((2, tm * SUBLANES, LANES), jnp.uint32), pltpu.SemaphoreType.DMA((2,)),
                        pltpu.SemaphoreType.DMA(())],
        compiler_params=_params("arbitrary"),
        name="moe_dispatch",
    )(dest3, ztail, h2, gf)


def _expert_kernel(be_ref, nact_ref, x_ref, wg_hbm, wu_hbm, wd_hbm, y_ref,
                   wg_f, wu_f, wd_f, wg_b, wu_b, wd_b, sem):
    i = pl.program_id(0)
    blk = i - 1
    nact = nact_ref[0]
    last_blk = pl.num_programs(0) - 2

    def weight_copies(e):
        return (pltpu.make_async_copy(wg_hbm.at[e], wg_f, sem),
                pltpu.make_async_copy(wu_hbm.at[e], wu_f, sem),
                pltpu.make_async_copy(wd_hbm.at[e], wd_f, sem))

    def fetch(e):
        for c in weight_copies(e):
            c.start()

    def land(e):
        for c in weight_copies(e):
            c.wait()
        half = wg_f.shape[0] // 2
        for c in range(SUBLANES):
            for part, src0 in enumerate((c * LANES, half + c * LANES)):
                dst0 = (2 * c + part) * LANES
                wg_b[dst0:dst0 + LANES, :] = wg_f[src0:src0 + LANES, :].astype(jnp.bfloat16)
                wu_b[dst0:dst0 + LANES, :] = wu_f[src0:src0 + LANES, :].astype(jnp.bfloat16)
        wd_b[...] = wd_f[...].astype(jnp.bfloat16)

    @pl.when(i == 0)
    def _():
        fetch(be_ref[0])
        land(be_ref[0])

    @pl.when((i > 0) & (blk < nact))
    def _():
        here = be_ref[blk]
        nxt = be_ref[jnp.minimum(blk + 1, last_blk)]
        prv = be_ref[jnp.maximum(blk - 1, 0)]
        seg_end = nact_ref[1 + N_EXPERTS + here] + nact_ref[1 + here]
        has_next = seg_end < nact
        after = be_ref[jnp.minimum(seg_end, last_blk)]
        is_first = (blk == 0) | (prv != here)
        is_last = (blk + 1 >= nact) | (nxt != here)

        @pl.when(is_first & has_next)
        def _():
            fetch(after)

        bm = x_ref.shape[0] // SUBLANES
        hb = bm // 2
        rows_here = nact_ref[1 + 2 * N_EXPERTS + here] - (blk - nact_ref[1 + N_EXPERTS + here]) * bm

        def swiglu(h):
            view = pl.ds(h * hb * SUBLANES, hb * SUBLANES)
            parts = []
            for chunk in _tiles_to_row_chunks(x_ref.at[view, :], hb):
                x_a, x_b = _unpack_bf16_pair(chunk)
                parts += [x_a.astype(jnp.bfloat16), x_b.astype(jnp.bfloat16)]
            xb = jnp.concatenate(parts, axis=1)
            gate = jnp.dot(xb, wg_b[...], preferred_element_type=jnp.float32)
            up = jnp.dot(xb, wu_b[...], preferred_element_type=jnp.float32)
            hid = (gate * _sigmoid(gate) * up).astype(jnp.bfloat16)
            y = jnp.dot(hid, wd_b[...], preferred_element_type=jnp.float32)
            half = y.shape[1] // 2
            _rows_to_tiles(y_ref.at[view, :], _pack_bf16_pair(y[:, :half], y[:, half:]))

        swiglu(0)

        @pl.when(rows_here > hb)
        def _():
            swiglu(1)

        @pl.when(rows_here <= hb)
        def _():
            y_ref[pl.ds(hb * SUBLANES, hb * SUBLANES), :] = jnp.zeros((hb * SUBLANES, LANES), y_ref.dtype)

        @pl.when(is_last & has_next)
        def _():
            land(after)

    @pl.when((i > 0) & (blk >= nact))
    def _():
        y_ref[...] = jnp.zeros_like(y_ref)


def _experts(block_expert, sched, xb, w_gate, w_up, w_down, bm):
    p_rows = xb.shape[0] // SUBLANES
    d = w_gate.shape[1]
    de = w_gate.shape[-1]

    def x_map(i, be, sc):
        return (jnp.clip(i - 1, 0, sc[0] - 1), 0)

    grid_spec = pltpu.PrefetchScalarGridSpec(
        num_scalar_prefetch=2,
        grid=(p_rows // bm + 1,),
        in_specs=[pl.BlockSpec((bm * SUBLANES, LANES), x_map),
                  pl.BlockSpec(memory_space=pl.ANY),
                  pl.BlockSpec(memory_space=pl.ANY),
                  pl.BlockSpec(memory_space=pl.ANY)],
        out_specs=pl.BlockSpec((bm * SUBLANES, LANES), lambda i, be, sc: (jnp.maximum(i - 1, 0), 0)),
        scratch_shapes=[pltpu.VMEM((d, de), jnp.float32), pltpu.VMEM((d, de), jnp.float32),
                        pltpu.VMEM((de, d), jnp.float32),
                        pltpu.VMEM((d, de), jnp.bfloat16), pltpu.VMEM((d, de), jnp.bfloat16),
                        pltpu.VMEM((de, d), jnp.bfloat16),
                        pltpu.SemaphoreType.DMA(())],
    )
    return pl.pallas_call(
        _expert_kernel,
        grid_spec=grid_spec,
        out_shape=jax.ShapeDtypeStruct(xb.shape, jnp.uint32),
        compiler_params=_params("arbitrary"),
        name="moe_experts",
    )(block_expert, sched, xb, w_gate, w_up, w_down)


def _combine_kernel(dest_ref, dest_next_ref, h_ref, wts_ref, g_ref, y_ref, o_ref, ybuf, sem):
    i = pl.program_id(0)
    tm = h_ref.shape[0]
    slot = i % 2

    def gather(idx_ref, which):
        for r in range(tm):
            for k in range(2):
                src = pl.multiple_of(idx_ref[0, k * tm + r] * SUBLANES, SUBLANES)
                pltpu.make_async_copy(y_ref.at[pl.ds(src, SUBLANES), :],
                                      ybuf.at[which, k, pl.ds(r * SUBLANES, SUBLANES), :], sem.at[which]).start()

    def wait_tile(which):
        for k in range(2):
            pltpu.make_async_copy(y_ref.at[pl.ds(0, tm * SUBLANES), :], ybuf.at[which, k], sem.at[which]).wait()

    @pl.when(i == 0)
    def _():
        gather(dest_ref, slot)

    wait_tile(slot)
    gather(dest_next_ref, 1 - slot)

    w = wts_ref[...]
    half = h_ref.shape[1] // 2
    lo_parts, hi_parts = [], []
    for c, (c0, c1) in enumerate(zip(_tiles_to_row_chunks(ybuf.at[slot, 0], tm),
                                      _tiles_to_row_chunks(ybuf.at[slot, 1], tm))):
        a0, b0 = _unpack_bf16_pair(c0)
        a1, b1 = _unpack_bf16_pair(c1)
        lo_parts.append(h_ref[:, c * LANES:(c + 1) * LANES] + w[:, 0:1] * a0 + w[:, 1:2] * a1)
        hi_parts.append(h_ref[:, half + c * LANES:half + (c + 1) * LANES] + w[:, 0:1] * b0 + w[:, 1:2] * b1)
    h3 = jnp.concatenate(lo_parts + hi_parts, axis=1)
    o_ref[...] = h3 * _rms_scale(h3, NORM_EPS) * g_ref[...]

    @pl.when(i == pl.num_programs(0) - 1)
    def _():
        wait_tile(1 - slot)


def _combine(dest3, h2, wts, g_final, y, tm):
    n_tok, d = h2.shape
    last = n_tok // tm - 1
    return pl.pallas_call(
        _combine_kernel,
        grid=(n_tok // tm,),
        in_specs=[pl.BlockSpec((None, 1, 2 * tm), lambda i: (i, 0, 0), memory_space=pltpu.SMEM),
                  pl.BlockSpec((None, 1, 2 * tm), lambda i: (jnp.minimum(i + 1, last), 0, 0),
                               memory_space=pltpu.SMEM),
                  pl.BlockSpec((tm, d), lambda i: (i, 0)),
                  pl.BlockSpec((tm, 2), lambda i: (i, 0)),
                  _resident(g_final.shape),
                  pl.BlockSpec(memory_space=pl.ANY)],
        out_specs=pl.BlockSpec((tm, d), lambda i: (i, 0)),
        out_shape=jax.ShapeDtypeStruct((n_tok, d), jnp.float32),
        scratch_shapes=[pltpu.VMEM((2, 2, tm * SUBLANES, LANES), jnp.uint32), pltpu.SemaphoreType.DMA((2,))],
        compiler_params=_params("arbitrary"),
        name="moe_combine",
    )(dest3, dest3, h2, wts, g_final, y)


def _transpose_w_in(w_in):
    return jnp.swapaxes(w_in, 0, 1).astype(jnp.bfloat16)


def _split_w_uq(w_uq):
    half = MLA_ROPE_DIM // 2
    w = w_uq.reshape(MLA_Q_RANK, MLA_HEADS, MLA_QK_DIM).transpose(1, 0, 2)
    pe = w[:, :, MLA_NOPE_DIM:]
    pe_swapped = jnp.concatenate([pe[:, :, half:], pe[:, :, :half]], axis=2)
    return jnp.concatenate([w, pe_swapped], axis=2).astype(jnp.bfloat16)


def _split_w_ukv(w_ukv):
    w = w_ukv.reshape(MLA_KV_RANK, MLA_HEADS, MLA_NOPE_DIM + MLA_V_DIM)
    wuk = w[:, :, :MLA_NOPE_DIM].transpose(1, 0, 2).astype(jnp.bfloat16)
    wuv = w[:, :, MLA_NOPE_DIM:].reshape(MLA_KV_RANK, MLA_HEADS * MLA_V_DIM).astype(jnp.bfloat16)
    return wuk, wuv


def kernel(x, mem, positions, attn_norm_g, w_in, diff_lambda_q1, diff_lambda_k1, diff_lambda_q2, diff_lambda_k2, diff_subln_g, w_o_diff, mla_q_norm_g, w_uq, mla_kv_norm_g, w_ukv, w_o_mla, w_out, cross_norm_g, mem_norm_g, w_cq, w_ckv, w_co, ffn_norm_g, w_router_group, b_router_group, w_router_expert, b_router_expert, w_expert_gate, w_expert_up, w_expert_down, final_norm_g):
    batch, seq, d = x.shape
    assert d == D_MODEL and w_in.shape[0] == 1, "single-layer kernel"
    n_tok = batch * seq
    bf = jnp.bfloat16
    x2 = x.reshape(n_tok, d)

    tm_proj = min(1024, seq)
    tm_row = min(256, seq)
    tm_moe = min(MOE_ROWS_PER_BLOCK, seq)
    tm_cross = min(512, seq)
    tq = min(2048, seq)
    rg_diff = 128
    rg_mla = 256

    cos_t, sin_t = _rope_tables(positions, n_tok, tm_proj)

    g_attn = attn_norm_g[0].reshape(1, d)
    qkv, latent, gates = _inproj(x2, g_attn, _transpose_w_in(w_in[0]), cos_t, sin_t, tm_proj)

    o_a = _diff_attn(qkv.reshape(batch, seq, QKV_COLS),
                     diff_lambda_q1[0].reshape(1, -1), diff_lambda_k1[0].reshape(1, -1),
                     diff_lambda_q2[0].reshape(1, -1), diff_lambda_k2[0].reshape(1, -1),
                     diff_subln_g[0].reshape(1, -1), tq, rg_diff)

    wuk, wuv = _split_w_ukv(w_ukv[0])
    q_cat, k_cat, v_mla = _mla_proj(latent, mla_q_norm_g[0].reshape(1, -1), mla_kv_norm_g[0].reshape(1, -1),
                                    _split_w_uq(w_uq[0]), wuk, wuv, cos_t, sin_t, batch, seq, tm_proj)
    o_b = _mla_attn(q_cat, k_cat, v_mla.reshape(batch, seq, MLA_HEADS * MLA_V_DIM), tq, rg_mla)

    h1 = _merge_out(o_a.reshape(n_tok, -1), o_b.reshape(n_tok, -1), gates, x2,
                    w_o_diff[0].astype(bf), w_o_mla[0].astype(bf), w_out[0].astype(bf), tm_cross)

    kv_mem = _mem_kv(mem, mem_norm_g[0].reshape(1, d), w_ckv[0].astype(bf))
    n_router = N_GROUPS + N_EXPERTS
    w_r = jnp.concatenate([w_router_group[0].astype(jnp.float32), w_router_expert[0].astype(jnp.float32),
                           jnp.zeros((d, LANES - n_router), jnp.float32)], axis=1)
    w_r_hi = w_r.astype(bf)
    w_r_lo = (w_r - w_r_hi.astype(jnp.float32)).astype(bf)
    w_r = jnp.concatenate([w_r_hi, w_r_lo], axis=1)
    b_r = jnp.concatenate([b_router_group[0].astype(jnp.float32), b_router_expert[0].astype(jnp.float32),
                           jnp.zeros((LANES - n_router,), jnp.float32)]).reshape(1, LANES)
    g_ffn = ffn_norm_g[0].reshape(1, d)
    h2, eid, rank, wts, cnt = _cross_router(h1, cross_norm_g[0].reshape(1, d), w_cq[0].astype(bf), kv_mem,
                                            w_co[0].astype(bf), g_ffn, w_r, b_r, seq, tm_cross)

    bm = MOE_ROWS_PER_BLOCK
    assert bm % tm_row == 0 and tm_moe == bm
    counts = cnt[0, ROUTER_EXPERT_LANE0:ROUTER_EXPERT_LANE0 + N_EXPERTS].astype(jnp.int32)
    padded = ((counts + bm - 1) // bm) * bm
    padded_end = jnp.cumsum(padded)
    padded_off = padded_end - padded
    seg_start = jnp.sum(jnp.where(eid[..., None] == jnp.arange(N_EXPERTS, dtype=jnp.int32), padded_off, 0), axis=-1)
    dest = seg_start + rank
    p_rows = ((2 * n_tok + bm - 1) // bm) * bm + N_EXPERTS * bm
    n_blocks = p_rows // bm
    n_active = (padded_end[-1] // bm).astype(jnp.int32)
    blk = jnp.minimum(jnp.arange(n_blocks, dtype=jnp.int32), n_active - 1)
    block_expert = jnp.sum((padded_end[None, :] <= (blk * bm)[:, None]).astype(jnp.int32), axis=1)
    block_expert = jnp.minimum(block_expert, N_EXPERTS - 1)
    def tile_slots(tm):
        return dest.reshape(-1, 2, tm_cross // tm, tm).transpose(0, 2, 1, 3).reshape(n_tok // tm, 1, 2 * tm)
    unused = n_active + jnp.arange(N_EXPERTS, dtype=jnp.int32)
    ztail = jnp.stack([jnp.concatenate([jnp.maximum(padded_end - bm, 0), jnp.minimum(unused, n_blocks - 1) * bm]),
                       jnp.concatenate([padded > 0, unused < n_blocks]).astype(jnp.int32)]).astype(jnp.int32)
    sched = jnp.concatenate([n_active.reshape(1), padded // bm, padded_off // bm, counts]).astype(jnp.int32)

    xb = _dispatch(tile_slots(tm_row), ztail, h2, g_ffn, p_rows, tm_row)
    y = _experts(block_expert, sched, xb, w_expert_gate[0], w_expert_up[0], w_expert_down[0], bm)
    out = _combine(tile_slots(tm_moe), h2, wts, final_norm_g.reshape(1, d), y, tm_moe)
    return out.reshape(batch, seq, d)
```

```python
import functools
import math

import jax
import jax.numpy as jnp
from jax import lax
from jax.experimental import pallas as pl
from jax.experimental.pallas import tpu as pltpu

D_MODEL = 2048
ROPE_THETA = 500000.0
NORM_EPS = 1e-6

DIFF_HEADS = 8
DIFF_HEAD_DIM = 64
DIFF_V_DIM = 2 * DIFF_HEAD_DIM
DIFF_ROT = DIFF_HEAD_DIM // 4
DIFF_SUBLN_EPS = 1e-5
DIFF_LAMBDA_INIT = 0.8 - 0.6 * math.exp(-0.3 * 0)

MLA_HEADS = 8
MLA_Q_RANK = 512
MLA_KV_RANK = 256
MLA_NOPE_DIM = 128
MLA_ROPE_DIM = 64
MLA_V_DIM = 128
MLA_QK_DIM = MLA_NOPE_DIM + MLA_ROPE_DIM

CROSS_HEADS = 4
CROSS_HEAD_DIM = 128

N_GROUPS = 4
EXPERTS_PER_GROUP = 8
N_EXPERTS = N_GROUPS * EXPERTS_PER_GROUP
D_EXPERT = 512

LANES = 128
LOG2E = 1.4426950408889634
VMEM_LIMIT_BYTES = 56 * 1024 * 1024

ROUTER_EXPERT_LANE0 = N_GROUPS

QKV_COLS = 3 * DIFF_HEADS * DIFF_V_DIM
LATENT_COLS = 1024
GATE_COLS = 2 * D_MODEL
KPE_COL0 = MLA_Q_RANK + MLA_KV_RANK

MOE_ROWS_PER_BLOCK = 512
SUBLANES = 8


def _params(*semantics):
    return pltpu.CompilerParams(dimension_semantics=semantics, vmem_limit_bytes=VMEM_LIMIT_BYTES)


def _resident(shape):
    zeros = (0,) * len(shape)
    return pl.BlockSpec(shape, lambda *_: zeros, pipeline_mode=pl.Buffered(1))


def _rms_scale(xf, eps):
    return lax.rsqrt(jnp.mean(xf * xf, axis=-1, keepdims=True) + eps)


def _sigmoid(x):
    return 0.5 * jnp.tanh(0.5 * x) + 0.5


def _pack_bf16_pair(a, b):
    hi = lax.bitcast_convert_type(a.astype(jnp.bfloat16).astype(jnp.float32), jnp.uint32)
    lo = lax.bitcast_convert_type(b.astype(jnp.bfloat16).astype(jnp.float32), jnp.uint32)
    return hi | (lo >> 16)


def _unpack_bf16_pair(w):
    a = lax.bitcast_convert_type(w & jnp.uint32(0xFFFF0000), jnp.float32)
    b = lax.bitcast_convert_type(w << 16, jnp.float32)
    return a, b


def _rows_to_tiles(ref_view, packed):
    rows = packed.shape[0]
    for c in range(SUBLANES):
        ref_view[pl.ds(c, rows, stride=SUBLANES), :] = packed[:, c * LANES:(c + 1) * LANES]


def _tiles_to_row_chunks(ref_view, rows):
    return [ref_view[pl.ds(c, rows, stride=SUBLANES), :] for c in range(SUBLANES)]


def _lane_iota(shape):
    return lax.broadcasted_iota(jnp.int32, shape, len(shape) - 1)


def _trig_kernel(pos_ref, invf_ref, cos_ref, sin_ref):
    ang = pos_ref[...].astype(jnp.float32) * invf_ref[...]
    cos_ref[...] = jnp.cos(ang)
    sin_ref[...] = jnp.sin(ang)


def _rope_tables(positions, n_tok, tm):
    half_m = MLA_ROPE_DIM // 2
    half_d = DIFF_ROT // 2
    inv_m = jnp.float32(ROPE_THETA) ** (-jnp.arange(half_m, dtype=jnp.float32) * 2.0 / MLA_ROPE_DIM)
    inv_d = jnp.float32(ROPE_THETA) ** (-jnp.arange(half_d, dtype=jnp.float32) * 2.0 / DIFF_ROT)
    invf = jnp.concatenate([inv_m, inv_m, inv_d, inv_d,
                            jnp.zeros((DIFF_HEAD_DIM - DIFF_ROT,), jnp.float32)]).reshape(1, LANES)
    pos = positions.reshape(n_tok, 1)
    return pl.pallas_call(
        _trig_kernel,
        grid=(n_tok // tm,),
        in_specs=[pl.BlockSpec((tm, 1), lambda i: (i, 0)), _resident((1, LANES))],
        out_specs=[pl.BlockSpec((tm, LANES), lambda i: (i, 0))] * 2,
        out_shape=[jax.ShapeDtypeStruct((n_tok, LANES), jnp.float32)] * 2,
        compiler_params=_params("parallel"),
        name="rope_tables",
    )(pos, invf)


def _diff_rope_coeffs(cos_t, sin_t):
    lane = _lane_iota(cos_t.shape)
    upper = lane >= DIFF_HEAD_DIM
    cos_d = jnp.where(upper, cos_t, pltpu.roll(cos_t, DIFF_HEAD_DIM, 1))
    sin_d = jnp.where(upper, sin_t, pltpu.roll(sin_t, DIFF_HEAD_DIM, 1))
    in_head = lane % DIFF_HEAD_DIM
    half = DIFF_ROT // 2
    s_next = jnp.where(in_head < half, -sin_d, 0.0)
    s_prev = jnp.where((in_head >= half) & (in_head < DIFF_ROT), sin_d, 0.0)
    return cos_d, s_next, s_prev


def _mla_rope(pair, cos_t, sin_t):
    lane = _lane_iota(pair.shape)
    sin_signed = jnp.where(lane < MLA_ROPE_DIM // 2, -sin_t, sin_t)
    return pair * cos_t + pltpu.roll(pair, MLA_ROPE_DIM, 1) * sin_signed


INPROJ_TN = 1024
INPROJ_PIECE = 256
Q_TILES = DIFF_HEADS * DIFF_V_DIM // INPROJ_TN
ROPE_TILES = 2 * Q_TILES
QKV_TILES = QKV_COLS // INPROJ_TN
LATENT_TILES = LATENT_COLS // INPROJ_TN
GATE_TILES = GATE_COLS // INPROJ_TN
INPROJ_TILES = QKV_TILES + LATENT_TILES + GATE_TILES
GATE_ROW0 = QKV_COLS + KPE_COL0 + MLA_ROPE_DIM


def _inproj_kernel(x_ref, g_ref, w_ref, cos_ref, sin_ref, qkv_ref, lat_ref, gate_ref, xn_ref):
    j = pl.program_id(1)

    @pl.when(j == 0)
    def _():
        xf = x_ref[...]
        xn_ref[...] = (xf * _rms_scale(xf, NORM_EPS) * g_ref[...]).astype(jnp.bfloat16)

    def pieces(epilogue):
        for c in range(INPROJ_TN // INPROJ_PIECE):
            cols = slice(c * INPROJ_PIECE, (c + 1) * INPROJ_PIECE)
            acc = lax.dot_general(xn_ref[...], w_ref[cols, :], (((1,), (1,)), ((), ())),
                                  preferred_element_type=jnp.float32)
            epilogue(acc, cols)

    @pl.when(j < ROPE_TILES)
    def _():
        cos_d, s_next, s_prev = _diff_rope_coeffs(cos_ref[...], sin_ref[...])
        qscale = jnp.where(j < Q_TILES, DIFF_HEAD_DIM ** -0.5 * LOG2E, 1.0).astype(jnp.float32)

        def rope(acc, cols):
            for c in range(INPROJ_PIECE // LANES):
                xc = acc[:, c * LANES:(c + 1) * LANES]
                rot = (xc * cos_d + pltpu.roll(xc, LANES - DIFF_ROT // 2, 1) * s_next
                       + pltpu.roll(xc, DIFF_ROT // 2, 1) * s_prev)
                lo = cols.start + c * LANES
                qkv_ref[:, lo:lo + LANES] = (rot * qscale).astype(qkv_ref.dtype)

        pieces(rope)

    @pl.when((j >= ROPE_TILES) & (j < QKV_TILES))
    def _():
        def value(acc, cols):
            qkv_ref[:, cols] = acc.astype(qkv_ref.dtype)

        pieces(value)

    @pl.when((j >= QKV_TILES) & (j < QKV_TILES + LATENT_TILES))
    def _():
        def latent(acc, cols):
            if cols.start <= KPE_COL0 < cols.stop:
                c0 = KPE_COL0 - cols.start
                v = acc[:, c0:c0 + LANES]
                lane = _lane_iota(v.shape)
                half = MLA_ROPE_DIM // 2
                swapped = jnp.where(lane < MLA_ROPE_DIM + half, pltpu.roll(v, half, 1),
                                    pltpu.roll(v, MLA_ROPE_DIM + half, 1))
                parts = [acc[:, :c0], jnp.where(lane < MLA_ROPE_DIM, v, swapped), acc[:, c0 + LANES:]]
                acc = jnp.concatenate([p for p in parts if p.shape[1]], axis=1)
            lat_ref[:, cols] = acc

        pieces(latent)

    @pl.when(j >= QKV_TILES + LATENT_TILES)
    def _():
        def gate(acc, cols):
            gate_ref[:, cols] = _sigmoid(acc).astype(gate_ref.dtype)

        pieces(gate)


def _inproj(x2, g, w_all, cos_t, sin_t, tm):
    n_tok, d = x2.shape
    tn = INPROJ_TN
    lat0 = QKV_TILES
    gate0 = QKV_TILES + LATENT_TILES
    return pl.pallas_call(
        _inproj_kernel,
        grid=(n_tok // tm, INPROJ_TILES),
        in_specs=[pl.BlockSpec((tm, d), lambda i, j: (i, 0)),
                  _resident((1, d)),
                  pl.BlockSpec((pl.Element(tn), pl.Element(d)),
                               lambda i, j: (pl.multiple_of(jnp.where(j < gate0, j * tn, GATE_ROW0 + (j - gate0) * tn), 16), 0)),
                  pl.BlockSpec((tm, LANES), lambda i, j: (i, 0)),
                  pl.BlockSpec((tm, LANES), lambda i, j: (i, 0))],
        out_specs=[pl.BlockSpec((tm, tn), lambda i, j: (i, jnp.clip(j, 0, QKV_TILES - 1))),
                   pl.BlockSpec((tm, tn), lambda i, j: (i, jnp.clip(j - lat0, 0, LATENT_TILES - 1))),
                   pl.BlockSpec((tm, tn), lambda i, j: (i, jnp.clip(j - gate0, 0, GATE_TILES - 1)))],
        out_shape=[jax.ShapeDtypeStruct((n_tok, QKV_COLS), jnp.bfloat16),
                   jax.ShapeDtypeStruct((n_tok, LATENT_COLS), jnp.float32),
                   jax.ShapeDtypeStruct((n_tok, GATE_COLS), jnp.bfloat16)],
        scratch_shapes=[pltpu.VMEM((tm, d), jnp.bfloat16)],
        compiler_params=_params("parallel", "arbitrary"),
        name="inproj",
    )(x2, g, w_all, cos_t, sin_t)


def _mla_proj_kernel(c_ref, gq_ref, gkv_ref, wuq_ref, wuk_ref, wuv_ref, cos_ref, sin_ref,
                     q_ref, k_ref, v_ref):
    cos_t = cos_ref[...]
    sin_t = sin_ref[...]
    cq = c_ref[:, :MLA_Q_RANK]
    cqn = (cq * _rms_scale(cq, NORM_EPS) * gq_ref[...]).astype(jnp.bfloat16)
    ckv = c_ref[:, MLA_Q_RANK:KPE_COL0]
    ckvn = (ckv * _rms_scale(ckv, NORM_EPS) * gkv_ref[...]).astype(jnp.bfloat16)
    kpe = _mla_rope(c_ref[:, KPE_COL0:KPE_COL0 + LANES], cos_t, sin_t)[:, :MLA_ROPE_DIM].astype(k_ref.dtype)
    qscale = MLA_QK_DIM ** -0.5 * LOG2E
    for h in range(MLA_HEADS):
        r = jnp.dot(cqn, wuq_ref[h], preferred_element_type=jnp.float32)
        q_ref[0, h, :, :MLA_NOPE_DIM] = (r[:, :MLA_NOPE_DIM] * qscale).astype(q_ref.dtype)
        qpe = _mla_rope(r[:, MLA_NOPE_DIM:], cos_t, sin_t)[:, :MLA_ROPE_DIM]
        q_ref[0, h, :, MLA_NOPE_DIM:] = (qpe * qscale).astype(q_ref.dtype)
        kn = jnp.dot(ckvn, wuk_ref[h], preferred_element_type=jnp.float32)
        k_ref[0, h, :, :MLA_NOPE_DIM] = kn.astype(k_ref.dtype)
        k_ref[0, h, :, MLA_NOPE_DIM:] = kpe
    v_ref[...] = jnp.dot(ckvn, wuv_ref[...], preferred_element_type=jnp.float32).astype(v_ref.dtype)


def _mla_proj(latent, gq, gkv, wuq, wuk, wuv, cos_t, sin_t, batch, seq, tm):
    n_tok = latent.shape[0]
    per_b = seq // tm
    head_spec = pl.BlockSpec((1, MLA_HEADS, tm, MLA_QK_DIM), lambda i: (i // per_b, 0, i % per_b, 0))
    head_shape = jax.ShapeDtypeStruct((batch, MLA_HEADS, seq, MLA_QK_DIM), jnp.bfloat16)
    return pl.pallas_call(
        _mla_proj_kernel,
        grid=(n_tok // tm,),
        in_specs=[pl.BlockSpec((tm, LATENT_COLS), lambda i: (i, 0)),
                  _resident(gq.shape), _resident(gkv.shape),
                  _resident(wuq.shape), _resident(wuk.shape), _resident(wuv.shape),
                  pl.BlockSpec((tm, LANES), lambda i: (i, 0)),
                  pl.BlockSpec((tm, LANES), lambda i: (i, 0))],
        out_specs=[head_spec, head_spec,
                   pl.BlockSpec((tm, MLA_HEADS * MLA_V_DIM), lambda i: (i, 0))],
        out_shape=[head_shape, head_shape,
                   jax.ShapeDtypeStruct((n_tok, MLA_HEADS * MLA_V_DIM), jnp.bfloat16)],
        compiler_params=_params("parallel"),
        name="mla_proj",
    )(latent, gq, gkv, wuq, wuk, wuv, cos_t, sin_t)


def _with_ones(v):
    return jnp.concatenate([v, jnp.ones((v.shape[0], LANES), v.dtype)], axis=-1)


def _softmax_pv(s, v_ones):
    m = jnp.max(s, axis=-1, keepdims=True)
    p = jnp.exp2(s - m).astype(v_ones.dtype)
    pv = jnp.dot(p, v_ones, preferred_element_type=jnp.float32)
    dv = v_ones.shape[1] - LANES
    return pv[:, :dv] / pv[:, dv:]


ATTN_HEADS_PER_STEP = 2


def _diff_attn_kernel(q_ref, k_ref, v_ref, lq1_ref, lk1_ref, lq2_ref, lk2_ref, g_ref, o_ref, v1_ref, *, rg):
    heads = q_ref.shape[2] // LANES

    @pl.when(pl.program_id(2) == 0)
    def _():
        for h in range(heads):
            v1_ref[h] = _with_ones(v_ref[0, :, h * DIFF_V_DIM:(h + 1) * DIFF_V_DIM])

    lam = (jnp.exp(jnp.sum(lq1_ref[...] * lk1_ref[...], axis=-1, keepdims=True))
           - jnp.exp(jnp.sum(lq2_ref[...] * lk2_ref[...], axis=-1, keepdims=True))
           + DIFF_LAMBDA_INIT)
    lane = _lane_iota((rg, LANES))
    for g in range(q_ref.shape[1] // rg):
        for h in range(heads):
            cols = slice(h * LANES, (h + 1) * LANES)
            q = q_ref[0, g * rg:(g + 1) * rg, cols]
            k = k_ref[0, :, cols]
            zero = jnp.zeros_like(q)
            q12 = jnp.concatenate([jnp.where(lane < DIFF_HEAD_DIM, q, zero),
                                   jnp.where(lane >= DIFF_HEAD_DIM, q, zero)], axis=0)
            s = lax.dot_general(q12, k, (((1,), (1,)), ((), ())), preferred_element_type=jnp.float32)
            a = _softmax_pv(s, v1_ref[h])
            o = a[:rg] - lam * a[rg:]
            o = o * _rms_scale(o, DIFF_SUBLN_EPS) * g_ref[...] * (1.0 - DIFF_LAMBDA_INIT)
            o_ref[0, g * rg:(g + 1) * rg, cols] = o.astype(o_ref.dtype)


def _diff_attn(qkv3, lq1, lk1, lq2, lk2, subln_g, tq, rg):
    batch, seq, _ = qkv3.shape
    hp = ATTN_HEADS_PER_STEP
    steps = DIFF_HEADS // hp
    width = hp * DIFF_V_DIM
    return pl.pallas_call(
        functools.partial(_diff_attn_kernel, rg=rg),
        grid=(batch, steps, seq // tq),
        in_specs=[pl.BlockSpec((1, tq, width), lambda b, hh, i: (b, i, hh)),
                  pl.BlockSpec((1, seq, width), lambda b, hh, i: (b, 0, steps + hh)),
                  pl.BlockSpec((1, seq, width), lambda b, hh, i: (b, 0, 2 * steps + hh)),
                  _resident(lq1.shape), _resident(lk1.shape), _resident(lq2.shape), _resident(lk2.shape),
                  _resident(subln_g.shape)],
        out_specs=pl.BlockSpec((1, tq, width), lambda b, hh, i: (b, i, hh)),
        out_shape=jax.ShapeDtypeStruct((batch, seq, DIFF_HEADS * DIFF_V_DIM), jnp.bfloat16),
        scratch_shapes=[pltpu.VMEM((hp, seq, DIFF_V_DIM + LANES), jnp.bfloat16)],
        compiler_params=_params("parallel", "parallel", "arbitrary"),
        name="diff_attn",
    )(qkv3, qkv3, qkv3, lq1, lk1, lq2, lk2, subln_g)


def _mla_attn_kernel(q_ref, k_ref, v_ref, o_ref, v1_ref, *, rg):
    heads = q_ref.shape[1]

    @pl.when(pl.program_id(2) == 0)
    def _():
        for h in range(heads):
            v1_ref[h] = _with_ones(v_ref[0, :, h * MLA_V_DIM:(h + 1) * MLA_V_DIM])

    for g in range(q_ref.shape[2] // rg):
        for h in range(heads):
            s = lax.dot_general(q_ref[0, h, g * rg:(g + 1) * rg], k_ref[0, h], (((1,), (1,)), ((), ())),
                                preferred_element_type=jnp.float32)
            o_ref[0, g * rg:(g + 1) * rg, h * MLA_V_DIM:(h + 1) * MLA_V_DIM] = (
                _softmax_pv(s, v1_ref[h]).astype(o_ref.dtype))


def _mla_attn(q_cat, k_cat, v3, tq, rg):
    batch, heads, seq, dqk = q_cat.shape
    hp = ATTN_HEADS_PER_STEP
    return pl.pallas_call(
        functools.partial(_mla_attn_kernel, rg=rg),
        grid=(batch, heads // hp, seq // tq),
        in_specs=[pl.BlockSpec((1, hp, tq, dqk), lambda b, h, i: (b, h, i, 0)),
                  pl.BlockSpec((1, hp, seq, dqk), lambda b, h, i: (b, h, 0, 0)),
                  pl.BlockSpec((1, seq, hp * MLA_V_DIM), lambda b, h, i: (b, 0, h))],
        out_specs=pl.BlockSpec((1, tq, hp * MLA_V_DIM), lambda b, h, i: (b, i, h)),
        out_shape=jax.ShapeDtypeStruct((batch, seq, heads * MLA_V_DIM), jnp.bfloat16),
        scratch_shapes=[pltpu.VMEM((hp, seq, MLA_V_DIM + LANES), jnp.bfloat16)],
        compiler_params=_params("parallel", "parallel", "arbitrary"),
        name="mla_attn",
    )(q_cat, k_cat, v3)


def _merge_out_kernel(oa_ref, ob_ref, sga_ref, sgb_ref, x_ref, woa_ref, wob_ref, wout_ref, h_ref):
    ya = jnp.dot(oa_ref[...], woa_ref[...], preferred_element_type=jnp.float32)
    yb = jnp.dot(ob_ref[...], wob_ref[...], preferred_element_type=jnp.float32)
    merged = sga_ref[...].astype(jnp.float32) * ya + sgb_ref[...].astype(jnp.float32) * yb
    h_ref[...] = x_ref[...] + jnp.dot(merged.astype(jnp.bfloat16), wout_ref[...],
                                       preferred_element_type=jnp.float32)


def _merge_out(o_a, o_b, gates, x2, w_oa, w_ob, w_out, tm):
    n_tok, d = x2.shape
    return pl.pallas_call(
        _merge_out_kernel,
        grid=(n_tok // tm,),
        in_specs=[pl.BlockSpec((tm, o_a.shape[1]), lambda i: (i, 0)),
                  pl.BlockSpec((tm, o_b.shape[1]), lambda i: (i, 0)),
                  pl.BlockSpec((tm, d), lambda i: (i, 0)),
                  pl.BlockSpec((tm, d), lambda i: (i, 1)),
                  pl.BlockSpec((tm, d), lambda i: (i, 0)),
                  _resident(w_oa.shape), _resident(w_ob.shape), _resident(w_out.shape)],
        out_specs=pl.BlockSpec((tm, d), lambda i: (i, 0)),
        out_shape=jax.ShapeDtypeStruct((n_tok, d), jnp.float32),
        compiler_params=_params("parallel"),
        name="merge_out",
    )(o_a, o_b, gates, gates, x2, w_oa, w_ob, w_out)


def _mem_kv_kernel(mem_ref, g_ref, w_ref, kv_ref):
    mf = mem_ref[0]
    mn = (mf * _rms_scale(mf, NORM_EPS) * g_ref[...]).astype(jnp.bfloat16)
    kv_ref[0] = jnp.dot(mn, w_ref[...], preferred_element_type=jnp.float32).astype(kv_ref.dtype)


def _mem_kv(mem, g, w_ckv):
    batch, m, d = mem.shape
    return pl.pallas_call(
        _mem_kv_kernel,
        grid=(batch,),
        in_specs=[pl.BlockSpec((1, m, d), lambda b: (b, 0, 0)), _resident(g.shape), _resident(w_ckv.shape)],
        out_specs=pl.BlockSpec((1, m, w_ckv.shape[1]), lambda b: (b, 0, 0)),
        out_shape=jax.ShapeDtypeStruct((batch, m, w_ckv.shape[1]), jnp.bfloat16),
        compiler_params=_params("parallel"),
        name="mem_kv",
    )(mem, g, w_ckv)


def _cross_router_kernel(h_ref, gc_ref, wcq_ref, kv_ref, wco_ref, gf_ref, wr_ref, br_ref,
                         h2_ref, eid_ref, rank_ref, wts_ref, cnt_ref, carry_ref):
    i = pl.program_id(0)

    @pl.when(i == 0)
    def _():
        carry_ref[...] = jnp.zeros_like(carry_ref)

    h1 = h_ref[...]
    tm = h1.shape[0]
    hn = (h1 * _rms_scale(h1, NORM_EPS) * gc_ref[...]).astype(jnp.bfloat16)
    q = jnp.dot(hn, wcq_ref[...], preferred_element_type=jnp.float32) * (CROSS_HEAD_DIM ** -0.5 * LOG2E)
    q = q.astype(jnp.bfloat16)
    kv_cols = CROSS_HEADS * CROSS_HEAD_DIM
    heads = []
    for hd in range(CROSS_HEADS):
        lo = hd * CROSS_HEAD_DIM
        kh = kv_ref[0, :, lo:lo + CROSS_HEAD_DIM]
        vh = kv_ref[0, :, kv_cols + lo:kv_cols + lo + CROSS_HEAD_DIM]
        s = lax.dot_general(q[:, lo:lo + CROSS_HEAD_DIM], kh, (((1,), (1,)), ((), ())),
                            preferred_element_type=jnp.float32)
        heads.append(_softmax_pv(s, _with_ones(vh)).astype(jnp.bfloat16))
    o = jnp.concatenate(heads, axis=-1)
    h2 = h1 + jnp.dot(o, wco_ref[...], preferred_element_type=jnp.float32)
    h2_ref[...] = h2

    t = h2 * _rms_scale(h2, NORM_EPS) * gf_ref[...]
    t_hi = t.astype(jnp.bfloat16)
    t_lo = (t - t_hi.astype(jnp.float32)).astype(jnp.bfloat16)
    hi = jnp.dot(t_hi, wr_ref[...], preferred_element_type=jnp.float32)
    lo = jnp.dot(t_lo, wr_ref[:, :LANES], preferred_element_type=jnp.float32)
    logits = hi[:, :LANES] + (hi[:, LANES:] + lo) + br_ref[...]
    lane = _lane_iota(logits.shape)
    neg = jnp.float32(-jnp.inf)
    big = jnp.int32(2 * LANES)
    is_group = lane < N_GROUPS
    lg = jnp.where(is_group, logits, neg)
    mg = jnp.max(lg, axis=-1, keepdims=True)
    g_idx = jnp.min(jnp.where(is_group & (logits == mg), lane, big), axis=-1, keepdims=True)
    g_p = 1.0 / jnp.sum(jnp.exp(lg - mg), axis=-1, keepdims=True)
    lo_lane = ROUTER_EXPERT_LANE0 + EXPERTS_PER_GROUP * g_idx
    in_grp = (lane >= lo_lane) & (lane < lo_lane + EXPERTS_PER_GROUP)
    l1 = jnp.max(jnp.where(in_grp, logits, neg), axis=-1, keepdims=True)
    i1 = jnp.min(jnp.where(in_grp & (logits == l1), lane, big), axis=-1, keepdims=True)
    rest = in_grp & (lane != i1)
    l2 = jnp.max(jnp.where(rest, logits, neg), axis=-1, keepdims=True)
    i2 = jnp.min(jnp.where(rest & (logits == l2), lane, big), axis=-1, keepdims=True)
    d = jnp.exp(l2 - l1)
    w1 = g_p / (1.0 + d)
    w2 = w1 * d

    oh1 = lane == i1
    oh2 = lane == i2
    cnt = (oh1 | oh2).astype(jnp.bfloat16)
    row = lax.broadcasted_iota(jnp.int32, (tm, tm), 0)
    col = lax.broadcasted_iota(jnp.int32, (tm, tm), 1)
    before = (col < row).astype(jnp.bfloat16)
    slot = jnp.dot(before, cnt, preferred_element_type=jnp.float32) + carry_ref[...]
    r1 = jnp.sum(jnp.where(oh1, slot, 0.0), axis=-1, keepdims=True)
    r2 = jnp.sum(jnp.where(oh2, slot, 0.0), axis=-1, keepdims=True)
    carry_ref[...] += jnp.sum(cnt.astype(jnp.float32), axis=0, keepdims=True)
    cnt_ref[...] = carry_ref[...]

    eye = row == col

    def to_row(c, dtype):
        return jnp.sum(jnp.where(eye, c.astype(jnp.float32), 0.0), axis=0, keepdims=True).astype(dtype)

    eid_ref[0] = jnp.concatenate([to_row(i1 - ROUTER_EXPERT_LANE0, jnp.int32),
                                  to_row(i2 - ROUTER_EXPERT_LANE0, jnp.int32)], axis=0)
    rank_ref[0] = jnp.concatenate([to_row(r1, jnp.int32), to_row(r2, jnp.int32)], axis=0)
    wts_ref[...] = jnp.where(_lane_iota((tm, 2)) == 0, w1, w2)


def _cross_router(h1, gc, w_cq, kv_mem, w_co, gf, w_r, b_r, seq, tm):
    n_tok, d = h1.shape
    per_b = seq // tm
    row2 = pl.BlockSpec((tm, 2), lambda i: (i, 0))
    lane2 = pl.BlockSpec((1, 2, tm), lambda i: (i, 0, 0))
    return pl.pallas_call(
        _cross_router_kernel,
        grid=(n_tok // tm,),
        in_specs=[pl.BlockSpec((tm, d), lambda i: (i, 0)),
                  _resident(gc.shape), _resident(w_cq.shape),
                  pl.BlockSpec((1,) + kv_mem.shape[1:], lambda i: (i // per_b, 0, 0)),
                  _resident(w_co.shape), _resident(gf.shape), _resident(w_r.shape), _resident(b_r.shape)],
        out_specs=[pl.BlockSpec((tm, d), lambda i: (i, 0)), lane2, lane2, row2,
                   pl.BlockSpec((1, LANES), lambda i: (0, 0))],
        out_shape=[jax.ShapeDtypeStruct((n_tok, d), jnp.float32),
                   jax.ShapeDtypeStruct((n_tok // tm, 2, tm), jnp.int32),
                   jax.ShapeDtypeStruct((n_tok // tm, 2, tm), jnp.int32),
                   jax.ShapeDtypeStruct((n_tok, 2), jnp.float32),
                   jax.ShapeDtypeStruct((1, LANES), jnp.float32)],
        scratch_shapes=[pltpu.VMEM((1, LANES), jnp.float32)],
        compiler_params=_params("arbitrary"),
        name="cross_router",
    )(h1, gc, w_cq, kv_mem, w_co, gf, w_r, b_r)


def _dispatch_kernel(dest_ref, ztail_ref, h_ref, g_ref, xb_ref, t_ref, sem, zsem):
    i = pl.program_id(0)
    tm = h_ref.shape[0]
    slot = i % 2

    @pl.when(i == 0)
    def _():
        t_ref[1] = jnp.zeros(t_ref.shape[1:], t_ref.dtype)
        tile_rows = tm * SUBLANES

        pieces = MOE_ROWS_PER_BLOCK // tm

        def zero_copy(e, piece):
            start = pl.multiple_of(ztail_ref[0, e] * SUBLANES + piece * tile_rows, tile_rows)
            return pltpu.make_async_copy(t_ref.at[1], xb_ref.at[pl.ds(start, tile_rows), :], zsem)

        for e in range(ztail_ref.shape[1]):
            @pl.when(ztail_ref[1, e] > 0)
            def _():
                for piece in range(pieces):
                    zero_copy(e, piece).start()
        for e in range(ztail_ref.shape[1]):
            @pl.when(ztail_ref[1, e] > 0)
            def _():
                for piece in range(pieces):
                    zero_copy(e, piece).wait()

    h2 = h_ref[...]
    t = h2 * _rms_scale(h2, NORM_EPS) * g_ref[...]
    half = t.shape[1] // 2
    _rows_to_tiles(t_ref.at[slot], _pack_bf16_pair(t[:, :half], t[:, half:]))

    for r in range(tm):
        for k in range(2):
            dst = pl.multiple_of(dest_ref[0, k * tm + r] * SUBLANES, SUBLANES)
            pltpu.make_async_copy(t_ref.at[slot, pl.ds(r * SUBLANES, SUBLANES), :],
                                  xb_ref.at[pl.ds(dst, SUBLANES), :], sem.at[slot]).start(priority=k)

    def wait_tile(which):
        for _ in range(2):
            pltpu.make_async_copy(t_ref.at[which], xb_ref.at[pl.ds(0, tm * SUBLANES), :], sem.at[which]).wait()

    @pl.when(i > 0)
    def _():
        wait_tile(1 - slot)

    @pl.when(i == pl.num_programs(0) - 1)
    def _():
        wait_tile(slot)


def _dispatch(dest3, ztail, h2, gf, p_rows, tm):
    n_tok, d = h2.shape
    return pl.pallas_call(
        _dispatch_kernel,
        grid=(n_tok // tm,),
        in_specs=[pl.BlockSpec((None, 1, 2 * tm), lambda i: (i, 0, 0), memory_space=pltpu.SMEM),
                  pl.BlockSpec(memory_space=pltpu.SMEM),
                  pl.BlockSpec((tm, d), lambda i: (i, 0)),
                  _resident(gf.shape)],
        out_specs=pl.BlockSpec(memory_space=pl.ANY),
        out_shape=jax.ShapeDtypeStruct((p_rows * SUBLANES, LANES), jnp.uint32),
        scratch_shapes=[pltpu.VMEM((2, tm * SUBLANES, LANES), jnp.uint32), pltpu.SemaphoreType.DMA((2,)),
                        pltpu.SemaphoreType.DMA(())],
        compiler_params=_params("arbitrary"),
        name="moe_dispatch",
    )(dest3, ztail, h2, gf)


def _expert_kernel(be_ref, nact_ref, x_ref, wg_hbm, wu_hbm, wd_hbm, y_ref,
                   wg_f, wu_f, wd_f, wg_b, wu_b, wd_b, sem):
    i = pl.program_id(0)
    blk = i - 1
    nact = nact_ref[0]
    last_blk = pl.num_programs(0) - 2

    def weight_copies(e):
        return (pltpu.make_async_copy(wg_hbm.at[e], wg_f, sem),
                pltpu.make_async_copy(wu_hbm.at[e], wu_f, sem),
                pltpu.make_async_copy(wd_hbm.at[e], wd_f, sem))

    def fetch(e):
        for c in weight_copies(e):
            c.start()

    def land(e):
        for c in weight_copies(e):
            c.wait()
        half = wg_f.shape[0] // 2
        for c in range(SUBLANES):
            for part, src0 in enumerate((c * LANES, half + c * LANES)):
                dst0 = (2 * c + part) * LANES
                wg_b[dst0:dst0 + LANES, :] = wg_f[src0:src0 + LANES, :].astype(jnp.bfloat16)
                wu_b[dst0:dst0 + LANES, :] = wu_f[src0:src0 + LANES, :].astype(jnp.bfloat16)
        wd_b[...] = wd_f[...].astype(jnp.bfloat16)

    @pl.when(i == 0)
    def _():
        fetch(be_ref[0])
        land(be_ref[0])

    @pl.when((i > 0) & (blk < nact))
    def _():
        here = be_ref[blk]
        nxt = be_ref[jnp.minimum(blk + 1, last_blk)]
        prv = be_ref[jnp.maximum(blk - 1, 0)]
        seg_end = nact_ref[1 + N_EXPERTS + here] + nact_ref[1 + here]
        has_next = seg_end < nact
        after = be_ref[jnp.minimum(seg_end, last_blk)]
        is_first = (blk == 0) | (prv != here)
        is_last = (blk + 1 >= nact) | (nxt != here)

        @pl.when(is_first & has_next)
        def _():
            fetch(after)

        bm = x_ref.shape[0] // SUBLANES
        hb = bm // 2
        rows_here = nact_ref[1 + 2 * N_EXPERTS + here] - (blk - nact_ref[1 + N_EXPERTS + here]) * bm

        def swiglu(h):
            view = pl.ds(h * hb * SUBLANES, hb * SUBLANES)
            parts = []
            for chunk in _tiles_to_row_chunks(x_ref.at[view, :], hb):
                x_a, x_b = _unpack_bf16_pair(chunk)
                parts += [x_a.astype(jnp.bfloat16), x_b.astype(jnp.bfloat16)]
            xb = jnp.concatenate(parts, axis=1)
            gate = jnp.dot(xb, wg_b[...], preferred_element_type=jnp.float32)
            up = jnp.dot(xb, wu_b[...], preferred_element_type=jnp.float32)
            hid = (gate * _sigmoid(gate) * up).astype(jnp.bfloat16)
            y = jnp.dot(hid, wd_b[...], preferred_element_type=jnp.float32)
            half = y.shape[1] // 2
            _rows_to_tiles(y_ref.at[view, :], _pack_bf16_pair(y[:, :half], y[:, half:]))

        swiglu(0)

        @pl.when(rows_here > hb)
        def _():
            swiglu(1)

        @pl.when(rows_here <= hb)
        def _():
            y_ref[pl.ds(hb * SUBLANES, hb * SUBLANES), :] = jnp.zeros((hb * SUBLANES, LANES), y_ref.dtype)

        @pl.when(is_last & has_next)
        def _():
            land(after)

    @pl.when((i > 0) & (blk >= nact))
    def _():
        y_ref[...] = jnp.zeros_like(y_ref)


def _experts(block_expert, sched, xb, w_gate, w_up, w_down, bm):
    p_rows = xb.shape[0] // SUBLANES
    d = w_gate.shape[1]
    de = w_gate.shape[-1]

    def x_map(i, be, sc):
        return (jnp.clip(i - 1, 0, sc[0] - 1), 0)

    grid_spec = pltpu.PrefetchScalarGridSpec(
        num_scalar_prefetch=2,
        grid=(p_rows // bm + 1,),
        in_specs=[pl.BlockSpec((bm * SUBLANES, LANES), x_map),
                  pl.BlockSpec(memory_space=pl.ANY),
                  pl.BlockSpec(memory_space=pl.ANY),
                  pl.BlockSpec(memory_space=pl.ANY)],
        out_specs=pl.BlockSpec((bm * SUBLANES, LANES), lambda i, be, sc: (jnp.maximum(i - 1, 0), 0)),
        scratch_shapes=[pltpu.VMEM((d, de), jnp.float32), pltpu.VMEM((d, de), jnp.float32),
                        pltpu.VMEM((de, d), jnp.float32),
                        pltpu.VMEM((d, de), jnp.bfloat16), pltpu.VMEM((d, de), jnp.bfloat16),
                        pltpu.VMEM((de, d), jnp.bfloat16),
                        pltpu.SemaphoreType.DMA(())],
    )
    return pl.pallas_call(
        _expert_kernel,
        grid_spec=grid_spec,
        out_shape=jax.ShapeDtypeStruct(xb.shape, jnp.uint32),
        compiler_params=_params("arbitrary"),
        name="moe_experts",
    )(block_expert, sched, xb, w_gate, w_up, w_down)


def _combine_kernel(dest_ref, dest_next_ref, h_ref, wts_ref, g_ref, y_ref, o_ref, ybuf, sem):
    i = pl.program_id(0)
    tm = h_ref.shape[0]
    slot = i % 2

    def gather(idx_ref, which):
        for r in range(tm):
            for k in range(2):
                src = pl.multiple_of(idx_ref[0, k * tm + r] * SUBLANES, SUBLANES)
                pltpu.make_async_copy(y_ref.at[pl.ds(src, SUBLANES), :],
                                      ybuf.at[which, k, pl.ds(r * SUBLANES, SUBLANES), :], sem.at[which]).start(priority=k)

    def wait_tile(which):
        for k in range(2):
            pltpu.make_async_copy(y_ref.at[pl.ds(0, tm * SUBLANES), :], ybuf.at[which, k], sem.at[which]).wait()

    @pl.when(i == 0)
    def _():
        gather(dest_ref, slot)

    wait_tile(slot)
    gather(dest_next_ref, 1 - slot)

    w = wts_ref[...]
    half = h_ref.shape[1] // 2
    lo_parts, hi_parts = [], []
    for c, (c0, c1) in enumerate(zip(_tiles_to_row_chunks(ybuf.at[slot, 0], tm),
                                      _tiles_to_row_chunks(ybuf.at[slot, 1], tm))):
        a0, b0 = _unpack_bf16_pair(c0)
        a1, b1 = _unpack_bf16_pair(c1)
        lo_parts.append(h_ref[:, c * LANES:(c + 1) * LANES] + w[:, 0:1] * a0 + w[:, 1:2] * a1)
        hi_parts.append(h_ref[:, half + c * LANES:half + (c + 1) * LANES] + w[:, 0:1] * b0 + w[:, 1:2] * b1)
    h3 = jnp.concatenate(lo_parts + hi_parts, axis=1)
    o_ref[...] = h3 * _rms_scale(h3, NORM_EPS) * g_ref[...]

    @pl.when(i == pl.num_programs(0) - 1)
    def _():
        wait_tile(1 - slot)


def _combine(dest3, h2, wts, g_final, y, tm):
    n_tok, d = h2.shape
    last = n_tok // tm - 1
    return pl.pallas_call(
        _combine_kernel,
        grid=(n_tok // tm,),
        in_specs=[pl.BlockSpec((None, 1, 2 * tm), lambda i: (i, 0, 0), memory_space=pltpu.SMEM),
                  pl.BlockSpec((None, 1, 2 * tm), lambda i: (jnp.minimum(i + 1, last), 0, 0),
                               memory_space=pltpu.SMEM),
                  pl.BlockSpec((tm, d), lambda i: (i, 0)),
                  pl.BlockSpec((tm, 2), lambda i: (i, 0)),
                  _resident(g_final.shape),
                  pl.BlockSpec(memory_space=pl.ANY)],
        out_specs=pl.BlockSpec((tm, d), lambda i: (i, 0)),
        out_shape=jax.ShapeDtypeStruct((n_tok, d), jnp.float32),
        scratch_shapes=[pltpu.VMEM((2, 2, tm * SUBLANES, LANES), jnp.uint32), pltpu.SemaphoreType.DMA((2,))],
        compiler_params=_params("arbitrary"),
        name="moe_combine",
    )(dest3, dest3, h2, wts, g_final, y)


def _transpose_w_in(w_in):
    return jnp.swapaxes(w_in, 0, 1).astype(jnp.bfloat16)


def _split_w_uq(w_uq):
    half = MLA_ROPE_DIM // 2
    w = w_uq.reshape(MLA_Q_RANK, MLA_HEADS, MLA_QK_DIM).transpose(1, 0, 2)
    pe = w[:, :, MLA_NOPE_DIM:]
    pe_swapped = jnp.concatenate([pe[:, :, half:], pe[:, :, :half]], axis=2)
    return jnp.concatenate([w, pe_swapped], axis=2).astype(jnp.bfloat16)


def _split_w_ukv(w_ukv):
    w = w_ukv.reshape(MLA_KV_RANK, MLA_HEADS, MLA_NOPE_DIM + MLA_V_DIM)
    wuk = w[:, :, :MLA_NOPE_DIM].transpose(1, 0, 2).astype(jnp.bfloat16)
    wuv = w[:, :, MLA_NOPE_DIM:].reshape(MLA_KV_RANK, MLA_HEADS * MLA_V_DIM).astype(jnp.bfloat16)
    return wuk, wuv


def kernel(x, mem, positions, attn_norm_g, w_in, diff_lambda_q1, diff_lambda_k1, diff_lambda_q2, diff_lambda_k2, diff_subln_g, w_o_diff, mla_q_norm_g, w_uq, mla_kv_norm_g, w_ukv, w_o_mla, w_out, cross_norm_g, mem_norm_g, w_cq, w_ckv, w_co, ffn_norm_g, w_router_group, b_router_group, w_router_expert, b_router_expert, w_expert_gate, w_expert_up, w_expert_down, final_norm_g):
    batch, seq, d = x.shape
    assert d == D_MODEL and w_in.shape[0] == 1, "single-layer kernel"
    n_tok = batch * seq
    bf = jnp.bfloat16
    x2 = x.reshape(n_tok, d)

    tm_proj = min(1024, seq)
    tm_row = min(256, seq)
    tm_moe = min(MOE_ROWS_PER_BLOCK, seq)
    tm_cross = min(512, seq)
    tq = min(2048, seq)
    rg_diff = 128
    rg_mla = 256

    cos_t, sin_t = _rope_tables(positions, n_tok, tm_proj)

    g_attn = attn_norm_g[0].reshape(1, d)
    qkv, latent, gates = _inproj(x2, g_attn, _transpose_w_in(w_in[0]), cos_t, sin_t, tm_proj)

    o_a = _diff_attn(qkv.reshape(batch, seq, QKV_COLS),
                     diff_lambda_q1[0].reshape(1, -1), diff_lambda_k1[0].reshape(1, -1),
                     diff_lambda_q2[0].reshape(1, -1), diff_lambda_k2[0].reshape(1, -1),
                     diff_subln_g[0].reshape(1, -1), tq, rg_diff)

    wuk, wuv = _split_w_ukv(w_ukv[0])
    q_cat, k_cat, v_mla = _mla_proj(latent, mla_q_norm_g[0].reshape(1, -1), mla_kv_norm_g[0].reshape(1, -1),
                                    _split_w_uq(w_uq[0]), wuk, wuv, cos_t, sin_t, batch, seq, tm_proj)
    o_b = _mla_attn(q_cat, k_cat, v_mla.reshape(batch, seq, MLA_HEADS * MLA_V_DIM), tq, rg_mla)

    h1 = _merge_out(o_a.reshape(n_tok, -1), o_b.reshape(n_tok, -1), gates, x2,
                    w_o_diff[0].astype(bf), w_o_mla[0].astype(bf), w_out[0].astype(bf), tm_cross)

    kv_mem = _mem_kv(mem, mem_norm_g[0].reshape(1, d), w_ckv[0].astype(bf))
    n_router = N_GROUPS + N_EXPERTS
    w_r = jnp.concatenate([w_router_group[0].astype(jnp.float32), w_router_expert[0].astype(jnp.float32),
                           jnp.zeros((d, LANES - n_router), jnp.float32)], axis=1)
    w_r_hi = w_r.astype(bf)
    w_r_lo = (w_r - w_r_hi.astype(jnp.float32)).astype(bf)
    w_r = jnp.concatenate([w_r_hi, w_r_lo], axis=1)
    b_r = jnp.concatenate([b_router_group[0].astype(jnp.float32), b_router_expert[0].astype(jnp.float32),
                           jnp.zeros((LANES - n_router,), jnp.float32)]).reshape(1, LANES)
    g_ffn = ffn_norm_g[0].reshape(1, d)
    h2, eid, rank, wts, cnt = _cross_router(h1, cross_norm_g[0].reshape(1, d), w_cq[0].astype(bf), kv_mem,
                                            w_co[0].astype(bf), g_ffn, w_r, b_r, seq, tm_cross)

    bm = MOE_ROWS_PER_BLOCK
    assert bm % tm_row == 0 and tm_moe == bm
    counts = cnt[0, ROUTER_EXPERT_LANE0:ROUTER_EXPERT_LANE0 + N_EXPERTS].astype(jnp.int32)
    padded = ((counts + bm - 1) // bm) * bm
    padded_end = jnp.cumsum(padded)
    padded_off = padded_end - padded
    seg_start = jnp.sum(jnp.where(eid[..., None] == jnp.arange(N_EXPERTS, dtype=jnp.int32), padded_off, 0), axis=-1)
    dest = seg_start + rank
    p_rows = ((2 * n_tok + bm - 1) // bm) * bm + N_EXPERTS * bm
    n_blocks = p_rows // bm
    n_active = (padded_end[-1] // bm).astype(jnp.int32)
    blk = jnp.minimum(jnp.arange(n_blocks, dtype=jnp.int32), n_active - 1)
    block_expert = jnp.sum((padded_end[None, :] <= (blk * bm)[:, None]).astype(jnp.int32), axis=1)
    block_expert = jnp.minimum(block_expert, N_EXPERTS - 1)
    def tile_slots(tm):
        return dest.reshape(-1, 2, tm_cross // tm, tm).transpose(0, 2, 1, 3).reshape(n_tok // tm, 1, 2 * tm)
    unused = n_active + jnp.arange(N_EXPERTS, dtype=jnp.int32)
    ztail = jnp.stack([jnp.concatenate([jnp.maximum(padded_end - bm, 0), jnp.minimum(unused, n_blocks - 1) * bm]),
                       jnp.concatenate([padded > 0, unused < n_blocks]).astype(jnp.int32)]).astype(jnp.int32)
    sched = jnp.concatenate([n_active.reshape(1), padded // bm, padded_off // bm, counts]).astype(jnp.int32)

    xb = _dispatch(tile_slots(tm_row), ztail, h2, g_ffn, p_rows, tm_row)
    y = _experts(block_expert, sched, xb, w_expert_gate[0], w_expert_up[0], w_expert_down[0], bm)
    out = _combine(tile_slots(tm_moe), h2, wts, final_norm_g.reshape(1, d), y, tm_moe)
    return out.reshape(batch, seq, d)
```

```python
import functools
import math

import jax
import jax.numpy as jnp
from jax import lax
from jax.experimental import pallas as pl
from jax.experimental.pallas import tpu as pltpu

D_MODEL = 2048
ROPE_THETA = 500000.0
NORM_EPS = 1e-6

DIFF_HEADS = 8
DIFF_HEAD_DIM = 64
DIFF_V_DIM = 2 * DIFF_HEAD_DIM
DIFF_ROT = DIFF_HEAD_DIM // 4
DIFF_SUBLN_EPS = 1e-5
DIFF_LAMBDA_INIT = 0.8 - 0.6 * math.exp(-0.3 * 0)

MLA_HEADS = 8
MLA_Q_RANK = 512
MLA_KV_RANK = 256
MLA_NOPE_DIM = 128
MLA_ROPE_DIM = 64
MLA_V_DIM = 128
MLA_QK_DIM = MLA_NOPE_DIM + MLA_ROPE_DIM

CROSS_HEADS = 4
CROSS_HEAD_DIM = 128

N_GROUPS = 4
EXPERTS_PER_GROUP = 8
N_EXPERTS = N_GROUPS * EXPERTS_PER_GROUP

LANES = 128
BF16_TILE_ROWS = 16
LOG2E = 1.4426950408889634
VMEM_LIMIT_BYTES = 56 * 1024 * 1024

ROUTER_EXPERT_LANE0 = N_GROUPS

QKV_COLS = 3 * DIFF_HEADS * DIFF_V_DIM
LATENT_COLS = 1024
GATE_COLS = 2 * D_MODEL
KPE_COL0 = MLA_Q_RANK + MLA_KV_RANK

MOE_ROWS_PER_BLOCK = 512
SUBLANES = 8


def _params(*semantics):
    return pltpu.CompilerParams(dimension_semantics=semantics, vmem_limit_bytes=VMEM_LIMIT_BYTES)


def _resident(shape):
    zeros = (0,) * len(shape)
    return pl.BlockSpec(shape, lambda *_: zeros, pipeline_mode=pl.Buffered(1))


def _rms_scale(xf, eps):
    return lax.rsqrt(jnp.mean(xf * xf, axis=-1, keepdims=True) + eps)


def _sigmoid(x):
    return 0.5 * jnp.tanh(0.5 * x) + 0.5


def _pack_bf16_pair(a, b):
    hi = lax.bitcast_convert_type(a.astype(jnp.bfloat16).astype(jnp.float32), jnp.uint32)
    lo = lax.bitcast_convert_type(b.astype(jnp.bfloat16).astype(jnp.float32), jnp.uint32)
    return hi | (lo >> 16)


def _unpack_bf16_pair(w):
    a = lax.bitcast_convert_type(w & jnp.uint32(0xFFFF0000), jnp.float32)
    b = lax.bitcast_convert_type(w << 16, jnp.float32)
    return a, b


def _rows_to_tiles(ref_view, packed):
    rows = packed.shape[0]
    for c in range(SUBLANES):
        ref_view[pl.ds(c, rows, stride=SUBLANES), :] = packed[:, c * LANES:(c + 1) * LANES]


def _tiles_to_row_chunks(ref_view, rows):
    return [ref_view[pl.ds(c, rows, stride=SUBLANES), :] for c in range(SUBLANES)]


def _lane_iota(shape):
    return lax.broadcasted_iota(jnp.int32, shape, len(shape) - 1)


def _trig_kernel(pos_ref, invf_ref, cos_ref, sin_ref):
    ang = pos_ref[...].astype(jnp.float32) * invf_ref[...]
    cos_ref[...] = jnp.cos(ang)
    sin_ref[...] = jnp.sin(ang)


def _rope_tables(positions, n_tok, tm):
    half_m = MLA_ROPE_DIM // 2
    half_d = DIFF_ROT // 2
    inv_m = jnp.float32(ROPE_THETA) ** (-jnp.arange(half_m, dtype=jnp.float32) * 2.0 / MLA_ROPE_DIM)
    inv_d = jnp.float32(ROPE_THETA) ** (-jnp.arange(half_d, dtype=jnp.float32) * 2.0 / DIFF_ROT)
    invf = jnp.concatenate([inv_m, inv_m, inv_d, inv_d,
                            jnp.zeros((DIFF_HEAD_DIM - DIFF_ROT,), jnp.float32)]).reshape(1, LANES)
    pos = positions.reshape(n_tok, 1)
    return pl.pallas_call(
        _trig_kernel,
        grid=(n_tok // tm,),
        in_specs=[pl.BlockSpec((tm, 1), lambda i: (i, 0)), _resident((1, LANES))],
        out_specs=[pl.BlockSpec((tm, LANES), lambda i: (i, 0))] * 2,
        out_shape=[jax.ShapeDtypeStruct((n_tok, LANES), jnp.float32)] * 2,
        compiler_params=_params("parallel"),
        name="rope_tables",
    )(pos, invf)


def _diff_rope_coeffs(cos_t, sin_t):
    lane = _lane_iota(cos_t.shape)
    upper = lane >= DIFF_HEAD_DIM
    cos_d = jnp.where(upper, cos_t, pltpu.roll(cos_t, DIFF_HEAD_DIM, 1))
    sin_d = jnp.where(upper, sin_t, pltpu.roll(sin_t, DIFF_HEAD_DIM, 1))
    in_head = lane % DIFF_HEAD_DIM
    half = DIFF_ROT // 2
    s_next = jnp.where(in_head < half, -sin_d, 0.0)
    s_prev = jnp.where((in_head >= half) & (in_head < DIFF_ROT), sin_d, 0.0)
    return cos_d, s_next, s_prev


def _mla_rope(pair, cos_t, sin_t):
    lane = _lane_iota(pair.shape)
    sin_signed = jnp.where(lane < MLA_ROPE_DIM // 2, -sin_t, sin_t)
    return pair * cos_t + pltpu.roll(pair, MLA_ROPE_DIM, 1) * sin_signed


INPROJ_TN = 1024
INPROJ_PIECE = 256
Q_TILES = DIFF_HEADS * DIFF_V_DIM // INPROJ_TN
ROPE_TILES = 2 * Q_TILES
QKV_TILES = QKV_COLS // INPROJ_TN
LATENT_TILES = LATENT_COLS // INPROJ_TN
GATE_TILES = GATE_COLS // INPROJ_TN
INPROJ_TILES = QKV_TILES + LATENT_TILES + GATE_TILES
GATE_ROW0 = QKV_COLS + KPE_COL0 + MLA_ROPE_DIM


def _inproj_kernel(x_ref, g_ref, w_ref, cos_ref, sin_ref, qkv_ref, lat_ref, gate_ref, xn_ref):
    j = pl.program_id(1)

    @pl.when(j == 0)
    def _():
        xf = x_ref[...]
        xn_ref[...] = (xf * _rms_scale(xf, NORM_EPS) * g_ref[...]).astype(jnp.bfloat16)

    def pieces(epilogue):
        for c in range(INPROJ_TN // INPROJ_PIECE):
            cols = slice(c * INPROJ_PIECE, (c + 1) * INPROJ_PIECE)
            acc = lax.dot_general(xn_ref[...], w_ref[cols, :], (((1,), (1,)), ((), ())),
                                  preferred_element_type=jnp.float32)
            epilogue(acc, cols)

    @pl.when(j < ROPE_TILES)
    def _():
        cos_d, s_next, s_prev = _diff_rope_coeffs(cos_ref[...], sin_ref[...])
        qscale = jnp.where(j < Q_TILES, DIFF_HEAD_DIM ** -0.5 * LOG2E, 1.0).astype(jnp.float32)

        def rope(acc, cols):
            for c in range(INPROJ_PIECE // LANES):
                xc = acc[:, c * LANES:(c + 1) * LANES]
                rot = (xc * cos_d + pltpu.roll(xc, LANES - DIFF_ROT // 2, 1) * s_next
                       + pltpu.roll(xc, DIFF_ROT // 2, 1) * s_prev)
                lo = cols.start + c * LANES
                qkv_ref[:, lo:lo + LANES] = (rot * qscale).astype(qkv_ref.dtype)

        pieces(rope)

    @pl.when((j >= ROPE_TILES) & (j < QKV_TILES))
    def _():
        def value(acc, cols):
            qkv_ref[:, cols] = acc.astype(qkv_ref.dtype)

        pieces(value)

    @pl.when((j >= QKV_TILES) & (j < QKV_TILES + LATENT_TILES))
    def _():
        def latent(acc, cols):
            if cols.start <= KPE_COL0 < cols.stop:
                c0 = KPE_COL0 - cols.start
                v = acc[:, c0:c0 + LANES]
                lane = _lane_iota(v.shape)
                half = MLA_ROPE_DIM // 2
                swapped = jnp.where(lane < MLA_ROPE_DIM + half, pltpu.roll(v, half, 1),
                                    pltpu.roll(v, MLA_ROPE_DIM + half, 1))
                parts = [acc[:, :c0], jnp.where(lane < MLA_ROPE_DIM, v, swapped), acc[:, c0 + LANES:]]
                acc = jnp.concatenate([p for p in parts if p.shape[1]], axis=1)
            lat_ref[:, cols] = acc

        pieces(latent)

    @pl.when(j >= QKV_TILES + LATENT_TILES)
    def _():
        def gate(acc, cols):
            gate_ref[:, cols] = _sigmoid(acc).astype(gate_ref.dtype)

        pieces(gate)


def _inproj(x2, g, w_all, cos_t, sin_t, tm):
    n_tok, d = x2.shape
    tn = INPROJ_TN
    lat0 = QKV_TILES
    gate0 = QKV_TILES + LATENT_TILES

    def w_rows(i, j):
        start = jnp.where(j < gate0, j * tn, GATE_ROW0 + (j - gate0) * tn)
        return (pl.multiple_of(start, BF16_TILE_ROWS), 0)

    assert GATE_ROW0 % BF16_TILE_ROWS == 0
    return pl.pallas_call(
        _inproj_kernel,
        grid=(n_tok // tm, INPROJ_TILES),
        in_specs=[pl.BlockSpec((tm, d), lambda i, j: (i, 0)),
                  _resident((1, d)),
                  pl.BlockSpec((pl.Element(tn), pl.Element(d)), w_rows),
                  pl.BlockSpec((tm, LANES), lambda i, j: (i, 0)),
                  pl.BlockSpec((tm, LANES), lambda i, j: (i, 0))],
        out_specs=[pl.BlockSpec((tm, tn), lambda i, j: (i, jnp.clip(j, 0, QKV_TILES - 1))),
                   pl.BlockSpec((tm, tn), lambda i, j: (i, jnp.clip(j - lat0, 0, LATENT_TILES - 1))),
                   pl.BlockSpec((tm, tn), lambda i, j: (i, jnp.clip(j - gate0, 0, GATE_TILES - 1)))],
        out_shape=[jax.ShapeDtypeStruct((n_tok, QKV_COLS), jnp.bfloat16),
                   jax.ShapeDtypeStruct((n_tok, LATENT_COLS), jnp.float32),
                   jax.ShapeDtypeStruct((n_tok, GATE_COLS), jnp.bfloat16)],
        scratch_shapes=[pltpu.VMEM((tm, d), jnp.bfloat16)],
        compiler_params=_params("parallel", "arbitrary"),
        name="inproj",
    )(x2, g, w_all, cos_t, sin_t)


def _mla_proj_kernel(c_ref, gq_ref, gkv_ref, wuq_ref, wuk_ref, wuv_ref, cos_ref, sin_ref,
                     q_ref, k_ref, v_ref):
    cos_t = cos_ref[...]
    sin_t = sin_ref[...]
    cq = c_ref[:, :MLA_Q_RANK]
    cqn = (cq * _rms_scale(cq, NORM_EPS) * gq_ref[...]).astype(jnp.bfloat16)
    ckv = c_ref[:, MLA_Q_RANK:KPE_COL0]
    ckvn = (ckv * _rms_scale(ckv, NORM_EPS) * gkv_ref[...]).astype(jnp.bfloat16)
    kpe = _mla_rope(c_ref[:, KPE_COL0:KPE_COL0 + LANES], cos_t, sin_t)[:, :MLA_ROPE_DIM].astype(k_ref.dtype)
    qscale = MLA_QK_DIM ** -0.5 * LOG2E
    for h in range(MLA_HEADS):
        r = jnp.dot(cqn, wuq_ref[h], preferred_element_type=jnp.float32)
        q_ref[0, h, :, :MLA_NOPE_DIM] = (r[:, :MLA_NOPE_DIM] * qscale).astype(q_ref.dtype)
        qpe = _mla_rope(r[:, MLA_NOPE_DIM:], cos_t, sin_t)[:, :MLA_ROPE_DIM]
        q_ref[0, h, :, MLA_NOPE_DIM:] = (qpe * qscale).astype(q_ref.dtype)
        kn = jnp.dot(ckvn, wuk_ref[h], preferred_element_type=jnp.float32)
        k_ref[0, h, :, :MLA_NOPE_DIM] = kn.astype(k_ref.dtype)
        k_ref[0, h, :, MLA_NOPE_DIM:] = kpe
    v_ref[...] = jnp.dot(ckvn, wuv_ref[...], preferred_element_type=jnp.float32).astype(v_ref.dtype)


def _mla_proj(latent, gq, gkv, wuq, wuk, wuv, cos_t, sin_t, batch, seq, tm):
    n_tok = latent.shape[0]
    per_b = seq // tm
    head_spec = pl.BlockSpec((1, MLA_HEADS, tm, MLA_QK_DIM), lambda i: (i // per_b, 0, i % per_b, 0))
    head_shape = jax.ShapeDtypeStruct((batch, MLA_HEADS, seq, MLA_QK_DIM), jnp.bfloat16)
    return pl.pallas_call(
        _mla_proj_kernel,
        grid=(n_tok // tm,),
        in_specs=[pl.BlockSpec((tm, LATENT_COLS), lambda i: (i, 0)),
                  _resident(gq.shape), _resident(gkv.shape),
                  _resident(wuq.shape), _resident(wuk.shape), _resident(wuv.shape),
                  pl.BlockSpec((tm, LANES), lambda i: (i, 0)),
                  pl.BlockSpec((tm, LANES), lambda i: (i, 0))],
        out_specs=[head_spec, head_spec,
                   pl.BlockSpec((tm, MLA_HEADS * MLA_V_DIM), lambda i: (i, 0))],
        out_shape=[head_shape, head_shape,
                   jax.ShapeDtypeStruct((n_tok, MLA_HEADS * MLA_V_DIM), jnp.bfloat16)],
        compiler_params=_params("parallel"),
        name="mla_proj",
    )(latent, gq, gkv, wuq, wuk, wuv, cos_t, sin_t)


def _with_ones(v):
    return jnp.concatenate([v, jnp.ones((v.shape[0], LANES), v.dtype)], axis=-1)


def _softmax_pv(s, v_ones):
    m = jnp.max(s, axis=-1, keepdims=True)
    p = jnp.exp2(s - m).astype(v_ones.dtype)
    pv = jnp.dot(p, v_ones, preferred_element_type=jnp.float32)
    dv = v_ones.shape[1] - LANES
    return pv[:, :dv] / pv[:, dv:]


ATTN_HEADS_PER_STEP = 2


def _diff_attn_kernel(q_ref, k_ref, v_ref, lq1_ref, lk1_ref, lq2_ref, lk2_ref, g_ref, o_ref, v1_ref, *, rg):
    heads = q_ref.shape[2] // LANES

    @pl.when(pl.program_id(2) == 0)
    def _():
        for h in range(heads):
            v1_ref[h] = _with_ones(v_ref[0, :, h * DIFF_V_DIM:(h + 1) * DIFF_V_DIM])

    lam = (jnp.exp(jnp.sum(lq1_ref[...] * lk1_ref[...], axis=-1, keepdims=True))
           - jnp.exp(jnp.sum(lq2_ref[...] * lk2_ref[...], axis=-1, keepdims=True))
           + DIFF_LAMBDA_INIT)
    lane = _lane_iota((rg, LANES))
    for g in range(q_ref.shape[1] // rg):
        for h in range(heads):
            cols = slice(h * LANES, (h + 1) * LANES)
            q = q_ref[0, g * rg:(g + 1) * rg, cols]
            k = k_ref[0, :, cols]
            zero = jnp.zeros_like(q)
            q12 = jnp.concatenate([jnp.where(lane < DIFF_HEAD_DIM, q, zero),
                                   jnp.where(lane >= DIFF_HEAD_DIM, q, zero)], axis=0)
            s = lax.dot_general(q12, k, (((1,), (1,)), ((), ())), preferred_element_type=jnp.float32)
            a = _softmax_pv(s, v1_ref[h])
            o = a[:rg] - lam * a[rg:]
            o = o * _rms_scale(o, DIFF_SUBLN_EPS) * g_ref[...] * (1.0 - DIFF_LAMBDA_INIT)
            o_ref[0, g * rg:(g + 1) * rg, cols] = o.astype(o_ref.dtype)


def _diff_attn(qkv3, lq1, lk1, lq2, lk2, subln_g, tq, rg):
    batch, seq, _ = qkv3.shape
    hp = ATTN_HEADS_PER_STEP
    steps = DIFF_HEADS // hp
    width = hp * DIFF_V_DIM
    return pl.pallas_call(
        functools.partial(_diff_attn_kernel, rg=rg),
        grid=(batch, steps, seq // tq),
        in_specs=[pl.BlockSpec((1, tq, width), lambda b, hh, i: (b, i, hh)),
                  pl.BlockSpec((1, seq, width), lambda b, hh, i: (b, 0, steps + hh)),
                  pl.BlockSpec((1, seq, width), lambda b, hh, i: (b, 0, 2 * steps + hh)),
                  _resident(lq1.shape), _resident(lk1.shape), _resident(lq2.shape), _resident(lk2.shape),
                  _resident(subln_g.shape)],
        out_specs=pl.BlockSpec((1, tq, width), lambda b, hh, i: (b, i, hh)),
        out_shape=jax.ShapeDtypeStruct((batch, seq, DIFF_HEADS * DIFF_V_DIM), jnp.bfloat16),
        scratch_shapes=[pltpu.VMEM((hp, seq, DIFF_V_DIM + LANES), jnp.bfloat16)],
        compiler_params=_params("parallel", "parallel", "arbitrary"),
        name="diff_attn",
    )(qkv3, qkv3, qkv3, lq1, lk1, lq2, lk2, subln_g)


def _mla_attn_kernel(q_ref, k_ref, v_ref, o_ref, v1_ref, *, rg):
    heads = q_ref.shape[1]

    @pl.when(pl.program_id(2) == 0)
    def _():
        for h in range(heads):
            v1_ref[h] = _with_ones(v_ref[0, :, h * MLA_V_DIM:(h + 1) * MLA_V_DIM])

    for g in range(q_ref.shape[2] // rg):
        for h in range(heads):
            s = lax.dot_general(q_ref[0, h, g * rg:(g + 1) * rg], k_ref[0, h], (((1,), (1,)), ((), ())),
                                preferred_element_type=jnp.float32)
            o_ref[0, g * rg:(g + 1) * rg, h * MLA_V_DIM:(h + 1) * MLA_V_DIM] = (
                _softmax_pv(s, v1_ref[h]).astype(o_ref.dtype))


def _mla_attn(q_cat, k_cat, v3, tq, rg):
    batch, heads, seq, dqk = q_cat.shape
    hp = ATTN_HEADS_PER_STEP
    return pl.pallas_call(
        functools.partial(_mla_attn_kernel, rg=rg),
        grid=(batch, heads // hp, seq // tq),
        in_specs=[pl.BlockSpec((1, hp, tq, dqk), lambda b, h, i: (b, h, i, 0)),
                  pl.BlockSpec((1, hp, seq, dqk), lambda b, h, i: (b, h, 0, 0)),
                  pl.BlockSpec((1, seq, hp * MLA_V_DIM), lambda b, h, i: (b, 0, h))],
        out_specs=pl.BlockSpec((1, tq, hp * MLA_V_DIM), lambda b, h, i: (b, i, h)),
        out_shape=jax.ShapeDtypeStruct((batch, seq, heads * MLA_V_DIM), jnp.bfloat16),
        scratch_shapes=[pltpu.VMEM((hp, seq, MLA_V_DIM + LANES), jnp.bfloat16)],
        compiler_params=_params("parallel", "parallel", "arbitrary"),
        name="mla_attn",
    )(q_cat, k_cat, v3)


def _merge_out_kernel(oa_ref, ob_ref, sga_ref, sgb_ref, x_ref, woa_ref, wob_ref, wout_ref, h_ref):
    ya = jnp.dot(oa_ref[...], woa_ref[...], preferred_element_type=jnp.float32)
    yb = jnp.dot(ob_ref[...], wob_ref[...], preferred_element_type=jnp.float32)
    merged = sga_ref[...].astype(jnp.float32) * ya + sgb_ref[...].astype(jnp.float32) * yb
    h_ref[...] = x_ref[...] + jnp.dot(merged.astype(jnp.bfloat16), wout_ref[...],
                                       preferred_element_type=jnp.float32)


def _merge_out(o_a, o_b, gates, x2, w_oa, w_ob, w_out, tm):
    n_tok, d = x2.shape
    return pl.pallas_call(
        _merge_out_kernel,
        grid=(n_tok // tm,),
        in_specs=[pl.BlockSpec((tm, o_a.shape[1]), lambda i: (i, 0)),
                  pl.BlockSpec((tm, o_b.shape[1]), lambda i: (i, 0)),
                  pl.BlockSpec((tm, d), lambda i: (i, 0)),
                  pl.BlockSpec((tm, d), lambda i: (i, 1)),
                  pl.BlockSpec((tm, d), lambda i: (i, 0)),
                  _resident(w_oa.shape), _resident(w_ob.shape), _resident(w_out.shape)],
        out_specs=pl.BlockSpec((tm, d), lambda i: (i, 0)),
        out_shape=jax.ShapeDtypeStruct((n_tok, d), jnp.float32),
        compiler_params=_params("parallel"),
        name="merge_out",
    )(o_a, o_b, gates, gates, x2, w_oa, w_ob, w_out)


def _mem_kv_kernel(mem_ref, g_ref, w_ref, kv_ref):
    mf = mem_ref[0]
    mn = (mf * _rms_scale(mf, NORM_EPS) * g_ref[...]).astype(jnp.bfloat16)
    kv_ref[0] = jnp.dot(mn, w_ref[...], preferred_element_type=jnp.float32).astype(kv_ref.dtype)


def _mem_kv(mem, g, w_ckv):
    batch, m, d = mem.shape
    return pl.pallas_call(
        _mem_kv_kernel,
        grid=(batch,),
        in_specs=[pl.BlockSpec((1, m, d), lambda b: (b, 0, 0)), _resident(g.shape), _resident(w_ckv.shape)],
        out_specs=pl.BlockSpec((1, m, w_ckv.shape[1]), lambda b: (b, 0, 0)),
        out_shape=jax.ShapeDtypeStruct((batch, m, w_ckv.shape[1]), jnp.bfloat16),
        compiler_params=_params("parallel"),
        name="mem_kv",
    )(mem, g, w_ckv)


def _cross_router_kernel(h_ref, gc_ref, wcq_ref, kv_ref, wco_ref, gf_ref, wr_ref, br_ref,
                         h2_ref, eid_ref, rank_ref, wts_ref, cnt_ref, carry_ref):
    i = pl.program_id(0)

    @pl.when(i == 0)
    def _():
        carry_ref[...] = jnp.zeros_like(carry_ref)

    h1 = h_ref[...]
    tm = h1.shape[0]
    hn = (h1 * _rms_scale(h1, NORM_EPS) * gc_ref[...]).astype(jnp.bfloat16)
    q = jnp.dot(hn, wcq_ref[...], preferred_element_type=jnp.float32) * (CROSS_HEAD_DIM ** -0.5 * LOG2E)
    q = q.astype(jnp.bfloat16)
    kv_cols = CROSS_HEADS * CROSS_HEAD_DIM
    heads = []
    for hd in range(CROSS_HEADS):
        lo = hd * CROSS_HEAD_DIM
        kh = kv_ref[0, :, lo:lo + CROSS_HEAD_DIM]
        vh = kv_ref[0, :, kv_cols + lo:kv_cols + lo + CROSS_HEAD_DIM]
        s = lax.dot_general(q[:, lo:lo + CROSS_HEAD_DIM], kh, (((1,), (1,)), ((), ())),
                            preferred_element_type=jnp.float32)
        heads.append(_softmax_pv(s, _with_ones(vh)).astype(jnp.bfloat16))
    o = jnp.concatenate(heads, axis=-1)
    h2 = h1 + jnp.dot(o, wco_ref[...], preferred_element_type=jnp.float32)
    h2_ref[...] = h2

    t = h2 * _rms_scale(h2, NORM_EPS) * gf_ref[...]
    t_hi = t.astype(jnp.bfloat16)
    t_lo = (t - t_hi.astype(jnp.float32)).astype(jnp.bfloat16)
    hi = jnp.dot(t_hi, wr_ref[...], preferred_element_type=jnp.float32)
    lo = jnp.dot(t_lo, wr_ref[:, :LANES], preferred_element_type=jnp.float32)
    logits = hi[:, :LANES] + (hi[:, LANES:] + lo) + br_ref[...]
    lane = _lane_iota(logits.shape)
    neg = jnp.float32(-jnp.inf)
    big = jnp.int32(2 * LANES)
    is_group = lane < N_GROUPS
    lg = jnp.where(is_group, logits, neg)
    mg = jnp.max(lg, axis=-1, keepdims=True)
    g_idx = jnp.min(jnp.where(is_group & (logits == mg), lane, big), axis=-1, keepdims=True)
    g_p = 1.0 / jnp.sum(jnp.exp(lg - mg), axis=-1, keepdims=True)
    lo_lane = ROUTER_EXPERT_LANE0 + EXPERTS_PER_GROUP * g_idx
    in_grp = (lane >= lo_lane) & (lane < lo_lane + EXPERTS_PER_GROUP)
    l1 = jnp.max(jnp.where(in_grp, logits, neg), axis=-1, keepdims=True)
    i1 = jnp.min(jnp.where(in_grp & (logits == l1), lane, big), axis=-1, keepdims=True)
    rest = in_grp & (lane != i1)
    l2 = jnp.max(jnp.where(rest, logits, neg), axis=-1, keepdims=True)
    i2 = jnp.min(jnp.where(rest & (logits == l2), lane, big), axis=-1, keepdims=True)
    d = jnp.exp(l2 - l1)
    w1 = g_p / (1.0 + d)
    w2 = w1 * d

    oh1 = lane == i1
    oh2 = lane == i2
    cnt = (oh1 | oh2).astype(jnp.bfloat16)
    row = lax.broadcasted_iota(jnp.int32, (tm, tm), 0)
    col = lax.broadcasted_iota(jnp.int32, (tm, tm), 1)
    before = (col < row).astype(jnp.bfloat16)
    slot = jnp.dot(before, cnt, preferred_element_type=jnp.float32) + carry_ref[...]
    r1 = jnp.sum(jnp.where(oh1, slot, 0.0), axis=-1, keepdims=True)
    r2 = jnp.sum(jnp.where(oh2, slot, 0.0), axis=-1, keepdims=True)
    carry_ref[...] += jnp.sum(cnt.astype(jnp.float32), axis=0, keepdims=True)
    cnt_ref[...] = carry_ref[...]

    eye = row == col

    def to_row(c, dtype):
        return jnp.sum(jnp.where(eye, c.astype(jnp.float32), 0.0), axis=0, keepdims=True).astype(dtype)

    eid_ref[0] = jnp.concatenate([to_row(i1 - ROUTER_EXPERT_LANE0, jnp.int32),
                                  to_row(i2 - ROUTER_EXPERT_LANE0, jnp.int32)], axis=0)
    rank_ref[0] = jnp.concatenate([to_row(r1, jnp.int32), to_row(r2, jnp.int32)], axis=0)
    wts_ref[...] = jnp.where(_lane_iota((tm, 2)) == 0, w1, w2)


def _cross_router(h1, gc, w_cq, kv_mem, w_co, gf, w_r, b_r, seq, tm):
    n_tok, d = h1.shape
    per_b = seq // tm
    row2 = pl.BlockSpec((tm, 2), lambda i: (i, 0))
    lane2 = pl.BlockSpec((1, 2, tm), lambda i: (i, 0, 0))
    return pl.pallas_call(
        _cross_router_kernel,
        grid=(n_tok // tm,),
        in_specs=[pl.BlockSpec((tm, d), lambda i: (i, 0)),
                  _resident(gc.shape), _resident(w_cq.shape),
                  pl.BlockSpec((1,) + kv_mem.shape[1:], lambda i: (i // per_b, 0, 0)),
                  _resident(w_co.shape), _resident(gf.shape), _resident(w_r.shape), _resident(b_r.shape)],
        out_specs=[pl.BlockSpec((tm, d), lambda i: (i, 0)), lane2, lane2, row2,
                   pl.BlockSpec((1, LANES), lambda i: (0, 0))],
        out_shape=[jax.ShapeDtypeStruct((n_tok, d), jnp.float32),
                   jax.ShapeDtypeStruct((n_tok // tm, 2, tm), jnp.int32),
                   jax.ShapeDtypeStruct((n_tok // tm, 2, tm), jnp.int32),
                   jax.ShapeDtypeStruct((n_tok, 2), jnp.float32),
                   jax.ShapeDtypeStruct((1, LANES), jnp.float32)],
        scratch_shapes=[pltpu.VMEM((1, LANES), jnp.float32)],
        compiler_params=_params("arbitrary"),
        name="cross_router",
    )(h1, gc, w_cq, kv_mem, w_co, gf, w_r, b_r)


def _dispatch_kernel(dest_ref, ztail_ref, h_ref, g_ref, xb_ref, t_ref, sem, zsem):
    i = pl.program_id(0)
    tm = h_ref.shape[0]
    slot = i % 2

    @pl.when(i == 0)
    def _():
        t_ref[1] = jnp.zeros(t_ref.shape[1:], t_ref.dtype)
        tile_rows = tm * SUBLANES

        pieces = MOE_ROWS_PER_BLOCK // tm

        def zero_copy(e, piece):
            start = pl.multiple_of(ztail_ref[0, e] * SUBLANES + piece * tile_rows, tile_rows)
            return pltpu.make_async_copy(t_ref.at[1], xb_ref.at[pl.ds(start, tile_rows), :], zsem)

        for e in range(ztail_ref.shape[1]):
            @pl.when(ztail_ref[1, e] > 0)
            def _():
                for piece in range(pieces):
                    zero_copy(e, piece).start()
        for e in range(ztail_ref.shape[1]):
            @pl.when(ztail_ref[1, e] > 0)
            def _():
                for piece in range(pieces):
                    zero_copy(e, piece).wait()

    h2 = h_ref[...]
    t = h2 * _rms_scale(h2, NORM_EPS) * g_ref[...]
    half = t.shape[1] // 2
    _rows_to_tiles(t_ref.at[slot], _pack_bf16_pair(t[:, :half], t[:, half:]))

    for r in range(tm):
        for k in range(2):
            dst = pl.multiple_of(dest_ref[0, k * tm + r] * SUBLANES, SUBLANES)
            pltpu.make_async_copy(t_ref.at[slot, pl.ds(r * SUBLANES, SUBLANES), :],
                                  xb_ref.at[pl.ds(dst, SUBLANES), :], sem.at[slot]).start(priority=k)

    def wait_tile(which):
        for _ in range(2):
            pltpu.make_async_copy(t_ref.at[which], xb_ref.at[pl.ds(0, tm * SUBLANES), :], sem.at[which]).wait()

    @pl.when(i > 0)
    def _():
        wait_tile(1 - slot)

    @pl.when(i == pl.num_programs(0) - 1)
    def _():
        wait_tile(slot)


def _dispatch(dest3, ztail, h2, gf, p_rows, tm):
    n_tok, d = h2.shape
    return pl.pallas_call(
        _dispatch_kernel,
        grid=(n_tok // tm,),
        in_specs=[pl.BlockSpec((None, 1, 2 * tm), lambda i: (i, 0, 0), memory_space=pltpu.SMEM),
                  pl.BlockSpec(memory_space=pltpu.SMEM),
                  pl.BlockSpec((tm, d), lambda i: (i, 0)),
                  _resident(gf.shape)],
        out_specs=pl.BlockSpec(memory_space=pl.ANY),
        out_shape=jax.ShapeDtypeStruct((p_rows * SUBLANES, LANES), jnp.uint32),
        scratch_shapes=[pltpu.VMEM((2, tm * SUBLANES, LANES), jnp.uint32), pltpu.SemaphoreType.DMA((2,)),
                        pltpu.SemaphoreType.DMA(())],
        compiler_params=_params("arbitrary"),
        name="moe_dispatch",
    )(dest3, ztail, h2, gf)


def _expert_kernel(be_ref, nact_ref, x_ref, wg_hbm, wu_hbm, wd_hbm, y_ref,
                   wg_f, wu_f, wd_f, wg_b, wu_b, wd_b, sem):
    i = pl.program_id(0)
    blk = i - 1
    nact = nact_ref[0]
    last_blk = pl.num_programs(0) - 2

    def weight_copies(e):
        return (pltpu.make_async_copy(wg_hbm.at[e], wg_f, sem),
                pltpu.make_async_copy(wu_hbm.at[e], wu_f, sem),
                pltpu.make_async_copy(wd_hbm.at[e], wd_f, sem))

    def fetch(e):
        for c in weight_copies(e):
            c.start()

    def land(e):
        for c in weight_copies(e):
            c.wait()
        half = wg_f.shape[0] // 2
        for c in range(SUBLANES):
            for part, src0 in enumerate((c * LANES, half + c * LANES)):
                dst0 = (2 * c + part) * LANES
                wg_b[dst0:dst0 + LANES, :] = wg_f[src0:src0 + LANES, :].astype(jnp.bfloat16)
                wu_b[dst0:dst0 + LANES, :] = wu_f[src0:src0 + LANES, :].astype(jnp.bfloat16)
        wd_b[...] = wd_f[...].astype(jnp.bfloat16)

    @pl.when(i == 0)
    def _():
        fetch(be_ref[0])
        land(be_ref[0])

    @pl.when((i > 0) & (blk < nact))
    def _():
        here = be_ref[blk]
        nxt = be_ref[jnp.minimum(blk + 1, last_blk)]
        prv = be_ref[jnp.maximum(blk - 1, 0)]
        seg_end = nact_ref[1 + N_EXPERTS + here] + nact_ref[1 + here]
        has_next = seg_end < nact
        after = be_ref[jnp.minimum(seg_end, last_blk)]
        is_first = (blk == 0) | (prv != here)
        is_last = (blk + 1 >= nact) | (nxt != here)

        @pl.when(is_first & has_next)
        def _():
            fetch(after)

        bm = x_ref.shape[0] // SUBLANES
        hb = bm // 2
        rows_here = nact_ref[1 + 2 * N_EXPERTS + here] - (blk - nact_ref[1 + N_EXPERTS + here]) * bm

        def swiglu(h):
            view = pl.ds(h * hb * SUBLANES, hb * SUBLANES)
            parts = []
            for chunk in _tiles_to_row_chunks(x_ref.at[view, :], hb):
                x_a, x_b = _unpack_bf16_pair(chunk)
                parts += [x_a.astype(jnp.bfloat16), x_b.astype(jnp.bfloat16)]
            xb = jnp.concatenate(parts, axis=1)
            gate = jnp.dot(xb, wg_b[...], preferred_element_type=jnp.float32)
            up = jnp.dot(xb, wu_b[...], preferred_element_type=jnp.float32)
            hid = (gate * _sigmoid(gate) * up).astype(jnp.bfloat16)
            y = jnp.dot(hid, wd_b[...], preferred_element_type=jnp.float32)
            half = y.shape[1] // 2
            _rows_to_tiles(y_ref.at[view, :], _pack_bf16_pair(y[:, :half], y[:, half:]))

        swiglu(0)

        @pl.when(rows_here > hb)
        def _():
            swiglu(1)

        @pl.when(rows_here <= hb)
        def _():
            y_ref[pl.ds(hb * SUBLANES, hb * SUBLANES), :] = jnp.zeros((hb * SUBLANES, LANES), y_ref.dtype)

        @pl.when(is_last & has_next)
        def _():
            land(after)

    @pl.when((i > 0) & (blk >= nact))
    def _():
        y_ref[...] = jnp.zeros_like(y_ref)


def _experts(block_expert, sched, xb, w_gate, w_up, w_down, bm):
    p_rows = xb.shape[0] // SUBLANES
    d = w_gate.shape[1]
    de = w_gate.shape[-1]

    def x_map(i, be, sc):
        return (jnp.clip(i - 1, 0, sc[0] - 1), 0)

    grid_spec = pltpu.PrefetchScalarGridSpec(
        num_scalar_prefetch=2,
        grid=(p_rows // bm + 1,),
        in_specs=[pl.BlockSpec((bm * SUBLANES, LANES), x_map),
                  pl.BlockSpec(memory_space=pl.ANY),
                  pl.BlockSpec(memory_space=pl.ANY),
                  pl.BlockSpec(memory_space=pl.ANY)],
        out_specs=pl.BlockSpec((bm * SUBLANES, LANES), lambda i, be, sc: (jnp.maximum(i - 1, 0), 0)),
        scratch_shapes=[pltpu.VMEM((d, de), jnp.float32), pltpu.VMEM((d, de), jnp.float32),
                        pltpu.VMEM((de, d), jnp.float32),
                        pltpu.VMEM((d, de), jnp.bfloat16), pltpu.VMEM((d, de), jnp.bfloat16),
                        pltpu.VMEM((de, d), jnp.bfloat16),
                        pltpu.SemaphoreType.DMA(())],
    )
    return pl.pallas_call(
        _expert_kernel,
        grid_spec=grid_spec,
        out_shape=jax.ShapeDtypeStruct(xb.shape, jnp.uint32),
        compiler_params=_params("arbitrary"),
        name="moe_experts",
    )(block_expert, sched, xb, w_gate, w_up, w_down)


def _combine_kernel(dest_ref, dest_next_ref, h_ref, wts_ref, g_ref, y_ref, o_ref, ybuf, sem):
    i = pl.program_id(0)
    tm = h_ref.shape[0]
    slot = i % 2

    def gather(idx_ref, which):
        for r in range(tm):
            for k in range(2):
                src = pl.multiple_of(idx_ref[0, k * tm + r] * SUBLANES, SUBLANES)
                pltpu.make_async_copy(y_ref.at[pl.ds(src, SUBLANES), :],
                                      ybuf.at[which, k, pl.ds(r * SUBLANES, SUBLANES), :], sem.at[which]).start(priority=k)

    def wait_tile(which):
        for k in range(2):
            pltpu.make_async_copy(y_ref.at[pl.ds(0, tm * SUBLANES), :], ybuf.at[which, k], sem.at[which]).wait()

    @pl.when(i == 0)
    def _():
        gather(dest_ref, slot)

    wait_tile(slot)
    gather(dest_next_ref, 1 - slot)

    w = wts_ref[...]
    half = h_ref.shape[1] // 2
    lo_parts, hi_parts = [], []
    for c, (c0, c1) in enumerate(zip(_tiles_to_row_chunks(ybuf.at[slot, 0], tm),
                                      _tiles_to_row_chunks(ybuf.at[slot, 1], tm))):
        a0, b0 = _unpack_bf16_pair(c0)
        a1, b1 = _unpack_bf16_pair(c1)
        lo_parts.append(h_ref[:, c * LANES:(c + 1) * LANES] + w[:, 0:1] * a0 + w[:, 1:2] * a1)
        hi_parts.append(h_ref[:, half + c * LANES:half + (c + 1) * LANES] + w[:, 0:1] * b0 + w[:, 1:2] * b1)
    h3 = jnp.concatenate(lo_parts + hi_parts, axis=1)
    o_ref[...] = h3 * _rms_scale(h3, NORM_EPS) * g_ref[...]

    @pl.when(i == pl.num_programs(0) - 1)
    def _():
        wait_tile(1 - slot)


def _combine(dest3, h2, wts, g_final, y, tm):
    n_tok, d = h2.shape
    last = n_tok // tm - 1
    return pl.pallas_call(
        _combine_kernel,
        grid=(n_tok // tm,),
        in_specs=[pl.BlockSpec((None, 1, 2 * tm), lambda i: (i, 0, 0), memory_space=pltpu.SMEM),
                  pl.BlockSpec((None, 1, 2 * tm), lambda i: (jnp.minimum(i + 1, last), 0, 0),
                               memory_space=pltpu.SMEM),
                  pl.BlockSpec((tm, d), lambda i: (i, 0)),
                  pl.BlockSpec((tm, 2), lambda i: (i, 0)),
                  _resident(g_final.shape),
                  pl.BlockSpec(memory_space=pl.ANY)],
        out_specs=pl.BlockSpec((tm, d), lambda i: (i, 0)),
        out_shape=jax.ShapeDtypeStruct((n_tok, d), jnp.float32),
        scratch_shapes=[pltpu.VMEM((2, 2, tm * SUBLANES, LANES), jnp.uint32), pltpu.SemaphoreType.DMA((2,))],
        compiler_params=_params("arbitrary"),
        name="moe_combine",
    )(dest3, dest3, h2, wts, g_final, y)


def _transpose_w_in(w_in):
    return jnp.swapaxes(w_in, 0, 1).astype(jnp.bfloat16)


def _split_w_uq(w_uq):
    half = MLA_ROPE_DIM // 2
    w = w_uq.reshape(MLA_Q_RANK, MLA_HEADS, MLA_QK_DIM).transpose(1, 0, 2)
    pe = w[:, :, MLA_NOPE_DIM:]
    pe_swapped = jnp.concatenate([pe[:, :, half:], pe[:, :, :half]], axis=2)
    return jnp.concatenate([w, pe_swapped], axis=2).astype(jnp.bfloat16)


def _split_w_ukv(w_ukv):
    w = w_ukv.reshape(MLA_KV_RANK, MLA_HEADS, MLA_NOPE_DIM + MLA_V_DIM)
    wuk = w[:, :, :MLA_NOPE_DIM].transpose(1, 0, 2).astype(jnp.bfloat16)
    wuv = w[:, :, MLA_NOPE_DIM:].reshape(MLA_KV_RANK, MLA_HEADS * MLA_V_DIM).astype(jnp.bfloat16)
    return wuk, wuv


def kernel(x, mem, positions, attn_norm_g, w_in, diff_lambda_q1, diff_lambda_k1, diff_lambda_q2, diff_lambda_k2, diff_subln_g, w_o_diff, mla_q_norm_g, w_uq, mla_kv_norm_g, w_ukv, w_o_mla, w_out, cross_norm_g, mem_norm_g, w_cq, w_ckv, w_co, ffn_norm_g, w_router_group, b_router_group, w_router_expert, b_router_expert, w_expert_gate, w_expert_up, w_expert_down, final_norm_g):
    batch, seq, d = x.shape
    assert d == D_MODEL and w_in.shape[0] == 1, "single-layer kernel"
    n_tok = batch * seq
    bf = jnp.bfloat16
    x2 = x.reshape(n_tok, d)

    tm_proj = min(1024, seq)
    tm_row = min(256, seq)
    tm_moe = min(MOE_ROWS_PER_BLOCK, seq)
    tm_cross = min(512, seq)
    tq = min(2048, seq)
    rg_diff = 128
    rg_mla = 256

    cos_t, sin_t = _rope_tables(positions, n_tok, tm_proj)

    g_attn = attn_norm_g[0].reshape(1, d)
    qkv, latent, gates = _inproj(x2, g_attn, _transpose_w_in(w_in[0]), cos_t, sin_t, tm_proj)

    o_a = _diff_attn(qkv.reshape(batch, seq, QKV_COLS),
                     diff_lambda_q1[0].reshape(1, -1), diff_lambda_k1[0].reshape(1, -1),
                     diff_lambda_q2[0].reshape(1, -1), diff_lambda_k2[0].reshape(1, -1),
                     diff_subln_g[0].reshape(1, -1), tq, rg_diff)

    wuk, wuv = _split_w_ukv(w_ukv[0])
    q_cat, k_cat, v_mla = _mla_proj(latent, mla_q_norm_g[0].reshape(1, -1), mla_kv_norm_g[0].reshape(1, -1),
                                    _split_w_uq(w_uq[0]), wuk, wuv, cos_t, sin_t, batch, seq, tm_proj)
    o_b = _mla_attn(q_cat, k_cat, v_mla.reshape(batch, seq, MLA_HEADS * MLA_V_DIM), tq, rg_mla)

    h1 = _merge_out(o_a.reshape(n_tok, -1), o_b.reshape(n_tok, -1), gates, x2,
                    w_o_diff[0].astype(bf), w_o_mla[0].astype(bf), w_out[0].astype(bf), tm_cross)

    kv_mem = _mem_kv(mem, mem_norm_g[0].reshape(1, d), w_ckv[0].astype(bf))
    n_router = N_GROUPS + N_EXPERTS
    w_r = jnp.concatenate([w_router_group[0].astype(jnp.float32), w_router_expert[0].astype(jnp.float32),
                           jnp.zeros((d, LANES - n_router), jnp.float32)], axis=1)
    w_r_hi = w_r.astype(bf)
    w_r_lo = (w_r - w_r_hi.astype(jnp.float32)).astype(bf)
    w_r = jnp.concatenate([w_r_hi, w_r_lo], axis=1)
    b_r = jnp.concatenate([b_router_group[0].astype(jnp.float32), b_router_expert[0].astype(jnp.float32),
                           jnp.zeros((LANES - n_router,), jnp.float32)]).reshape(1, LANES)
    g_ffn = ffn_norm_g[0].reshape(1, d)
    h2, eid, rank, wts, cnt = _cross_router(h1, cross_norm_g[0].reshape(1, d), w_cq[0].astype(bf), kv_mem,
                                            w_co[0].astype(bf), g_ffn, w_r, b_r, seq, tm_cross)

    bm = MOE_ROWS_PER_BLOCK
    assert bm % tm_row == 0 and tm_moe == bm
    counts = cnt[0, ROUTER_EXPERT_LANE0:ROUTER_EXPERT_LANE0 + N_EXPERTS].astype(jnp.int32)
    padded = ((counts + bm - 1) // bm) * bm
    padded_end = jnp.cumsum(padded)
    padded_off = padded_end - padded
    seg_start = jnp.sum(jnp.where(eid[..., None] == jnp.arange(N_EXPERTS, dtype=jnp.int32), padded_off, 0), axis=-1)
    dest = seg_start + rank
    p_rows = ((2 * n_tok + bm - 1) // bm) * bm + N_EXPERTS * bm
    n_blocks = p_rows // bm
    n_active = (padded_end[-1] // bm).astype(jnp.int32)
    blk = jnp.minimum(jnp.arange(n_blocks, dtype=jnp.int32), n_active - 1)
    block_expert = jnp.sum((padded_end[None, :] <= (blk * bm)[:, None]).astype(jnp.int32), axis=1)
    block_expert = jnp.minimum(block_expert, N_EXPERTS - 1)
    def tile_slots(tm):
        return dest.reshape(-1, 2, tm_cross // tm, tm).transpose(0, 2, 1, 3).reshape(n_tok // tm, 1, 2 * tm)
    unused = n_active + jnp.arange(N_EXPERTS, dtype=jnp.int32)
    ztail = jnp.stack([jnp.concatenate([jnp.maximum(padded_end - bm, 0), jnp.minimum(unused, n_blocks - 1) * bm]),
                       jnp.concatenate([padded > 0, unused < n_blocks]).astype(jnp.int32)]).astype(jnp.int32)
    sched = jnp.concatenate([n_active.reshape(1), padded // bm, padded_off // bm, counts]).astype(jnp.int32)

    xb = _dispatch(tile_slots(tm_row), ztail, h2, g_ffn, p_rows, tm_row)
    y = _experts(block_expert, sched, xb, w_expert_gate[0], w_expert_up[0], w_expert_down[0], bm)
    out = _combine(tile_slots(tm_moe), h2, wts, final_norm_g.reshape(1, d), y, tm_moe)
    return out.reshape(batch, seq, d)
```

```python
import functools
import math

import jax
import jax.numpy as jnp
from jax import lax
from jax.experimental import pallas as pl
from jax.experimental.pallas import tpu as pltpu

D_MODEL = 2048
ROPE_THETA = 500000.0
NORM_EPS = 1e-6

DIFF_HEADS = 8
DIFF_HEAD_DIM = 64
DIFF_V_DIM = 2 * DIFF_HEAD_DIM
DIFF_ROT = DIFF_HEAD_DIM // 4
DIFF_SUBLN_EPS = 1e-5
DIFF_LAMBDA_INIT = 0.8 - 0.6 * math.exp(-0.3 * 0)

MLA_HEADS = 8
MLA_Q_RANK = 512
MLA_KV_RANK = 256
MLA_NOPE_DIM = 128
MLA_ROPE_DIM = 64
MLA_V_DIM = 128
MLA_QK_DIM = MLA_NOPE_DIM + MLA_ROPE_DIM

CROSS_HEADS = 4
CROSS_HEAD_DIM = 128

N_GROUPS = 4
EXPERTS_PER_GROUP = 8
N_EXPERTS = N_GROUPS * EXPERTS_PER_GROUP

LANES = 128
BF16_TILE_ROWS = 16
LOG2E = 1.4426950408889634
VMEM_LIMIT_BYTES = 56 * 1024 * 1024

ROUTER_EXPERT_LANE0 = N_GROUPS

QKV_COLS = 3 * DIFF_HEADS * DIFF_V_DIM
LATENT_COLS = 1024
GATE_COLS = 2 * D_MODEL
KPE_COL0 = MLA_Q_RANK + MLA_KV_RANK

MOE_ROWS_PER_BLOCK = 512
SUBLANES = 8


def _params(*semantics):
    return pltpu.CompilerParams(dimension_semantics=semantics, vmem_limit_bytes=VMEM_LIMIT_BYTES)


def _resident(shape):
    zeros = (0,) * len(shape)
    return pl.BlockSpec(shape, lambda *_: zeros, pipeline_mode=pl.Buffered(1))


def _rms_scale(xf, eps):
    return lax.rsqrt(jnp.mean(xf * xf, axis=-1, keepdims=True) + eps)


def _sigmoid(x):
    return 0.5 * jnp.tanh(0.5 * x) + 0.5


def _pack_bf16_pair(a, b):
    hi = lax.bitcast_convert_type(a.astype(jnp.bfloat16).astype(jnp.float32), jnp.uint32)
    lo = lax.bitcast_convert_type(b.astype(jnp.bfloat16).astype(jnp.float32), jnp.uint32)
    return hi | (lo >> 16)


def _unpack_bf16_pair(w):
    a = lax.bitcast_convert_type(w & jnp.uint32(0xFFFF0000), jnp.float32)
    b = lax.bitcast_convert_type(w << 16, jnp.float32)
    return a, b


def _rows_to_tiles(ref_view, packed):
    rows = packed.shape[0]
    for c in range(SUBLANES):
        ref_view[pl.ds(c, rows, stride=SUBLANES), :] = packed[:, c * LANES:(c + 1) * LANES]


def _tiles_to_row_chunks(ref_view, rows):
    return [ref_view[pl.ds(c, rows, stride=SUBLANES), :] for c in range(SUBLANES)]


def _lane_iota(shape):
    return lax.broadcasted_iota(jnp.int32, shape, len(shape) - 1)


def _trig_kernel(pos_ref, invf_ref, cos_ref, sin_ref):
    ang = pos_ref[...].astype(jnp.float32) * invf_ref[...]
    cos_ref[...] = jnp.cos(ang)
    sin_ref[...] = jnp.sin(ang)


def _rope_tables(positions, n_tok, tm):
    half_m = MLA_ROPE_DIM // 2
    half_d = DIFF_ROT // 2
    inv_m = jnp.float32(ROPE_THETA) ** (-jnp.arange(half_m, dtype=jnp.float32) * 2.0 / MLA_ROPE_DIM)
    inv_d = jnp.float32(ROPE_THETA) ** (-jnp.arange(half_d, dtype=jnp.float32) * 2.0 / DIFF_ROT)
    invf = jnp.concatenate([inv_m, inv_m, inv_d, inv_d,
                            jnp.zeros((DIFF_HEAD_DIM - DIFF_ROT,), jnp.float32)]).reshape(1, LANES)
    pos = positions.reshape(n_tok, 1)
    return pl.pallas_call(
        _trig_kernel,
        grid=(n_tok // tm,),
        in_specs=[pl.BlockSpec((tm, 1), lambda i: (i, 0)), _resident((1, LANES))],
        out_specs=[pl.BlockSpec((tm, LANES), lambda i: (i, 0))] * 2,
        out_shape=[jax.ShapeDtypeStruct((n_tok, LANES), jnp.float32)] * 2,
        compiler_params=_params("parallel"),
        name="rope_tables",
    )(pos, invf)


def _diff_rope_coeffs(cos_t, sin_t):
    lane = _lane_iota(cos_t.shape)
    upper = lane >= DIFF_HEAD_DIM
    cos_d = jnp.where(upper, cos_t, pltpu.roll(cos_t, DIFF_HEAD_DIM, 1))
    sin_d = jnp.where(upper, sin_t, pltpu.roll(sin_t, DIFF_HEAD_DIM, 1))
    in_head = lane % DIFF_HEAD_DIM
    half = DIFF_ROT // 2
    s_next = jnp.where(in_head < half, -sin_d, 0.0)
    s_prev = jnp.where((in_head >= half) & (in_head < DIFF_ROT), sin_d, 0.0)
    return cos_d, s_next, s_prev


def _mla_rope(pair, cos_t, sin_t):
    lane = _lane_iota(pair.shape)
    sin_signed = jnp.where(lane < MLA_ROPE_DIM // 2, -sin_t, sin_t)
    return pair * cos_t + pltpu.roll(pair, MLA_ROPE_DIM, 1) * sin_signed


INPROJ_TN = 1024
INPROJ_PIECE = 256
Q_TILES = DIFF_HEADS * DIFF_V_DIM // INPROJ_TN
ROPE_TILES = 2 * Q_TILES
QKV_TILES = QKV_COLS // INPROJ_TN
LATENT_TILES = LATENT_COLS // INPROJ_TN
GATE_TILES = GATE_COLS // INPROJ_TN
INPROJ_TILES = QKV_TILES + LATENT_TILES + GATE_TILES
GATE_ROW0 = QKV_COLS + KPE_COL0 + MLA_ROPE_DIM


def _inproj_kernel(x_ref, g_ref, w_ref, cos_ref, sin_ref, qkv_ref, lat_ref, gate_ref, xn_ref):
    j = pl.program_id(1)

    @pl.when(j == 0)
    def _():
        xf = x_ref[...]
        xn_ref[...] = (xf * _rms_scale(xf, NORM_EPS) * g_ref[...]).astype(jnp.bfloat16)

    def pieces(epilogue):
        for c in range(INPROJ_TN // INPROJ_PIECE):
            cols = slice(c * INPROJ_PIECE, (c + 1) * INPROJ_PIECE)
            acc = lax.dot_general(xn_ref[...], w_ref[cols, :], (((1,), (1,)), ((), ())),
                                  preferred_element_type=jnp.float32)
            epilogue(acc, cols)

    @pl.when(j < ROPE_TILES)
    def _():
        cos_d, s_next, s_prev = _diff_rope_coeffs(cos_ref[...], sin_ref[...])
        qscale = jnp.where(j < Q_TILES, DIFF_HEAD_DIM ** -0.5 * LOG2E, 1.0).astype(jnp.float32)

        def rope(acc, cols):
            for c in range(INPROJ_PIECE // LANES):
                xc = acc[:, c * LANES:(c + 1) * LANES]
                rot = (xc * cos_d + pltpu.roll(xc, LANES - DIFF_ROT // 2, 1) * s_next
                       + pltpu.roll(xc, DIFF_ROT // 2, 1) * s_prev)
                lo = cols.start + c * LANES
                qkv_ref[:, lo:lo + LANES] = (rot * qscale).astype(qkv_ref.dtype)

        pieces(rope)

    @pl.when((j >= ROPE_TILES) & (j < QKV_TILES))
    def _():
        def value(acc, cols):
            qkv_ref[:, cols] = acc.astype(qkv_ref.dtype)

        pieces(value)

    @pl.when((j >= QKV_TILES) & (j < QKV_TILES + LATENT_TILES))
    def _():
        def latent(acc, cols):
            if cols.start <= KPE_COL0 < cols.stop:
                c0 = KPE_COL0 - cols.start
                v = acc[:, c0:c0 + LANES]
                lane = _lane_iota(v.shape)
                half = MLA_ROPE_DIM // 2
                swapped = jnp.where(lane < MLA_ROPE_DIM + half, pltpu.roll(v, half, 1),
                                    pltpu.roll(v, MLA_ROPE_DIM + half, 1))
                parts = [acc[:, :c0], jnp.where(lane < MLA_ROPE_DIM, v, swapped), acc[:, c0 + LANES:]]
                acc = jnp.concatenate([p for p in parts if p.shape[1]], axis=1)
            lat_ref[:, cols] = acc

        pieces(latent)

    @pl.when(j >= QKV_TILES + LATENT_TILES)
    def _():
        def gate(acc, cols):
            gate_ref[:, cols] = _sigmoid(acc).astype(gate_ref.dtype)

        pieces(gate)


def _inproj(x2, g, w_all, cos_t, sin_t, tm):
    n_tok, d = x2.shape
    tn = INPROJ_TN
    lat0 = QKV_TILES
    gate0 = QKV_TILES + LATENT_TILES

    def w_rows(i, j):
        start = jnp.where(j < gate0, j * tn, GATE_ROW0 + (j - gate0) * tn)
        return (pl.multiple_of(start, BF16_TILE_ROWS), 0)

    assert GATE_ROW0 % BF16_TILE_ROWS == 0
    return pl.pallas_call(
        _inproj_kernel,
        grid=(n_tok // tm, INPROJ_TILES),
        in_specs=[pl.BlockSpec((tm, d), lambda i, j: (i, 0)),
                  _resident((1, d)),
                  pl.BlockSpec((pl.Element(tn), pl.Element(d)), w_rows),
                  pl.BlockSpec((tm, LANES), lambda i, j: (i, 0)),
                  pl.BlockSpec((tm, LANES), lambda i, j: (i, 0))],
        out_specs=[pl.BlockSpec((tm, tn), lambda i, j: (i, jnp.clip(j, 0, QKV_TILES - 1))),
                   pl.BlockSpec((tm, tn), lambda i, j: (i, jnp.clip(j - lat0, 0, LATENT_TILES - 1))),
                   pl.BlockSpec((tm, tn), lambda i, j: (i, jnp.clip(j - gate0, 0, GATE_TILES - 1)))],
        out_shape=[jax.ShapeDtypeStruct((n_tok, QKV_COLS), jnp.bfloat16),
                   jax.ShapeDtypeStruct((n_tok, LATENT_COLS), jnp.float32),
                   jax.ShapeDtypeStruct((n_tok, GATE_COLS), jnp.bfloat16)],
        scratch_shapes=[pltpu.VMEM((tm, d), jnp.bfloat16)],
        compiler_params=_params("parallel", "arbitrary"),
        name="inproj",
    )(x2, g, w_all, cos_t, sin_t)


def _mla_proj_kernel(c_ref, gq_ref, gkv_ref, wuq_ref, wuk_ref, wuv_ref, cos_ref, sin_ref,
                     q_ref, k_ref, v_ref):
    cos_t = cos_ref[...]
    sin_t = sin_ref[...]
    cq = c_ref[:, :MLA_Q_RANK]
    cqn = (cq * _rms_scale(cq, NORM_EPS) * gq_ref[...]).astype(jnp.bfloat16)
    ckv = c_ref[:, MLA_Q_RANK:KPE_COL0]
    ckvn = (ckv * _rms_scale(ckv, NORM_EPS) * gkv_ref[...]).astype(jnp.bfloat16)
    kpe = _mla_rope(c_ref[:, KPE_COL0:KPE_COL0 + LANES], cos_t, sin_t)[:, :MLA_ROPE_DIM].astype(k_ref.dtype)
    qscale = MLA_QK_DIM ** -0.5 * LOG2E
    for h in range(MLA_HEADS):
        r = jnp.dot(cqn, wuq_ref[h], preferred_element_type=jnp.float32)
        q_ref[0, h, :, :MLA_NOPE_DIM] = (r[:, :MLA_NOPE_DIM] * qscale).astype(q_ref.dtype)
        qpe = _mla_rope(r[:, MLA_NOPE_DIM:], cos_t, sin_t)[:, :MLA_ROPE_DIM]
        q_ref[0, h, :, MLA_NOPE_DIM:] = (qpe * qscale).astype(q_ref.dtype)
        kn = jnp.dot(ckvn, wuk_ref[h], preferred_element_type=jnp.float32)
        k_ref[0, h, :, :MLA_NOPE_DIM] = kn.astype(k_ref.dtype)
        k_ref[0, h, :, MLA_NOPE_DIM:] = kpe
    v_ref[...] = jnp.dot(ckvn, wuv_ref[...], preferred_element_type=jnp.float32).astype(v_ref.dtype)


def _mla_proj(latent, gq, gkv, wuq, wuk, wuv, cos_t, sin_t, batch, seq, tm):
    n_tok = latent.shape[0]
    per_b = seq // tm
    head_spec = pl.BlockSpec((1, MLA_HEADS, tm, MLA_QK_DIM), lambda i: (i // per_b, 0, i % per_b, 0))
    head_shape = jax.ShapeDtypeStruct((batch, MLA_HEADS, seq, MLA_QK_DIM), jnp.bfloat16)
    return pl.pallas_call(
        _mla_proj_kernel,
        grid=(n_tok // tm,),
        in_specs=[pl.BlockSpec((tm, LATENT_COLS), lambda i: (i, 0)),
                  _resident(gq.shape), _resident(gkv.shape),
                  _resident(wuq.shape), _resident(wuk.shape), _resident(wuv.shape),
                  pl.BlockSpec((tm, LANES), lambda i: (i, 0)),
                  pl.BlockSpec((tm, LANES), lambda i: (i, 0))],
        out_specs=[head_spec, head_spec,
                   pl.BlockSpec((tm, MLA_HEADS * MLA_V_DIM), lambda i: (i, 0))],
        out_shape=[head_shape, head_shape,
                   jax.ShapeDtypeStruct((n_tok, MLA_HEADS * MLA_V_DIM), jnp.bfloat16)],
        compiler_params=_params("parallel"),
        name="mla_proj",
    )(latent, gq, gkv, wuq, wuk, wuv, cos_t, sin_t)


def _with_ones(v):
    return jnp.concatenate([v, jnp.ones((v.shape[0], LANES), v.dtype)], axis=-1)


def _softmax_pv(s, v_ones):
    m = jnp.max(s, axis=-1, keepdims=True)
    p = jnp.exp2(s - m).astype(v_ones.dtype)
    pv = jnp.dot(p, v_ones, preferred_element_type=jnp.float32)
    dv = v_ones.shape[1] - LANES
    return pv[:, :dv] / pv[:, dv:]


ATTN_HEADS_PER_STEP = 2


def _diff_attn_kernel(q_ref, k_ref, v_ref, lq1_ref, lk1_ref, lq2_ref, lk2_ref, g_ref, o_ref, v1_ref, *, rg):
    heads = q_ref.shape[2] // LANES

    @pl.when(pl.program_id(2) == 0)
    def _():
        for h in range(heads):
            v1_ref[h] = _with_ones(v_ref[0, :, h * DIFF_V_DIM:(h + 1) * DIFF_V_DIM])

    lam = (jnp.exp(jnp.sum(lq1_ref[...] * lk1_ref[...], axis=-1, keepdims=True))
           - jnp.exp(jnp.sum(lq2_ref[...] * lk2_ref[...], axis=-1, keepdims=True))
           + DIFF_LAMBDA_INIT)
    lane = _lane_iota((rg, LANES))
    for g in range(q_ref.shape[1] // rg):
        for h in range(heads):
            cols = slice(h * LANES, (h + 1) * LANES)
            q = q_ref[0, g * rg:(g + 1) * rg, cols]
            k = k_ref[0, :, cols]
            zero = jnp.zeros_like(q)
            q12 = jnp.concatenate([jnp.where(lane < DIFF_HEAD_DIM, q, zero),
                                   jnp.where(lane >= DIFF_HEAD_DIM, q, zero)], axis=0)
            s = lax.dot_general(q12, k, (((1,), (1,)), ((), ())), preferred_element_type=jnp.float32)
            a = _softmax_pv(s, v1_ref[h])
            o = a[:rg] - lam * a[rg:]
            o = o * _rms_scale(o, DIFF_SUBLN_EPS) * g_ref[...] * (1.0 - DIFF_LAMBDA_INIT)
            o_ref[0, g * rg:(g + 1) * rg, cols] = o.astype(o_ref.dtype)


def _diff_attn(qkv3, lq1, lk1, lq2, lk2, subln_g, tq, rg):
    batch, seq, _ = qkv3.shape
    hp = ATTN_HEADS_PER_STEP
    steps = DIFF_HEADS // hp
    width = hp * DIFF_V_DIM
    return pl.pallas_call(
        functools.partial(_diff_attn_kernel, rg=rg),
        grid=(batch, steps, seq // tq),
        in_specs=[pl.BlockSpec((1, tq, width), lambda b, hh, i: (b, i, hh)),
                  pl.BlockSpec((1, seq, width), lambda b, hh, i: (b, 0, steps + hh)),
                  pl.BlockSpec((1, seq, width), lambda b, hh, i: (b, 0, 2 * steps + hh)),
                  _resident(lq1.shape), _resident(lk1.shape), _resident(lq2.shape), _resident(lk2.shape),
                  _resident(subln_g.shape)],
        out_specs=pl.BlockSpec((1, tq, width), lambda b, hh, i: (b, i, hh)),
        out_shape=jax.ShapeDtypeStruct((batch, seq, DIFF_HEADS * DIFF_V_DIM), jnp.bfloat16),
        scratch_shapes=[pltpu.VMEM((hp, seq, DIFF_V_DIM + LANES), jnp.bfloat16)],
        compiler_params=_params("parallel", "parallel", "arbitrary"),
        name="diff_attn",
    )(qkv3, qkv3, qkv3, lq1, lk1, lq2, lk2, subln_g)


def _mla_attn_kernel(q_ref, k_ref, v_ref, o_ref, v1_ref, *, rg):
    heads = q_ref.shape[1]

    @pl.when(pl.program_id(2) == 0)
    def _():
        for h in range(heads):
            v1_ref[h] = _with_ones(v_ref[0, :, h * MLA_V_DIM:(h + 1) * MLA_V_DIM])

    for g in range(q_ref.shape[2] // rg):
        for h in range(heads):
            s = lax.dot_general(q_ref[0, h, g * rg:(g + 1) * rg], k_ref[0, h], (((1,), (1,)), ((), ())),
                                preferred_element_type=jnp.float32)
            o_ref[0, g * rg:(g + 1) * rg, h * MLA_V_DIM:(h + 1) * MLA_V_DIM] = (
                _softmax_pv(s, v1_ref[h]).astype(o_ref.dtype))


def _mla_attn(q_cat, k_cat, v3, tq, rg):
    batch, heads, seq, dqk = q_cat.shape
    hp = ATTN_HEADS_PER_STEP
    return pl.pallas_call(
        functools.partial(_mla_attn_kernel, rg=rg),
        grid=(batch, heads // hp, seq // tq),
        in_specs=[pl.BlockSpec((1, hp, tq, dqk), lambda b, h, i: (b, h, i, 0)),
                  pl.BlockSpec((1, hp, seq, dqk), lambda b, h, i: (b, h, 0, 0)),
                  pl.BlockSpec((1, seq, hp * MLA_V_DIM), lambda b, h, i: (b, 0, h))],
        out_specs=pl.BlockSpec((1, tq, hp * MLA_V_DIM), lambda b, h, i: (b, i, h)),
        out_shape=jax.ShapeDtypeStruct((batch, seq, heads * MLA_V_DIM), jnp.bfloat16),
        scratch_shapes=[pltpu.VMEM((hp, seq, MLA_V_DIM + LANES), jnp.bfloat16)],
        compiler_params=_params("parallel", "parallel", "arbitrary"),
        name="mla_attn",
    )(q_cat, k_cat, v3)


def _merge_out_kernel(oa_ref, ob_ref, sga_ref, sgb_ref, x_ref, woa_ref, wob_ref, wout_ref, h_ref):
    ya = jnp.dot(oa_ref[...], woa_ref[...], preferred_element_type=jnp.float32)
    yb = jnp.dot(ob_ref[...], wob_ref[...], preferred_element_type=jnp.float32)
    merged = sga_ref[...].astype(jnp.float32) * ya + sgb_ref[...].astype(jnp.float32) * yb
    h_ref[...] = x_ref[...] + jnp.dot(merged.astype(jnp.bfloat16), wout_ref[...],
                                       preferred_element_type=jnp.float32)


def _merge_out(o_a, o_b, gates, x2, w_oa, w_ob, w_out, tm):
    n_tok, d = x2.shape
    return pl.pallas_call(
        _merge_out_kernel,
        grid=(n_tok // tm,),
        in_specs=[pl.BlockSpec((tm, o_a.shape[1]), lambda i: (i, 0)),
                  pl.BlockSpec((tm, o_b.shape[1]), lambda i: (i, 0)),
                  pl.BlockSpec((tm, d), lambda i: (i, 0)),
                  pl.BlockSpec((tm, d), lambda i: (i, 1)),
                  pl.BlockSpec((tm, d), lambda i: (i, 0)),
                  _resident(w_oa.shape), _resident(w_ob.shape), _resident(w_out.shape)],
        out_specs=pl.BlockSpec((tm, d), lambda i: (i, 0)),
        out_shape=jax.ShapeDtypeStruct((n_tok, d), jnp.float32),
        compiler_params=_params("parallel"),
        name="merge_out",
    )(o_a, o_b, gates, gates, x2, w_oa, w_ob, w_out)


def _mem_kv_kernel(mem_ref, g_ref, w_ref, kv_ref):
    mf = mem_ref[0]
    mn = (mf * _rms_scale(mf, NORM_EPS) * g_ref[...]).astype(jnp.bfloat16)
    kv_ref[0] = jnp.dot(mn, w_ref[...], preferred_element_type=jnp.float32).astype(kv_ref.dtype)


def _mem_kv(mem, g, w_ckv):
    batch, m, d = mem.shape
    return pl.pallas_call(
        _mem_kv_kernel,
        grid=(batch,),
        in_specs=[pl.BlockSpec((1, m, d), lambda b: (b, 0, 0)), _resident(g.shape), _resident(w_ckv.shape)],
        out_specs=pl.BlockSpec((1, m, w_ckv.shape[1]), lambda b: (b, 0, 0)),
        out_shape=jax.ShapeDtypeStruct((batch, m, w_ckv.shape[1]), jnp.bfloat16),
        compiler_params=_params("parallel"),
        name="mem_kv",
    )(mem, g, w_ckv)


def _cross_router_kernel(h_ref, gc_ref, wcq_ref, kv_ref, wco_ref, gf_ref, wr_ref, br_ref,
                         h2_ref, eid_ref, rank_ref, wts_ref, cnt_ref, carry_ref):
    i = pl.program_id(0)

    @pl.when(i == 0)
    def _():
        carry_ref[...] = jnp.zeros_like(carry_ref)

    h1 = h_ref[...]
    tm = h1.shape[0]
    hn = (h1 * _rms_scale(h1, NORM_EPS) * gc_ref[...]).astype(jnp.bfloat16)
    q = jnp.dot(hn, wcq_ref[...], preferred_element_type=jnp.float32) * (CROSS_HEAD_DIM ** -0.5 * LOG2E)
    q = q.astype(jnp.bfloat16)
    kv_cols = CROSS_HEADS * CROSS_HEAD_DIM
    heads = []
    for hd in range(CROSS_HEADS):
        lo = hd * CROSS_HEAD_DIM
        kh = kv_ref[0, :, lo:lo + CROSS_HEAD_DIM]
        vh = kv_ref[0, :, kv_cols + lo:kv_cols + lo + CROSS_HEAD_DIM]
        s = lax.dot_general(q[:, lo:lo + CROSS_HEAD_DIM], kh, (((1,), (1,)), ((), ())),
                            preferred_element_type=jnp.float32)
        heads.append(_softmax_pv(s, _with_ones(vh)).astype(jnp.bfloat16))
    o = jnp.concatenate(heads, axis=-1)
    h2 = h1 + jnp.dot(o, wco_ref[...], preferred_element_type=jnp.float32)
    h2_ref[...] = h2

    t = h2 * _rms_scale(h2, NORM_EPS) * gf_ref[...]
    t_hi = t.astype(jnp.bfloat16)
    t_lo = (t - t_hi.astype(jnp.float32)).astype(jnp.bfloat16)
    hi = jnp.dot(t_hi, wr_ref[...], preferred_element_type=jnp.float32)
    lo = jnp.dot(t_lo, wr_ref[:, :LANES], preferred_element_type=jnp.float32)
    logits = hi[:, :LANES] + (hi[:, LANES:] + lo) + br_ref[...]
    lane = _lane_iota(logits.shape)
    neg = jnp.float32(-jnp.inf)
    big = jnp.int32(2 * LANES)
    is_group = lane < N_GROUPS
    lg = jnp.where(is_group, logits, neg)
    mg = jnp.max(lg, axis=-1, keepdims=True)
    g_idx = jnp.min(jnp.where(is_group & (logits == mg), lane, big), axis=-1, keepdims=True)
    g_p = 1.0 / jnp.sum(jnp.exp(lg - mg), axis=-1, keepdims=True)
    lo_lane = ROUTER_EXPERT_LANE0 + EXPERTS_PER_GROUP * g_idx
    in_grp = (lane >= lo_lane) & (lane < lo_lane + EXPERTS_PER_GROUP)
    l1 = jnp.max(jnp.where(in_grp, logits, neg), axis=-1, keepdims=True)
    i1 = jnp.min(jnp.where(in_grp & (logits == l1), lane, big), axis=-1, keepdims=True)
    rest = in_grp & (lane != i1)
    l2 = jnp.max(jnp.where(rest, logits, neg), axis=-1, keepdims=True)
    i2 = jnp.min(jnp.where(rest & (logits == l2), lane, big), axis=-1, keepdims=True)
    d = jnp.exp(l2 - l1)
    w1 = g_p / (1.0 + d)
    w2 = w1 * d

    oh1 = lane == i1
    oh2 = lane == i2
    cnt = (oh1 | oh2).astype(jnp.bfloat16)
    row = lax.broadcasted_iota(jnp.int32, (tm, tm), 0)
    col = lax.broadcasted_iota(jnp.int32, (tm, tm), 1)
    before = (col < row).astype(jnp.bfloat16)
    slot = jnp.dot(before, cnt, preferred_element_type=jnp.float32) + carry_ref[...]
    r1 = jnp.sum(jnp.where(oh1, slot, 0.0), axis=-1, keepdims=True)
    r2 = jnp.sum(jnp.where(oh2, slot, 0.0), axis=-1, keepdims=True)
    carry_ref[...] += jnp.sum(cnt.astype(jnp.float32), axis=0, keepdims=True)
    cnt_ref[...] = carry_ref[...]

    eye = row == col

    def to_row(c, dtype):
        return jnp.sum(jnp.where(eye, c.astype(jnp.float32), 0.0), axis=0, keepdims=True).astype(dtype)

    eid_ref[0] = jnp.concatenate([to_row(i1 - ROUTER_EXPERT_LANE0, jnp.int32),
                                  to_row(i2 - ROUTER_EXPERT_LANE0, jnp.int32)], axis=0)
    rank_ref[0] = jnp.concatenate([to_row(r1, jnp.int32), to_row(r2, jnp.int32)], axis=0)
    wts_ref[...] = jnp.where(_lane_iota((tm, 2)) == 0, w1, w2)


def _cross_router(h1, gc, w_cq, kv_mem, w_co, gf, w_r, b_r, seq, tm):
    n_tok, d = h1.shape
    per_b = seq // tm
    row2 = pl.BlockSpec((tm, 2), lambda i: (i, 0))
    lane2 = pl.BlockSpec((1, 2, tm), lambda i: (i, 0, 0))
    return pl.pallas_call(
        _cross_router_kernel,
        grid=(n_tok // tm,),
        in_specs=[pl.BlockSpec((tm, d), lambda i: (i, 0)),
                  _resident(gc.shape), _resident(w_cq.shape),
                  pl.BlockSpec((1,) + kv_mem.shape[1:], lambda i: (i // per_b, 0, 0)),
                  _resident(w_co.shape), _resident(gf.shape), _resident(w_r.shape), _resident(b_r.shape)],
        out_specs=[pl.BlockSpec((tm, d), lambda i: (i, 0)), lane2, lane2, row2,
                   pl.BlockSpec((1, LANES), lambda i: (0, 0))],
        out_shape=[jax.ShapeDtypeStruct((n_tok, d), jnp.float32),
                   jax.ShapeDtypeStruct((n_tok // tm, 2, tm), jnp.int32),
                   jax.ShapeDtypeStruct((n_tok // tm, 2, tm), jnp.int32),
                   jax.ShapeDtypeStruct((n_tok, 2), jnp.float32),
                   jax.ShapeDtypeStruct((1, LANES), jnp.float32)],
        scratch_shapes=[pltpu.VMEM((1, LANES), jnp.float32)],
        compiler_params=_params("arbitrary"),
        name="cross_router",
    )(h1, gc, w_cq, kv_mem, w_co, gf, w_r, b_r)


def _pad_bits():
    return [1 << b for b in reversed(range(MOE_ROWS_PER_BLOCK.bit_length() - 1))]


def _zero_fill_sizes():
    return _pad_bits() * N_EXPERTS + [MOE_ROWS_PER_BLOCK // 2] * (2 * N_EXPERTS)


def _dispatch_kernel(dest_ref, zfill_ref, h_ref, g_ref, xb_ref, t_ref, zero_ref, sem, zsem):
    i = pl.program_id(0)
    tm = h_ref.shape[0]
    slot = i % 2

    def zero_copies(action):
        for idx, rows in enumerate(_zero_fill_sizes()):
            @pl.when(zfill_ref[1, idx] > 0)
            def _():
                start = pl.multiple_of(zfill_ref[0, idx] * SUBLANES, SUBLANES)
                action(pltpu.make_async_copy(zero_ref.at[pl.ds(0, rows * SUBLANES), :],
                                             xb_ref.at[pl.ds(start, rows * SUBLANES), :], zsem))

    @pl.when(i == 0)
    def _():
        zero_ref[...] = jnp.zeros_like(zero_ref)
        zero_copies(lambda c: c.start())

    h2 = h_ref[...]
    t = h2 * _rms_scale(h2, NORM_EPS) * g_ref[...]
    half = t.shape[1] // 2
    _rows_to_tiles(t_ref.at[slot], _pack_bf16_pair(t[:, :half], t[:, half:]))

    for r in range(tm):
        for k in range(2):
            dst = pl.multiple_of(dest_ref[0, k * tm + r] * SUBLANES, SUBLANES)
            pltpu.make_async_copy(t_ref.at[slot, pl.ds(r * SUBLANES, SUBLANES), :],
                                  xb_ref.at[pl.ds(dst, SUBLANES), :], sem.at[slot]).start(priority=k)

    def wait_tile(which):
        for _ in range(2):
            pltpu.make_async_copy(t_ref.at[which], xb_ref.at[pl.ds(0, tm * SUBLANES), :], sem.at[which]).wait()

    @pl.when(i > 0)
    def _():
        wait_tile(1 - slot)

    @pl.when(i == pl.num_programs(0) - 1)
    def _():
        wait_tile(slot)
        zero_copies(lambda c: c.wait())


def _dispatch(dest3, zfill, h2, gf, p_rows, tm):
    n_tok, d = h2.shape
    return pl.pallas_call(
        _dispatch_kernel,
        grid=(n_tok // tm,),
        in_specs=[pl.BlockSpec((None, 1, 2 * tm), lambda i: (i, 0, 0), memory_space=pltpu.SMEM),
                  pl.BlockSpec(memory_space=pltpu.SMEM),
                  pl.BlockSpec((tm, d), lambda i: (i, 0)),
                  _resident(gf.shape)],
        out_specs=pl.BlockSpec(memory_space=pl.ANY),
        out_shape=jax.ShapeDtypeStruct((p_rows * SUBLANES, LANES), jnp.uint32),
        scratch_shapes=[pltpu.VMEM((2, tm * SUBLANES, LANES), jnp.uint32),
                        pltpu.VMEM((MOE_ROWS_PER_BLOCK // 2 * SUBLANES, LANES), jnp.uint32),
                        pltpu.SemaphoreType.DMA((2,)), pltpu.SemaphoreType.DMA(())],
        compiler_params=_params("arbitrary"),
        name="moe_dispatch",
    )(dest3, zfill, h2, gf)


def _expert_kernel(be_ref, nact_ref, x_ref, wg_hbm, wu_hbm, wd_hbm, y_ref,
                   wg_f, wu_f, wd_f, wg_b, wu_b, wd_b, sem):
    i = pl.program_id(0)
    blk = i - 1
    nact = nact_ref[0]
    last_blk = pl.num_programs(0) - 2

    def weight_copies(e):
        return (pltpu.make_async_copy(wg_hbm.at[e], wg_f, sem),
                pltpu.make_async_copy(wu_hbm.at[e], wu_f, sem),
                pltpu.make_async_copy(wd_hbm.at[e], wd_f, sem))

    def fetch(e):
        for c in weight_copies(e):
            c.start()

    def land(e):
        for c in weight_copies(e):
            c.wait()
        half = wg_f.shape[0] // 2
        for c in range(SUBLANES):
            for part, src0 in enumerate((c * LANES, half + c * LANES)):
                dst0 = (2 * c + part) * LANES
                wg_b[dst0:dst0 + LANES, :] = wg_f[src0:src0 + LANES, :].astype(jnp.bfloat16)
                wu_b[dst0:dst0 + LANES, :] = wu_f[src0:src0 + LANES, :].astype(jnp.bfloat16)
        wd_b[...] = wd_f[...].astype(jnp.bfloat16)

    @pl.when(i == 0)
    def _():
        fetch(be_ref[0])
        land(be_ref[0])

    @pl.when((i > 0) & (blk < nact))
    def _():
        here = be_ref[blk]
        nxt = be_ref[jnp.minimum(blk + 1, last_blk)]
        prv = be_ref[jnp.maximum(blk - 1, 0)]
        seg_end = nact_ref[1 + N_EXPERTS + here] + nact_ref[1 + here]
        has_next = seg_end < nact
        after = be_ref[jnp.minimum(seg_end, last_blk)]
        is_first = (blk == 0) | (prv != here)
        is_last = (blk + 1 >= nact) | (nxt != here)

        @pl.when(is_first & has_next)
        def _():
            fetch(after)

        bm = x_ref.shape[0] // SUBLANES
        hb = bm // 2
        rows_here = nact_ref[1 + 2 * N_EXPERTS + here] - (blk - nact_ref[1 + N_EXPERTS + here]) * bm

        def swiglu(h):
            view = pl.ds(h * hb * SUBLANES, hb * SUBLANES)
            parts = []
            for chunk in _tiles_to_row_chunks(x_ref.at[view, :], hb):
                x_a, x_b = _unpack_bf16_pair(chunk)
                parts += [x_a.astype(jnp.bfloat16), x_b.astype(jnp.bfloat16)]
            xb = jnp.concatenate(parts, axis=1)
            gate = jnp.dot(xb, wg_b[...], preferred_element_type=jnp.float32)
            up = jnp.dot(xb, wu_b[...], preferred_element_type=jnp.float32)
            hid = (gate * _sigmoid(gate) * up).astype(jnp.bfloat16)
            y = jnp.dot(hid, wd_b[...], preferred_element_type=jnp.float32)
            half = y.shape[1] // 2
            _rows_to_tiles(y_ref.at[view, :], _pack_bf16_pair(y[:, :half], y[:, half:]))

        swiglu(0)

        @pl.when(rows_here > hb)
        def _():
            swiglu(1)

        @pl.when(rows_here <= hb)
        def _():
            y_ref[pl.ds(hb * SUBLANES, hb * SUBLANES), :] = jnp.zeros((hb * SUBLANES, LANES), y_ref.dtype)

        @pl.when(is_last & has_next)
        def _():
            land(after)

    @pl.when((i > 0) & (blk >= nact))
    def _():
        y_ref[...] = jnp.zeros_like(y_ref)


def _experts(block_expert, sched, xb, w_gate, w_up, w_down, bm):
    p_rows = xb.shape[0] // SUBLANES
    d = w_gate.shape[1]
    de = w_gate.shape[-1]

    def x_map(i, be, sc):
        return (jnp.clip(i - 1, 0, sc[0] - 1), 0)

    grid_spec = pltpu.PrefetchScalarGridSpec(
        num_scalar_prefetch=2,
        grid=(p_rows // bm + 1,),
        in_specs=[pl.BlockSpec((bm * SUBLANES, LANES), x_map),
                  pl.BlockSpec(memory_space=pl.ANY),
                  pl.BlockSpec(memory_space=pl.ANY),
                  pl.BlockSpec(memory_space=pl.ANY)],
        out_specs=pl.BlockSpec((bm * SUBLANES, LANES), lambda i, be, sc: (jnp.maximum(i - 1, 0), 0)),
        scratch_shapes=[pltpu.VMEM((d, de), jnp.float32), pltpu.VMEM((d, de), jnp.float32),
                        pltpu.VMEM((de, d), jnp.float32),
                        pltpu.VMEM((d, de), jnp.bfloat16), pltpu.VMEM((d, de), jnp.bfloat16),
                        pltpu.VMEM((de, d), jnp.bfloat16),
                        pltpu.SemaphoreType.DMA(())],
    )
    return pl.pallas_call(
        _expert_kernel,
        grid_spec=grid_spec,
        out_shape=jax.ShapeDtypeStruct(xb.shape, jnp.uint32),
        compiler_params=_params("arbitrary"),
        name="moe_experts",
    )(block_expert, sched, xb, w_gate, w_up, w_down)


def _combine_kernel(dest_ref, dest_next_ref, h_ref, wts_ref, g_ref, y_ref, o_ref, ybuf, sem):
    i = pl.program_id(0)
    tm = h_ref.shape[0]
    slot = i % 2

    def gather(idx_ref, which):
        for r in range(tm):
            for k in range(2):
                src = pl.multiple_of(idx_ref[0, k * tm + r] * SUBLANES, SUBLANES)
                pltpu.make_async_copy(y_ref.at[pl.ds(src, SUBLANES), :],
                                      ybuf.at[which, k, pl.ds(r * SUBLANES, SUBLANES), :], sem.at[which]).start(priority=k)

    def wait_tile(which):
        for k in range(2):
            pltpu.make_async_copy(y_ref.at[pl.ds(0, tm * SUBLANES), :], ybuf.at[which, k], sem.at[which]).wait()

    @pl.when(i == 0)
    def _():
        gather(dest_ref, slot)

    wait_tile(slot)
    gather(dest_next_ref, 1 - slot)

    w = wts_ref[...]
    half = h_ref.shape[1] // 2
    lo_parts, hi_parts = [], []
    for c, (c0, c1) in enumerate(zip(_tiles_to_row_chunks(ybuf.at[slot, 0], tm),
                                      _tiles_to_row_chunks(ybuf.at[slot, 1], tm))):
        a0, b0 = _unpack_bf16_pair(c0)
        a1, b1 = _unpack_bf16_pair(c1)
        lo_parts.append(h_ref[:, c * LANES:(c + 1) * LANES] + w[:, 0:1] * a0 + w[:, 1:2] * a1)
        hi_parts.append(h_ref[:, half + c * LANES:half + (c + 1) * LANES] + w[:, 0:1] * b0 + w[:, 1:2] * b1)
    h3 = jnp.concatenate(lo_parts + hi_parts, axis=1)
    o_ref[...] = h3 * _rms_scale(h3, NORM_EPS) * g_ref[...]

    @pl.when(i == pl.num_programs(0) - 1)
    def _():
        wait_tile(1 - slot)


def _combine(dest3, h2, wts, g_final, y, tm):
    n_tok, d = h2.shape
    last = n_tok // tm - 1
    return pl.pallas_call(
        _combine_kernel,
        grid=(n_tok // tm,),
        in_specs=[pl.BlockSpec((None, 1, 2 * tm), lambda i: (i, 0, 0), memory_space=pltpu.SMEM),
                  pl.BlockSpec((None, 1, 2 * tm), lambda i: (jnp.minimum(i + 1, last), 0, 0),
                               memory_space=pltpu.SMEM),
                  pl.BlockSpec((tm, d), lambda i: (i, 0)),
                  pl.BlockSpec((tm, 2), lambda i: (i, 0)),
                  _resident(g_final.shape),
                  pl.BlockSpec(memory_space=pl.ANY)],
        out_specs=pl.BlockSpec((tm, d), lambda i: (i, 0)),
        out_shape=jax.ShapeDtypeStruct((n_tok, d), jnp.float32),
        scratch_shapes=[pltpu.VMEM((2, 2, tm * SUBLANES, LANES), jnp.uint32), pltpu.SemaphoreType.DMA((2,))],
        compiler_params=_params("arbitrary"),
        name="moe_combine",
    )(dest3, dest3, h2, wts, g_final, y)


def _transpose_w_in(w_in):
    return jnp.swapaxes(w_in, 0, 1).astype(jnp.bfloat16)


def _split_w_uq(w_uq):
    half = MLA_ROPE_DIM // 2
    w = w_uq.reshape(MLA_Q_RANK, MLA_HEADS, MLA_QK_DIM).transpose(1, 0, 2)
    pe = w[:, :, MLA_NOPE_DIM:]
    pe_swapped = jnp.concatenate([pe[:, :, half:], pe[:, :, :half]], axis=2)
    return jnp.concatenate([w, pe_swapped], axis=2).astype(jnp.bfloat16)


def _split_w_ukv(w_ukv):
    w = w_ukv.reshape(MLA_KV_RANK, MLA_HEADS, MLA_NOPE_DIM + MLA_V_DIM)
    wuk = w[:, :, :MLA_NOPE_DIM].transpose(1, 0, 2).astype(jnp.bfloat16)
    wuv = w[:, :, MLA_NOPE_DIM:].reshape(MLA_KV_RANK, MLA_HEADS * MLA_V_DIM).astype(jnp.bfloat16)
    return wuk, wuv


def kernel(x, mem, positions, attn_norm_g, w_in, diff_lambda_q1, diff_lambda_k1, diff_lambda_q2, diff_lambda_k2, diff_subln_g, w_o_diff, mla_q_norm_g, w_uq, mla_kv_norm_g, w_ukv, w_o_mla, w_out, cross_norm_g, mem_norm_g, w_cq, w_ckv, w_co, ffn_norm_g, w_router_group, b_router_group, w_router_expert, b_router_expert, w_expert_gate, w_expert_up, w_expert_down, final_norm_g):
    batch, seq, d = x.shape
    assert d == D_MODEL and w_in.shape[0] == 1, "single-layer kernel"
    n_tok = batch * seq
    bf = jnp.bfloat16
    x2 = x.reshape(n_tok, d)

    tm_proj = min(1024, seq)
    tm_row = min(256, seq)
    tm_moe = min(MOE_ROWS_PER_BLOCK, seq)
    tm_cross = min(512, seq)
    tq = min(2048, seq)
    rg_diff = 128
    rg_mla = 256

    cos_t, sin_t = _rope_tables(positions, n_tok, tm_proj)

    g_attn = attn_norm_g[0].reshape(1, d)
    qkv, latent, gates = _inproj(x2, g_attn, _transpose_w_in(w_in[0]), cos_t, sin_t, tm_proj)

    o_a = _diff_attn(qkv.reshape(batch, seq, QKV_COLS),
                     diff_lambda_q1[0].reshape(1, -1), diff_lambda_k1[0].reshape(1, -1),
                     diff_lambda_q2[0].reshape(1, -1), diff_lambda_k2[0].reshape(1, -1),
                     diff_subln_g[0].reshape(1, -1), tq, rg_diff)

    wuk, wuv = _split_w_ukv(w_ukv[0])
    q_cat, k_cat, v_mla = _mla_proj(latent, mla_q_norm_g[0].reshape(1, -1), mla_kv_norm_g[0].reshape(1, -1),
                                    _split_w_uq(w_uq[0]), wuk, wuv, cos_t, sin_t, batch, seq, tm_proj)
    o_b = _mla_attn(q_cat, k_cat, v_mla.reshape(batch, seq, MLA_HEADS * MLA_V_DIM), tq, rg_mla)

    h1 = _merge_out(o_a.reshape(n_tok, -1), o_b.reshape(n_tok, -1), gates, x2,
                    w_o_diff[0].astype(bf), w_o_mla[0].astype(bf), w_out[0].astype(bf), tm_cross)

    kv_mem = _mem_kv(mem, mem_norm_g[0].reshape(1, d), w_ckv[0].astype(bf))
    n_router = N_GROUPS + N_EXPERTS
    w_r = jnp.concatenate([w_router_group[0].astype(jnp.float32), w_router_expert[0].astype(jnp.float32),
                           jnp.zeros((d, LANES - n_router), jnp.float32)], axis=1)
    w_r_hi = w_r.astype(bf)
    w_r_lo = (w_r - w_r_hi.astype(jnp.float32)).astype(bf)
    w_r = jnp.concatenate([w_r_hi, w_r_lo], axis=1)
    b_r = jnp.concatenate([b_router_group[0].astype(jnp.float32), b_router_expert[0].astype(jnp.float32),
                           jnp.zeros((LANES - n_router,), jnp.float32)]).reshape(1, LANES)
    g_ffn = ffn_norm_g[0].reshape(1, d)
    h2, eid, rank, wts, cnt = _cross_router(h1, cross_norm_g[0].reshape(1, d), w_cq[0].astype(bf), kv_mem,
                                            w_co[0].astype(bf), g_ffn, w_r, b_r, seq, tm_cross)

    bm = MOE_ROWS_PER_BLOCK
    assert tm_moe == bm
    counts = cnt[0, ROUTER_EXPERT_LANE0:ROUTER_EXPERT_LANE0 + N_EXPERTS].astype(jnp.int32)
    padded = ((counts + bm - 1) // bm) * bm
    padded_end = jnp.cumsum(padded)
    padded_off = padded_end - padded
    seg_start = jnp.sum(jnp.where(eid[..., None] == jnp.arange(N_EXPERTS, dtype=jnp.int32), padded_off, 0), axis=-1)
    dest = seg_start + rank
    p_rows = ((2 * n_tok + bm - 1) // bm) * bm + N_EXPERTS * bm
    n_blocks = p_rows // bm
    n_active = (padded_end[-1] // bm).astype(jnp.int32)
    blk = jnp.minimum(jnp.arange(n_blocks, dtype=jnp.int32), n_active - 1)
    block_expert = jnp.sum((padded_end[None, :] <= (blk * bm)[:, None]).astype(jnp.int32), axis=1)
    block_expert = jnp.minimum(block_expert, N_EXPERTS - 1)
    def tile_slots(tm):
        return dest.reshape(-1, 2, tm_cross // tm, tm).transpose(0, 2, 1, 3).reshape(n_tok // tm, 1, 2 * tm)
    pad_len = padded - counts
    bit = jnp.asarray(_pad_bits(), jnp.int32)
    higher = pad_len[:, None] & ~(2 * bit[None, :] - 1)
    pad_start = (padded_off + counts)[:, None] + higher
    pad_flag = (pad_len[:, None] & bit[None, :]) > 0
    half_blk = bm // 2
    unused = (n_active + jnp.arange(N_EXPERTS, dtype=jnp.int32))[:, None] * bm + jnp.arange(2, dtype=jnp.int32) * half_blk
    unused_flag = unused < n_blocks * bm
    zfill = jnp.stack([jnp.concatenate([pad_start.reshape(-1), jnp.minimum(unused, (n_blocks * bm - half_blk)).reshape(-1)]),
                       jnp.concatenate([pad_flag.reshape(-1), unused_flag.reshape(-1)]).astype(jnp.int32)]).astype(jnp.int32)
    sched = jnp.concatenate([n_active.reshape(1), padded // bm, padded_off // bm, counts]).astype(jnp.int32)

    xb = _dispatch(tile_slots(tm_row), zfill, h2, g_ffn, p_rows, tm_row)
    y = _experts(block_expert, sched, xb, w_expert_gate[0], w_expert_up[0], w_expert_down[0], bm)
    out = _combine(tile_slots(tm_moe), h2, wts, final_norm_g.reshape(1, d), y, tm_moe)
    return out.reshape(batch, seq, d)
```

```python
import functools
import math

import jax
import jax.numpy as jnp
from jax import lax
from jax.experimental import pallas as pl
from jax.experimental.pallas import tpu as pltpu

D_MODEL = 2048
ROPE_THETA = 500000.0
NORM_EPS = 1e-6

DIFF_HEADS = 8
DIFF_HEAD_DIM = 64
DIFF_V_DIM = 2 * DIFF_HEAD_DIM
DIFF_ROT = DIFF_HEAD_DIM // 4
DIFF_SUBLN_EPS = 1e-5
DIFF_LAMBDA_INIT = 0.8 - 0.6 * math.exp(-0.3 * 0)

MLA_HEADS = 8
MLA_Q_RANK = 512
MLA_KV_RANK = 256
MLA_NOPE_DIM = 128
MLA_ROPE_DIM = 64
MLA_V_DIM = 128
MLA_QK_DIM = MLA_NOPE_DIM + MLA_ROPE_DIM

CROSS_HEADS = 4
CROSS_HEAD_DIM = 128

N_GROUPS = 4
EXPERTS_PER_GROUP = 8
N_EXPERTS = N_GROUPS * EXPERTS_PER_GROUP

LANES = 128
BF16_TILE_ROWS = 16
LOG2E = 1.4426950408889634
VMEM_LIMIT_BYTES = 56 * 1024 * 1024

ROUTER_EXPERT_LANE0 = N_GROUPS

QKV_COLS = 3 * DIFF_HEADS * DIFF_V_DIM
LATENT_COLS = 1024
GATE_COLS = 2 * D_MODEL
KPE_COL0 = MLA_Q_RANK + MLA_KV_RANK

MOE_ROWS_PER_BLOCK = 512
SUBLANES = 8


def _params(*semantics):
    return pltpu.CompilerParams(dimension_semantics=semantics, vmem_limit_bytes=VMEM_LIMIT_BYTES)


def _resident(shape):
    zeros = (0,) * len(shape)
    return pl.BlockSpec(shape, lambda *_: zeros, pipeline_mode=pl.Buffered(1))


def _rms_scale(xf, eps):
    return lax.rsqrt(jnp.mean(xf * xf, axis=-1, keepdims=True) + eps)


def _sigmoid(x):
    return 0.5 * jnp.tanh(0.5 * x) + 0.5


def _pack_bf16_pair(a, b):
    hi = lax.bitcast_convert_type(a.astype(jnp.bfloat16).astype(jnp.float32), jnp.uint32)
    lo = lax.bitcast_convert_type(b.astype(jnp.bfloat16).astype(jnp.float32), jnp.uint32)
    return hi | (lo >> 16)


def _unpack_bf16_pair(w):
    a = lax.bitcast_convert_type(w & jnp.uint32(0xFFFF0000), jnp.float32)
    b = lax.bitcast_convert_type(w << 16, jnp.float32)
    return a, b


def _rows_to_tiles(ref_view, packed):
    rows = packed.shape[0]
    for c in range(SUBLANES):
        ref_view[pl.ds(c, rows, stride=SUBLANES), :] = packed[:, c * LANES:(c + 1) * LANES]


def _tiles_to_row_chunks(ref_view, rows):
    return [ref_view[pl.ds(c, rows, stride=SUBLANES), :] for c in range(SUBLANES)]


def _lane_iota(shape):
    return lax.broadcasted_iota(jnp.int32, shape, len(shape) - 1)


def _trig_kernel(pos_ref, invf_ref, cos_ref, sin_ref):
    ang = pos_ref[...].astype(jnp.float32) * invf_ref[...]
    cos_ref[...] = jnp.cos(ang)
    sin_ref[...] = jnp.sin(ang)


def _rope_tables(positions, n_tok, tm):
    half_m = MLA_ROPE_DIM // 2
    half_d = DIFF_ROT // 2
    inv_m = jnp.float32(ROPE_THETA) ** (-jnp.arange(half_m, dtype=jnp.float32) * 2.0 / MLA_ROPE_DIM)
    inv_d = jnp.float32(ROPE_THETA) ** (-jnp.arange(half_d, dtype=jnp.float32) * 2.0 / DIFF_ROT)
    invf = jnp.concatenate([inv_m, inv_m, inv_d, inv_d,
                            jnp.zeros((DIFF_HEAD_DIM - DIFF_ROT,), jnp.float32)]).reshape(1, LANES)
    pos = positions.reshape(n_tok, 1)
    return pl.pallas_call(
        _trig_kernel,
        grid=(n_tok // tm,),
        in_specs=[pl.BlockSpec((tm, 1), lambda i: (i, 0)), _resident((1, LANES))],
        out_specs=[pl.BlockSpec((tm, LANES), lambda i: (i, 0))] * 2,
        out_shape=[jax.ShapeDtypeStruct((n_tok, LANES), jnp.float32)] * 2,
        compiler_params=_params("parallel"),
        name="rope_tables",
    )(pos, invf)


def _diff_rope_coeffs(cos_t, sin_t):
    lane = _lane_iota(cos_t.shape)
    upper = lane >= DIFF_HEAD_DIM
    cos_d = jnp.where(upper, cos_t, pltpu.roll(cos_t, DIFF_HEAD_DIM, 1))
    sin_d = jnp.where(upper, sin_t, pltpu.roll(sin_t, DIFF_HEAD_DIM, 1))
    in_head = lane % DIFF_HEAD_DIM
    half = DIFF_ROT // 2
    s_next = jnp.where(in_head < half, -sin_d, 0.0)
    s_prev = jnp.where((in_head >= half) & (in_head < DIFF_ROT), sin_d, 0.0)
    return cos_d, s_next, s_prev


def _mla_rope(pair, cos_t, sin_t):
    lane = _lane_iota(pair.shape)
    sin_signed = jnp.where(lane < MLA_ROPE_DIM // 2, -sin_t, sin_t)
    return pair * cos_t + pltpu.roll(pair, MLA_ROPE_DIM, 1) * sin_signed


INPROJ_TN = 1024
INPROJ_PIECE = 256
Q_TILES = DIFF_HEADS * DIFF_V_DIM // INPROJ_TN
ROPE_TILES = 2 * Q_TILES
QKV_TILES = QKV_COLS // INPROJ_TN
LATENT_TILES = LATENT_COLS // INPROJ_TN
GATE_TILES = GATE_COLS // INPROJ_TN
INPROJ_TILES = QKV_TILES + LATENT_TILES + GATE_TILES
GATE_ROW0 = QKV_COLS + KPE_COL0 + MLA_ROPE_DIM


def _inproj_kernel(x_ref, g_ref, w_ref, cos_ref, sin_ref, qkv_ref, lat_ref, gate_ref, xn_ref):
    j = pl.program_id(1)

    @pl.when(j == 0)
    def _():
        xf = x_ref[...]
        xn_ref[...] = (xf * _rms_scale(xf, NORM_EPS) * g_ref[...]).astype(jnp.bfloat16)

    def pieces(epilogue):
        for c in range(INPROJ_TN // INPROJ_PIECE):
            cols = slice(c * INPROJ_PIECE, (c + 1) * INPROJ_PIECE)
            acc = lax.dot_general(xn_ref[...], w_ref[cols, :], (((1,), (1,)), ((), ())),
                                  preferred_element_type=jnp.float32)
            epilogue(acc, cols)

    @pl.when(j < ROPE_TILES)
    def _():
        cos_d, s_next, s_prev = _diff_rope_coeffs(cos_ref[...], sin_ref[...])
        qscale = jnp.where(j < Q_TILES, DIFF_HEAD_DIM ** -0.5 * LOG2E, 1.0).astype(jnp.float32)

        def rope(acc, cols):
            for c in range(INPROJ_PIECE // LANES):
                xc = acc[:, c * LANES:(c + 1) * LANES]
                rot = (xc * cos_d + pltpu.roll(xc, LANES - DIFF_ROT // 2, 1) * s_next
                       + pltpu.roll(xc, DIFF_ROT // 2, 1) * s_prev)
                lo = cols.start + c * LANES
                qkv_ref[:, lo:lo + LANES] = (rot * qscale).astype(qkv_ref.dtype)

        pieces(rope)

    @pl.when((j >= ROPE_TILES) & (j < QKV_TILES))
    def _():
        def value(acc, cols):
            qkv_ref[:, cols] = acc.astype(qkv_ref.dtype)

        pieces(value)

    @pl.when((j >= QKV_TILES) & (j < QKV_TILES + LATENT_TILES))
    def _():
        def latent(acc, cols):
            if cols.start <= KPE_COL0 < cols.stop:
                c0 = KPE_COL0 - cols.start
                v = acc[:, c0:c0 + LANES]
                lane = _lane_iota(v.shape)
                half = MLA_ROPE_DIM // 2
                swapped = jnp.where(lane < MLA_ROPE_DIM + half, pltpu.roll(v, half, 1),
                                    pltpu.roll(v, MLA_ROPE_DIM + half, 1))
                parts = [acc[:, :c0], jnp.where(lane < MLA_ROPE_DIM, v, swapped), acc[:, c0 + LANES:]]
                acc = jnp.concatenate([p for p in parts if p.shape[1]], axis=1)
            lat_ref[:, cols] = acc

        pieces(latent)

    @pl.when(j >= QKV_TILES + LATENT_TILES)
    def _():
        def gate(acc, cols):
            gate_ref[:, cols] = _sigmoid(acc).astype(gate_ref.dtype)

        pieces(gate)


def _inproj(x2, g, w_all, cos_t, sin_t, tm):
    n_tok, d = x2.shape
    tn = INPROJ_TN
    lat0 = QKV_TILES
    gate0 = QKV_TILES + LATENT_TILES

    def w_rows(i, j):
        start = jnp.where(j < gate0, j * tn, GATE_ROW0 + (j - gate0) * tn)
        return (pl.multiple_of(start, BF16_TILE_ROWS), 0)

    assert GATE_ROW0 % BF16_TILE_ROWS == 0
    return pl.pallas_call(
        _inproj_kernel,
        grid=(n_tok // tm, INPROJ_TILES),
        in_specs=[pl.BlockSpec((tm, d), lambda i, j: (i, 0)),
                  _resident((1, d)),
                  pl.BlockSpec((pl.Element(tn), pl.Element(d)), w_rows),
                  pl.BlockSpec((tm, LANES), lambda i, j: (i, 0)),
                  pl.BlockSpec((tm, LANES), lambda i, j: (i, 0))],
        out_specs=[pl.BlockSpec((tm, tn), lambda i, j: (i, jnp.clip(j, 0, QKV_TILES - 1))),
                   pl.BlockSpec((tm, tn), lambda i, j: (i, jnp.clip(j - lat0, 0, LATENT_TILES - 1))),
                   pl.BlockSpec((tm, tn), lambda i, j: (i, jnp.clip(j - gate0, 0, GATE_TILES - 1)))],
        out_shape=[jax.ShapeDtypeStruct((n_tok, QKV_COLS), jnp.bfloat16),
                   jax.ShapeDtypeStruct((n_tok, LATENT_COLS), jnp.float32),
                   jax.ShapeDtypeStruct((n_tok, GATE_COLS), jnp.bfloat16)],
        scratch_shapes=[pltpu.VMEM((tm, d), jnp.bfloat16)],
        compiler_params=_params("parallel", "arbitrary"),
        name="inproj",
    )(x2, g, w_all, cos_t, sin_t)


def _mla_proj_kernel(c_ref, gq_ref, gkv_ref, wuq_ref, wuk_ref, wuv_ref, cos_ref, sin_ref,
                     q_ref, k_ref, v_ref):
    cos_t = cos_ref[...]
    sin_t = sin_ref[...]
    cq = c_ref[:, :MLA_Q_RANK]
    cqn = (cq * _rms_scale(cq, NORM_EPS) * gq_ref[...]).astype(jnp.bfloat16)
    ckv = c_ref[:, MLA_Q_RANK:KPE_COL0]
    ckvn = (ckv * _rms_scale(ckv, NORM_EPS) * gkv_ref[...]).astype(jnp.bfloat16)
    kpe = _mla_rope(c_ref[:, KPE_COL0:KPE_COL0 + LANES], cos_t, sin_t)[:, :MLA_ROPE_DIM].astype(k_ref.dtype)
    qscale = MLA_QK_DIM ** -0.5 * LOG2E
    for h in range(MLA_HEADS):
        r = jnp.dot(cqn, wuq_ref[h], preferred_element_type=jnp.float32)
        q_ref[0, h, :, :MLA_NOPE_DIM] = (r[:, :MLA_NOPE_DIM] * qscale).astype(q_ref.dtype)
        qpe = _mla_rope(r[:, MLA_NOPE_DIM:], cos_t, sin_t)[:, :MLA_ROPE_DIM]
        q_ref[0, h, :, MLA_NOPE_DIM:] = (qpe * qscale).astype(q_ref.dtype)
        kn = jnp.dot(ckvn, wuk_ref[h], preferred_element_type=jnp.float32)
        k_ref[0, h, :, :MLA_NOPE_DIM] = kn.astype(k_ref.dtype)
        k_ref[0, h, :, MLA_NOPE_DIM:] = kpe
    v_ref[...] = jnp.dot(ckvn, wuv_ref[...], preferred_element_type=jnp.float32).astype(v_ref.dtype)


def _mla_proj(latent, gq, gkv, wuq, wuk, wuv, cos_t, sin_t, batch, seq, tm):
    n_tok = latent.shape[0]
    per_b = seq // tm
    head_spec = pl.BlockSpec((1, MLA_HEADS, tm, MLA_QK_DIM), lambda i: (i // per_b, 0, i % per_b, 0))
    head_shape = jax.ShapeDtypeStruct((batch, MLA_HEADS, seq, MLA_QK_DIM), jnp.bfloat16)
    return pl.pallas_call(
        _mla_proj_kernel,
        grid=(n_tok // tm,),
        in_specs=[pl.BlockSpec((tm, LATENT_COLS), lambda i: (i, 0)),
                  _resident(gq.shape), _resident(gkv.shape),
                  _resident(wuq.shape), _resident(wuk.shape), _resident(wuv.shape),
                  pl.BlockSpec((tm, LANES), lambda i: (i, 0)),
                  pl.BlockSpec((tm, LANES), lambda i: (i, 0))],
        out_specs=[head_spec, head_spec,
                   pl.BlockSpec((tm, MLA_HEADS * MLA_V_DIM), lambda i: (i, 0))],
        out_shape=[head_shape, head_shape,
                   jax.ShapeDtypeStruct((n_tok, MLA_HEADS * MLA_V_DIM), jnp.bfloat16)],
        compiler_params=_params("parallel"),
        name="mla_proj",
    )(latent, gq, gkv, wuq, wuk, wuv, cos_t, sin_t)


def _with_ones(v):
    return jnp.concatenate([v, jnp.ones((v.shape[0], LANES), v.dtype)], axis=-1)


def _softmax_pv(s, v_ones):
    m = jnp.max(s, axis=-1, keepdims=True)
    p = jnp.exp2(s - m).astype(v_ones.dtype)
    pv = jnp.dot(p, v_ones, preferred_element_type=jnp.float32)
    dv = v_ones.shape[1] - LANES
    return pv[:, :dv] / pv[:, dv:]


ATTN_HEADS_PER_STEP = 2


def _diff_attn_kernel(q_ref, k_ref, v_ref, lq1_ref, lk1_ref, lq2_ref, lk2_ref, g_ref, o_ref, v1_ref, *, rg):
    heads = q_ref.shape[2] // LANES

    @pl.when(pl.program_id(2) == 0)
    def _():
        for h in range(heads):
            v1_ref[h] = _with_ones(v_ref[0, :, h * DIFF_V_DIM:(h + 1) * DIFF_V_DIM])

    lam = (jnp.exp(jnp.sum(lq1_ref[...] * lk1_ref[...], axis=-1, keepdims=True))
           - jnp.exp(jnp.sum(lq2_ref[...] * lk2_ref[...], axis=-1, keepdims=True))
           + DIFF_LAMBDA_INIT)
    lane = _lane_iota((rg, LANES))
    for g in range(q_ref.shape[1] // rg):
        for h in range(heads):
            cols = slice(h * LANES, (h + 1) * LANES)
            q = q_ref[0, g * rg:(g + 1) * rg, cols]
            k = k_ref[0, :, cols]
            zero = jnp.zeros_like(q)
            q12 = jnp.concatenate([jnp.where(lane < DIFF_HEAD_DIM, q, zero),
                                   jnp.where(lane >= DIFF_HEAD_DIM, q, zero)], axis=0)
            s = lax.dot_general(q12, k, (((1,), (1,)), ((), ())), preferred_element_type=jnp.float32)
            a = _softmax_pv(s, v1_ref[h])
            o = a[:rg] - lam * a[rg:]
            o = o * _rms_scale(o, DIFF_SUBLN_EPS) * g_ref[...] * (1.0 - DIFF_LAMBDA_INIT)
            o_ref[0, g * rg:(g + 1) * rg, cols] = o.astype(o_ref.dtype)


def _diff_attn(qkv3, lq1, lk1, lq2, lk2, subln_g, tq, rg):
    batch, seq, _ = qkv3.shape
    hp = ATTN_HEADS_PER_STEP
    steps = DIFF_HEADS // hp
    width = hp * DIFF_V_DIM
    return pl.pallas_call(
        functools.partial(_diff_attn_kernel, rg=rg),
        grid=(batch, steps, seq // tq),
        in_specs=[pl.BlockSpec((1, tq, width), lambda b, hh, i: (b, i, hh)),
                  pl.BlockSpec((1, seq, width), lambda b, hh, i: (b, 0, steps + hh)),
                  pl.BlockSpec((1, seq, width), lambda b, hh, i: (b, 0, 2 * steps + hh)),
                  _resident(lq1.shape), _resident(lk1.shape), _resident(lq2.shape), _resident(lk2.shape),
                  _resident(subln_g.shape)],
        out_specs=pl.BlockSpec((1, tq, width), lambda b, hh, i: (b, i, hh)),
        out_shape=jax.ShapeDtypeStruct((batch, seq, DIFF_HEADS * DIFF_V_DIM), jnp.bfloat16),
        scratch_shapes=[pltpu.VMEM((hp, seq, DIFF_V_DIM + LANES), jnp.bfloat16)],
        compiler_params=_params("parallel", "parallel", "arbitrary"),
        name="diff_attn",
    )(qkv3, qkv3, qkv3, lq1, lk1, lq2, lk2, subln_g)


def _mla_attn_kernel(q_ref, k_ref, v_ref, o_ref, v1_ref, *, rg):
    heads = q_ref.shape[1]

    @pl.when(pl.program_id(2) == 0)
    def _():
        for h in range(heads):
            v1_ref[h] = _with_ones(v_ref[0, :, h * MLA_V_DIM:(h + 1) * MLA_V_DIM])

    for g in range(q_ref.shape[2] // rg):
        for h in range(heads):
            s = lax.dot_general(q_ref[0, h, g * rg:(g + 1) * rg], k_ref[0, h], (((1,), (1,)), ((), ())),
                                preferred_element_type=jnp.float32)
            o_ref[0, g * rg:(g + 1) * rg, h * MLA_V_DIM:(h + 1) * MLA_V_DIM] = (
                _softmax_pv(s, v1_ref[h]).astype(o_ref.dtype))


def _mla_attn(q_cat, k_cat, v3, tq, rg):
    batch, heads, seq, dqk = q_cat.shape
    hp = ATTN_HEADS_PER_STEP
    return pl.pallas_call(
        functools.partial(_mla_attn_kernel, rg=rg),
        grid=(batch, heads // hp, seq // tq),
        in_specs=[pl.BlockSpec((1, hp, tq, dqk), lambda b, h, i: (b, h, i, 0)),
                  pl.BlockSpec((1, hp, seq, dqk), lambda b, h, i: (b, h, 0, 0)),
                  pl.BlockSpec((1, seq, hp * MLA_V_DIM), lambda b, h, i: (b, 0, h))],
        out_specs=pl.BlockSpec((1, tq, hp * MLA_V_DIM), lambda b, h, i: (b, i, h)),
        out_shape=jax.ShapeDtypeStruct((batch, seq, heads * MLA_V_DIM), jnp.bfloat16),
        scratch_shapes=[pltpu.VMEM((hp, seq, MLA_V_DIM + LANES), jnp.bfloat16)],
        compiler_params=_params("parallel", "parallel", "arbitrary"),
        name="mla_attn",
    )(q_cat, k_cat, v3)


def _merge_out_kernel(oa_ref, ob_ref, sga_ref, sgb_ref, x_ref, woa_ref, wob_ref, wout_ref, h_ref):
    ya = jnp.dot(oa_ref[...], woa_ref[...], preferred_element_type=jnp.float32)
    yb = jnp.dot(ob_ref[...], wob_ref[...], preferred_element_type=jnp.float32)
    merged = sga_ref[...].astype(jnp.float32) * ya + sgb_ref[...].astype(jnp.float32) * yb
    h_ref[...] = x_ref[...] + jnp.dot(merged.astype(jnp.bfloat16), wout_ref[...],
                                       preferred_element_type=jnp.float32)


def _merge_out(o_a, o_b, gates, x2, w_oa, w_ob, w_out, tm):
    n_tok, d = x2.shape
    return pl.pallas_call(
        _merge_out_kernel,
        grid=(n_tok // tm,),
        in_specs=[pl.BlockSpec((tm, o_a.shape[1]), lambda i: (i, 0)),
                  pl.BlockSpec((tm, o_b.shape[1]), lambda i: (i, 0)),
                  pl.BlockSpec((tm, d), lambda i: (i, 0)),
                  pl.BlockSpec((tm, d), lambda i: (i, 1)),
                  pl.BlockSpec((tm, d), lambda i: (i, 0)),
                  _resident(w_oa.shape), _resident(w_ob.shape), _resident(w_out.shape)],
        out_specs=pl.BlockSpec((tm, d), lambda i: (i, 0)),
        out_shape=jax.ShapeDtypeStruct((n_tok, d), jnp.float32),
        compiler_params=_params("parallel"),
        name="merge_out",
    )(o_a, o_b, gates, gates, x2, w_oa, w_ob, w_out)


def _mem_kv_kernel(mem_ref, g_ref, w_ref, kv_ref):
    mf = mem_ref[0]
    mn = (mf * _rms_scale(mf, NORM_EPS) * g_ref[...]).astype(jnp.bfloat16)
    kv_ref[0] = jnp.dot(mn, w_ref[...], preferred_element_type=jnp.float32).astype(kv_ref.dtype)


def _mem_kv(mem, g, w_ckv):
    batch, m, d = mem.shape
    return pl.pallas_call(
        _mem_kv_kernel,
        grid=(batch,),
        in_specs=[pl.BlockSpec((1, m, d), lambda b: (b, 0, 0)), _resident(g.shape), _resident(w_ckv.shape)],
        out_specs=pl.BlockSpec((1, m, w_ckv.shape[1]), lambda b: (b, 0, 0)),
        out_shape=jax.ShapeDtypeStruct((batch, m, w_ckv.shape[1]), jnp.bfloat16),
        compiler_params=_params("parallel"),
        name="mem_kv",
    )(mem, g, w_ckv)


def _cross_router_kernel(h_ref, gc_ref, wcq_ref, kv_ref, wco_ref, gf_ref, wr_ref, br_ref,
                         h2_ref, eid_ref, rank_ref, wts_ref, cnt_ref, carry_ref):
    i = pl.program_id(0)

    @pl.when(i == 0)
    def _():
        carry_ref[...] = jnp.zeros_like(carry_ref)

    h1 = h_ref[...]
    tm = h1.shape[0]
    hn = (h1 * _rms_scale(h1, NORM_EPS) * gc_ref[...]).astype(jnp.bfloat16)
    q = jnp.dot(hn, wcq_ref[...], preferred_element_type=jnp.float32) * (CROSS_HEAD_DIM ** -0.5 * LOG2E)
    q = q.astype(jnp.bfloat16)
    kv_cols = CROSS_HEADS * CROSS_HEAD_DIM
    heads = []
    for hd in range(CROSS_HEADS):
        lo = hd * CROSS_HEAD_DIM
        kh = kv_ref[0, :, lo:lo + CROSS_HEAD_DIM]
        vh = kv_ref[0, :, kv_cols + lo:kv_cols + lo + CROSS_HEAD_DIM]
        s = lax.dot_general(q[:, lo:lo + CROSS_HEAD_DIM], kh, (((1,), (1,)), ((), ())),
                            preferred_element_type=jnp.float32)
        heads.append(_softmax_pv(s, _with_ones(vh)).astype(jnp.bfloat16))
    o = jnp.concatenate(heads, axis=-1)
    h2 = h1 + jnp.dot(o, wco_ref[...], preferred_element_type=jnp.float32)
    h2_ref[...] = h2

    t = h2 * _rms_scale(h2, NORM_EPS) * gf_ref[...]
    t_hi = t.astype(jnp.bfloat16)
    t_lo = (t - t_hi.astype(jnp.float32)).astype(jnp.bfloat16)
    hi = jnp.dot(t_hi, wr_ref[...], preferred_element_type=jnp.float32)
    lo = jnp.dot(t_lo, wr_ref[:, :LANES], preferred_element_type=jnp.float32)
    logits = hi[:, :LANES] + (hi[:, LANES:] + lo) + br_ref[...]
    lane = _lane_iota(logits.shape)
    neg = jnp.float32(-jnp.inf)
    big = jnp.int32(2 * LANES)
    is_group = lane < N_GROUPS
    lg = jnp.where(is_group, logits, neg)
    mg = jnp.max(lg, axis=-1, keepdims=True)
    g_idx = jnp.min(jnp.where(is_group & (logits == mg), lane, big), axis=-1, keepdims=True)
    g_p = 1.0 / jnp.sum(jnp.exp(lg - mg), axis=-1, keepdims=True)
    lo_lane = ROUTER_EXPERT_LANE0 + EXPERTS_PER_GROUP * g_idx
    in_grp = (lane >= lo_lane) & (lane < lo_lane + EXPERTS_PER_GROUP)
    l1 = jnp.max(jnp.where(in_grp, logits, neg), axis=-1, keepdims=True)
    i1 = jnp.min(jnp.where(in_grp & (logits == l1), lane, big), axis=-1, keepdims=True)
    rest = in_grp & (lane != i1)
    l2 = jnp.max(jnp.where(rest, logits, neg), axis=-1, keepdims=True)
    i2 = jnp.min(jnp.where(rest & (logits == l2), lane, big), axis=-1, keepdims=True)
    d = jnp.exp(l2 - l1)
    w1 = g_p / (1.0 + d)
    w2 = w1 * d

    oh1 = lane == i1
    oh2 = lane == i2
    cnt = (oh1 | oh2).astype(jnp.bfloat16)
    row = lax.broadcasted_iota(jnp.int32, (tm, tm), 0)
    col = lax.broadcasted_iota(jnp.int32, (tm, tm), 1)
    before = (col < row).astype(jnp.bfloat16)
    slot = jnp.dot(before, cnt, preferred_element_type=jnp.float32) + carry_ref[...]
    r1 = jnp.sum(jnp.where(oh1, slot, 0.0), axis=-1, keepdims=True)
    r2 = jnp.sum(jnp.where(oh2, slot, 0.0), axis=-1, keepdims=True)
    carry_ref[...] += jnp.sum(cnt.astype(jnp.float32), axis=0, keepdims=True)
    cnt_ref[...] = carry_ref[...]

    eye = row == col

    def to_row(c, dtype):
        return jnp.sum(jnp.where(eye, c.astype(jnp.float32), 0.0), axis=0, keepdims=True).astype(dtype)

    eid_ref[0] = jnp.concatenate([to_row(i1 - ROUTER_EXPERT_LANE0, jnp.int32),
                                  to_row(i2 - ROUTER_EXPERT_LANE0, jnp.int32)], axis=0)
    rank_ref[0] = jnp.concatenate([to_row(r1, jnp.int32), to_row(r2, jnp.int32)], axis=0)
    wts_ref[...] = jnp.where(_lane_iota((tm, 2)) == 0, w1, w2)


def _cross_router(h1, gc, w_cq, kv_mem, w_co, gf, w_r, b_r, seq, tm):
    n_tok, d = h1.shape
    per_b = seq // tm
    row2 = pl.BlockSpec((tm, 2), lambda i: (i, 0))
    lane2 = pl.BlockSpec((1, 2, tm), lambda i: (i, 0, 0))
    return pl.pallas_call(
        _cross_router_kernel,
        grid=(n_tok // tm,),
        in_specs=[pl.BlockSpec((tm, d), lambda i: (i, 0)),
                  _resident(gc.shape), _resident(w_cq.shape),
                  pl.BlockSpec((1,) + kv_mem.shape[1:], lambda i: (i // per_b, 0, 0)),
                  _resident(w_co.shape), _resident(gf.shape), _resident(w_r.shape), _resident(b_r.shape)],
        out_specs=[pl.BlockSpec((tm, d), lambda i: (i, 0)), lane2, lane2, row2,
                   pl.BlockSpec((1, LANES), lambda i: (0, 0))],
        out_shape=[jax.ShapeDtypeStruct((n_tok, d), jnp.float32),
                   jax.ShapeDtypeStruct((n_tok // tm, 2, tm), jnp.int32),
                   jax.ShapeDtypeStruct((n_tok // tm, 2, tm), jnp.int32),
                   jax.ShapeDtypeStruct((n_tok, 2), jnp.float32),
                   jax.ShapeDtypeStruct((1, LANES), jnp.float32)],
        scratch_shapes=[pltpu.VMEM((1, LANES), jnp.float32)],
        compiler_params=_params("arbitrary"),
        name="cross_router",
    )(h1, gc, w_cq, kv_mem, w_co, gf, w_r, b_r)


def _pad_bits():
    return [1 << b for b in reversed(range(MOE_ROWS_PER_BLOCK.bit_length() - 1))]


def _zero_fill_sizes():
    return _pad_bits() * N_EXPERTS + [MOE_ROWS_PER_BLOCK // 2] * (2 * N_EXPERTS)


def _dispatch_kernel(dest_ref, zfill_ref, h_ref, g_ref, xb_ref, t_ref, zero_ref, sem, zsem):
    i = pl.program_id(0)
    tm = h_ref.shape[0]
    slot = i % 2

    def zero_copies(action):
        for idx, rows in enumerate(_zero_fill_sizes()):
            @pl.when(zfill_ref[1, idx] > 0)
            def _():
                start = pl.multiple_of(zfill_ref[0, idx] * SUBLANES, SUBLANES)
                action(pltpu.make_async_copy(zero_ref.at[pl.ds(0, rows * SUBLANES), :],
                                             xb_ref.at[pl.ds(start, rows * SUBLANES), :], zsem))

    @pl.when(i == 0)
    def _():
        zero_ref[...] = jnp.zeros_like(zero_ref)
        zero_copies(lambda c: c.start())

    h2 = h_ref[...]
    t = h2 * _rms_scale(h2, NORM_EPS) * g_ref[...]
    half = t.shape[1] // 2
    _rows_to_tiles(t_ref.at[slot], _pack_bf16_pair(t[:, :half], t[:, half:]))

    for r in range(tm):
        for k in range(2):
            dst = pl.multiple_of(dest_ref[0, k * tm + r] * SUBLANES, SUBLANES)
            pltpu.make_async_copy(t_ref.at[slot, pl.ds(r * SUBLANES, SUBLANES), :],
                                  xb_ref.at[pl.ds(dst, SUBLANES), :], sem.at[slot]).start(priority=k)

    def wait_tile(which):
        for _ in range(2):
            pltpu.make_async_copy(t_ref.at[which], xb_ref.at[pl.ds(0, tm * SUBLANES), :], sem.at[which]).wait()

    @pl.when(i > 0)
    def _():
        wait_tile(1 - slot)

    @pl.when(i == pl.num_programs(0) - 1)
    def _():
        wait_tile(slot)
        zero_copies(lambda c: c.wait())


def _dispatch(dest3, zfill, h2, gf, p_rows, tm):
    n_tok, d = h2.shape
    return pl.pallas_call(
        _dispatch_kernel,
        grid=(n_tok // tm,),
        in_specs=[pl.BlockSpec((None, 1, 2 * tm), lambda i: (i, 0, 0), memory_space=pltpu.SMEM),
                  pl.BlockSpec(memory_space=pltpu.SMEM),
                  pl.BlockSpec((tm, d), lambda i: (i, 0)),
                  _resident(gf.shape)],
        out_specs=pl.BlockSpec(memory_space=pl.ANY),
        out_shape=jax.ShapeDtypeStruct((p_rows * SUBLANES, LANES), jnp.uint32),
        scratch_shapes=[pltpu.VMEM((2, tm * SUBLANES, LANES), jnp.uint32),
                        pltpu.VMEM((MOE_ROWS_PER_BLOCK // 2 * SUBLANES, LANES), jnp.uint32),
                        pltpu.SemaphoreType.DMA((2,)), pltpu.SemaphoreType.DMA(())],
        compiler_params=_params("arbitrary"),
        name="moe_dispatch",
    )(dest3, zfill, h2, gf)


def _expert_kernel(be_ref, nact_ref, x_ref, wg_hbm, wu_hbm, wd_hbm, y_ref,
                   wg_f, wu_f, wd_f, wg_b, wu_b, wd_b, sem):
    i = pl.program_id(0)
    blk = i - 1
    nact = nact_ref[0]
    last_blk = pl.num_programs(0) - 2

    def weight_copies(e):
        return (pltpu.make_async_copy(wg_hbm.at[e], wg_f, sem),
                pltpu.make_async_copy(wu_hbm.at[e], wu_f, sem),
                pltpu.make_async_copy(wd_hbm.at[e], wd_f, sem))

    def fetch(e):
        for c in weight_copies(e):
            c.start(priority=1)

    def land(e):
        for c in weight_copies(e):
            c.wait()
        half = wg_f.shape[0] // 2
        for c in range(SUBLANES):
            for part, src0 in enumerate((c * LANES, half + c * LANES)):
                dst0 = (2 * c + part) * LANES
                wg_b[dst0:dst0 + LANES, :] = wg_f[src0:src0 + LANES, :].astype(jnp.bfloat16)
                wu_b[dst0:dst0 + LANES, :] = wu_f[src0:src0 + LANES, :].astype(jnp.bfloat16)
        wd_b[...] = wd_f[...].astype(jnp.bfloat16)

    @pl.when(i == 0)
    def _():
        fetch(be_ref[0])
        land(be_ref[0])

    @pl.when((i > 0) & (blk < nact))
    def _():
        here = be_ref[blk]
        nxt = be_ref[jnp.minimum(blk + 1, last_blk)]
        prv = be_ref[jnp.maximum(blk - 1, 0)]
        seg_end = nact_ref[1 + N_EXPERTS + here] + nact_ref[1 + here]
        has_next = seg_end < nact
        after = be_ref[jnp.minimum(seg_end, last_blk)]
        is_first = (blk == 0) | (prv != here)
        is_last = (blk + 1 >= nact) | (nxt != here)

        @pl.when(is_first & has_next)
        def _():
            fetch(after)

        bm = x_ref.shape[0] // SUBLANES
        hb = bm // 2
        rows_here = nact_ref[1 + 2 * N_EXPERTS + here] - (blk - nact_ref[1 + N_EXPERTS + here]) * bm

        def swiglu(h):
            view = pl.ds(h * hb * SUBLANES, hb * SUBLANES)
            parts = []
            for chunk in _tiles_to_row_chunks(x_ref.at[view, :], hb):
                x_a, x_b = _unpack_bf16_pair(chunk)
                parts += [x_a.astype(jnp.bfloat16), x_b.astype(jnp.bfloat16)]
            xb = jnp.concatenate(parts, axis=1)
            gate = jnp.dot(xb, wg_b[...], preferred_element_type=jnp.float32)
            up = jnp.dot(xb, wu_b[...], preferred_element_type=jnp.float32)
            hid = (gate * _sigmoid(gate) * up).astype(jnp.bfloat16)
            y = jnp.dot(hid, wd_b[...], preferred_element_type=jnp.float32)
            half = y.shape[1] // 2
            _rows_to_tiles(y_ref.at[view, :], _pack_bf16_pair(y[:, :half], y[:, half:]))

        swiglu(0)

        @pl.when(rows_here > hb)
        def _():
            swiglu(1)

        @pl.when(rows_here <= hb)
        def _():
            y_ref[pl.ds(hb * SUBLANES, hb * SUBLANES), :] = jnp.zeros((hb * SUBLANES, LANES), y_ref.dtype)

        @pl.when(is_last & has_next)
        def _():
            land(after)

    @pl.when((i > 0) & (blk >= nact))
    def _():
        y_ref[...] = jnp.zeros_like(y_ref)


def _experts(block_expert, sched, xb, w_gate, w_up, w_down, bm):
    p_rows = xb.shape[0] // SUBLANES
    d = w_gate.shape[1]
    de = w_gate.shape[-1]

    def x_map(i, be, sc):
        return (jnp.clip(i - 1, 0, sc[0] - 1), 0)

    grid_spec = pltpu.PrefetchScalarGridSpec(
        num_scalar_prefetch=2,
        grid=(p_rows // bm + 1,),
        in_specs=[pl.BlockSpec((bm * SUBLANES, LANES), x_map),
                  pl.BlockSpec(memory_space=pl.ANY),
                  pl.BlockSpec(memory_space=pl.ANY),
                  pl.BlockSpec(memory_space=pl.ANY)],
        out_specs=pl.BlockSpec((bm * SUBLANES, LANES), lambda i, be, sc: (jnp.maximum(i - 1, 0), 0)),
        scratch_shapes=[pltpu.VMEM((d, de), jnp.float32), pltpu.VMEM((d, de), jnp.float32),
                        pltpu.VMEM((de, d), jnp.float32),
                        pltpu.VMEM((d, de), jnp.bfloat16), pltpu.VMEM((d, de), jnp.bfloat16),
                        pltpu.VMEM((de, d), jnp.bfloat16),
                        pltpu.SemaphoreType.DMA(())],
    )
    return pl.pallas_call(
        _expert_kernel,
        grid_spec=grid_spec,
        out_shape=jax.ShapeDtypeStruct(xb.shape, jnp.uint32),
        compiler_params=_params("arbitrary"),
        name="moe_experts",
    )(block_expert, sched, xb, w_gate, w_up, w_down)


def _combine_kernel(dest_ref, dest_next_ref, h_ref, wts_ref, g_ref, y_ref, o_ref, ybuf, sem):
    i = pl.program_id(0)
    tm = h_ref.shape[0]
    slot = i % 2

    def gather(idx_ref, which):
        for r in range(tm):
            for k in range(2):
                src = pl.multiple_of(idx_ref[0, k * tm + r] * SUBLANES, SUBLANES)
                pltpu.make_async_copy(y_ref.at[pl.ds(src, SUBLANES), :],
                                      ybuf.at[which, k, pl.ds(r * SUBLANES, SUBLANES), :], sem.at[which]).start(priority=k)

    def wait_tile(which):
        for k in range(2):
            pltpu.make_async_copy(y_ref.at[pl.ds(0, tm * SUBLANES), :], ybuf.at[which, k], sem.at[which]).wait()

    @pl.when(i == 0)
    def _():
        gather(dest_ref, slot)

    wait_tile(slot)
    gather(dest_next_ref, 1 - slot)

    w = wts_ref[...]
    half = h_ref.shape[1] // 2
    lo_parts, hi_parts = [], []
    for c, (c0, c1) in enumerate(zip(_tiles_to_row_chunks(ybuf.at[slot, 0], tm),
                                      _tiles_to_row_chunks(ybuf.at[slot, 1], tm))):
        a0, b0 = _unpack_bf16_pair(c0)
        a1, b1 = _unpack_bf16_pair(c1)
        lo_parts.append(h_ref[:, c * LANES:(c + 1) * LANES] + w[:, 0:1] * a0 + w[:, 1:2] * a1)
        hi_parts.append(h_ref[:, half + c * LANES:half + (c + 1) * LANES] + w[:, 0:1] * b0 + w[:, 1:2] * b1)
    h3 = jnp.concatenate(lo_parts + hi_parts, axis=1)
    o_ref[...] = h3 * _rms_scale(h3, NORM_EPS) * g_ref[...]

    @pl.when(i == pl.num_programs(0) - 1)
    def _():
        wait_tile(1 - slot)


def _combine(dest3, h2, wts, g_final, y, tm):
    n_tok, d = h2.shape
    last = n_tok // tm - 1
    return pl.pallas_call(
        _combine_kernel,
        grid=(n_tok // tm,),
        in_specs=[pl.BlockSpec((None, 1, 2 * tm), lambda i: (i, 0, 0), memory_space=pltpu.SMEM),
                  pl.BlockSpec((None, 1, 2 * tm), lambda i: (jnp.minimum(i + 1, last), 0, 0),
                               memory_space=pltpu.SMEM),
                  pl.BlockSpec((tm, d), lambda i: (i, 0)),
                  pl.BlockSpec((tm, 2), lambda i: (i, 0)),
                  _resident(g_final.shape),
                  pl.BlockSpec(memory_space=pl.ANY)],
        out_specs=pl.BlockSpec((tm, d), lambda i: (i, 0)),
        out_shape=jax.ShapeDtypeStruct((n_tok, d), jnp.float32),
        scratch_shapes=[pltpu.VMEM((2, 2, tm * SUBLANES, LANES), jnp.uint32), pltpu.SemaphoreType.DMA((2,))],
        compiler_params=_params("arbitrary"),
        name="moe_combine",
    )(dest3, dest3, h2, wts, g_final, y)


def _transpose_w_in(w_in):
    return jnp.swapaxes(w_in, 0, 1).astype(jnp.bfloat16)


def _split_w_uq(w_uq):
    half = MLA_ROPE_DIM // 2
    w = w_uq.reshape(MLA_Q_RANK, MLA_HEADS, MLA_QK_DIM).transpose(1, 0, 2)
    pe = w[:, :, MLA_NOPE_DIM:]
    pe_swapped = jnp.concatenate([pe[:, :, half:], pe[:, :, :half]], axis=2)
    return jnp.concatenate([w, pe_swapped], axis=2).astype(jnp.bfloat16)


def _split_w_ukv(w_ukv):
    w = w_ukv.reshape(MLA_KV_RANK, MLA_HEADS, MLA_NOPE_DIM + MLA_V_DIM)
    wuk = w[:, :, :MLA_NOPE_DIM].transpose(1, 0, 2).astype(jnp.bfloat16)
    wuv = w[:, :, MLA_NOPE_DIM:].reshape(MLA_KV_RANK, MLA_HEADS * MLA_V_DIM).astype(jnp.bfloat16)
    return wuk, wuv


def kernel(x, mem, positions, attn_norm_g, w_in, diff_lambda_q1, diff_lambda_k1, diff_lambda_q2, diff_lambda_k2, diff_subln_g, w_o_diff, mla_q_norm_g, w_uq, mla_kv_norm_g, w_ukv, w_o_mla, w_out, cross_norm_g, mem_norm_g, w_cq, w_ckv, w_co, ffn_norm_g, w_router_group, b_router_group, w_router_expert, b_router_expert, w_expert_gate, w_expert_up, w_expert_down, final_norm_g):
    batch, seq, d = x.shape
    assert d == D_MODEL and w_in.shape[0] == 1, "single-layer kernel"
    n_tok = batch * seq
    bf = jnp.bfloat16
    x2 = x.reshape(n_tok, d)

    tm_proj = min(1024, seq)
    tm_row = min(256, seq)
    tm_moe = min(MOE_ROWS_PER_BLOCK, seq)
    tm_cross = min(512, seq)
    tq = min(2048, seq)
    rg_diff = 128
    rg_mla = 256

    cos_t, sin_t = _rope_tables(positions, n_tok, tm_proj)

    g_attn = attn_norm_g[0].reshape(1, d)
    qkv, latent, gates = _inproj(x2, g_attn, _transpose_w_in(w_in[0]), cos_t, sin_t, tm_proj)

    o_a = _diff_attn(qkv.reshape(batch, seq, QKV_COLS),
                     diff_lambda_q1[0].reshape(1, -1), diff_lambda_k1[0].reshape(1, -1),
                     diff_lambda_q2[0].reshape(1, -1), diff_lambda_k2[0].reshape(1, -1),
                     diff_subln_g[0].reshape(1, -1), tq, rg_diff)

    wuk, wuv = _split_w_ukv(w_ukv[0])
    q_cat, k_cat, v_mla = _mla_proj(latent, mla_q_norm_g[0].reshape(1, -1), mla_kv_norm_g[0].reshape(1, -1),
                                    _split_w_uq(w_uq[0]), wuk, wuv, cos_t, sin_t, batch, seq, tm_proj)
    o_b = _mla_attn(q_cat, k_cat, v_mla.reshape(batch, seq, MLA_HEADS * MLA_V_DIM), tq, rg_mla)

    h1 = _merge_out(o_a.reshape(n_tok, -1), o_b.reshape(n_tok, -1), gates, x2,
                    w_o_diff[0].astype(bf), w_o_mla[0].astype(bf), w_out[0].astype(bf), tm_cross)

    kv_mem = _mem_kv(mem, mem_norm_g[0].reshape(1, d), w_ckv[0].astype(bf))
    n_router = N_GROUPS + N_EXPERTS
    w_r = jnp.concatenate([w_router_group[0].astype(jnp.float32), w_router_expert[0].astype(jnp.float32),
                           jnp.zeros((d, LANES - n_router), jnp.float32)], axis=1)
    w_r_hi = w_r.astype(bf)
    w_r_lo = (w_r - w_r_hi.astype(jnp.float32)).astype(bf)
    w_r = jnp.concatenate([w_r_hi, w_r_lo], axis=1)
    b_r = jnp.concatenate([b_router_group[0].astype(jnp.float32), b_router_expert[0].astype(jnp.float32),
                           jnp.zeros((LANES - n_router,), jnp.float32)]).reshape(1, LANES)
    g_ffn = ffn_norm_g[0].reshape(1, d)
    h2, eid, rank, wts, cnt = _cross_router(h1, cross_norm_g[0].reshape(1, d), w_cq[0].astype(bf), kv_mem,
                                            w_co[0].astype(bf), g_ffn, w_r, b_r, seq, tm_cross)

    bm = MOE_ROWS_PER_BLOCK
    assert tm_moe == bm
    counts = cnt[0, ROUTER_EXPERT_LANE0:ROUTER_EXPERT_LANE0 + N_EXPERTS].astype(jnp.int32)
    padded = ((counts + bm - 1) // bm) * bm
    padded_end = jnp.cumsum(padded)
    padded_off = padded_end - padded
    seg_start = jnp.sum(jnp.where(eid[..., None] == jnp.arange(N_EXPERTS, dtype=jnp.int32), padded_off, 0), axis=-1)
    dest = seg_start + rank
    p_rows = ((2 * n_tok + bm - 1) // bm) * bm + N_EXPERTS * bm
    n_blocks = p_rows // bm
    n_active = (padded_end[-1] // bm).astype(jnp.int32)
    blk = jnp.minimum(jnp.arange(n_blocks, dtype=jnp.int32), n_active - 1)
    block_expert = jnp.sum((padded_end[None, :] <= (blk * bm)[:, None]).astype(jnp.int32), axis=1)
    block_expert = jnp.minimum(block_expert, N_EXPERTS - 1)
    def tile_slots(tm):
        return dest.reshape(-1, 2, tm_cross // tm, tm).transpose(0, 2, 1, 3).reshape(n_tok // tm, 1, 2 * tm)
    pad_len = padded - counts
    bit = jnp.asarray(_pad_bits(), jnp.int32)
    higher = pad_len[:, None] & ~(2 * bit[None, :] - 1)
    pad_start = (padded_off + counts)[:, None] + higher
    pad_flag = (pad_len[:, None] & bit[None, :]) > 0
    half_blk = bm // 2
    unused = (n_active + jnp.arange(N_EXPERTS, dtype=jnp.int32))[:, None] * bm + jnp.arange(2, dtype=jnp.int32) * half_blk
    unused_flag = unused < n_blocks * bm
    zfill = jnp.stack([jnp.concatenate([pad_start.reshape(-1), jnp.minimum(unused, (n_blocks * bm - half_blk)).reshape(-1)]),
                       jnp.concatenate([pad_flag.reshape(-1), unused_flag.reshape(-1)]).astype(jnp.int32)]).astype(jnp.int32)
    sched = jnp.concatenate([n_active.reshape(1), padded // bm, padded_off // bm, counts]).astype(jnp.int32)

    xb = _dispatch(tile_slots(tm_row), zfill, h2, g_ffn, p_rows, tm_row)
    y = _experts(block_expert, sched, xb, w_expert_gate[0], w_expert_up[0], w_expert_down[0], bm)
    out = _combine(tile_slots(tm_moe), h2, wts, final_norm_g.reshape(1, d), y, tm_moe)
    return out.reshape(batch, seq, d)
```

```python
import functools
import math

import jax
import jax.numpy as jnp
from jax import lax
from jax.experimental import pallas as pl
from jax.experimental.pallas import tpu as pltpu

D_MODEL = 2048
ROPE_THETA = 500000.0
NORM_EPS = 1e-6

DIFF_HEADS = 8
DIFF_HEAD_DIM = 64
DIFF_V_DIM = 2 * DIFF_HEAD_DIM
DIFF_ROT = DIFF_HEAD_DIM // 4
DIFF_SUBLN_EPS = 1e-5
DIFF_LAMBDA_INIT = 0.8 - 0.6 * math.exp(-0.3 * 0)

MLA_HEADS = 8
MLA_Q_RANK = 512
MLA_KV_RANK = 256
MLA_NOPE_DIM = 128
MLA_ROPE_DIM = 64
MLA_V_DIM = 128
MLA_QK_DIM = MLA_NOPE_DIM + MLA_ROPE_DIM

CROSS_HEADS = 4
CROSS_HEAD_DIM = 128

N_GROUPS = 4
EXPERTS_PER_GROUP = 8
N_EXPERTS = N_GROUPS * EXPERTS_PER_GROUP

LANES = 128
BF16_TILE_ROWS = 16
LOG2E = 1.4426950408889634
VMEM_LIMIT_BYTES = 56 * 1024 * 1024

ROUTER_EXPERT_LANE0 = N_GROUPS

QKV_COLS = 3 * DIFF_HEADS * DIFF_V_DIM
LATENT_COLS = 1024
GATE_COLS = 2 * D_MODEL
KPE_COL0 = MLA_Q_RANK + MLA_KV_RANK

MOE_ROWS_PER_BLOCK = 512
SUBLANES = 8


def _params(*semantics):
    return pltpu.CompilerParams(dimension_semantics=semantics, vmem_limit_bytes=VMEM_LIMIT_BYTES)


def _resident(shape):
    zeros = (0,) * len(shape)
    return pl.BlockSpec(shape, lambda *_: zeros, pipeline_mode=pl.Buffered(1))


def _rms_scale(xf, eps):
    return lax.rsqrt(jnp.mean(xf * xf, axis=-1, keepdims=True) + eps)


def _sigmoid(x):
    return 0.5 * jnp.tanh(0.5 * x) + 0.5


def _pack_bf16_pair(a, b):
    hi = lax.bitcast_convert_type(a.astype(jnp.bfloat16).astype(jnp.float32), jnp.uint32)
    lo = lax.bitcast_convert_type(b.astype(jnp.bfloat16).astype(jnp.float32), jnp.uint32)
    return hi | (lo >> 16)


def _unpack_bf16_pair(w):
    a = lax.bitcast_convert_type(w & jnp.uint32(0xFFFF0000), jnp.float32)
    b = lax.bitcast_convert_type(w << 16, jnp.float32)
    return a, b


def _rows_to_tiles(ref_view, packed):
    rows = packed.shape[0]
    for c in range(SUBLANES):
        ref_view[pl.ds(c, rows, stride=SUBLANES), :] = packed[:, c * LANES:(c + 1) * LANES]


def _tiles_to_row_chunks(ref_view, rows):
    return [ref_view[pl.ds(c, rows, stride=SUBLANES), :] for c in range(SUBLANES)]


def _lane_iota(shape):
    return lax.broadcasted_iota(jnp.int32, shape, len(shape) - 1)


def _trig_kernel(pos_ref, invf_ref, cos_ref, sin_ref):
    ang = pos_ref[...].astype(jnp.float32) * invf_ref[...]
    cos_ref[...] = jnp.cos(ang)
    sin_ref[...] = jnp.sin(ang)


def _rope_tables(positions, n_tok, tm):
    half_m = MLA_ROPE_DIM // 2
    half_d = DIFF_ROT // 2
    inv_m = jnp.float32(ROPE_THETA) ** (-jnp.arange(half_m, dtype=jnp.float32) * 2.0 / MLA_ROPE_DIM)
    inv_d = jnp.float32(ROPE_THETA) ** (-jnp.arange(half_d, dtype=jnp.float32) * 2.0 / DIFF_ROT)
    invf = jnp.concatenate([inv_m, inv_m, inv_d, inv_d,
                            jnp.zeros((DIFF_HEAD_DIM - DIFF_ROT,), jnp.float32)]).reshape(1, LANES)
    pos = positions.reshape(n_tok, 1)
    return pl.pallas_call(
        _trig_kernel,
        grid=(n_tok // tm,),
        in_specs=[pl.BlockSpec((tm, 1), lambda i: (i, 0)), _resident((1, LANES))],
        out_specs=[pl.BlockSpec((tm, LANES), lambda i: (i, 0))] * 2,
        out_shape=[jax.ShapeDtypeStruct((n_tok, LANES), jnp.float32)] * 2,
        compiler_params=_params("parallel"),
        name="rope_tables",
    )(pos, invf)


def _diff_rope_coeffs(cos_t, sin_t):
    lane = _lane_iota(cos_t.shape)
    upper = lane >= DIFF_HEAD_DIM
    cos_d = jnp.where(upper, cos_t, pltpu.roll(cos_t, DIFF_HEAD_DIM, 1))
    sin_d = jnp.where(upper, sin_t, pltpu.roll(sin_t, DIFF_HEAD_DIM, 1))
    in_head = lane % DIFF_HEAD_DIM
    half = DIFF_ROT // 2
    s_next = jnp.where(in_head < half, -sin_d, 0.0)
    s_prev = jnp.where((in_head >= half) & (in_head < DIFF_ROT), sin_d, 0.0)
    return cos_d, s_next, s_prev


def _mla_rope(pair, cos_t, sin_t):
    lane = _lane_iota(pair.shape)
    sin_signed = jnp.where(lane < MLA_ROPE_DIM // 2, -sin_t, sin_t)
    return pair * cos_t + pltpu.roll(pair, MLA_ROPE_DIM, 1) * sin_signed


INPROJ_TN = 1024
INPROJ_PIECE = 256
Q_TILES = DIFF_HEADS * DIFF_V_DIM // INPROJ_TN
ROPE_TILES = 2 * Q_TILES
QKV_TILES = QKV_COLS // INPROJ_TN
LATENT_TILES = LATENT_COLS // INPROJ_TN
GATE_TILES = GATE_COLS // INPROJ_TN
INPROJ_TILES = QKV_TILES + LATENT_TILES + GATE_TILES
GATE_ROW0 = QKV_COLS + KPE_COL0 + MLA_ROPE_DIM


def _inproj_kernel(x_ref, g_ref, w_ref, cos_ref, sin_ref, qkv_ref, lat_ref, gate_ref, xn_ref):
    j = pl.program_id(1)

    @pl.when(j == 0)
    def _():
        xf = x_ref[...]
        xn_ref[...] = (xf * _rms_scale(xf, NORM_EPS) * g_ref[...]).astype(jnp.bfloat16)

    def pieces(epilogue):
        for c in range(INPROJ_TN // INPROJ_PIECE):
            cols = slice(c * INPROJ_PIECE, (c + 1) * INPROJ_PIECE)
            acc = lax.dot_general(xn_ref[...], w_ref[cols, :], (((1,), (1,)), ((), ())),
                                  preferred_element_type=jnp.float32)
            epilogue(acc, cols)

    @pl.when(j < ROPE_TILES)
    def _():
        cos_d, s_next, s_prev = _diff_rope_coeffs(cos_ref[...], sin_ref[...])
        qscale = jnp.where(j < Q_TILES, DIFF_HEAD_DIM ** -0.5 * LOG2E, 1.0).astype(jnp.float32)

        def rope(acc, cols):
            for c in range(INPROJ_PIECE // LANES):
                xc = acc[:, c * LANES:(c + 1) * LANES]
                rot = (xc * cos_d + pltpu.roll(xc, LANES - DIFF_ROT // 2, 1) * s_next
                       + pltpu.roll(xc, DIFF_ROT // 2, 1) * s_prev)
                lo = cols.start + c * LANES
                qkv_ref[:, lo:lo + LANES] = (rot * qscale).astype(qkv_ref.dtype)

        pieces(rope)

    @pl.when((j >= ROPE_TILES) & (j < QKV_TILES))
    def _():
        def value(acc, cols):
            qkv_ref[:, cols] = acc.astype(qkv_ref.dtype)

        pieces(value)

    @pl.when((j >= QKV_TILES) & (j < QKV_TILES + LATENT_TILES))
    def _():
        def latent(acc, cols):
            if cols.start <= KPE_COL0 < cols.stop:
                c0 = KPE_COL0 - cols.start
                v = acc[:, c0:c0 + LANES]
                lane = _lane_iota(v.shape)
                half = MLA_ROPE_DIM // 2
                swapped = jnp.where(lane < MLA_ROPE_DIM + half, pltpu.roll(v, half, 1),
                                    pltpu.roll(v, MLA_ROPE_DIM + half, 1))
                parts = [acc[:, :c0], jnp.where(lane < MLA_ROPE_DIM, v, swapped), acc[:, c0 + LANES:]]
                acc = jnp.concatenate([p for p in parts if p.shape[1]], axis=1)
            lat_ref[:, cols] = acc

        pieces(latent)

    @pl.when(j >= QKV_TILES + LATENT_TILES)
    def _():
        def gate(acc, cols):
            gate_ref[:, cols] = _sigmoid(acc).astype(gate_ref.dtype)

        pieces(gate)


def _inproj(x2, g, w_all, cos_t, sin_t, tm):
    n_tok, d = x2.shape
    tn = INPROJ_TN
    lat0 = QKV_TILES
    gate0 = QKV_TILES + LATENT_TILES

    def w_rows(i, j):
        start = jnp.where(j < gate0, j * tn, GATE_ROW0 + (j - gate0) * tn)
        return (pl.multiple_of(start, BF16_TILE_ROWS), 0)

    assert GATE_ROW0 % BF16_TILE_ROWS == 0
    return pl.pallas_call(
        _inproj_kernel,
        grid=(n_tok // tm, INPROJ_TILES),
        in_specs=[pl.BlockSpec((tm, d), lambda i, j: (i, 0)),
                  _resident((1, d)),
                  pl.BlockSpec((pl.Element(tn), pl.Element(d)), w_rows),
                  pl.BlockSpec((tm, LANES), lambda i, j: (i, 0)),
                  pl.BlockSpec((tm, LANES), lambda i, j: (i, 0))],
        out_specs=[pl.BlockSpec((tm, tn), lambda i, j: (i, jnp.clip(j, 0, QKV_TILES - 1))),
                   pl.BlockSpec((tm, tn), lambda i, j: (i, jnp.clip(j - lat0, 0, LATENT_TILES - 1))),
                   pl.BlockSpec((tm, tn), lambda i, j: (i, jnp.clip(j - gate0, 0, GATE_TILES - 1)))],
        out_shape=[jax.ShapeDtypeStruct((n_tok, QKV_COLS), jnp.bfloat16),
                   jax.ShapeDtypeStruct((n_tok, LATENT_COLS), jnp.float32),
                   jax.ShapeDtypeStruct((n_tok, GATE_COLS), jnp.bfloat16)],
        scratch_shapes=[pltpu.VMEM((tm, d), jnp.bfloat16)],
        compiler_params=_params("parallel", "arbitrary"),
        name="inproj",
    )(x2, g, w_all, cos_t, sin_t)


def _mla_proj_kernel(c_ref, gq_ref, gkv_ref, wuq_ref, wuk_ref, wuv_ref, cos_ref, sin_ref,
                     q_ref, k_ref, v_ref):
    cos_t = cos_ref[...]
    sin_t = sin_ref[...]
    cq = c_ref[:, :MLA_Q_RANK]
    cqn = (cq * _rms_scale(cq, NORM_EPS) * gq_ref[...]).astype(jnp.bfloat16)
    ckv = c_ref[:, MLA_Q_RANK:KPE_COL0]
    ckvn = (ckv * _rms_scale(ckv, NORM_EPS) * gkv_ref[...]).astype(jnp.bfloat16)
    kpe = _mla_rope(c_ref[:, KPE_COL0:KPE_COL0 + LANES], cos_t, sin_t)[:, :MLA_ROPE_DIM].astype(k_ref.dtype)
    qscale = MLA_QK_DIM ** -0.5 * LOG2E
    for h in range(MLA_HEADS):
        r = jnp.dot(cqn, wuq_ref[h], preferred_element_type=jnp.float32)
        q_ref[0, h, :, :MLA_NOPE_DIM] = (r[:, :MLA_NOPE_DIM] * qscale).astype(q_ref.dtype)
        qpe = _mla_rope(r[:, MLA_NOPE_DIM:], cos_t, sin_t)[:, :MLA_ROPE_DIM]
        q_ref[0, h, :, MLA_NOPE_DIM:] = (qpe * qscale).astype(q_ref.dtype)
        kn = jnp.dot(ckvn, wuk_ref[h], preferred_element_type=jnp.float32)
        k_ref[0, h, :, :MLA_NOPE_DIM] = kn.astype(k_ref.dtype)
        k_ref[0, h, :, MLA_NOPE_DIM:] = kpe
    v_ref[...] = jnp.dot(ckvn, wuv_ref[...], preferred_element_type=jnp.float32).astype(v_ref.dtype)


def _mla_proj(latent, gq, gkv, wuq, wuk, wuv, cos_t, sin_t, batch, seq, tm):
    n_tok = latent.shape[0]
    per_b = seq // tm
    head_spec = pl.BlockSpec((1, MLA_HEADS, tm, MLA_QK_DIM), lambda i: (i // per_b, 0, i % per_b, 0))
    head_shape = jax.ShapeDtypeStruct((batch, MLA_HEADS, seq, MLA_QK_DIM), jnp.bfloat16)
    return pl.pallas_call(
        _mla_proj_kernel,
        grid=(n_tok // tm,),
        in_specs=[pl.BlockSpec((tm, LATENT_COLS), lambda i: (i, 0)),
                  _resident(gq.shape), _resident(gkv.shape),
                  _resident(wuq.shape), _resident(wuk.shape), _resident(wuv.shape),
                  pl.BlockSpec((tm, LANES), lambda i: (i, 0)),
                  pl.BlockSpec((tm, LANES), lambda i: (i, 0))],
        out_specs=[head_spec, head_spec,
                   pl.BlockSpec((tm, MLA_HEADS * MLA_V_DIM), lambda i: (i, 0))],
        out_shape=[head_shape, head_shape,
                   jax.ShapeDtypeStruct((n_tok, MLA_HEADS * MLA_V_DIM), jnp.bfloat16)],
        compiler_params=_params("parallel"),
        name="mla_proj",
    )(latent, gq, gkv, wuq, wuk, wuv, cos_t, sin_t)


def _with_ones(v):
    return jnp.concatenate([v, jnp.ones((v.shape[0], LANES), v.dtype)], axis=-1)


def _softmax_pv(s, v_ones):
    m = jnp.max(s, axis=-1, keepdims=True)
    p = jnp.exp2(s - m).astype(v_ones.dtype)
    pv = jnp.dot(p, v_ones, preferred_element_type=jnp.float32)
    dv = v_ones.shape[1] - LANES
    return pv[:, :dv] / pv[:, dv:]


ATTN_HEADS_PER_STEP = 2


def _diff_attn_kernel(q_ref, k_ref, v_ref, lq1_ref, lk1_ref, lq2_ref, lk2_ref, g_ref, o_ref, v1_ref, *, rg):
    heads = q_ref.shape[2] // LANES

    @pl.when(pl.program_id(2) == 0)
    def _():
        for h in range(heads):
            v1_ref[h] = _with_ones(v_ref[0, :, h * DIFF_V_DIM:(h + 1) * DIFF_V_DIM])

    lam = (jnp.exp(jnp.sum(lq1_ref[...] * lk1_ref[...], axis=-1, keepdims=True))
           - jnp.exp(jnp.sum(lq2_ref[...] * lk2_ref[...], axis=-1, keepdims=True))
           + DIFF_LAMBDA_INIT)
    lane = _lane_iota((rg, LANES))
    for g in range(q_ref.shape[1] // rg):
        for h in range(heads):
            cols = slice(h * LANES, (h + 1) * LANES)
            q = q_ref[0, g * rg:(g + 1) * rg, cols]
            k = k_ref[0, :, cols]
            zero = jnp.zeros_like(q)
            q12 = jnp.concatenate([jnp.where(lane < DIFF_HEAD_DIM, q, zero),
                                   jnp.where(lane >= DIFF_HEAD_DIM, q, zero)], axis=0)
            s = lax.dot_general(q12, k, (((1,), (1,)), ((), ())), preferred_element_type=jnp.float32)
            a = _softmax_pv(s, v1_ref[h])
            o = a[:rg] - lam * a[rg:]
            o = o * _rms_scale(o, DIFF_SUBLN_EPS) * g_ref[...] * (1.0 - DIFF_LAMBDA_INIT)
            o_ref[0, g * rg:(g + 1) * rg, cols] = o.astype(o_ref.dtype)


def _diff_attn(qkv3, lq1, lk1, lq2, lk2, subln_g, tq, rg):
    batch, seq, _ = qkv3.shape
    hp = ATTN_HEADS_PER_STEP
    steps = DIFF_HEADS // hp
    width = hp * DIFF_V_DIM
    return pl.pallas_call(
        functools.partial(_diff_attn_kernel, rg=rg),
        grid=(batch, steps, seq // tq),
        in_specs=[pl.BlockSpec((1, tq, width), lambda b, hh, i: (b, i, hh)),
                  pl.BlockSpec((1, seq, width), lambda b, hh, i: (b, 0, steps + hh)),
                  pl.BlockSpec((1, seq, width), lambda b, hh, i: (b, 0, 2 * steps + hh)),
                  _resident(lq1.shape), _resident(lk1.shape), _resident(lq2.shape), _resident(lk2.shape),
                  _resident(subln_g.shape)],
        out_specs=pl.BlockSpec((1, tq, width), lambda b, hh, i: (b, i, hh)),
        out_shape=jax.ShapeDtypeStruct((batch, seq, DIFF_HEADS * DIFF_V_DIM), jnp.bfloat16),
        scratch_shapes=[pltpu.VMEM((hp, seq, DIFF_V_DIM + LANES), jnp.bfloat16)],
        compiler_params=_params("parallel", "parallel", "arbitrary"),
        name="diff_attn",
    )(qkv3, qkv3, qkv3, lq1, lk1, lq2, lk2, subln_g)


def _mla_attn_kernel(q_ref, k_ref, v_ref, o_ref, v1_ref, *, rg):
    heads = q_ref.shape[1]

    @pl.when(pl.program_id(2) == 0)
    def _():
        for h in range(heads):
            v1_ref[h] = _with_ones(v_ref[0, :, h * MLA_V_DIM:(h + 1) * MLA_V_DIM])

    for g in range(q_ref.shape[2] // rg):
        for h in range(heads):
            s = lax.dot_general(q_ref[0, h, g * rg:(g + 1) * rg], k_ref[0, h], (((1,), (1,)), ((), ())),
                                preferred_element_type=jnp.float32)
            o_ref[0, g * rg:(g + 1) * rg, h * MLA_V_DIM:(h + 1) * MLA_V_DIM] = (
                _softmax_pv(s, v1_ref[h]).astype(o_ref.dtype))


def _mla_attn(q_cat, k_cat, v3, tq, rg):
    batch, heads, seq, dqk = q_cat.shape
    hp = ATTN_HEADS_PER_STEP
    return pl.pallas_call(
        functools.partial(_mla_attn_kernel, rg=rg),
        grid=(batch, heads // hp, seq // tq),
        in_specs=[pl.BlockSpec((1, hp, tq, dqk), lambda b, h, i: (b, h, i, 0)),
                  pl.BlockSpec((1, hp, seq, dqk), lambda b, h, i: (b, h, 0, 0)),
                  pl.BlockSpec((1, seq, hp * MLA_V_DIM), lambda b, h, i: (b, 0, h))],
        out_specs=pl.BlockSpec((1, tq, hp * MLA_V_DIM), lambda b, h, i: (b, i, h)),
        out_shape=jax.ShapeDtypeStruct((batch, seq, heads * MLA_V_DIM), jnp.bfloat16),
        scratch_shapes=[pltpu.VMEM((hp, seq, MLA_V_DIM + LANES), jnp.bfloat16)],
        compiler_params=_params("parallel", "parallel", "arbitrary"),
        name="mla_attn",
    )(q_cat, k_cat, v3)


def _merge_out_kernel(oa_ref, ob_ref, sga_ref, sgb_ref, x_ref, woa_ref, wob_ref, wout_ref, h_ref):
    ya = jnp.dot(oa_ref[...], woa_ref[...], preferred_element_type=jnp.float32)
    yb = jnp.dot(ob_ref[...], wob_ref[...], preferred_element_type=jnp.float32)
    merged = sga_ref[...].astype(jnp.float32) * ya + sgb_ref[...].astype(jnp.float32) * yb
    h_ref[...] = x_ref[...] + jnp.dot(merged.astype(jnp.bfloat16), wout_ref[...],
                                       preferred_element_type=jnp.float32)


def _merge_out(o_a, o_b, gates, x2, w_oa, w_ob, w_out, tm):
    n_tok, d = x2.shape
    return pl.pallas_call(
        _merge_out_kernel,
        grid=(n_tok // tm,),
        in_specs=[pl.BlockSpec((tm, o_a.shape[1]), lambda i: (i, 0)),
                  pl.BlockSpec((tm, o_b.shape[1]), lambda i: (i, 0)),
                  pl.BlockSpec((tm, d), lambda i: (i, 0)),
                  pl.BlockSpec((tm, d), lambda i: (i, 1)),
                  pl.BlockSpec((tm, d), lambda i: (i, 0)),
                  _resident(w_oa.shape), _resident(w_ob.shape), _resident(w_out.shape)],
        out_specs=pl.BlockSpec((tm, d), lambda i: (i, 0)),
        out_shape=jax.ShapeDtypeStruct((n_tok, d), jnp.float32),
        compiler_params=_params("parallel"),
        name="merge_out",
    )(o_a, o_b, gates, gates, x2, w_oa, w_ob, w_out)


def _mem_kv_kernel(mem_ref, g_ref, w_ref, kv_ref):
    mf = mem_ref[0]
    mn = (mf * _rms_scale(mf, NORM_EPS) * g_ref[...]).astype(jnp.bfloat16)
    kv_ref[0] = jnp.dot(mn, w_ref[...], preferred_element_type=jnp.float32).astype(kv_ref.dtype)


def _mem_kv(mem, g, w_ckv):
    batch, m, d = mem.shape
    return pl.pallas_call(
        _mem_kv_kernel,
        grid=(batch,),
        in_specs=[pl.BlockSpec((1, m, d), lambda b: (b, 0, 0)), _resident(g.shape), _resident(w_ckv.shape)],
        out_specs=pl.BlockSpec((1, m, w_ckv.shape[1]), lambda b: (b, 0, 0)),
        out_shape=jax.ShapeDtypeStruct((batch, m, w_ckv.shape[1]), jnp.bfloat16),
        compiler_params=_params("parallel"),
        name="mem_kv",
    )(mem, g, w_ckv)


def _cross_router_kernel(h_ref, gc_ref, wcq_ref, kv_ref, wco_ref, gf_ref, wr_ref, br_ref,
                         h2_ref, eid_ref, rank_ref, wts_ref, cnt_ref, carry_ref):
    i = pl.program_id(0)

    @pl.when(i == 0)
    def _():
        carry_ref[...] = jnp.zeros_like(carry_ref)

    h1 = h_ref[...]
    tm = h1.shape[0]
    hn = (h1 * _rms_scale(h1, NORM_EPS) * gc_ref[...]).astype(jnp.bfloat16)
    q = jnp.dot(hn, wcq_ref[...], preferred_element_type=jnp.float32) * (CROSS_HEAD_DIM ** -0.5 * LOG2E)
    q = q.astype(jnp.bfloat16)
    kv_cols = CROSS_HEADS * CROSS_HEAD_DIM
    heads = []
    for hd in range(CROSS_HEADS):
        lo = hd * CROSS_HEAD_DIM
        kh = kv_ref[0, :, lo:lo + CROSS_HEAD_DIM]
        vh = kv_ref[0, :, kv_cols + lo:kv_cols + lo + CROSS_HEAD_DIM]
        s = lax.dot_general(q[:, lo:lo + CROSS_HEAD_DIM], kh, (((1,), (1,)), ((), ())),
                            preferred_element_type=jnp.float32)
        heads.append(_softmax_pv(s, _with_ones(vh)).astype(jnp.bfloat16))
    o = jnp.concatenate(heads, axis=-1)
    h2 = h1 + jnp.dot(o, wco_ref[...], preferred_element_type=jnp.float32)
    h2_ref[...] = h2

    t = h2 * _rms_scale(h2, NORM_EPS) * gf_ref[...]
    t_hi = t.astype(jnp.bfloat16)
    t_lo = (t - t_hi.astype(jnp.float32)).astype(jnp.bfloat16)
    hi = jnp.dot(t_hi, wr_ref[...], preferred_element_type=jnp.float32)
    lo = jnp.dot(t_lo, wr_ref[:, :LANES], preferred_element_type=jnp.float32)
    logits = hi[:, :LANES] + (hi[:, LANES:] + lo) + br_ref[...]
    lane = _lane_iota(logits.shape)
    neg = jnp.float32(-jnp.inf)
    big = jnp.int32(2 * LANES)
    is_group = lane < N_GROUPS
    lg = jnp.where(is_group, logits, neg)
    mg = jnp.max(lg, axis=-1, keepdims=True)
    g_idx = jnp.min(jnp.where(is_group & (logits == mg), lane, big), axis=-1, keepdims=True)
    g_p = 1.0 / jnp.sum(jnp.exp(lg - mg), axis=-1, keepdims=True)
    lo_lane = ROUTER_EXPERT_LANE0 + EXPERTS_PER_GROUP * g_idx
    in_grp = (lane >= lo_lane) & (lane < lo_lane + EXPERTS_PER_GROUP)
    l1 = jnp.max(jnp.where(in_grp, logits, neg), axis=-1, keepdims=True)
    i1 = jnp.min(jnp.where(in_grp & (logits == l1), lane, big), axis=-1, keepdims=True)
    rest = in_grp & (lane != i1)
    l2 = jnp.max(jnp.where(rest, logits, neg), axis=-1, keepdims=True)
    i2 = jnp.min(jnp.where(rest & (logits == l2), lane, big), axis=-1, keepdims=True)
    d = jnp.exp(l2 - l1)
    w1 = g_p / (1.0 + d)
    w2 = w1 * d

    oh1 = lane == i1
    oh2 = lane == i2
    cnt = (oh1 | oh2).astype(jnp.bfloat16)
    row = lax.broadcasted_iota(jnp.int32, (tm, tm), 0)
    col = lax.broadcasted_iota(jnp.int32, (tm, tm), 1)
    before = (col < row).astype(jnp.bfloat16)
    slot = jnp.dot(before, cnt, preferred_element_type=jnp.float32) + carry_ref[...]
    r1 = jnp.sum(jnp.where(oh1, slot, 0.0), axis=-1, keepdims=True)
    r2 = jnp.sum(jnp.where(oh2, slot, 0.0), axis=-1, keepdims=True)
    carry_ref[...] += jnp.sum(cnt.astype(jnp.float32), axis=0, keepdims=True)
    cnt_ref[...] = carry_ref[...]

    eye = row == col

    def to_row(c, dtype):
        return jnp.sum(jnp.where(eye, c.astype(jnp.float32), 0.0), axis=0, keepdims=True).astype(dtype)

    eid_ref[0] = jnp.concatenate([to_row(i1 - ROUTER_EXPERT_LANE0, jnp.int32),
                                  to_row(i2 - ROUTER_EXPERT_LANE0, jnp.int32)], axis=0)
    rank_ref[0] = jnp.concatenate([to_row(r1, jnp.int32), to_row(r2, jnp.int32)], axis=0)
    wts_ref[...] = jnp.where(_lane_iota((tm, 2)) == 0, w1, w2)


def _cross_router(h1, gc, w_cq, kv_mem, w_co, gf, w_r, b_r, seq, tm):
    n_tok, d = h1.shape
    per_b = seq // tm
    row2 = pl.BlockSpec((tm, 2), lambda i: (i, 0))
    lane2 = pl.BlockSpec((1, 2, tm), lambda i: (i, 0, 0))
    return pl.pallas_call(
        _cross_router_kernel,
        grid=(n_tok // tm,),
        in_specs=[pl.BlockSpec((tm, d), lambda i: (i, 0)),
                  _resident(gc.shape), _resident(w_cq.shape),
                  pl.BlockSpec((1,) + kv_mem.shape[1:], lambda i: (i // per_b, 0, 0)),
                  _resident(w_co.shape), _resident(gf.shape), _resident(w_r.shape), _resident(b_r.shape)],
        out_specs=[pl.BlockSpec((tm, d), lambda i: (i, 0)), lane2, lane2, row2,
                   pl.BlockSpec((1, LANES), lambda i: (0, 0))],
        out_shape=[jax.ShapeDtypeStruct((n_tok, d), jnp.float32),
                   jax.ShapeDtypeStruct((n_tok // tm, 2, tm), jnp.int32),
                   jax.ShapeDtypeStruct((n_tok // tm, 2, tm), jnp.int32),
                   jax.ShapeDtypeStruct((n_tok, 2), jnp.float32),
                   jax.ShapeDtypeStruct((1, LANES), jnp.float32)],
        scratch_shapes=[pltpu.VMEM((1, LANES), jnp.float32)],
        compiler_params=_params("arbitrary"),
        name="cross_router",
    )(h1, gc, w_cq, kv_mem, w_co, gf, w_r, b_r)


def _pad_bits():
    return [1 << b for b in reversed(range(MOE_ROWS_PER_BLOCK.bit_length() - 1))]


def _zero_fill_sizes():
    return _pad_bits() * N_EXPERTS + [MOE_ROWS_PER_BLOCK // 2] * (2 * N_EXPERTS)


def _dispatch_kernel(dest_ref, zfill_ref, h_ref, g_ref, xb_ref, t_ref, zero_ref, sem, zsem):
    i = pl.program_id(0)
    tm = h_ref.shape[0]
    slot = i % 2

    def zero_copies(action):
        for idx, rows in enumerate(_zero_fill_sizes()):
            @pl.when(zfill_ref[1, idx] > 0)
            def _():
                start = pl.multiple_of(zfill_ref[0, idx] * SUBLANES, SUBLANES)
                action(pltpu.make_async_copy(zero_ref.at[pl.ds(0, rows * SUBLANES), :],
                                             xb_ref.at[pl.ds(start, rows * SUBLANES), :], zsem))

    @pl.when(i == 0)
    def _():
        zero_ref[...] = jnp.zeros_like(zero_ref)
        zero_copies(lambda c: c.start())

    h2 = h_ref[...]
    t = h2 * _rms_scale(h2, NORM_EPS) * g_ref[...]
    half = t.shape[1] // 2
    _rows_to_tiles(t_ref.at[slot], _pack_bf16_pair(t[:, :half], t[:, half:]))

    for r in range(tm):
        for k in range(2):
            dst = pl.multiple_of(dest_ref[0, k * tm + r] * SUBLANES, SUBLANES)
            pltpu.make_async_copy(t_ref.at[slot, pl.ds(r * SUBLANES, SUBLANES), :],
                                  xb_ref.at[pl.ds(dst, SUBLANES), :], sem.at[slot]).start(priority=k)

    def wait_tile(which):
        for _ in range(2):
            pltpu.make_async_copy(t_ref.at[which], xb_ref.at[pl.ds(0, tm * SUBLANES), :], sem.at[which]).wait()

    @pl.when(i > 0)
    def _():
        wait_tile(1 - slot)

    @pl.when(i == pl.num_programs(0) - 1)
    def _():
        wait_tile(slot)
        zero_copies(lambda c: c.wait())


def _dispatch(dest3, zfill, h2, gf, p_rows, tm):
    n_tok, d = h2.shape
    return pl.pallas_call(
        _dispatch_kernel,
        grid=(n_tok // tm,),
        in_specs=[pl.BlockSpec((None, 1, 2 * tm), lambda i: (i, 0, 0), memory_space=pltpu.SMEM),
                  pl.BlockSpec(memory_space=pltpu.SMEM),
                  pl.BlockSpec((tm, d), lambda i: (i, 0)),
                  _resident(gf.shape)],
        out_specs=pl.BlockSpec(memory_space=pl.ANY),
        out_shape=jax.ShapeDtypeStruct((p_rows * SUBLANES, LANES), jnp.uint32),
        scratch_shapes=[pltpu.VMEM((2, tm * SUBLANES, LANES), jnp.uint32),
                        pltpu.VMEM((MOE_ROWS_PER_BLOCK // 2 * SUBLANES, LANES), jnp.uint32),
                        pltpu.SemaphoreType.DMA((2,)), pltpu.SemaphoreType.DMA(())],
        compiler_params=_params("arbitrary"),
        name="moe_dispatch",
    )(dest3, zfill, h2, gf)


def _expert_kernel(be_ref, nact_ref, x_ref, wg_hbm, wu_hbm, wd_hbm, y_ref,
                   wg_f, wu_f, wd_f, wg_b, wu_b, wd_b, sem):
    i = pl.program_id(0)
    blk = i - 1
    nact = nact_ref[0]
    last_blk = pl.num_programs(0) - 2

    def weight_copies(e):
        return (pltpu.make_async_copy(wg_hbm.at[e], wg_f, sem),
                pltpu.make_async_copy(wu_hbm.at[e], wu_f, sem),
                pltpu.make_async_copy(wd_hbm.at[e], wd_f, sem))

    def fetch(e):
        for c in weight_copies(e):
            c.start(priority=1)

    def land(e):
        for c in weight_copies(e):
            c.wait()
        half = wg_f.shape[0] // 2
        for c in range(SUBLANES):
            for part, src0 in enumerate((c * LANES, half + c * LANES)):
                dst0 = (2 * c + part) * LANES
                wg_b[dst0:dst0 + LANES, :] = wg_f[src0:src0 + LANES, :].astype(jnp.bfloat16)
                wu_b[dst0:dst0 + LANES, :] = wu_f[src0:src0 + LANES, :].astype(jnp.bfloat16)
        wd_b[...] = wd_f[...].astype(jnp.bfloat16)

    @pl.when(i == 0)
    def _():
        fetch(be_ref[0])
        land(be_ref[0])

    @pl.when((i > 0) & (blk < nact))
    def _():
        here = be_ref[blk]
        nxt = be_ref[jnp.minimum(blk + 1, last_blk)]
        prv = be_ref[jnp.maximum(blk - 1, 0)]
        seg_end = nact_ref[1 + N_EXPERTS + here] + nact_ref[1 + here]
        has_next = seg_end < nact
        after = be_ref[jnp.minimum(seg_end, last_blk)]
        is_first = (blk == 0) | (prv != here)
        is_last = (blk + 1 >= nact) | (nxt != here)

        @pl.when(is_first & has_next)
        def _():
            fetch(after)

        bm = x_ref.shape[0] // SUBLANES
        hb = bm // 2
        rows_here = nact_ref[1 + 2 * N_EXPERTS + here] - (blk - nact_ref[1 + N_EXPERTS + here]) * bm

        def swiglu(h):
            view = pl.ds(h * hb * SUBLANES, hb * SUBLANES)
            parts = []
            for chunk in _tiles_to_row_chunks(x_ref.at[view, :], hb):
                x_a, x_b = _unpack_bf16_pair(chunk)
                parts += [x_a.astype(jnp.bfloat16), x_b.astype(jnp.bfloat16)]
            xb = jnp.concatenate(parts, axis=1)
            gate = jnp.dot(xb, wg_b[...], preferred_element_type=jnp.float32)
            up = jnp.dot(xb, wu_b[...], preferred_element_type=jnp.float32)
            hid = (gate * _sigmoid(gate) * up).astype(jnp.bfloat16)
            y = jnp.dot(hid, wd_b[...], preferred_element_type=jnp.float32)
            half = y.shape[1] // 2
            _rows_to_tiles(y_ref.at[view, :], _pack_bf16_pair(y[:, :half], y[:, half:]))

        swiglu(0)

        @pl.when(rows_here > hb)
        def _():
            swiglu(1)

        @pl.when(rows_here <= hb)
        def _():
            y_ref[pl.ds(hb * SUBLANES, hb * SUBLANES), :] = jnp.zeros((hb * SUBLANES, LANES), y_ref.dtype)

        @pl.when(is_last & has_next)
        def _():
            land(after)

    @pl.when((i > 0) & (blk >= nact))
    def _():
        y_ref[...] = jnp.zeros_like(y_ref)


def _experts(block_expert, sched, xb, w_gate, w_up, w_down, bm):
    p_rows = xb.shape[0] // SUBLANES
    d = w_gate.shape[1]
    de = w_gate.shape[-1]

    def x_map(i, be, sc):
        return (jnp.clip(i - 1, 0, sc[0] - 1), 0)

    grid_spec = pltpu.PrefetchScalarGridSpec(
        num_scalar_prefetch=2,
        grid=(p_rows // bm + 1,),
        in_specs=[pl.BlockSpec((bm * SUBLANES, LANES), x_map),
                  pl.BlockSpec(memory_space=pl.ANY),
                  pl.BlockSpec(memory_space=pl.ANY),
                  pl.BlockSpec(memory_space=pl.ANY)],
        out_specs=pl.BlockSpec((bm * SUBLANES, LANES), lambda i, be, sc: (jnp.maximum(i - 1, 0), 0)),
        scratch_shapes=[pltpu.VMEM((d, de), jnp.float32), pltpu.VMEM((d, de), jnp.float32),
                        pltpu.VMEM((de, d), jnp.float32),
                        pltpu.VMEM((d, de), jnp.bfloat16), pltpu.VMEM((d, de), jnp.bfloat16),
                        pltpu.VMEM((de, d), jnp.bfloat16),
                        pltpu.SemaphoreType.DMA(())],
    )
    return pl.pallas_call(
        _expert_kernel,
        grid_spec=grid_spec,
        out_shape=jax.ShapeDtypeStruct(xb.shape, jnp.uint32),
        compiler_params=_params("arbitrary"),
        name="moe_experts",
    )(block_expert, sched, xb, w_gate, w_up, w_down)


def _combine_kernel(dest_ref, dest_next_ref, h_ref, wts_ref, g_ref, y_ref, o_ref, ybuf, sem):
    i = pl.program_id(0)
    tm = h_ref.shape[0]
    slot = i % 2

    def gather(idx_ref, which):
        for r in range(tm):
            for k in range(2):
                src = pl.multiple_of(idx_ref[0, k * tm + r] * SUBLANES, SUBLANES)
                pltpu.make_async_copy(y_ref.at[pl.ds(src, SUBLANES), :],
                                      ybuf.at[which, k, pl.ds(r * SUBLANES, SUBLANES), :], sem.at[which]).start(priority=k)

    def wait_tile(which):
        for k in range(2):
            pltpu.make_async_copy(y_ref.at[pl.ds(0, tm * SUBLANES), :], ybuf.at[which, k], sem.at[which]).wait()

    @pl.when(i == 0)
    def _():
        gather(dest_ref, slot)

    wait_tile(slot)
    gather(dest_next_ref, 1 - slot)

    w = wts_ref[...]
    half = h_ref.shape[1] // 2
    lo_parts, hi_parts = [], []
    for c, (c0, c1) in enumerate(zip(_tiles_to_row_chunks(ybuf.at[slot, 0], tm),
                                      _tiles_to_row_chunks(ybuf.at[slot, 1], tm))):
        a0, b0 = _unpack_bf16_pair(c0)
        a1, b1 = _unpack_bf16_pair(c1)
        lo_parts.append(h_ref[:, c * LANES:(c + 1) * LANES] + w[:, 0:1] * a0 + w[:, 1:2] * a1)
        hi_parts.append(h_ref[:, half + c * LANES:half + (c + 1) * LANES] + w[:, 0:1] * b0 + w[:, 1:2] * b1)
    h3 = jnp.concatenate(lo_parts + hi_parts, axis=1)
    o_ref[...] = h3 * _rms_scale(h3, NORM_EPS) * g_ref[...]

    @pl.when(i == pl.num_programs(0) - 1)
    def _():
        wait_tile(1 - slot)


def _combine(dest3, h2, wts, g_final, y, tm):
    n_tok, d = h2.shape
    last = n_tok // tm - 1
    return pl.pallas_call(
        _combine_kernel,
        grid=(n_tok // tm,),
        in_specs=[pl.BlockSpec((None, 1, 2 * tm), lambda i: (i, 0, 0), memory_space=pltpu.SMEM),
                  pl.BlockSpec((None, 1, 2 * tm), lambda i: (jnp.minimum(i + 1, last), 0, 0),
                               memory_space=pltpu.SMEM),
                  pl.BlockSpec((tm, d), lambda i: (i, 0)),
                  pl.BlockSpec((tm, 2), lambda i: (i, 0)),
                  _resident(g_final.shape),
                  pl.BlockSpec(memory_space=pl.ANY)],
        out_specs=pl.BlockSpec((tm, d), lambda i: (i, 0)),
        out_shape=jax.ShapeDtypeStruct((n_tok, d), jnp.float32),
        scratch_shapes=[pltpu.VMEM((2, 2, tm * SUBLANES, LANES), jnp.uint32), pltpu.SemaphoreType.DMA((2,))],
        compiler_params=_params("arbitrary"),
        name="moe_combine",
    )(dest3, dest3, h2, wts, g_final, y)


def _transpose_w_in(w_in):
    return jnp.swapaxes(w_in, 0, 1).astype(jnp.bfloat16)


def _split_w_uq(w_uq):
    half = MLA_ROPE_DIM // 2
    w = w_uq.reshape(MLA_Q_RANK, MLA_HEADS, MLA_QK_DIM).transpose(1, 0, 2)
    pe = w[:, :, MLA_NOPE_DIM:]
    pe_swapped = jnp.concatenate([pe[:, :, half:], pe[:, :, :half]], axis=2)
    return jnp.concatenate([w, pe_swapped], axis=2).astype(jnp.bfloat16)


def _split_w_ukv(w_ukv):
    w = w_ukv.reshape(MLA_KV_RANK, MLA_HEADS, MLA_NOPE_DIM + MLA_V_DIM)
    wuk = w[:, :, :MLA_NOPE_DIM].transpose(1, 0, 2).astype(jnp.bfloat16)
    wuv = w[:, :, MLA_NOPE_DIM:].reshape(MLA_KV_RANK, MLA_HEADS * MLA_V_DIM).astype(jnp.bfloat16)
    return wuk, wuv


def kernel(x, mem, positions, attn_norm_g, w_in, diff_lambda_q1, diff_lambda_k1, diff_lambda_q2, diff_lambda_k2, diff_subln_g, w_o_diff, mla_q_norm_g, w_uq, mla_kv_norm_g, w_ukv, w_o_mla, w_out, cross_norm_g, mem_norm_g, w_cq, w_ckv, w_co, ffn_norm_g, w_router_group, b_router_group, w_router_expert, b_router_expert, w_expert_gate, w_expert_up, w_expert_down, final_norm_g):
    batch, seq, d = x.shape
    assert d == D_MODEL and w_in.shape[0] == 1, "single-layer kernel"
    n_tok = batch * seq
    bf = jnp.bfloat16
    x2 = x.reshape(n_tok, d)

    tm_proj = min(1024, seq)
    tm_row = min(512, seq)
    tm_moe = min(MOE_ROWS_PER_BLOCK, seq)
    tm_comb = min(1024, seq)
    tm_cross = min(512, seq)
    tq = min(2048, seq)
    rg_diff = 128
    rg_mla = 256

    cos_t, sin_t = _rope_tables(positions, n_tok, tm_proj)

    g_attn = attn_norm_g[0].reshape(1, d)
    qkv, latent, gates = _inproj(x2, g_attn, _transpose_w_in(w_in[0]), cos_t, sin_t, tm_proj)

    o_a = _diff_attn(qkv.reshape(batch, seq, QKV_COLS),
                     diff_lambda_q1[0].reshape(1, -1), diff_lambda_k1[0].reshape(1, -1),
                     diff_lambda_q2[0].reshape(1, -1), diff_lambda_k2[0].reshape(1, -1),
                     diff_subln_g[0].reshape(1, -1), tq, rg_diff)

    wuk, wuv = _split_w_ukv(w_ukv[0])
    q_cat, k_cat, v_mla = _mla_proj(latent, mla_q_norm_g[0].reshape(1, -1), mla_kv_norm_g[0].reshape(1, -1),
                                    _split_w_uq(w_uq[0]), wuk, wuv, cos_t, sin_t, batch, seq, tm_proj)
    o_b = _mla_attn(q_cat, k_cat, v_mla.reshape(batch, seq, MLA_HEADS * MLA_V_DIM), tq, rg_mla)

    h1 = _merge_out(o_a.reshape(n_tok, -1), o_b.reshape(n_tok, -1), gates, x2,
                    w_o_diff[0].astype(bf), w_o_mla[0].astype(bf), w_out[0].astype(bf), tm_cross)

    kv_mem = _mem_kv(mem, mem_norm_g[0].reshape(1, d), w_ckv[0].astype(bf))
    n_router = N_GROUPS + N_EXPERTS
    w_r = jnp.concatenate([w_router_group[0].astype(jnp.float32), w_router_expert[0].astype(jnp.float32),
                           jnp.zeros((d, LANES - n_router), jnp.float32)], axis=1)
    w_r_hi = w_r.astype(bf)
    w_r_lo = (w_r - w_r_hi.astype(jnp.float32)).astype(bf)
    w_r = jnp.concatenate([w_r_hi, w_r_lo], axis=1)
    b_r = jnp.concatenate([b_router_group[0].astype(jnp.float32), b_router_expert[0].astype(jnp.float32),
                           jnp.zeros((LANES - n_router,), jnp.float32)]).reshape(1, LANES)
    g_ffn = ffn_norm_g[0].reshape(1, d)
    h2, eid, rank, wts, cnt = _cross_router(h1, cross_norm_g[0].reshape(1, d), w_cq[0].astype(bf), kv_mem,
                                            w_co[0].astype(bf), g_ffn, w_r, b_r, seq, tm_cross)

    bm = MOE_ROWS_PER_BLOCK
    assert tm_moe == bm
    counts = cnt[0, ROUTER_EXPERT_LANE0:ROUTER_EXPERT_LANE0 + N_EXPERTS].astype(jnp.int32)
    padded = ((counts + bm - 1) // bm) * bm
    padded_end = jnp.cumsum(padded)
    padded_off = padded_end - padded
    seg_start = jnp.sum(jnp.where(eid[..., None] == jnp.arange(N_EXPERTS, dtype=jnp.int32), padded_off, 0), axis=-1)
    dest = seg_start + rank
    p_rows = ((2 * n_tok + bm - 1) // bm) * bm + N_EXPERTS * bm
    n_blocks = p_rows // bm
    n_active = (padded_end[-1] // bm).astype(jnp.int32)
    blk = jnp.minimum(jnp.arange(n_blocks, dtype=jnp.int32), n_active - 1)
    block_expert = jnp.sum((padded_end[None, :] <= (blk * bm)[:, None]).astype(jnp.int32), axis=1)
    block_expert = jnp.minimum(block_expert, N_EXPERTS - 1)
    def tile_slots(tm):
        by_choice = dest.transpose(1, 0, 2).reshape(2, n_tok // tm, tm)
        return by_choice.transpose(1, 0, 2).reshape(n_tok // tm, 1, 2 * tm)
    pad_len = padded - counts
    bit = jnp.asarray(_pad_bits(), jnp.int32)
    higher = pad_len[:, None] & ~(2 * bit[None, :] - 1)
    pad_start = (padded_off + counts)[:, None] + higher
    pad_flag = (pad_len[:, None] & bit[None, :]) > 0
    half_blk = bm // 2
    unused = (n_active + jnp.arange(N_EXPERTS, dtype=jnp.int32))[:, None] * bm + jnp.arange(2, dtype=jnp.int32) * half_blk
    unused_flag = unused < n_blocks * bm
    zfill = jnp.stack([jnp.concatenate([pad_start.reshape(-1), jnp.minimum(unused, (n_blocks * bm - half_blk)).reshape(-1)]),
                       jnp.concatenate([pad_flag.reshape(-1), unused_flag.reshape(-1)]).astype(jnp.int32)]).astype(jnp.int32)
    sched = jnp.concatenate([n_active.reshape(1), padded // bm, padded_off // bm, counts]).astype(jnp.int32)

    xb = _dispatch(tile_slots(tm_row), zfill, h2, g_ffn, p_rows, tm_row)
    y = _experts(block_expert, sched, xb, w_expert_gate[0], w_expert_up[0], w_expert_down[0], bm)
    out = _combine(tile_slots(tm_comb), h2, wts, final_norm_g.reshape(1, d), y, tm_comb)
    return out.reshape(batch, seq, d)
```

```python
import functools
import math

import jax
import jax.numpy as jnp
from jax import lax
from jax.experimental import pallas as pl
from jax.experimental.pallas import tpu as pltpu

D_MODEL = 2048
ROPE_THETA = 500000.0
NORM_EPS = 1e-6

DIFF_HEADS = 8
DIFF_HEAD_DIM = 64
DIFF_V_DIM = 2 * DIFF_HEAD_DIM
DIFF_ROT = DIFF_HEAD_DIM // 4
DIFF_SUBLN_EPS = 1e-5
DIFF_LAMBDA_INIT = 0.8 - 0.6 * math.exp(-0.3 * 0)

MLA_HEADS = 8
MLA_Q_RANK = 512
MLA_KV_RANK = 256
MLA_NOPE_DIM = 128
MLA_ROPE_DIM = 64
MLA_V_DIM = 128
MLA_QK_DIM = MLA_NOPE_DIM + MLA_ROPE_DIM

CROSS_HEADS = 4
CROSS_HEAD_DIM = 128

N_GROUPS = 4
EXPERTS_PER_GROUP = 8
N_EXPERTS = N_GROUPS * EXPERTS_PER_GROUP

LANES = 128
BF16_TILE_ROWS = 16
LOG2E = 1.4426950408889634
VMEM_LIMIT_BYTES = 56 * 1024 * 1024

ROUTER_EXPERT_LANE0 = N_GROUPS

QKV_COLS = 3 * DIFF_HEADS * DIFF_V_DIM
LATENT_COLS = 1024
GATE_COLS = 2 * D_MODEL
KPE_COL0 = MLA_Q_RANK + MLA_KV_RANK

MOE_ROWS_PER_BLOCK = 512
SUBLANES = 8


def _params(*semantics):
    return pltpu.CompilerParams(dimension_semantics=semantics, vmem_limit_bytes=VMEM_LIMIT_BYTES)


def _resident(shape):
    zeros = (0,) * len(shape)
    return pl.BlockSpec(shape, lambda *_: zeros, pipeline_mode=pl.Buffered(1))


def _rms_scale(xf, eps):
    return lax.rsqrt(jnp.mean(xf * xf, axis=-1, keepdims=True) + eps)


def _sigmoid(x):
    return 0.5 * jnp.tanh(0.5 * x) + 0.5


def _pack_bf16_pair(a, b):
    hi = lax.bitcast_convert_type(a.astype(jnp.bfloat16).astype(jnp.float32), jnp.uint32)
    lo = lax.bitcast_convert_type(b.astype(jnp.bfloat16).astype(jnp.float32), jnp.uint32)
    return hi | (lo >> 16)


def _unpack_bf16_pair(w):
    a = lax.bitcast_convert_type(w & jnp.uint32(0xFFFF0000), jnp.float32)
    b = lax.bitcast_convert_type(w << 16, jnp.float32)
    return a, b


def _rows_to_tiles(ref_view, packed):
    rows = packed.shape[0]
    for c in range(SUBLANES):
        ref_view[pl.ds(c, rows, stride=SUBLANES), :] = packed[:, c * LANES:(c + 1) * LANES]


def _tiles_to_row_chunks(ref_view, rows):
    return [ref_view[pl.ds(c, rows, stride=SUBLANES), :] for c in range(SUBLANES)]


def _lane_iota(shape):
    return lax.broadcasted_iota(jnp.int32, shape, len(shape) - 1)


def _trig_kernel(pos_ref, invf_ref, cos_ref, sin_ref):
    ang = pos_ref[...].astype(jnp.float32) * invf_ref[...]
    cos_ref[...] = jnp.cos(ang)
    sin_ref[...] = jnp.sin(ang)


def _rope_tables(positions, n_tok, tm):
    half_m = MLA_ROPE_DIM // 2
    half_d = DIFF_ROT // 2
    inv_m = jnp.float32(ROPE_THETA) ** (-jnp.arange(half_m, dtype=jnp.float32) * 2.0 / MLA_ROPE_DIM)
    inv_d = jnp.float32(ROPE_THETA) ** (-jnp.arange(half_d, dtype=jnp.float32) * 2.0 / DIFF_ROT)
    invf = jnp.concatenate([inv_m, inv_m, inv_d, inv_d,
                            jnp.zeros((DIFF_HEAD_DIM - DIFF_ROT,), jnp.float32)]).reshape(1, LANES)
    pos = positions.reshape(n_tok, 1)
    return pl.pallas_call(
        _trig_kernel,
        grid=(n_tok // tm,),
        in_specs=[pl.BlockSpec((tm, 1), lambda i: (i, 0)), _resident((1, LANES))],
        out_specs=[pl.BlockSpec((tm, LANES), lambda i: (i, 0))] * 2,
        out_shape=[jax.ShapeDtypeStruct((n_tok, LANES), jnp.float32)] * 2,
        compiler_params=_params("parallel"),
        name="rope_tables",
    )(pos, invf)


def _diff_rope_coeffs(cos_t, sin_t):
    lane = _lane_iota(cos_t.shape)
    upper = lane >= DIFF_HEAD_DIM
    cos_d = jnp.where(upper, cos_t, pltpu.roll(cos_t, DIFF_HEAD_DIM, 1))
    sin_d = jnp.where(upper, sin_t, pltpu.roll(sin_t, DIFF_HEAD_DIM, 1))
    in_head = lane % DIFF_HEAD_DIM
    half = DIFF_ROT // 2
    s_next = jnp.where(in_head < half, -sin_d, 0.0)
    s_prev = jnp.where((in_head >= half) & (in_head < DIFF_ROT), sin_d, 0.0)
    return cos_d, s_next, s_prev


def _mla_rope(pair, cos_t, sin_t):
    lane = _lane_iota(pair.shape)
    sin_signed = jnp.where(lane < MLA_ROPE_DIM // 2, -sin_t, sin_t)
    return pair * cos_t + pltpu.roll(pair, MLA_ROPE_DIM, 1) * sin_signed


INPROJ_TN = 1024
INPROJ_PIECE = 256
Q_TILES = DIFF_HEADS * DIFF_V_DIM // INPROJ_TN
ROPE_TILES = 2 * Q_TILES
QKV_TILES = QKV_COLS // INPROJ_TN
LATENT_TILES = LATENT_COLS // INPROJ_TN
GATE_TILES = GATE_COLS // INPROJ_TN
INPROJ_TILES = QKV_TILES + LATENT_TILES + GATE_TILES
GATE_ROW0 = QKV_COLS + KPE_COL0 + MLA_ROPE_DIM


def _inproj_kernel(x_ref, g_ref, w_ref, cos_ref, sin_ref, qkv_ref, lat_ref, gate_ref, xn_ref):
    j = pl.program_id(1)

    @pl.when(j == 0)
    def _():
        xf = x_ref[...]
        xn_ref[...] = (xf * _rms_scale(xf, NORM_EPS) * g_ref[...]).astype(jnp.bfloat16)

    def pieces(epilogue):
        for c in range(INPROJ_TN // INPROJ_PIECE):
            cols = slice(c * INPROJ_PIECE, (c + 1) * INPROJ_PIECE)
            acc = lax.dot_general(xn_ref[...], w_ref[cols, :], (((1,), (1,)), ((), ())),
                                  preferred_element_type=jnp.float32)
            epilogue(acc, cols)

    @pl.when(j < ROPE_TILES)
    def _():
        cos_d, s_next, s_prev = _diff_rope_coeffs(cos_ref[...], sin_ref[...])
        qscale = jnp.where(j < Q_TILES, DIFF_HEAD_DIM ** -0.5 * LOG2E, 1.0).astype(jnp.float32)

        def rope(acc, cols):
            for c in range(INPROJ_PIECE // LANES):
                xc = acc[:, c * LANES:(c + 1) * LANES]
                rot = (xc * cos_d + pltpu.roll(xc, LANES - DIFF_ROT // 2, 1) * s_next
                       + pltpu.roll(xc, DIFF_ROT // 2, 1) * s_prev)
                lo = cols.start + c * LANES
                qkv_ref[:, lo:lo + LANES] = (rot * qscale).astype(qkv_ref.dtype)

        pieces(rope)

    @pl.when((j >= ROPE_TILES) & (j < QKV_TILES))
    def _():
        def value(acc, cols):
            qkv_ref[:, cols] = acc.astype(qkv_ref.dtype)

        pieces(value)

    @pl.when((j >= QKV_TILES) & (j < QKV_TILES + LATENT_TILES))
    def _():
        def latent(acc, cols):
            if cols.start <= KPE_COL0 < cols.stop:
                c0 = KPE_COL0 - cols.start
                v = acc[:, c0:c0 + LANES]
                lane = _lane_iota(v.shape)
                half = MLA_ROPE_DIM // 2
                swapped = jnp.where(lane < MLA_ROPE_DIM + half, pltpu.roll(v, half, 1),
                                    pltpu.roll(v, MLA_ROPE_DIM + half, 1))
                parts = [acc[:, :c0], jnp.where(lane < MLA_ROPE_DIM, v, swapped), acc[:, c0 + LANES:]]
                acc = jnp.concatenate([p for p in parts if p.shape[1]], axis=1)
            lat_ref[:, cols] = acc

        pieces(latent)

    @pl.when(j >= QKV_TILES + LATENT_TILES)
    def _():
        def gate(acc, cols):
            gate_ref[:, cols] = _sigmoid(acc).astype(gate_ref.dtype)

        pieces(gate)


def _inproj(x2, g, w_all, cos_t, sin_t, tm):
    n_tok, d = x2.shape
    tn = INPROJ_TN
    lat0 = QKV_TILES
    gate0 = QKV_TILES + LATENT_TILES

    def w_rows(i, j):
        start = jnp.where(j < gate0, j * tn, GATE_ROW0 + (j - gate0) * tn)
        return (pl.multiple_of(start, BF16_TILE_ROWS), 0)

    assert GATE_ROW0 % BF16_TILE_ROWS == 0
    return pl.pallas_call(
        _inproj_kernel,
        grid=(n_tok // tm, INPROJ_TILES),
        in_specs=[pl.BlockSpec((tm, d), lambda i, j: (i, 0)),
                  _resident((1, d)),
                  pl.BlockSpec((pl.Element(tn), pl.Element(d)), w_rows),
                  pl.BlockSpec((tm, LANES), lambda i, j: (i, 0)),
                  pl.BlockSpec((tm, LANES), lambda i, j: (i, 0))],
        out_specs=[pl.BlockSpec((tm, tn), lambda i, j: (i, jnp.clip(j, 0, QKV_TILES - 1))),
                   pl.BlockSpec((tm, tn), lambda i, j: (i, jnp.clip(j - lat0, 0, LATENT_TILES - 1))),
                   pl.BlockSpec((tm, tn), lambda i, j: (i, jnp.clip(j - gate0, 0, GATE_TILES - 1)))],
        out_shape=[jax.ShapeDtypeStruct((n_tok, QKV_COLS), jnp.bfloat16),
                   jax.ShapeDtypeStruct((n_tok, LATENT_COLS), jnp.float32),
                   jax.ShapeDtypeStruct((n_tok, GATE_COLS), jnp.bfloat16)],
        scratch_shapes=[pltpu.VMEM((tm, d), jnp.bfloat16)],
        compiler_params=_params("parallel", "arbitrary"),
        name="inproj",
    )(x2, g, w_all, cos_t, sin_t)


def _mla_proj_kernel(c_ref, gq_ref, gkv_ref, wuq_ref, wuk_ref, wuv_ref, cos_ref, sin_ref,
                     q_ref, k_ref, v_ref):
    cos_t = cos_ref[...]
    sin_t = sin_ref[...]
    cq = c_ref[:, :MLA_Q_RANK]
    cqn = (cq * _rms_scale(cq, NORM_EPS) * gq_ref[...]).astype(jnp.bfloat16)
    ckv = c_ref[:, MLA_Q_RANK:KPE_COL0]
    ckvn = (ckv * _rms_scale(ckv, NORM_EPS) * gkv_ref[...]).astype(jnp.bfloat16)
    kpe = _mla_rope(c_ref[:, KPE_COL0:KPE_COL0 + LANES], cos_t, sin_t)[:, :MLA_ROPE_DIM].astype(k_ref.dtype)
    qscale = MLA_QK_DIM ** -0.5 * LOG2E
    for h in range(MLA_HEADS):
        r = jnp.dot(cqn, wuq_ref[h], preferred_element_type=jnp.float32)
        q_ref[0, h, :, :MLA_NOPE_DIM] = (r[:, :MLA_NOPE_DIM] * qscale).astype(q_ref.dtype)
        qpe = _mla_rope(r[:, MLA_NOPE_DIM:], cos_t, sin_t)[:, :MLA_ROPE_DIM]
        q_ref[0, h, :, MLA_NOPE_DIM:] = (qpe * qscale).astype(q_ref.dtype)
        kn = jnp.dot(ckvn, wuk_ref[h], preferred_element_type=jnp.float32)
        k_ref[0, h, :, :MLA_NOPE_DIM] = kn.astype(k_ref.dtype)
        k_ref[0, h, :, MLA_NOPE_DIM:] = kpe
    v_ref[...] = jnp.dot(ckvn, wuv_ref[...], preferred_element_type=jnp.float32).astype(v_ref.dtype)


def _mla_proj(latent, gq, gkv, wuq, wuk, wuv, cos_t, sin_t, batch, seq, tm):
    n_tok = latent.shape[0]
    per_b = seq // tm
    head_spec = pl.BlockSpec((1, MLA_HEADS, tm, MLA_QK_DIM), lambda i: (i // per_b, 0, i % per_b, 0))
    head_shape = jax.ShapeDtypeStruct((batch, MLA_HEADS, seq, MLA_QK_DIM), jnp.bfloat16)
    return pl.pallas_call(
        _mla_proj_kernel,
        grid=(n_tok // tm,),
        in_specs=[pl.BlockSpec((tm, LATENT_COLS), lambda i: (i, 0)),
                  _resident(gq.shape), _resident(gkv.shape),
                  _resident(wuq.shape), _resident(wuk.shape), _resident(wuv.shape),
                  pl.BlockSpec((tm, LANES), lambda i: (i, 0)),
                  pl.BlockSpec((tm, LANES), lambda i: (i, 0))],
        out_specs=[head_spec, head_spec,
                   pl.BlockSpec((tm, MLA_HEADS * MLA_V_DIM), lambda i: (i, 0))],
        out_shape=[head_shape, head_shape,
                   jax.ShapeDtypeStruct((n_tok, MLA_HEADS * MLA_V_DIM), jnp.bfloat16)],
        compiler_params=_params("parallel"),
        name="mla_proj",
    )(latent, gq, gkv, wuq, wuk, wuv, cos_t, sin_t)


def _with_ones(v):
    return jnp.concatenate([v, jnp.ones((v.shape[0], LANES), v.dtype)], axis=-1)


def _softmax_pv(s, v_ones):
    m = jnp.max(s, axis=-1, keepdims=True)
    p = jnp.exp2(s - m).astype(v_ones.dtype)
    pv = jnp.dot(p, v_ones, preferred_element_type=jnp.float32)
    dv = v_ones.shape[1] - LANES
    return pv[:, :dv] / pv[:, dv:]


ATTN_HEADS_PER_STEP = 2


def _diff_attn_kernel(q_ref, k_ref, v_ref, lq1_ref, lk1_ref, lq2_ref, lk2_ref, g_ref, o_ref, v1_ref, *, rg):
    heads = q_ref.shape[2] // LANES

    @pl.when(pl.program_id(2) == 0)
    def _():
        for h in range(heads):
            v1_ref[h] = _with_ones(v_ref[0, :, h * DIFF_V_DIM:(h + 1) * DIFF_V_DIM])

    lam = (jnp.exp(jnp.sum(lq1_ref[...] * lk1_ref[...], axis=-1, keepdims=True))
           - jnp.exp(jnp.sum(lq2_ref[...] * lk2_ref[...], axis=-1, keepdims=True))
           + DIFF_LAMBDA_INIT)
    lane = _lane_iota((rg, LANES))
    for g in range(q_ref.shape[1] // rg):
        for h in range(heads):
            cols = slice(h * LANES, (h + 1) * LANES)
            q = q_ref[0, g * rg:(g + 1) * rg, cols]
            k = k_ref[0, :, cols]
            zero = jnp.zeros_like(q)
            q12 = jnp.concatenate([jnp.where(lane < DIFF_HEAD_DIM, q, zero),
                                   jnp.where(lane >= DIFF_HEAD_DIM, q, zero)], axis=0)
            s = lax.dot_general(q12, k, (((1,), (1,)), ((), ())), preferred_element_type=jnp.float32)
            a = _softmax_pv(s, v1_ref[h])
            o = a[:rg] - lam * a[rg:]
            o = o * _rms_scale(o, DIFF_SUBLN_EPS) * g_ref[...] * (1.0 - DIFF_LAMBDA_INIT)
            o_ref[0, g * rg:(g + 1) * rg, cols] = o.astype(o_ref.dtype)


def _diff_attn(qkv3, lq1, lk1, lq2, lk2, subln_g, tq, rg):
    batch, seq, _ = qkv3.shape
    hp = ATTN_HEADS_PER_STEP
    steps = DIFF_HEADS // hp
    width = hp * DIFF_V_DIM
    return pl.pallas_call(
        functools.partial(_diff_attn_kernel, rg=rg),
        grid=(batch, steps, seq // tq),
        in_specs=[pl.BlockSpec((1, tq, width), lambda b, hh, i: (b, i, hh)),
                  pl.BlockSpec((1, seq, width), lambda b, hh, i: (b, 0, steps + hh)),
                  pl.BlockSpec((1, seq, width), lambda b, hh, i: (b, 0, 2 * steps + hh)),
                  _resident(lq1.shape), _resident(lk1.shape), _resident(lq2.shape), _resident(lk2.shape),
                  _resident(subln_g.shape)],
        out_specs=pl.BlockSpec((1, tq, width), lambda b, hh, i: (b, i, hh)),
        out_shape=jax.ShapeDtypeStruct((batch, seq, DIFF_HEADS * DIFF_V_DIM), jnp.bfloat16),
        scratch_shapes=[pltpu.VMEM((hp, seq, DIFF_V_DIM + LANES), jnp.bfloat16)],
        compiler_params=_params("parallel", "parallel", "arbitrary"),
        name="diff_attn",
    )(qkv3, qkv3, qkv3, lq1, lk1, lq2, lk2, subln_g)


def _mla_attn_kernel(q_ref, k_ref, v_ref, o_ref, v1_ref, *, rg):
    heads = q_ref.shape[1]

    @pl.when(pl.program_id(2) == 0)
    def _():
        for h in range(heads):
            v1_ref[h] = _with_ones(v_ref[0, :, h * MLA_V_DIM:(h + 1) * MLA_V_DIM])

    for g in range(q_ref.shape[2] // rg):
        for h in range(heads):
            s = lax.dot_general(q_ref[0, h, g * rg:(g + 1) * rg], k_ref[0, h], (((1,), (1,)), ((), ())),
                                preferred_element_type=jnp.float32)
            o_ref[0, g * rg:(g + 1) * rg, h * MLA_V_DIM:(h + 1) * MLA_V_DIM] = (
                _softmax_pv(s, v1_ref[h]).astype(o_ref.dtype))


def _mla_attn(q_cat, k_cat, v3, tq, rg):
    batch, heads, seq, dqk = q_cat.shape
    hp = ATTN_HEADS_PER_STEP
    return pl.pallas_call(
        functools.partial(_mla_attn_kernel, rg=rg),
        grid=(batch, heads // hp, seq // tq),
        in_specs=[pl.BlockSpec((1, hp, tq, dqk), lambda b, h, i: (b, h, i, 0)),
                  pl.BlockSpec((1, hp, seq, dqk), lambda b, h, i: (b, h, 0, 0)),
                  pl.BlockSpec((1, seq, hp * MLA_V_DIM), lambda b, h, i: (b, 0, h))],
        out_specs=pl.BlockSpec((1, tq, hp * MLA_V_DIM), lambda b, h, i: (b, i, h)),
        out_shape=jax.ShapeDtypeStruct((batch, seq, heads * MLA_V_DIM), jnp.bfloat16),
        scratch_shapes=[pltpu.VMEM((hp, seq, MLA_V_DIM + LANES), jnp.bfloat16)],
        compiler_params=_params("parallel", "parallel", "arbitrary"),
        name="mla_attn",
    )(q_cat, k_cat, v3)


def _merge_out_kernel(oa_ref, ob_ref, sga_ref, sgb_ref, x_ref, woa_ref, wob_ref, wout_ref, h_ref):
    ya = jnp.dot(oa_ref[...], woa_ref[...], preferred_element_type=jnp.float32)
    yb = jnp.dot(ob_ref[...], wob_ref[...], preferred_element_type=jnp.float32)
    merged = sga_ref[...].astype(jnp.float32) * ya + sgb_ref[...].astype(jnp.float32) * yb
    h_ref[...] = x_ref[...] + jnp.dot(merged.astype(jnp.bfloat16), wout_ref[...],
                                       preferred_element_type=jnp.float32)


def _merge_out(o_a, o_b, gates, x2, w_oa, w_ob, w_out, tm):
    n_tok, d = x2.shape
    return pl.pallas_call(
        _merge_out_kernel,
        grid=(n_tok // tm,),
        in_specs=[pl.BlockSpec((tm, o_a.shape[1]), lambda i: (i, 0)),
                  pl.BlockSpec((tm, o_b.shape[1]), lambda i: (i, 0)),
                  pl.BlockSpec((tm, d), lambda i: (i, 0)),
                  pl.BlockSpec((tm, d), lambda i: (i, 1)),
                  pl.BlockSpec((tm, d), lambda i: (i, 0)),
                  _resident(w_oa.shape), _resident(w_ob.shape), _resident(w_out.shape)],
        out_specs=pl.BlockSpec((tm, d), lambda i: (i, 0)),
        out_shape=jax.ShapeDtypeStruct((n_tok, d), jnp.float32),
        compiler_params=_params("parallel"),
        name="merge_out",
    )(o_a, o_b, gates, gates, x2, w_oa, w_ob, w_out)


def _mem_kv_kernel(mem_ref, g_ref, w_ref, kv_ref):
    mf = mem_ref[0]
    mn = (mf * _rms_scale(mf, NORM_EPS) * g_ref[...]).astype(jnp.bfloat16)
    kv_ref[0] = jnp.dot(mn, w_ref[...], preferred_element_type=jnp.float32).astype(kv_ref.dtype)


def _mem_kv(mem, g, w_ckv):
    batch, m, d = mem.shape
    return pl.pallas_call(
        _mem_kv_kernel,
        grid=(batch,),
        in_specs=[pl.BlockSpec((1, m, d), lambda b: (b, 0, 0)), _resident(g.shape), _resident(w_ckv.shape)],
        out_specs=pl.BlockSpec((1, m, w_ckv.shape[1]), lambda b: (b, 0, 0)),
        out_shape=jax.ShapeDtypeStruct((batch, m, w_ckv.shape[1]), jnp.bfloat16),
        compiler_params=_params("parallel"),
        name="mem_kv",
    )(mem, g, w_ckv)


def _cross_router_kernel(h_ref, gc_ref, wcq_ref, kv_ref, wco_ref, gf_ref, wr_ref, br_ref,
                         h2_ref, eid_ref, rank_ref, wts_ref, cnt_ref, carry_ref):
    i = pl.program_id(0)

    @pl.when(i == 0)
    def _():
        carry_ref[...] = jnp.zeros_like(carry_ref)

    h1 = h_ref[...]
    tm = h1.shape[0]
    hn = (h1 * _rms_scale(h1, NORM_EPS) * gc_ref[...]).astype(jnp.bfloat16)
    q = jnp.dot(hn, wcq_ref[...], preferred_element_type=jnp.float32) * (CROSS_HEAD_DIM ** -0.5 * LOG2E)
    q = q.astype(jnp.bfloat16)
    kv_cols = CROSS_HEADS * CROSS_HEAD_DIM
    heads = []
    for hd in range(CROSS_HEADS):
        lo = hd * CROSS_HEAD_DIM
        kh = kv_ref[0, :, lo:lo + CROSS_HEAD_DIM]
        vh = kv_ref[0, :, kv_cols + lo:kv_cols + lo + CROSS_HEAD_DIM]
        s = lax.dot_general(q[:, lo:lo + CROSS_HEAD_DIM], kh, (((1,), (1,)), ((), ())),
                            preferred_element_type=jnp.float32)
        heads.append(_softmax_pv(s, _with_ones(vh)).astype(jnp.bfloat16))
    o = jnp.concatenate(heads, axis=-1)
    h2 = h1 + jnp.dot(o, wco_ref[...], preferred_element_type=jnp.float32)
    h2_ref[...] = h2

    t = h2 * _rms_scale(h2, NORM_EPS) * gf_ref[...]
    t_hi = t.astype(jnp.bfloat16)
    t_lo = (t - t_hi.astype(jnp.float32)).astype(jnp.bfloat16)
    hi = jnp.dot(t_hi, wr_ref[...], preferred_element_type=jnp.float32)
    lo = jnp.dot(t_lo, wr_ref[:, :LANES], preferred_element_type=jnp.float32)
    logits = hi[:, :LANES] + (hi[:, LANES:] + lo) + br_ref[...]
    lane = _lane_iota(logits.shape)
    neg = jnp.float32(-jnp.inf)
    big = jnp.int32(2 * LANES)
    is_group = lane < N_GROUPS
    lg = jnp.where(is_group, logits, neg)
    mg = jnp.max(lg, axis=-1, keepdims=True)
    g_idx = jnp.min(jnp.where(is_group & (logits == mg), lane, big), axis=-1, keepdims=True)
    g_p = 1.0 / jnp.sum(jnp.exp(lg - mg), axis=-1, keepdims=True)
    lo_lane = ROUTER_EXPERT_LANE0 + EXPERTS_PER_GROUP * g_idx
    in_grp = (lane >= lo_lane) & (lane < lo_lane + EXPERTS_PER_GROUP)
    l1 = jnp.max(jnp.where(in_grp, logits, neg), axis=-1, keepdims=True)
    i1 = jnp.min(jnp.where(in_grp & (logits == l1), lane, big), axis=-1, keepdims=True)
    rest = in_grp & (lane != i1)
    l2 = jnp.max(jnp.where(rest, logits, neg), axis=-1, keepdims=True)
    i2 = jnp.min(jnp.where(rest & (logits == l2), lane, big), axis=-1, keepdims=True)
    d = jnp.exp(l2 - l1)
    w1 = g_p / (1.0 + d)
    w2 = w1 * d

    oh1 = lane == i1
    oh2 = lane == i2
    cnt = (oh1 | oh2).astype(jnp.bfloat16)
    row = lax.broadcasted_iota(jnp.int32, (tm, tm), 0)
    col = lax.broadcasted_iota(jnp.int32, (tm, tm), 1)
    before = (col < row).astype(jnp.bfloat16)
    slot = jnp.dot(before, cnt, preferred_element_type=jnp.float32) + carry_ref[...]
    r1 = jnp.sum(jnp.where(oh1, slot, 0.0), axis=-1, keepdims=True)
    r2 = jnp.sum(jnp.where(oh2, slot, 0.0), axis=-1, keepdims=True)
    carry_ref[...] += jnp.sum(cnt.astype(jnp.float32), axis=0, keepdims=True)
    cnt_ref[...] = carry_ref[...]

    eye = row == col

    def to_row(c, dtype):
        return jnp.sum(jnp.where(eye, c.astype(jnp.float32), 0.0), axis=0, keepdims=True).astype(dtype)

    eid_ref[0] = jnp.concatenate([to_row(i1 - ROUTER_EXPERT_LANE0, jnp.int32),
                                  to_row(i2 - ROUTER_EXPERT_LANE0, jnp.int32)], axis=0)
    rank_ref[0] = jnp.concatenate([to_row(r1, jnp.int32), to_row(r2, jnp.int32)], axis=0)
    wts_ref[...] = jnp.where(_lane_iota((tm, 2)) == 0, w1, w2)


def _cross_router(h1, gc, w_cq, kv_mem, w_co, gf, w_r, b_r, seq, tm):
    n_tok, d = h1.shape
    per_b = seq // tm
    row2 = pl.BlockSpec((tm, 2), lambda i: (i, 0))
    lane2 = pl.BlockSpec((1, 2, tm), lambda i: (i, 0, 0))
    return pl.pallas_call(
        _cross_router_kernel,
        grid=(n_tok // tm,),
        in_specs=[pl.BlockSpec((tm, d), lambda i: (i, 0)),
                  _resident(gc.shape), _resident(w_cq.shape),
                  pl.BlockSpec((1,) + kv_mem.shape[1:], lambda i: (i // per_b, 0, 0)),
                  _resident(w_co.shape), _resident(gf.shape), _resident(w_r.shape), _resident(b_r.shape)],
        out_specs=[pl.BlockSpec((tm, d), lambda i: (i, 0)), lane2, lane2, row2,
                   pl.BlockSpec((1, LANES), lambda i: (0, 0))],
        out_shape=[jax.ShapeDtypeStruct((n_tok, d), jnp.float32),
                   jax.ShapeDtypeStruct((n_tok // tm, 2, tm), jnp.int32),
                   jax.ShapeDtypeStruct((n_tok // tm, 2, tm), jnp.int32),
                   jax.ShapeDtypeStruct((n_tok, 2), jnp.float32),
                   jax.ShapeDtypeStruct((1, LANES), jnp.float32)],
        scratch_shapes=[pltpu.VMEM((1, LANES), jnp.float32)],
        compiler_params=_params("arbitrary"),
        name="cross_router",
    )(h1, gc, w_cq, kv_mem, w_co, gf, w_r, b_r)


def _pad_bits():
    return [1 << b for b in reversed(range(MOE_ROWS_PER_BLOCK.bit_length() - 1))]


def _zero_fill_sizes():
    return _pad_bits() * N_EXPERTS + [MOE_ROWS_PER_BLOCK // 2] * (2 * N_EXPERTS)


def _dispatch_kernel(dest_ref, zfill_ref, h_ref, g_ref, xb_ref, t_ref, zero_ref, sem, zsem):
    i = pl.program_id(0)
    tm = h_ref.shape[0]
    slot = i % 2

    def zero_copies(action):
        for idx, rows in enumerate(_zero_fill_sizes()):
            @pl.when(zfill_ref[1, idx] > 0)
            def _():
                start = pl.multiple_of(zfill_ref[0, idx] * SUBLANES, SUBLANES)
                action(pltpu.make_async_copy(zero_ref.at[pl.ds(0, rows * SUBLANES), :],
                                             xb_ref.at[pl.ds(start, rows * SUBLANES), :], zsem))

    @pl.when(i == 0)
    def _():
        zero_ref[...] = jnp.zeros_like(zero_ref)
        zero_copies(lambda c: c.start())

    h2 = h_ref[...]
    t = h2 * _rms_scale(h2, NORM_EPS) * g_ref[...]
    half = t.shape[1] // 2
    _rows_to_tiles(t_ref.at[slot], _pack_bf16_pair(t[:, :half], t[:, half:]))

    for r in range(tm):
        for k in range(2):
            dst = pl.multiple_of(dest_ref[0, k * tm + r] * SUBLANES, SUBLANES)
            pltpu.make_async_copy(t_ref.at[slot, pl.ds(r * SUBLANES, SUBLANES), :],
                                  xb_ref.at[pl.ds(dst, SUBLANES), :], sem.at[slot]).start(priority=k)

    def wait_tile(which):
        for _ in range(2):
            pltpu.make_async_copy(t_ref.at[which], xb_ref.at[pl.ds(0, tm * SUBLANES), :], sem.at[which]).wait()

    @pl.when(i > 0)
    def _():
        wait_tile(1 - slot)

    @pl.when(i == pl.num_programs(0) - 1)
    def _():
        wait_tile(slot)
        zero_copies(lambda c: c.wait())


def _dispatch(dest3, zfill, h2, gf, p_rows, tm):
    n_tok, d = h2.shape
    return pl.pallas_call(
        _dispatch_kernel,
        grid=(n_tok // tm,),
        in_specs=[pl.BlockSpec((None, 1, 2 * tm), lambda i: (i, 0, 0), memory_space=pltpu.SMEM),
                  pl.BlockSpec(memory_space=pltpu.SMEM),
                  pl.BlockSpec((tm, d), lambda i: (i, 0)),
                  _resident(gf.shape)],
        out_specs=pl.BlockSpec(memory_space=pl.ANY),
        out_shape=jax.ShapeDtypeStruct((p_rows * SUBLANES, LANES), jnp.uint32),
        scratch_shapes=[pltpu.VMEM((2, tm * SUBLANES, LANES), jnp.uint32),
                        pltpu.VMEM((MOE_ROWS_PER_BLOCK // 2 * SUBLANES, LANES), jnp.uint32),
                        pltpu.SemaphoreType.DMA((2,)), pltpu.SemaphoreType.DMA(())],
        compiler_params=_params("arbitrary"),
        name="moe_dispatch",
    )(dest3, zfill, h2, gf)


def _expert_kernel(be_ref, nact_ref, x_ref, wg_hbm, wu_hbm, wd_hbm, y_ref,
                   wg_f, wu_f, wd_f, wg_b, wu_b, wd_b, sem):
    i = pl.program_id(0)
    blk = i - 1
    nact = nact_ref[0]
    last_blk = pl.num_programs(0) - 2

    def weight_copies(e):
        return (pltpu.make_async_copy(wg_hbm.at[e], wg_f, sem),
                pltpu.make_async_copy(wu_hbm.at[e], wu_f, sem),
                pltpu.make_async_copy(wd_hbm.at[e], wd_f, sem))

    def fetch(e):
        for c in weight_copies(e):
            c.start(priority=1)

    def land(e):
        for c in weight_copies(e):
            c.wait()
        half = wg_f.shape[0] // 2
        for c in range(SUBLANES):
            for part, src0 in enumerate((c * LANES, half + c * LANES)):
                dst0 = (2 * c + part) * LANES
                wg_b[dst0:dst0 + LANES, :] = wg_f[src0:src0 + LANES, :].astype(jnp.bfloat16)
                wu_b[dst0:dst0 + LANES, :] = wu_f[src0:src0 + LANES, :].astype(jnp.bfloat16)
        wd_b[...] = wd_f[...].astype(jnp.bfloat16)

    @pl.when(i == 0)
    def _():
        fetch(be_ref[0])
        land(be_ref[0])

    @pl.when((i > 0) & (blk < nact))
    def _():
        here = be_ref[blk]
        nxt = be_ref[jnp.minimum(blk + 1, last_blk)]
        prv = be_ref[jnp.maximum(blk - 1, 0)]
        seg_end = nact_ref[1 + N_EXPERTS + here] + nact_ref[1 + here]
        has_next = seg_end < nact
        after = be_ref[jnp.minimum(seg_end, last_blk)]
        is_first = (blk == 0) | (prv != here)
        is_last = (blk + 1 >= nact) | (nxt != here)

        @pl.when(is_first & has_next)
        def _():
            fetch(after)

        bm = x_ref.shape[0] // SUBLANES
        hb = bm // 2
        rows_here = nact_ref[1 + 2 * N_EXPERTS + here] - (blk - nact_ref[1 + N_EXPERTS + here]) * bm

        def swiglu(h):
            view = pl.ds(h * hb * SUBLANES, hb * SUBLANES)
            parts = []
            for chunk in _tiles_to_row_chunks(x_ref.at[view, :], hb):
                x_a, x_b = _unpack_bf16_pair(chunk)
                parts += [x_a.astype(jnp.bfloat16), x_b.astype(jnp.bfloat16)]
            xb = jnp.concatenate(parts, axis=1)
            gate = jnp.dot(xb, wg_b[...], preferred_element_type=jnp.float32)
            up = jnp.dot(xb, wu_b[...], preferred_element_type=jnp.float32)
            hid = (gate * _sigmoid(gate) * up).astype(jnp.bfloat16)
            y = jnp.dot(hid, wd_b[...], preferred_element_type=jnp.float32)
            half = y.shape[1] // 2
            _rows_to_tiles(y_ref.at[view, :], _pack_bf16_pair(y[:, :half], y[:, half:]))

        swiglu(0)

        @pl.when(rows_here > hb)
        def _():
            swiglu(1)

        @pl.when(rows_here <= hb)
        def _():
            y_ref[pl.ds(hb * SUBLANES, hb * SUBLANES), :] = jnp.zeros((hb * SUBLANES, LANES), y_ref.dtype)

        @pl.when(is_last & has_next)
        def _():
            land(after)

    @pl.when((i > 0) & (blk >= nact))
    def _():
        y_ref[...] = jnp.zeros_like(y_ref)


def _experts(block_expert, sched, xb, w_gate, w_up, w_down, bm):
    p_rows = xb.shape[0] // SUBLANES
    d = w_gate.shape[1]
    de = w_gate.shape[-1]

    def x_map(i, be, sc):
        return (jnp.clip(i - 1, 0, sc[0] - 1), 0)

    grid_spec = pltpu.PrefetchScalarGridSpec(
        num_scalar_prefetch=2,
        grid=(p_rows // bm + 1,),
        in_specs=[pl.BlockSpec((bm * SUBLANES, LANES), x_map),
                  pl.BlockSpec(memory_space=pl.ANY),
                  pl.BlockSpec(memory_space=pl.ANY),
                  pl.BlockSpec(memory_space=pl.ANY)],
        out_specs=pl.BlockSpec((bm * SUBLANES, LANES), lambda i, be, sc: (jnp.maximum(i - 1, 0), 0)),
        scratch_shapes=[pltpu.VMEM((d, de), jnp.float32), pltpu.VMEM((d, de), jnp.float32),
                        pltpu.VMEM((de, d), jnp.float32),
                        pltpu.VMEM((d, de), jnp.bfloat16), pltpu.VMEM((d, de), jnp.bfloat16),
                        pltpu.VMEM((de, d), jnp.bfloat16),
                        pltpu.SemaphoreType.DMA(())],
    )
    return pl.pallas_call(
        _expert_kernel,
        grid_spec=grid_spec,
        out_shape=jax.ShapeDtypeStruct(xb.shape, jnp.uint32),
        compiler_params=_params("arbitrary"),
        name="moe_experts",
    )(block_expert, sched, xb, w_gate, w_up, w_down)


def _combine_kernel(dest_ref, dest_next_ref, h_ref, wts_ref, g_ref, y_ref, o_ref, ybuf, sem):
    i = pl.program_id(0)
    tm = h_ref.shape[0]
    slot = i % 2

    def gather(idx_ref, which):
        for r in range(tm):
            for k in range(2):
                src = pl.multiple_of(idx_ref[0, k * tm + r] * SUBLANES, SUBLANES)
                pltpu.make_async_copy(y_ref.at[pl.ds(src, SUBLANES), :],
                                      ybuf.at[which, k, pl.ds(r * SUBLANES, SUBLANES), :], sem.at[which]).start(priority=k)

    def wait_tile(which):
        for k in range(2):
            pltpu.make_async_copy(y_ref.at[pl.ds(0, tm * SUBLANES), :], ybuf.at[which, k], sem.at[which]).wait()

    @pl.when(i == 0)
    def _():
        gather(dest_ref, slot)

    wait_tile(slot)
    gather(dest_next_ref, 1 - slot)

    w = wts_ref[...]
    half = h_ref.shape[1] // 2
    lo_parts, hi_parts = [], []
    for c, (c0, c1) in enumerate(zip(_tiles_to_row_chunks(ybuf.at[slot, 0], tm),
                                      _tiles_to_row_chunks(ybuf.at[slot, 1], tm))):
        a0, b0 = _unpack_bf16_pair(c0)
        a1, b1 = _unpack_bf16_pair(c1)
        lo_parts.append(h_ref[:, c * LANES:(c + 1) * LANES] + w[:, 0:1] * a0 + w[:, 1:2] * a1)
        hi_parts.append(h_ref[:, half + c * LANES:half + (c + 1) * LANES] + w[:, 0:1] * b0 + w[:, 1:2] * b1)
    h3 = jnp.concatenate(lo_parts + hi_parts, axis=1)
    o_ref[...] = h3 * _rms_scale(h3, NORM_EPS) * g_ref[...]

    @pl.when(i == pl.num_programs(0) - 1)
    def _():
        wait_tile(1 - slot)


def _combine(dest3, h2, wts, g_final, y, tm):
    n_tok, d = h2.shape
    last = n_tok // tm - 1
    return pl.pallas_call(
        _combine_kernel,
        grid=(n_tok // tm,),
        in_specs=[pl.BlockSpec((None, 1, 2 * tm), lambda i: (i, 0, 0), memory_space=pltpu.SMEM),
                  pl.BlockSpec((None, 1, 2 * tm), lambda i: (jnp.minimum(i + 1, last), 0, 0),
                               memory_space=pltpu.SMEM),
                  pl.BlockSpec((tm, d), lambda i: (i, 0)),
                  pl.BlockSpec((tm, 2), lambda i: (i, 0)),
                  _resident(g_final.shape),
                  pl.BlockSpec(memory_space=pl.ANY)],
        out_specs=pl.BlockSpec((tm, d), lambda i: (i, 0)),
        out_shape=jax.ShapeDtypeStruct((n_tok, d), jnp.float32),
        scratch_shapes=[pltpu.VMEM((2, 2, tm * SUBLANES, LANES), jnp.uint32), pltpu.SemaphoreType.DMA((2,))],
        compiler_params=_params("arbitrary"),
        name="moe_combine",
    )(dest3, dest3, h2, wts, g_final, y)


def _transpose_w_in(w_in):
    return jnp.swapaxes(w_in, 0, 1).astype(jnp.bfloat16)


def _split_w_uq(w_uq):
    half = MLA_ROPE_DIM // 2
    w = w_uq.reshape(MLA_Q_RANK, MLA_HEADS, MLA_QK_DIM).transpose(1, 0, 2)
    pe = w[:, :, MLA_NOPE_DIM:]
    pe_swapped = jnp.concatenate([pe[:, :, half:], pe[:, :, :half]], axis=2)
    return jnp.concatenate([w, pe_swapped], axis=2).astype(jnp.bfloat16)


def _split_w_ukv(w_ukv):
    w = w_ukv.reshape(MLA_KV_RANK, MLA_HEADS, MLA_NOPE_DIM + MLA_V_DIM)
    wuk = w[:, :, :MLA_NOPE_DIM].transpose(1, 0, 2).astype(jnp.bfloat16)
    wuv = w[:, :, MLA_NOPE_DIM:].reshape(MLA_KV_RANK, MLA_HEADS * MLA_V_DIM).astype(jnp.bfloat16)
    return wuk, wuv


def kernel(x, mem, positions, attn_norm_g, w_in, diff_lambda_q1, diff_lambda_k1, diff_lambda_q2, diff_lambda_k2, diff_subln_g, w_o_diff, mla_q_norm_g, w_uq, mla_kv_norm_g, w_ukv, w_o_mla, w_out, cross_norm_g, mem_norm_g, w_cq, w_ckv, w_co, ffn_norm_g, w_router_group, b_router_group, w_router_expert, b_router_expert, w_expert_gate, w_expert_up, w_expert_down, final_norm_g):
    batch, seq, d = x.shape
    assert d == D_MODEL and w_in.shape[0] == 1, "single-layer kernel"
    n_tok = batch * seq
    bf = jnp.bfloat16
    x2 = x.reshape(n_tok, d)

    tm_proj = min(1024, seq)
    tm_row = min(512, seq)
    tm_moe = min(MOE_ROWS_PER_BLOCK, seq)
    tm_comb = min(512, seq)
    tm_cross = min(512, seq)
    tq = min(2048, seq)
    rg_diff = 128
    rg_mla = 256

    cos_t, sin_t = _rope_tables(positions, n_tok, tm_proj)

    g_attn = attn_norm_g[0].reshape(1, d)
    qkv, latent, gates = _inproj(x2, g_attn, _transpose_w_in(w_in[0]), cos_t, sin_t, tm_proj)

    o_a = _diff_attn(qkv.reshape(batch, seq, QKV_COLS),
                     diff_lambda_q1[0].reshape(1, -1), diff_lambda_k1[0].reshape(1, -1),
                     diff_lambda_q2[0].reshape(1, -1), diff_lambda_k2[0].reshape(1, -1),
                     diff_subln_g[0].reshape(1, -1), tq, rg_diff)

    wuk, wuv = _split_w_ukv(w_ukv[0])
    q_cat, k_cat, v_mla = _mla_proj(latent, mla_q_norm_g[0].reshape(1, -1), mla_kv_norm_g[0].reshape(1, -1),
                                    _split_w_uq(w_uq[0]), wuk, wuv, cos_t, sin_t, batch, seq, tm_proj)
    o_b = _mla_attn(q_cat, k_cat, v_mla.reshape(batch, seq, MLA_HEADS * MLA_V_DIM), tq, rg_mla)

    h1 = _merge_out(o_a.reshape(n_tok, -1), o_b.reshape(n_tok, -1), gates, x2,
                    w_o_diff[0].astype(bf), w_o_mla[0].astype(bf), w_out[0].astype(bf), tm_cross)

    kv_mem = _mem_kv(mem, mem_norm_g[0].reshape(1, d), w_ckv[0].astype(bf))
    n_router = N_GROUPS + N_EXPERTS
    w_r = jnp.concatenate([w_router_group[0].astype(jnp.float32), w_router_expert[0].astype(jnp.float32),
                           jnp.zeros((d, LANES - n_router), jnp.float32)], axis=1)
    w_r_hi = w_r.astype(bf)
    w_r_lo = (w_r - w_r_hi.astype(jnp.float32)).astype(bf)
    w_r = jnp.concatenate([w_r_hi, w_r_lo], axis=1)
    b_r = jnp.concatenate([b_router_group[0].astype(jnp.float32), b_router_expert[0].astype(jnp.float32),
                           jnp.zeros((LANES - n_router,), jnp.float32)]).reshape(1, LANES)
    g_ffn = ffn_norm_g[0].reshape(1, d)
    h2, eid, rank, wts, cnt = _cross_router(h1, cross_norm_g[0].reshape(1, d), w_cq[0].astype(bf), kv_mem,
                                            w_co[0].astype(bf), g_ffn, w_r, b_r, seq, tm_cross)

    bm = MOE_ROWS_PER_BLOCK
    assert tm_moe == bm
    counts = cnt[0, ROUTER_EXPERT_LANE0:ROUTER_EXPERT_LANE0 + N_EXPERTS].astype(jnp.int32)
    padded = ((counts + bm - 1) // bm) * bm
    padded_end = jnp.cumsum(padded)
    padded_off = padded_end - padded
    seg_start = jnp.sum(jnp.where(eid[..., None] == jnp.arange(N_EXPERTS, dtype=jnp.int32), padded_off, 0), axis=-1)
    dest = seg_start + rank
    p_rows = ((2 * n_tok + bm - 1) // bm) * bm + N_EXPERTS * bm
    n_blocks = p_rows // bm
    n_active = (padded_end[-1] // bm).astype(jnp.int32)
    blk = jnp.minimum(jnp.arange(n_blocks, dtype=jnp.int32), n_active - 1)
    block_expert = jnp.sum((padded_end[None, :] <= (blk * bm)[:, None]).astype(jnp.int32), axis=1)
    block_expert = jnp.minimum(block_expert, N_EXPERTS - 1)
    def tile_slots(tm):
        by_choice = dest.transpose(1, 0, 2).reshape(2, n_tok // tm, tm)
        return by_choice.transpose(1, 0, 2).reshape(n_tok // tm, 1, 2 * tm)
    pad_len = padded - counts
    bit = jnp.asarray(_pad_bits(), jnp.int32)
    higher = pad_len[:, None] & ~(2 * bit[None, :] - 1)
    pad_start = (padded_off + counts)[:, None] + higher
    pad_flag = (pad_len[:, None] & bit[None, :]) > 0
    half_blk = bm // 2
    unused = (n_active + jnp.arange(N_EXPERTS, dtype=jnp.int32))[:, None] * bm + jnp.arange(2, dtype=jnp.int32) * half_blk
    unused_flag = unused < n_blocks * bm
    zfill = jnp.stack([jnp.concatenate([pad_start.reshape(-1), jnp.minimum(unused, (n_blocks * bm - half_blk)).reshape(-1)]),
                       jnp.concatenate([pad_flag.reshape(-1), unused_flag.reshape(-1)]).astype(jnp.int32)]).astype(jnp.int32)
    sched = jnp.concatenate([n_active.reshape(1), padded // bm, padded_off // bm, counts]).astype(jnp.int32)

    xb = _dispatch(tile_slots(tm_row), zfill, h2, g_ffn, p_rows, tm_row)
    y = _experts(block_expert, sched, xb, w_expert_gate[0], w_expert_up[0], w_expert_down[0], bm)
    out = _combine(tile_slots(tm_comb), h2, wts, final_norm_g.reshape(1, d), y, tm_comb)
    return out.reshape(batch, seq, d)
```

```python
import functools
import math

import jax
import jax.numpy as jnp
from jax import lax
from jax.experimental import pallas as pl
from jax.experimental.pallas import tpu as pltpu

D_MODEL = 2048
ROPE_THETA = 500000.0
NORM_EPS = 1e-6

DIFF_HEADS = 8
DIFF_HEAD_DIM = 64
DIFF_V_DIM = 2 * DIFF_HEAD_DIM
DIFF_ROT = DIFF_HEAD_DIM // 4
DIFF_SUBLN_EPS = 1e-5
DIFF_LAMBDA_INIT = 0.8 - 0.6 * math.exp(-0.3 * 0)

MLA_HEADS = 8
MLA_Q_RANK = 512
MLA_KV_RANK = 256
MLA_NOPE_DIM = 128
MLA_ROPE_DIM = 64
MLA_V_DIM = 128
MLA_QK_DIM = MLA_NOPE_DIM + MLA_ROPE_DIM

CROSS_HEADS = 4
CROSS_HEAD_DIM = 128

N_GROUPS = 4
EXPERTS_PER_GROUP = 8
N_EXPERTS = N_GROUPS * EXPERTS_PER_GROUP

LANES = 128
BF16_TILE_ROWS = 16
LOG2E = 1.4426950408889634
VMEM_LIMIT_BYTES = 56 * 1024 * 1024

ROUTER_EXPERT_LANE0 = N_GROUPS

QKV_COLS = 3 * DIFF_HEADS * DIFF_V_DIM
LATENT_COLS = 1024
GATE_COLS = 2 * D_MODEL
KPE_COL0 = MLA_Q_RANK + MLA_KV_RANK

MOE_ROWS_PER_BLOCK = 512
SUBLANES = 8


def _params(*semantics):
    return pltpu.CompilerParams(dimension_semantics=semantics, vmem_limit_bytes=VMEM_LIMIT_BYTES)


def _resident(shape):
    zeros = (0,) * len(shape)
    return pl.BlockSpec(shape, lambda *_: zeros, pipeline_mode=pl.Buffered(1))


def _rms_scale(xf, eps):
    return lax.rsqrt(jnp.mean(xf * xf, axis=-1, keepdims=True) + eps)


def _sigmoid(x):
    return 0.5 * jnp.tanh(0.5 * x) + 0.5


def _pack_bf16_pair(a, b):
    hi = lax.bitcast_convert_type(a.astype(jnp.bfloat16).astype(jnp.float32), jnp.uint32)
    lo = lax.bitcast_convert_type(b.astype(jnp.bfloat16).astype(jnp.float32), jnp.uint32)
    return hi | (lo >> 16)


def _unpack_bf16_pair(w):
    a = lax.bitcast_convert_type(w & jnp.uint32(0xFFFF0000), jnp.float32)
    b = lax.bitcast_convert_type(w << 16, jnp.float32)
    return a, b


def _rows_to_tiles(ref_view, packed):
    rows = packed.shape[0]
    for c in range(SUBLANES):
        ref_view[pl.ds(c, rows, stride=SUBLANES), :] = packed[:, c * LANES:(c + 1) * LANES]


def _tiles_to_row_chunks(ref_view, rows):
    return [ref_view[pl.ds(c, rows, stride=SUBLANES), :] for c in range(SUBLANES)]


def _lane_iota(shape):
    return lax.broadcasted_iota(jnp.int32, shape, len(shape) - 1)


def _trig_kernel(pos_ref, invf_ref, cos_ref, sin_ref):
    ang = pos_ref[...].astype(jnp.float32) * invf_ref[...]
    cos_ref[...] = jnp.cos(ang)
    sin_ref[...] = jnp.sin(ang)


def _rope_tables(positions, n_tok, tm):
    half_m = MLA_ROPE_DIM // 2
    half_d = DIFF_ROT // 2
    inv_m = jnp.float32(ROPE_THETA) ** (-jnp.arange(half_m, dtype=jnp.float32) * 2.0 / MLA_ROPE_DIM)
    inv_d = jnp.float32(ROPE_THETA) ** (-jnp.arange(half_d, dtype=jnp.float32) * 2.0 / DIFF_ROT)
    invf = jnp.concatenate([inv_m, inv_m, inv_d, inv_d,
                            jnp.zeros((DIFF_HEAD_DIM - DIFF_ROT,), jnp.float32)]).reshape(1, LANES)
    pos = positions.reshape(n_tok, 1)
    return pl.pallas_call(
        _trig_kernel,
        grid=(n_tok // tm,),
        in_specs=[pl.BlockSpec((tm, 1), lambda i: (i, 0)), _resident((1, LANES))],
        out_specs=[pl.BlockSpec((tm, LANES), lambda i: (i, 0))] * 2,
        out_shape=[jax.ShapeDtypeStruct((n_tok, LANES), jnp.float32)] * 2,
        compiler_params=_params("parallel"),
        name="rope_tables",
    )(pos, invf)


def _diff_rope_coeffs(cos_t, sin_t):
    lane = _lane_iota(cos_t.shape)
    upper = lane >= DIFF_HEAD_DIM
    cos_d = jnp.where(upper, cos_t, pltpu.roll(cos_t, DIFF_HEAD_DIM, 1))
    sin_d = jnp.where(upper, sin_t, pltpu.roll(sin_t, DIFF_HEAD_DIM, 1))
    in_head = lane % DIFF_HEAD_DIM
    half = DIFF_ROT // 2
    s_next = jnp.where(in_head < half, -sin_d, 0.0)
    s_prev = jnp.where((in_head >= half) & (in_head < DIFF_ROT), sin_d, 0.0)
    return cos_d, s_next, s_prev


def _mla_rope(pair, cos_t, sin_t):
    lane = _lane_iota(pair.shape)
    sin_signed = jnp.where(lane < MLA_ROPE_DIM // 2, -sin_t, sin_t)
    return pair * cos_t + pltpu.roll(pair, MLA_ROPE_DIM, 1) * sin_signed


INPROJ_TN = 1024
INPROJ_PIECE = 256
Q_TILES = DIFF_HEADS * DIFF_V_DIM // INPROJ_TN
ROPE_TILES = 2 * Q_TILES
QKV_TILES = QKV_COLS // INPROJ_TN
LATENT_TILES = LATENT_COLS // INPROJ_TN
GATE_TILES = GATE_COLS // INPROJ_TN
INPROJ_TILES = QKV_TILES + LATENT_TILES + GATE_TILES
GATE_ROW0 = QKV_COLS + KPE_COL0 + MLA_ROPE_DIM


def _inproj_kernel(x_ref, g_ref, w_ref, cos_ref, sin_ref, qkv_ref, lat_ref, gate_ref, xn_ref):
    j = pl.program_id(1)

    @pl.when(j == 0)
    def _():
        xf = x_ref[...]
        xn_ref[...] = (xf * _rms_scale(xf, NORM_EPS) * g_ref[...]).astype(jnp.bfloat16)

    def pieces(epilogue):
        for c in range(INPROJ_TN // INPROJ_PIECE):
            cols = slice(c * INPROJ_PIECE, (c + 1) * INPROJ_PIECE)
            acc = lax.dot_general(xn_ref[...], w_ref[cols, :], (((1,), (1,)), ((), ())),
                                  preferred_element_type=jnp.float32)
            epilogue(acc, cols)

    @pl.when(j < ROPE_TILES)
    def _():
        cos_d, s_next, s_prev = _diff_rope_coeffs(cos_ref[...], sin_ref[...])
        qscale = jnp.where(j < Q_TILES, DIFF_HEAD_DIM ** -0.5 * LOG2E, 1.0).astype(jnp.float32)

        def rope(acc, cols):
            for c in range(INPROJ_PIECE // LANES):
                xc = acc[:, c * LANES:(c + 1) * LANES]
                rot = (xc * cos_d + pltpu.roll(xc, LANES - DIFF_ROT // 2, 1) * s_next
                       + pltpu.roll(xc, DIFF_ROT // 2, 1) * s_prev)
                lo = cols.start + c * LANES
                qkv_ref[:, lo:lo + LANES] = (rot * qscale).astype(qkv_ref.dtype)

        pieces(rope)

    @pl.when((j >= ROPE_TILES) & (j < QKV_TILES))
    def _():
        def value(acc, cols):
            qkv_ref[:, cols] = acc.astype(qkv_ref.dtype)

        pieces(value)

    @pl.when((j >= QKV_TILES) & (j < QKV_TILES + LATENT_TILES))
    def _():
        def latent(acc, cols):
            if cols.start <= KPE_COL0 < cols.stop:
                c0 = KPE_COL0 - cols.start
                v = acc[:, c0:c0 + LANES]
                lane = _lane_iota(v.shape)
                half = MLA_ROPE_DIM // 2
                swapped = jnp.where(lane < MLA_ROPE_DIM + half, pltpu.roll(v, half, 1),
                                    pltpu.roll(v, MLA_ROPE_DIM + half, 1))
                parts = [acc[:, :c0], jnp.where(lane < MLA_ROPE_DIM, v, swapped), acc[:, c0 + LANES:]]
                acc = jnp.concatenate([p for p in parts if p.shape[1]], axis=1)
            lat_ref[:, cols] = acc

        pieces(latent)

    @pl.when(j >= QKV_TILES + LATENT_TILES)
    def _():
        def gate(acc, cols):
            gate_ref[:, cols] = _sigmoid(acc).astype(gate_ref.dtype)

        pieces(gate)


def _inproj(x2, g, w_all, cos_t, sin_t, tm):
    n_tok, d = x2.shape
    tn = INPROJ_TN
    lat0 = QKV_TILES
    gate0 = QKV_TILES + LATENT_TILES

    def w_rows(i, j):
        start = jnp.where(j < gate0, j * tn, GATE_ROW0 + (j - gate0) * tn)
        return (pl.multiple_of(start, BF16_TILE_ROWS), 0)

    assert GATE_ROW0 % BF16_TILE_ROWS == 0
    return pl.pallas_call(
        _inproj_kernel,
        grid=(n_tok // tm, INPROJ_TILES),
        in_specs=[pl.BlockSpec((tm, d), lambda i, j: (i, 0)),
                  _resident((1, d)),
                  pl.BlockSpec((pl.Element(tn), pl.Element(d)), w_rows),
                  pl.BlockSpec((tm, LANES), lambda i, j: (i, 0)),
                  pl.BlockSpec((tm, LANES), lambda i, j: (i, 0))],
        out_specs=[pl.BlockSpec((tm, tn), lambda i, j: (i, jnp.clip(j, 0, QKV_TILES - 1))),
                   pl.BlockSpec((tm, tn), lambda i, j: (i, jnp.clip(j - lat0, 0, LATENT_TILES - 1))),
                   pl.BlockSpec((tm, tn), lambda i, j: (i, jnp.clip(j - gate0, 0, GATE_TILES - 1)))],
        out_shape=[jax.ShapeDtypeStruct((n_tok, QKV_COLS), jnp.bfloat16),
                   jax.ShapeDtypeStruct((n_tok, LATENT_COLS), jnp.float32),
                   jax.ShapeDtypeStruct((n_tok, GATE_COLS), jnp.bfloat16)],
        scratch_shapes=[pltpu.VMEM((tm, d), jnp.bfloat16)],
        compiler_params=_params("parallel", "arbitrary"),
        name="inproj",
    )(x2, g, w_all, cos_t, sin_t)


def _mla_proj_kernel(c_ref, gq_ref, gkv_ref, wuq_ref, wuk_ref, wuv_ref, cos_ref, sin_ref,
                     q_ref, k_ref, v_ref):
    cos_t = cos_ref[...]
    sin_t = sin_ref[...]
    cq = c_ref[:, :MLA_Q_RANK]
    cqn = (cq * _rms_scale(cq, NORM_EPS) * gq_ref[...]).astype(jnp.bfloat16)
    ckv = c_ref[:, MLA_Q_RANK:KPE_COL0]
    ckvn = (ckv * _rms_scale(ckv, NORM_EPS) * gkv_ref[...]).astype(jnp.bfloat16)
    kpe = _mla_rope(c_ref[:, KPE_COL0:KPE_COL0 + LANES], cos_t, sin_t)[:, :MLA_ROPE_DIM].astype(k_ref.dtype)
    qscale = MLA_QK_DIM ** -0.5 * LOG2E
    for h in range(MLA_HEADS):
        r = jnp.dot(cqn, wuq_ref[h], preferred_element_type=jnp.float32)
        q_ref[0, h, :, :MLA_NOPE_DIM] = (r[:, :MLA_NOPE_DIM] * qscale).astype(q_ref.dtype)
        qpe = _mla_rope(r[:, MLA_NOPE_DIM:], cos_t, sin_t)[:, :MLA_ROPE_DIM]
        q_ref[0, h, :, MLA_NOPE_DIM:] = (qpe * qscale).astype(q_ref.dtype)
        kn = jnp.dot(ckvn, wuk_ref[h], preferred_element_type=jnp.float32)
        k_ref[0, h, :, :MLA_NOPE_DIM] = kn.astype(k_ref.dtype)
        k_ref[0, h, :, MLA_NOPE_DIM:] = kpe
    v_ref[...] = jnp.dot(ckvn, wuv_ref[...], preferred_element_type=jnp.float32).astype(v_ref.dtype)


def _mla_proj(latent, gq, gkv, wuq, wuk, wuv, cos_t, sin_t, batch, seq, tm):
    n_tok = latent.shape[0]
    per_b = seq // tm
    head_spec = pl.BlockSpec((1, MLA_HEADS, tm, MLA_QK_DIM), lambda i: (i // per_b, 0, i % per_b, 0))
    head_shape = jax.ShapeDtypeStruct((batch, MLA_HEADS, seq, MLA_QK_DIM), jnp.bfloat16)
    return pl.pallas_call(
        _mla_proj_kernel,
        grid=(n_tok // tm,),
        in_specs=[pl.BlockSpec((tm, LATENT_COLS), lambda i: (i, 0)),
                  _resident(gq.shape), _resident(gkv.shape),
                  _resident(wuq.shape), _resident(wuk.shape), _resident(wuv.shape),
                  pl.BlockSpec((tm, LANES), lambda i: (i, 0)),
                  pl.BlockSpec((tm, LANES), lambda i: (i, 0))],
        out_specs=[head_spec, head_spec,
                   pl.BlockSpec((tm, MLA_HEADS * MLA_V_DIM), lambda i: (i, 0))],
        out_shape=[head_shape, head_shape,
                   jax.ShapeDtypeStruct((n_tok, MLA_HEADS * MLA_V_DIM), jnp.bfloat16)],
        compiler_params=_params("parallel"),
        name="mla_proj",
    )(latent, gq, gkv, wuq, wuk, wuv, cos_t, sin_t)


def _with_ones(v):
    return jnp.concatenate([v, jnp.ones((v.shape[0], LANES), v.dtype)], axis=-1)


def _softmax_pv(s, v_ones):
    m = jnp.max(s, axis=-1, keepdims=True)
    p = jnp.exp2(s - m).astype(v_ones.dtype)
    pv = jnp.dot(p, v_ones, preferred_element_type=jnp.float32)
    dv = v_ones.shape[1] - LANES
    return pv[:, :dv] / pv[:, dv:]


ATTN_HEADS_PER_STEP = 2


def _diff_attn_kernel(q_ref, k_ref, v_ref, lq1_ref, lk1_ref, lq2_ref, lk2_ref, g_ref, o_ref, v1_ref, *, rg):
    heads = q_ref.shape[2] // LANES

    @pl.when(pl.program_id(2) == 0)
    def _():
        for h in range(heads):
            v1_ref[h] = _with_ones(v_ref[0, :, h * DIFF_V_DIM:(h + 1) * DIFF_V_DIM])

    lam = (jnp.exp(jnp.sum(lq1_ref[...] * lk1_ref[...], axis=-1, keepdims=True))
           - jnp.exp(jnp.sum(lq2_ref[...] * lk2_ref[...], axis=-1, keepdims=True))
           + DIFF_LAMBDA_INIT)
    lane = _lane_iota((rg, LANES))
    for g in range(q_ref.shape[1] // rg):
        for h in range(heads):
            cols = slice(h * LANES, (h + 1) * LANES)
            q = q_ref[0, g * rg:(g + 1) * rg, cols]
            k = k_ref[0, :, cols]
            zero = jnp.zeros_like(q)
            q12 = jnp.concatenate([jnp.where(lane < DIFF_HEAD_DIM, q, zero),
                                   jnp.where(lane >= DIFF_HEAD_DIM, q, zero)], axis=0)
            s = lax.dot_general(q12, k, (((1,), (1,)), ((), ())), preferred_element_type=jnp.float32)
            a = _softmax_pv(s, v1_ref[h])
            o = a[:rg] - lam * a[rg:]
            o = o * _rms_scale(o, DIFF_SUBLN_EPS) * g_ref[...] * (1.0 - DIFF_LAMBDA_INIT)
            o_ref[0, g * rg:(g + 1) * rg, cols] = o.astype(o_ref.dtype)


def _diff_attn(qkv3, lq1, lk1, lq2, lk2, subln_g, tq, rg):
    batch, seq, _ = qkv3.shape
    hp = ATTN_HEADS_PER_STEP
    steps = DIFF_HEADS // hp
    width = hp * DIFF_V_DIM
    return pl.pallas_call(
        functools.partial(_diff_attn_kernel, rg=rg),
        grid=(batch, steps, seq // tq),
        in_specs=[pl.BlockSpec((1, tq, width), lambda b, hh, i: (b, i, hh)),
                  pl.BlockSpec((1, seq, width), lambda b, hh, i: (b, 0, steps + hh)),
                  pl.BlockSpec((1, seq, width), lambda b, hh, i: (b, 0, 2 * steps + hh)),
                  _resident(lq1.shape), _resident(lk1.shape), _resident(lq2.shape), _resident(lk2.shape),
                  _resident(subln_g.shape)],
        out_specs=pl.BlockSpec((1, tq, width), lambda b, hh, i: (b, i, hh)),
        out_shape=jax.ShapeDtypeStruct((batch, seq, DIFF_HEADS * DIFF_V_DIM), jnp.bfloat16),
        scratch_shapes=[pltpu.VMEM((hp, seq, DIFF_V_DIM + LANES), jnp.bfloat16)],
        compiler_params=_params("parallel", "parallel", "arbitrary"),
        name="diff_attn",
    )(qkv3, qkv3, qkv3, lq1, lk1, lq2, lk2, subln_g)


def _mla_attn_kernel(q_ref, k_ref, v_ref, o_ref, v1_ref, *, rg):
    heads = q_ref.shape[1]

    @pl.when(pl.program_id(2) == 0)
    def _():
        for h in range(heads):
            v1_ref[h] = _with_ones(v_ref[0, :, h * MLA_V_DIM:(h + 1) * MLA_V_DIM])

    for g in range(q_ref.shape[2] // rg):
        for h in range(heads):
            s = lax.dot_general(q_ref[0, h, g * rg:(g + 1) * rg], k_ref[0, h], (((1,), (1,)), ((), ())),
                                preferred_element_type=jnp.float32)
            o_ref[0, g * rg:(g + 1) * rg, h * MLA_V_DIM:(h + 1) * MLA_V_DIM] = (
                _softmax_pv(s, v1_ref[h]).astype(o_ref.dtype))


def _mla_attn(q_cat, k_cat, v3, tq, rg):
    batch, heads, seq, dqk = q_cat.shape
    hp = ATTN_HEADS_PER_STEP
    return pl.pallas_call(
        functools.partial(_mla_attn_kernel, rg=rg),
        grid=(batch, heads // hp, seq // tq),
        in_specs=[pl.BlockSpec((1, hp, tq, dqk), lambda b, h, i: (b, h, i, 0)),
                  pl.BlockSpec((1, hp, seq, dqk), lambda b, h, i: (b, h, 0, 0)),
                  pl.BlockSpec((1, seq, hp * MLA_V_DIM), lambda b, h, i: (b, 0, h))],
        out_specs=pl.BlockSpec((1, tq, hp * MLA_V_DIM), lambda b, h, i: (b, i, h)),
        out_shape=jax.ShapeDtypeStruct((batch, seq, heads * MLA_V_DIM), jnp.bfloat16),
        scratch_shapes=[pltpu.VMEM((hp, seq, MLA_V_DIM + LANES), jnp.bfloat16)],
        compiler_params=_params("parallel", "parallel", "arbitrary"),
        name="mla_attn",
    )(q_cat, k_cat, v3)


def _merge_out_kernel(oa_ref, ob_ref, sga_ref, sgb_ref, x_ref, woa_ref, wob_ref, wout_ref, h_ref):
    ya = jnp.dot(oa_ref[...], woa_ref[...], preferred_element_type=jnp.float32)
    yb = jnp.dot(ob_ref[...], wob_ref[...], preferred_element_type=jnp.float32)
    merged = sga_ref[...].astype(jnp.float32) * ya + sgb_ref[...].astype(jnp.float32) * yb
    h_ref[...] = x_ref[...] + jnp.dot(merged.astype(jnp.bfloat16), wout_ref[...],
                                       preferred_element_type=jnp.float32)


def _merge_out(o_a, o_b, gates, x2, w_oa, w_ob, w_out, tm):
    n_tok, d = x2.shape
    return pl.pallas_call(
        _merge_out_kernel,
        grid=(n_tok // tm,),
        in_specs=[pl.BlockSpec((tm, o_a.shape[1]), lambda i: (i, 0)),
                  pl.BlockSpec((tm, o_b.shape[1]), lambda i: (i, 0)),
                  pl.BlockSpec((tm, d), lambda i: (i, 0)),
                  pl.BlockSpec((tm, d), lambda i: (i, 1)),
                  pl.BlockSpec((tm, d), lambda i: (i, 0)),
                  _resident(w_oa.shape), _resident(w_ob.shape), _resident(w_out.shape)],
        out_specs=pl.BlockSpec((tm, d), lambda i: (i, 0)),
        out_shape=jax.ShapeDtypeStruct((n_tok, d), jnp.float32),
        compiler_params=_params("parallel"),
        name="merge_out",
    )(o_a, o_b, gates, gates, x2, w_oa, w_ob, w_out)


def _mem_kv_kernel(mem_ref, g_ref, w_ref, kv_ref):
    mf = mem_ref[0]
    mn = (mf * _rms_scale(mf, NORM_EPS) * g_ref[...]).astype(jnp.bfloat16)
    kv_ref[0] = jnp.dot(mn, w_ref[...], preferred_element_type=jnp.float32).astype(kv_ref.dtype)


def _mem_kv(mem, g, w_ckv):
    batch, m, d = mem.shape
    return pl.pallas_call(
        _mem_kv_kernel,
        grid=(batch,),
        in_specs=[pl.BlockSpec((1, m, d), lambda b: (b, 0, 0)), _resident(g.shape), _resident(w_ckv.shape)],
        out_specs=pl.BlockSpec((1, m, w_ckv.shape[1]), lambda b: (b, 0, 0)),
        out_shape=jax.ShapeDtypeStruct((batch, m, w_ckv.shape[1]), jnp.bfloat16),
        compiler_params=_params("parallel"),
        name="mem_kv",
    )(mem, g, w_ckv)


def _cross_router_kernel(h_ref, gc_ref, wcq_ref, kv_ref, wco_ref, gf_ref, wr_ref, br_ref,
                         h2_ref, eid_ref, rank_ref, wts_ref, cnt_ref, carry_ref):
    i = pl.program_id(0)

    @pl.when(i == 0)
    def _():
        carry_ref[...] = jnp.zeros_like(carry_ref)

    h1 = h_ref[...]
    tm = h1.shape[0]
    hn = (h1 * _rms_scale(h1, NORM_EPS) * gc_ref[...]).astype(jnp.bfloat16)
    q = jnp.dot(hn, wcq_ref[...], preferred_element_type=jnp.float32) * (CROSS_HEAD_DIM ** -0.5 * LOG2E)
    q = q.astype(jnp.bfloat16)
    kv_cols = CROSS_HEADS * CROSS_HEAD_DIM
    heads = []
    for hd in range(CROSS_HEADS):
        lo = hd * CROSS_HEAD_DIM
        kh = kv_ref[0, :, lo:lo + CROSS_HEAD_DIM]
        vh = kv_ref[0, :, kv_cols + lo:kv_cols + lo + CROSS_HEAD_DIM]
        s = lax.dot_general(q[:, lo:lo + CROSS_HEAD_DIM], kh, (((1,), (1,)), ((), ())),
                            preferred_element_type=jnp.float32)
        heads.append(_softmax_pv(s, _with_ones(vh)).astype(jnp.bfloat16))
    o = jnp.concatenate(heads, axis=-1)
    h2 = h1 + jnp.dot(o, wco_ref[...], preferred_element_type=jnp.float32)
    h2_ref[...] = h2

    t = h2 * _rms_scale(h2, NORM_EPS) * gf_ref[...]
    t_hi = t.astype(jnp.bfloat16)
    t_lo = (t - t_hi.astype(jnp.float32)).astype(jnp.bfloat16)
    hi = jnp.dot(t_hi, wr_ref[...], preferred_element_type=jnp.float32)
    lo = jnp.dot(t_lo, wr_ref[:, :LANES], preferred_element_type=jnp.float32)
    logits = hi[:, :LANES] + (hi[:, LANES:] + lo) + br_ref[...]
    lane = _lane_iota(logits.shape)
    neg = jnp.float32(-jnp.inf)
    big = jnp.int32(2 * LANES)
    is_group = lane < N_GROUPS
    lg = jnp.where(is_group, logits, neg)
    mg = jnp.max(lg, axis=-1, keepdims=True)
    g_idx = jnp.min(jnp.where(is_group & (logits == mg), lane, big), axis=-1, keepdims=True)
    g_p = 1.0 / jnp.sum(jnp.exp(lg - mg), axis=-1, keepdims=True)
    lo_lane = ROUTER_EXPERT_LANE0 + EXPERTS_PER_GROUP * g_idx
    in_grp = (lane >= lo_lane) & (lane < lo_lane + EXPERTS_PER_GROUP)
    l1 = jnp.max(jnp.where(in_grp, logits, neg), axis=-1, keepdims=True)
    i1 = jnp.min(jnp.where(in_grp & (logits == l1), lane, big), axis=-1, keepdims=True)
    rest = in_grp & (lane != i1)
    l2 = jnp.max(jnp.where(rest, logits, neg), axis=-1, keepdims=True)
    i2 = jnp.min(jnp.where(rest & (logits == l2), lane, big), axis=-1, keepdims=True)
    d = jnp.exp(l2 - l1)
    w1 = g_p / (1.0 + d)
    w2 = w1 * d

    oh1 = lane == i1
    oh2 = lane == i2
    cnt = (oh1 | oh2).astype(jnp.bfloat16)
    row = lax.broadcasted_iota(jnp.int32, (tm, tm), 0)
    col = lax.broadcasted_iota(jnp.int32, (tm, tm), 1)
    before = (col < row).astype(jnp.bfloat16)
    slot = jnp.dot(before, cnt, preferred_element_type=jnp.float32) + carry_ref[...]
    r1 = jnp.sum(jnp.where(oh1, slot, 0.0), axis=-1, keepdims=True)
    r2 = jnp.sum(jnp.where(oh2, slot, 0.0), axis=-1, keepdims=True)
    carry_ref[...] += jnp.sum(cnt.astype(jnp.float32), axis=0, keepdims=True)
    cnt_ref[...] = carry_ref[...]

    eye = row == col

    def to_row(c, dtype):
        return jnp.sum(jnp.where(eye, c.astype(jnp.float32), 0.0), axis=0, keepdims=True).astype(dtype)

    eid_ref[0] = jnp.concatenate([to_row(i1 - ROUTER_EXPERT_LANE0, jnp.int32),
                                  to_row(i2 - ROUTER_EXPERT_LANE0, jnp.int32)], axis=0)
    rank_ref[0] = jnp.concatenate([to_row(r1, jnp.int32), to_row(r2, jnp.int32)], axis=0)
    wts_ref[...] = jnp.where(_lane_iota((tm, 2)) == 0, w1, w2)


def _cross_router(h1, gc, w_cq, kv_mem, w_co, gf, w_r, b_r, seq, tm):
    n_tok, d = h1.shape
    per_b = seq // tm
    row2 = pl.BlockSpec((tm, 2), lambda i: (i, 0))
    lane2 = pl.BlockSpec((1, 2, tm), lambda i: (i, 0, 0))
    return pl.pallas_call(
        _cross_router_kernel,
        grid=(n_tok // tm,),
        in_specs=[pl.BlockSpec((tm, d), lambda i: (i, 0)),
                  _resident(gc.shape), _resident(w_cq.shape),
                  pl.BlockSpec((1,) + kv_mem.shape[1:], lambda i: (i // per_b, 0, 0)),
                  _resident(w_co.shape), _resident(gf.shape), _resident(w_r.shape), _resident(b_r.shape)],
        out_specs=[pl.BlockSpec((tm, d), lambda i: (i, 0)), lane2, lane2, row2,
                   pl.BlockSpec((1, LANES), lambda i: (0, 0))],
        out_shape=[jax.ShapeDtypeStruct((n_tok, d), jnp.float32),
                   jax.ShapeDtypeStruct((n_tok // tm, 2, tm), jnp.int32),
                   jax.ShapeDtypeStruct((n_tok // tm, 2, tm), jnp.int32),
                   jax.ShapeDtypeStruct((n_tok, 2), jnp.float32),
                   jax.ShapeDtypeStruct((1, LANES), jnp.float32)],
        scratch_shapes=[pltpu.VMEM((1, LANES), jnp.float32)],
        compiler_params=_params("arbitrary"),
        name="cross_router",
    )(h1, gc, w_cq, kv_mem, w_co, gf, w_r, b_r)


def _pad_bits():
    return [1 << b for b in reversed(range(MOE_ROWS_PER_BLOCK.bit_length() - 1))]


def _zero_fill_sizes():
    return _pad_bits() * N_EXPERTS + [MOE_ROWS_PER_BLOCK // 2] * (2 * N_EXPERTS)


def _dispatch_kernel(dest_ref, zfill_ref, h_ref, g_ref, xb_ref, t_ref, zero_ref, sem, zsem):
    i = pl.program_id(0)
    tm = h_ref.shape[0]
    slot = i % 2

    def zero_copies(action):
        for idx, rows in enumerate(_zero_fill_sizes()):
            @pl.when(zfill_ref[1, idx] > 0)
            def _():
                start = pl.multiple_of(zfill_ref[0, idx] * SUBLANES, SUBLANES)
                action(pltpu.make_async_copy(zero_ref.at[pl.ds(0, rows * SUBLANES), :],
                                             xb_ref.at[pl.ds(start, rows * SUBLANES), :], zsem))

    @pl.when(i == 0)
    def _():
        zero_ref[...] = jnp.zeros_like(zero_ref)
        zero_copies(lambda c: c.start())

    h2 = h_ref[...]
    t = h2 * _rms_scale(h2, NORM_EPS) * g_ref[...]
    half = t.shape[1] // 2
    _rows_to_tiles(t_ref.at[slot], _pack_bf16_pair(t[:, :half], t[:, half:]))

    for r in range(tm):
        for k in range(2):
            dst = pl.multiple_of(dest_ref[0, k * tm + r] * SUBLANES, SUBLANES)
            pltpu.make_async_copy(t_ref.at[slot, pl.ds(r * SUBLANES, SUBLANES), :],
                                  xb_ref.at[pl.ds(dst, SUBLANES), :], sem.at[slot]).start(priority=k)

    def wait_tile(which):
        for _ in range(2):
            pltpu.make_async_copy(t_ref.at[which], xb_ref.at[pl.ds(0, tm * SUBLANES), :], sem.at[which]).wait()

    @pl.when(i > 0)
    def _():
        wait_tile(1 - slot)

    @pl.when(i == pl.num_programs(0) - 1)
    def _():
        wait_tile(slot)
        zero_copies(lambda c: c.wait())


def _dispatch(dest3, zfill, h2, gf, p_rows, tm):
    n_tok, d = h2.shape
    return pl.pallas_call(
        _dispatch_kernel,
        grid=(n_tok // tm,),
        in_specs=[pl.BlockSpec((None, 1, 2 * tm), lambda i: (i, 0, 0), memory_space=pltpu.SMEM),
                  pl.BlockSpec(memory_space=pltpu.SMEM),
                  pl.BlockSpec((tm, d), lambda i: (i, 0)),
                  _resident(gf.shape)],
        out_specs=pl.BlockSpec(memory_space=pl.ANY),
        out_shape=jax.ShapeDtypeStruct((p_rows * SUBLANES, LANES), jnp.uint32),
        scratch_shapes=[pltpu.VMEM((2, tm * SUBLANES, LANES), jnp.uint32),
                        pltpu.VMEM((MOE_ROWS_PER_BLOCK // 2 * SUBLANES, LANES), jnp.uint32),
                        pltpu.SemaphoreType.DMA((2,)), pltpu.SemaphoreType.DMA(())],
        compiler_params=_params("arbitrary"),
        name="moe_dispatch",
    )(dest3, zfill, h2, gf)


def _expert_kernel(be_ref, nact_ref, x_ref, wg_hbm, wu_hbm, wd_hbm, y_ref,
                   wg_f, wu_f, wd_f, wg_b, wu_b, wd_b, sem):
    i = pl.program_id(0)
    blk = i - 1
    nact = nact_ref[0]
    last_blk = pl.num_programs(0) - 2

    def weight_copies(e):
        return (pltpu.make_async_copy(wg_hbm.at[e], wg_f, sem),
                pltpu.make_async_copy(wu_hbm.at[e], wu_f, sem),
                pltpu.make_async_copy(wd_hbm.at[e], wd_f, sem))

    def fetch(e):
        for c in weight_copies(e):
            c.start(priority=1)

    def land(e):
        for c in weight_copies(e):
            c.wait()
        half = wg_f.shape[0] // 2
        for c in range(SUBLANES):
            for part, src0 in enumerate((c * LANES, half + c * LANES)):
                dst0 = (2 * c + part) * LANES
                wg_b[dst0:dst0 + LANES, :] = wg_f[src0:src0 + LANES, :].astype(jnp.bfloat16)
                wu_b[dst0:dst0 + LANES, :] = wu_f[src0:src0 + LANES, :].astype(jnp.bfloat16)
        wd_b[...] = wd_f[...].astype(jnp.bfloat16)

    @pl.when(i == 0)
    def _():
        fetch(be_ref[0])
        land(be_ref[0])

    @pl.when((i > 0) & (blk < nact))
    def _():
        here = be_ref[blk]
        nxt = be_ref[jnp.minimum(blk + 1, last_blk)]
        prv = be_ref[jnp.maximum(blk - 1, 0)]
        seg_end = nact_ref[1 + N_EXPERTS + here] + nact_ref[1 + here]
        has_next = seg_end < nact
        after = be_ref[jnp.minimum(seg_end, last_blk)]
        is_first = (blk == 0) | (prv != here)
        is_last = (blk + 1 >= nact) | (nxt != here)

        @pl.when(is_first & has_next)
        def _():
            fetch(after)

        bm = x_ref.shape[0] // SUBLANES
        hb = bm // 2
        rows_here = nact_ref[1 + 2 * N_EXPERTS + here] - (blk - nact_ref[1 + N_EXPERTS + here]) * bm

        def swiglu(h):
            view = pl.ds(h * hb * SUBLANES, hb * SUBLANES)
            parts = []
            for chunk in _tiles_to_row_chunks(x_ref.at[view, :], hb):
                x_a, x_b = _unpack_bf16_pair(chunk)
                parts += [x_a.astype(jnp.bfloat16), x_b.astype(jnp.bfloat16)]
            xb = jnp.concatenate(parts, axis=1)
            gate = jnp.dot(xb, wg_b[...], preferred_element_type=jnp.float32)
            up = jnp.dot(xb, wu_b[...], preferred_element_type=jnp.float32)
            hid = (gate * _sigmoid(gate) * up).astype(jnp.bfloat16)
            y = jnp.dot(hid, wd_b[...], preferred_element_type=jnp.float32)
            half = y.shape[1] // 2
            _rows_to_tiles(y_ref.at[view, :], _pack_bf16_pair(y[:, :half], y[:, half:]))

        swiglu(0)

        @pl.when(rows_here > hb)
        def _():
            swiglu(1)

        @pl.when(rows_here <= hb)
        def _():
            y_ref[pl.ds(hb * SUBLANES, hb * SUBLANES), :] = jnp.zeros((hb * SUBLANES, LANES), y_ref.dtype)

        @pl.when(is_last & has_next)
        def _():
            land(after)

    @pl.when((i > 0) & (blk >= nact))
    def _():
        y_ref[...] = jnp.zeros_like(y_ref)


def _experts(block_expert, sched, xb, w_gate, w_up, w_down, bm):
    p_rows = xb.shape[0] // SUBLANES
    d = w_gate.shape[1]
    de = w_gate.shape[-1]

    def x_map(i, be, sc):
        return (jnp.clip(i - 1, 0, sc[0] - 1), 0)

    grid_spec = pltpu.PrefetchScalarGridSpec(
        num_scalar_prefetch=2,
        grid=(p_rows // bm + 1,),
        in_specs=[pl.BlockSpec((bm * SUBLANES, LANES), x_map),
                  pl.BlockSpec(memory_space=pl.ANY),
                  pl.BlockSpec(memory_space=pl.ANY),
                  pl.BlockSpec(memory_space=pl.ANY)],
        out_specs=pl.BlockSpec((bm * SUBLANES, LANES), lambda i, be, sc: (jnp.maximum(i - 1, 0), 0)),
        scratch_shapes=[pltpu.VMEM((d, de), jnp.float32), pltpu.VMEM((d, de), jnp.float32),
                        pltpu.VMEM((de, d), jnp.float32),
                        pltpu.VMEM((d, de), jnp.bfloat16), pltpu.VMEM((d, de), jnp.bfloat16),
                        pltpu.VMEM((de, d), jnp.bfloat16),
                        pltpu.SemaphoreType.DMA(())],
    )
    return pl.pallas_call(
        _expert_kernel,
        grid_spec=grid_spec,
        out_shape=jax.ShapeDtypeStruct(xb.shape, jnp.uint32),
        compiler_params=_params("arbitrary"),
        name="moe_experts",
    )(block_expert, sched, xb, w_gate, w_up, w_down)


def _combine_kernel(dest_ref, dest_next_ref, h_ref, wts_ref, g_ref, y_ref, o_ref, ybuf, sem):
    i = pl.program_id(0)
    tm = h_ref.shape[0]
    slot = i % 2

    def gather(idx_ref, which):
        for r in range(tm):
            for k in range(2):
                src = pl.multiple_of(idx_ref[0, k * tm + r] * SUBLANES, SUBLANES)
                pltpu.make_async_copy(y_ref.at[pl.ds(src, SUBLANES), :],
                                      ybuf.at[which, k, pl.ds(r * SUBLANES, SUBLANES), :], sem.at[which]).start(priority=k)

    def wait_tile(which):
        for k in range(2):
            pltpu.make_async_copy(y_ref.at[pl.ds(0, tm * SUBLANES), :], ybuf.at[which, k], sem.at[which]).wait()

    @pl.when(i == 0)
    def _():
        gather(dest_ref, slot)

    wait_tile(slot)
    gather(dest_next_ref, 1 - slot)

    w = wts_ref[...]
    half = h_ref.shape[1] // 2
    lo_parts, hi_parts = [], []
    for c, (c0, c1) in enumerate(zip(_tiles_to_row_chunks(ybuf.at[slot, 0], tm),
                                      _tiles_to_row_chunks(ybuf.at[slot, 1], tm))):
        a0, b0 = _unpack_bf16_pair(c0)
        a1, b1 = _unpack_bf16_pair(c1)
        lo_parts.append(h_ref[:, c * LANES:(c + 1) * LANES] + w[:, 0:1] * a0 + w[:, 1:2] * a1)
        hi_parts.append(h_ref[:, half + c * LANES:half + (c + 1) * LANES] + w[:, 0:1] * b0 + w[:, 1:2] * b1)
    h3 = jnp.concatenate(lo_parts + hi_parts, axis=1)
    o_ref[...] = h3 * _rms_scale(h3, NORM_EPS) * g_ref[...]

    @pl.when(i == pl.num_programs(0) - 1)
    def _():
        wait_tile(1 - slot)


def _combine(dest3, h2, wts, g_final, y, tm):
    n_tok, d = h2.shape
    last = n_tok // tm - 1
    return pl.pallas_call(
        _combine_kernel,
        grid=(n_tok // tm,),
        in_specs=[pl.BlockSpec((None, 1, 2 * tm), lambda i: (i, 0, 0), memory_space=pltpu.SMEM),
                  pl.BlockSpec((None, 1, 2 * tm), lambda i: (jnp.minimum(i + 1, last), 0, 0),
                               memory_space=pltpu.SMEM),
                  pl.BlockSpec((tm, d), lambda i: (i, 0)),
                  pl.BlockSpec((tm, 2), lambda i: (i, 0)),
                  _resident(g_final.shape),
                  pl.BlockSpec(memory_space=pl.ANY)],
        out_specs=pl.BlockSpec((tm, d), lambda i: (i, 0)),
        out_shape=jax.ShapeDtypeStruct((n_tok, d), jnp.float32),
        scratch_shapes=[pltpu.VMEM((2, 2, tm * SUBLANES, LANES), jnp.uint32), pltpu.SemaphoreType.DMA((2,))],
        compiler_params=_params("arbitrary"),
        name="moe_combine",
    )(dest3, dest3, h2, wts, g_final, y)


def _transpose_w_in(w_in):
    return jnp.swapaxes(w_in, 0, 1).astype(jnp.bfloat16)


def _split_w_uq(w_uq):
    half = MLA_ROPE_DIM // 2
    w = w_uq.reshape(MLA_Q_RANK, MLA_HEADS, MLA_QK_DIM).transpose(1, 0, 2)
    pe = w[:, :, MLA_NOPE_DIM:]
    pe_swapped = jnp.concatenate([pe[:, :, half:], pe[:, :, :half]], axis=2)
    return jnp.concatenate([w, pe_swapped], axis=2).astype(jnp.bfloat16)


def _split_w_ukv(w_ukv):
    w = w_ukv.reshape(MLA_KV_RANK, MLA_HEADS, MLA_NOPE_DIM + MLA_V_DIM)
    wuk = w[:, :, :MLA_NOPE_DIM].transpose(1, 0, 2).astype(jnp.bfloat16)
    wuv = w[:, :, MLA_NOPE_DIM:].reshape(MLA_KV_RANK, MLA_HEADS * MLA_V_DIM).astype(jnp.bfloat16)
    return wuk, wuv


def kernel(x, mem, positions, attn_norm_g, w_in, diff_lambda_q1, diff_lambda_k1, diff_lambda_q2, diff_lambda_k2, diff_subln_g, w_o_diff, mla_q_norm_g, w_uq, mla_kv_norm_g, w_ukv, w_o_mla, w_out, cross_norm_g, mem_norm_g, w_cq, w_ckv, w_co, ffn_norm_g, w_router_group, b_router_group, w_router_expert, b_router_expert, w_expert_gate, w_expert_up, w_expert_down, final_norm_g):
    batch, seq, d = x.shape
    assert d == D_MODEL and w_in.shape[0] == 1, "single-layer kernel"
    n_tok = batch * seq
    bf = jnp.bfloat16
    x2 = x.reshape(n_tok, d)

    tm_proj = min(1024, seq)
    tm_row = min(1024, seq)
    tm_moe = min(MOE_ROWS_PER_BLOCK, seq)
    tm_comb = min(512, seq)
    tm_cross = min(512, seq)
    tq = min(2048, seq)
    rg_diff = 128
    rg_mla = 256

    cos_t, sin_t = _rope_tables(positions, n_tok, min(2048, seq))

    g_attn = attn_norm_g[0].reshape(1, d)
    qkv, latent, gates = _inproj(x2, g_attn, _transpose_w_in(w_in[0]), cos_t, sin_t, tm_proj)

    o_a = _diff_attn(qkv.reshape(batch, seq, QKV_COLS),
                     diff_lambda_q1[0].reshape(1, -1), diff_lambda_k1[0].reshape(1, -1),
                     diff_lambda_q2[0].reshape(1, -1), diff_lambda_k2[0].reshape(1, -1),
                     diff_subln_g[0].reshape(1, -1), tq, rg_diff)

    wuk, wuv = _split_w_ukv(w_ukv[0])
    q_cat, k_cat, v_mla = _mla_proj(latent, mla_q_norm_g[0].reshape(1, -1), mla_kv_norm_g[0].reshape(1, -1),
                                    _split_w_uq(w_uq[0]), wuk, wuv, cos_t, sin_t, batch, seq, tm_proj)
    o_b = _mla_attn(q_cat, k_cat, v_mla.reshape(batch, seq, MLA_HEADS * MLA_V_DIM), tq, rg_mla)

    h1 = _merge_out(o_a.reshape(n_tok, -1), o_b.reshape(n_tok, -1), gates, x2,
                    w_o_diff[0].astype(bf), w_o_mla[0].astype(bf), w_out[0].astype(bf), tm_cross)

    kv_mem = _mem_kv(mem, mem_norm_g[0].reshape(1, d), w_ckv[0].astype(bf))
    n_router = N_GROUPS + N_EXPERTS
    w_r = jnp.concatenate([w_router_group[0].astype(jnp.float32), w_router_expert[0].astype(jnp.float32),
                           jnp.zeros((d, LANES - n_router), jnp.float32)], axis=1)
    w_r_hi = w_r.astype(bf)
    w_r_lo = (w_r - w_r_hi.astype(jnp.float32)).astype(bf)
    w_r = jnp.concatenate([w_r_hi, w_r_lo], axis=1)
    b_r = jnp.concatenate([b_router_group[0].astype(jnp.float32), b_router_expert[0].astype(jnp.float32),
                           jnp.zeros((LANES - n_router,), jnp.float32)]).reshape(1, LANES)
    g_ffn = ffn_norm_g[0].reshape(1, d)
    h2, eid, rank, wts, cnt = _cross_router(h1, cross_norm_g[0].reshape(1, d), w_cq[0].astype(bf), kv_mem,
                                            w_co[0].astype(bf), g_ffn, w_r, b_r, seq, tm_cross)

    bm = MOE_ROWS_PER_BLOCK
    assert tm_moe == bm
    counts = cnt[0, ROUTER_EXPERT_LANE0:ROUTER_EXPERT_LANE0 + N_EXPERTS].astype(jnp.int32)
    padded = ((counts + bm - 1) // bm) * bm
    padded_end = jnp.cumsum(padded)
    padded_off = padded_end - padded
    seg_start = jnp.sum(jnp.where(eid[..., None] == jnp.arange(N_EXPERTS, dtype=jnp.int32), padded_off, 0), axis=-1)
    dest = seg_start + rank
    p_rows = ((2 * n_tok + bm - 1) // bm) * bm + N_EXPERTS * bm
    n_blocks = p_rows // bm
    n_active = (padded_end[-1] // bm).astype(jnp.int32)
    blk = jnp.minimum(jnp.arange(n_blocks, dtype=jnp.int32), n_active - 1)
    block_expert = jnp.sum((padded_end[None, :] <= (blk * bm)[:, None]).astype(jnp.int32), axis=1)
    block_expert = jnp.minimum(block_expert, N_EXPERTS - 1)
    def tile_slots(tm):
        by_choice = dest.transpose(1, 0, 2).reshape(2, n_tok // tm, tm)
        return by_choice.transpose(1, 0, 2).reshape(n_tok // tm, 1, 2 * tm)
    pad_len = padded - counts
    bit = jnp.asarray(_pad_bits(), jnp.int32)
    higher = pad_len[:, None] & ~(2 * bit[None, :] - 1)
    pad_start = (padded_off + counts)[:, None] + higher
    pad_flag = (pad_len[:, None] & bit[None, :]) > 0
    half_blk = bm // 2
    unused = (n_active + jnp.arange(N_EXPERTS, dtype=jnp.int32))[:, None] * bm + jnp.arange(2, dtype=jnp.int32) * half_blk
    unused_flag = unused < n_blocks * bm
    zfill = jnp.stack([jnp.concatenate([pad_start.reshape(-1), jnp.minimum(unused, (n_blocks * bm - half_blk)).reshape(-1)]),
                       jnp.concatenate([pad_flag.reshape(-1), unused_flag.reshape(-1)]).astype(jnp.int32)]).astype(jnp.int32)
    sched = jnp.concatenate([n_active.reshape(1), padded // bm, padded_off // bm, counts]).astype(jnp.int32)

    xb = _dispatch(tile_slots(tm_row), zfill, h2, g_ffn, p_rows, tm_row)
    y = _experts(block_expert, sched, xb, w_expert_gate[0], w_expert_up[0], w_expert_down[0], bm)
    out = _combine(tile_slots(tm_comb), h2, wts, final_norm_g.reshape(1, d), y, tm_comb)
    return out.reshape(batch, seq, d)
```
